```python
import math
import jax, jax.numpy as jnp
from jax import lax
import numpy as np

D_MODEL = 1024
BATCH = 16
SEQ = 256
DEPTH = 2
DEC_BATCH = 2
DEC_SEQ = 4096
PAST_LEN = 512

GRID_W = 64
N_EVEN = (DEPTH + 1) // 2
N_ODD = DEPTH // 2
D_A = D_MODEL // 2
CONV_A = 3
H_B = 4
DK_B = 64
DV_B = 2 * DK_B
D_B = H_B * DV_B
W_IN_EVEN = 3 * D_A + 3 * D_B
D_C = D_MODEL // 2
CONV_C = 31
D_D = D_MODEL // 2
POOL_WINDOWS = (2, 4, 8, 16)
N_POOL = len(POOL_WINDOWS)
D_DG = D_D // N_POOL
W_IN_ODD = 2 * D_C + D_D
W_MIX = D_MODEL
N_EXPERTS = 64
TOP_K = 8
N_GROUPS = 8
TOPK_GROUPS = 4
D_EXPERT = 256
D_SHARED = 256
ROUTED_SCALE = 2.5
TOKEN_BLOCK = 128
Q_BLOCK = 128
ROPE_BASE = 10000.0
EPS = 1e-6

kernel_name = 'hybrid_diffusion_conv_diffattn_conformer_pool_moe_step'

F32 = jnp.float32


def rms_norm(x, g):
    xf = x.astype(F32)
    y = xf * lax.rsqrt(jnp.mean(xf * xf, axis=-1, keepdims=True) + EPS)
    return y.astype(x.dtype) * g.astype(x.dtype)


def layer_norm(x, g, b):
    xf = x.astype(F32)
    mu = jnp.mean(xf, axis=-1, keepdims=True)
    var = jnp.mean(jnp.square(xf - mu), axis=-1, keepdims=True)
    y = (xf - mu) * lax.rsqrt(var + EPS)
    return y.astype(x.dtype) * g.astype(x.dtype) + b.astype(x.dtype)


def depthwise_conv(x, w):
    k = w.shape[0]
    return lax.conv_general_dilated(
        x, w[:, None, :].astype(x.dtype), window_strides=(1,),
        padding=[(k // 2, k // 2)], dimension_numbers=('NWC', 'WIO', 'NWC'),
        feature_group_count=x.shape[-1])


def axial_rope_tables(length):
    rows = length // GRID_W
    half = DK_B // 2
    pos_r = jnp.repeat(jnp.arange(rows), GRID_W).astype(F32)
    pos_c = jnp.tile(jnp.arange(GRID_W), rows).astype(F32)
    freqs = ROPE_BASE ** (-jnp.arange(0, half, 2, dtype=F32) / half)
    ang_r = pos_r[:, None] * freqs[None, :]
    ang_c = pos_c[:, None] * freqs[None, :]
    return (jnp.cos(ang_r), jnp.sin(ang_r), jnp.cos(ang_c), jnp.sin(ang_c))


def _rotate(x, cos, sin):
    h = x.shape[-1] // 2
    x1, x2 = x[..., :h], x[..., h:]
    return jnp.concatenate([x1 * cos - x2 * sin, x2 * cos + x1 * sin], axis=-1)


def apply_axial_rope(x, rope):
    cos_r, sin_r, cos_c, sin_c = rope
    xf = x.astype(F32)
    half = DK_B // 2
    out = jnp.concatenate([_rotate(xf[..., :half], cos_r, sin_r),
                           _rotate(xf[..., half:], cos_c, sin_c)], axis=-1)
    return out.astype(x.dtype)


def diff_attention(q1, q2, k1, k2, v, lam):
    b, h, lq, dk = q1.shape
    nb = lq // Q_BLOCK
    scale = 1.0 / math.sqrt(dk)
    q1b = jnp.moveaxis(q1.reshape(b, h, nb, Q_BLOCK, dk), 2, 0)
    q2b = jnp.moveaxis(q2.reshape(b, h, nb, Q_BLOCK, dk), 2, 0)

    def block(args):
        qa, qb = args
        s1 = jnp.einsum('bhqd,bhkd->bhqk', qa, k1, preferred_element_type=F32) * scale
        s2 = jnp.einsum('bhqd,bhkd->bhqk', qb, k2, preferred_element_type=F32) * scale
        a = jax.nn.softmax(s1, axis=-1) - lam * jax.nn.softmax(s2, axis=-1)
        return jnp.einsum('bhqk,bhkd->bhqd', a.astype(v.dtype), v)

    out = lax.map(block, (q1b, q2b))
    return jnp.moveaxis(out, 0, 2).reshape(b, h, lq, v.shape[-1])


def even_mixer(h, w_in, conv_w, qn, kn, lq1, lk1, lq2, lk2, subg, w_out,
               lam_init, rope, ctx_kv):
    b, l, _ = h.shape
    u = h @ w_in
    bg, cg, hin, q, k, v = jnp.split(
        u, [D_A, 2 * D_A, 3 * D_A, 3 * D_A + D_B, 3 * D_A + 2 * D_B], axis=-1)
    ya = bg * depthwise_conv(cg * hin, conv_w)
    q = rms_norm(q.reshape(b, l, H_B, 2, DK_B), qn).transpose(0, 2, 3, 1, 4)
    k = rms_norm(k.reshape(b, l, H_B, 2, DK_B), kn).transpose(0, 2, 3, 1, 4)
    v = v.reshape(b, l, H_B, DV_B).transpose(0, 2, 1, 3)
    if rope is not None:
        q = apply_axial_rope(q, rope)
        k = apply_axial_rope(k, rope)
    k_cache = k.transpose(0, 1, 3, 2, 4).reshape(b, H_B, l, 2 * DK_B)
    k_all, v_all = k, v
    if ctx_kv is not None:
        ck, cv = ctx_kv
        lc = ck.shape[2]
        ck = ck.reshape(b, H_B, lc, 2, DK_B).transpose(0, 1, 3, 2, 4).astype(k.dtype)
        k_all = jnp.concatenate([ck, k], axis=3)
        v_all = jnp.concatenate([cv.astype(v.dtype), v], axis=2)
    lam = (jnp.exp(jnp.sum(lq1.astype(F32) * lk1.astype(F32)))
           - jnp.exp(jnp.sum(lq2.astype(F32) * lk2.astype(F32))) + lam_init)
    o = diff_attention(q[:, :, 0], q[:, :, 1], k_all[:, :, 0], k_all[:, :, 1], v_all, lam)
    o = rms_norm(o, subg) * (1.0 - lam_init)
    o = o.transpose(0, 2, 1, 3).reshape(b, l, D_B)
    y = jnp.concatenate([ya, o], axis=-1) @ w_out
    return y, (k_cache, v)


def multi_scale_pool_minus_self(x):
    b, l, _ = x.shape
    xf = x.astype(F32).reshape(b, l, N_POOL, D_DG)
    cs = jnp.concatenate([jnp.zeros_like(xf[:, :1]), lax.cumsum(xf, axis=1)], axis=1)
    t = jnp.arange(l)
    outs = []
    for g, w in enumerate(POOL_WINDOWS):
        lo = jnp.clip(t - w // 2, 0, l)
        hi = jnp.clip(t - w // 2 + w, 0, l)
        s = cs[:, hi, g] - cs[:, lo, g]
        outs.append(s / (hi - lo).astype(F32)[None, :, None])
    pooled = jnp.stack(outs, axis=2)
    return (pooled - xf).astype(x.dtype)


def odd_mixer(h, w_in, conv_w, conv_b, ln_g, ln_b, w_pool, p_scale, w_out):
    b, l, _ = h.shape
    u = h @ w_in
    glu_in, pd = u[..., :2 * D_C], u[..., 2 * D_C:]
    g = glu_in[..., :D_C] * jax.nn.sigmoid(glu_in[..., D_C:])
    g = depthwise_conv(g, conv_w) + conv_b
    g = jax.nn.silu(layer_norm(g, ln_g, ln_b))
    pooled = multi_scale_pool_minus_self(pd)
    yd = jnp.einsum('blgc,gcd->blgd', pooled, w_pool).reshape(b, l, D_D) * p_scale
    return jnp.concatenate([g, yd], axis=-1) @ w_out


def moe(x, w_router, b_router, w_gate, w_up, w_down, ws_gate, ws_up, ws_down):
    b, l, d = x.shape
    t = x.reshape(-1, d)
    n_tok = t.shape[0]
    scores = jax.nn.sigmoid(jnp.matmul(t, w_router, preferred_element_type=F32))
    biased = scores + b_router.astype(F32)
    grp = biased.reshape(n_tok, N_GROUPS, N_EXPERTS // N_GROUPS)
    gscore = jnp.sum(lax.top_k(grp, 2)[0], axis=-1)
    _, gidx = lax.top_k(gscore, TOPK_GROUPS)
    gmask = jnp.sum(jax.nn.one_hot(gidx, N_GROUPS, dtype=F32), axis=1) > 0
    emask = jnp.repeat(gmask, N_EXPERTS // N_GROUPS, axis=1)
    _, eidx = lax.top_k(jnp.where(emask, biased, -jnp.inf), TOP_K)
    wsel = jnp.take_along_axis(scores, eidx, axis=-1)
    wsel = wsel / jnp.sum(wsel, axis=-1, keepdims=True) * ROUTED_SCALE
    gate = jnp.sum(jax.nn.one_hot(eidx, N_EXPERTS, dtype=F32) * wsel[..., None], axis=1)

    def block(args):
        tb, gb = args
        hg = jnp.einsum('td,edf->tef', tb, w_gate)
        hu = jnp.einsum('td,edf->tef', tb, w_up)
        a = jax.nn.silu(hg) * hu * gb[:, :, None].astype(tb.dtype)
        return jnp.einsum('tef,efd->td', a, w_down)

    nb = n_tok // TOKEN_BLOCK
    routed = lax.map(block, (t.reshape(nb, TOKEN_BLOCK, d),
                             gate.reshape(nb, TOKEN_BLOCK, N_EXPERTS))).reshape(n_tok, d)
    shared = (jax.nn.silu(t @ ws_gate) * (t @ ws_up)) @ ws_down
    return (routed + shared).reshape(b, l, d)


def setup_inputs(seed: int = 0) -> dict:
    key = jax.random.key(seed)
    ks = iter(jax.random.split(key, 64))

    def nrm(shape, scale=1.0):
        return jax.random.normal(next(ks), shape, F32) * scale

    def gain(shape):
        return 1.0 + nrm(shape, 0.02)

    d = D_MODEL
    return {
        'x_prompt': nrm((BATCH, SEQ, d)),
        'x_sample': nrm((DEC_BATCH, DEC_SEQ, d)),
        'cache_k': nrm((DEC_BATCH, N_EVEN, H_B, PAST_LEN, 2 * DK_B)),
        'cache_v': nrm((DEC_BATCH, N_EVEN, H_B, PAST_LEN, DV_B)),
        'c': nrm((DEC_BATCH, d)),
        'c_ctx': nrm((d,)),
        'w_mod': nrm((DEPTH, d, 6 * d), 0.5 * d ** -0.5),
        'b_mod': nrm((DEPTH, 6 * d), 0.02),
        'norm1': gain((DEPTH, d)),
        'norm2': gain((DEPTH, d)),
        'w_in_even': nrm((N_EVEN, d, W_IN_EVEN), d ** -0.5),
        'conv_a': nrm((N_EVEN, CONV_A, D_A), CONV_A ** -0.5),
        'q_norm': gain((N_EVEN, DK_B)),
        'k_norm': gain((N_EVEN, DK_B)),
        'lam_q1': nrm((N_EVEN, DK_B), 0.1),
        'lam_k1': nrm((N_EVEN, DK_B), 0.1),
        'lam_q2': nrm((N_EVEN, DK_B), 0.1),
        'lam_k2': nrm((N_EVEN, DK_B), 0.1),
        'subln': gain((N_EVEN, DV_B)),
        'w_out_even': nrm((N_EVEN, W_MIX, d), W_MIX ** -0.5),
        'w_in_odd': nrm((N_ODD, d, W_IN_ODD), d ** -0.5),
        'conv_c': nrm((N_ODD, CONV_C, D_C), CONV_C ** -0.5),
        'conv_c_b': nrm((N_ODD, D_C), 0.02),
        'ln_c_g': gain((N_ODD, D_C)),
        'ln_c_b': nrm((N_ODD, D_C), 0.02),
        'w_pool': nrm((N_ODD, N_POOL, D_DG, D_DG), D_DG ** -0.5),
        'pool_scale': 1.0 + nrm((N_ODD, D_D), 0.1),
        'w_out_odd': nrm((N_ODD, W_MIX, d), W_MIX ** -0.5),
        'w_router': nrm((DEPTH, d, N_EXPERTS), d ** -0.5),
        'b_router': nrm((DEPTH, N_EXPERTS), 0.01),
        'w_gate': nrm((DEPTH, N_EXPERTS, d, D_EXPERT), d ** -0.5),
        'w_up': nrm((DEPTH, N_EXPERTS, d, D_EXPERT), d ** -0.5),
        'w_down': nrm((DEPTH, N_EXPERTS, D_EXPERT, d), D_EXPERT ** -0.5),
        'ws_gate': nrm((DEPTH, d, D_SHARED), d ** -0.5),
        'ws_up': nrm((DEPTH, d, D_SHARED), d ** -0.5),
        'ws_down': nrm((DEPTH, D_SHARED, d), D_SHARED ** -0.5),
    }


def reference(x_prompt, x_sample, cache_k, cache_v, c, c_ctx, w_mod, b_mod, norm1, norm2,
              w_in_even, conv_a, q_norm, k_norm, lam_q1, lam_k1, lam_q2, lam_k2, subln,
              w_out_even, w_in_odd, conv_c, conv_c_b, ln_c_g, ln_c_b, w_pool, pool_scale,
              w_out_odd, w_router, b_router, w_gate, w_up, w_down, ws_gate, ws_up, ws_down):

    def run_layer(x, li, mod, rope, ctx_kv):
        shift1, scale1, gate1, shift2, scale2, gate2 = jnp.split(mod[:, None, :], 6, axis=-1)
        h = rms_norm(x, norm1[li]) * (1.0 + scale1) + shift1
        kv = None
        if li % 2 == 0:
            e = li // 2
            lam_init = 0.8 - 0.6 * math.exp(-0.3 * li)
            y, kv = even_mixer(h, w_in_even[e], conv_a[e], q_norm[e], k_norm[e],
                               lam_q1[e], lam_k1[e], lam_q2[e], lam_k2[e], subln[e],
                               w_out_even[e], lam_init, rope, ctx_kv)
        else:
            o = li // 2
            y = odd_mixer(h, w_in_odd[o], conv_c[o], conv_c_b[o], ln_c_g[o], ln_c_b[o],
                          w_pool[o], pool_scale[o], w_out_odd[o])
        x = x + gate1 * y
        h = rms_norm(x, norm2[li]) * (1.0 + scale2) + shift2
        x = x + gate2 * moe(h, w_router[li], b_router[li], w_gate[li], w_up[li], w_down[li],
                            ws_gate[li], ws_up[li], ws_down[li])
        return x, kv

    xp = x_prompt
    new_k, new_v = [], []
    for li in range(DEPTH):
        mod_ctx = (jax.nn.silu(c_ctx)[None, :] @ w_mod[li]) + b_mod[li]
        xp, kv = run_layer(xp, li, mod_ctx, None, None)
        if kv is not None:
            new_k.append(kv[0])
            new_v.append(kv[1])
    y_prompt = xp
    new_cache_k = jnp.stack(new_k, axis=1)
    new_cache_v = jnp.stack(new_v, axis=1)

    xs = x_sample
    rope = axial_rope_tables(xs.shape[1])
    for li in range(DEPTH):
        mod_lat = (jax.nn.silu(c) @ w_mod[li]) + b_mod[li]
        ctx_kv = (cache_k[:, li // 2], cache_v[:, li // 2]) if li % 2 == 0 else None
        xs, _ = run_layer(xs, li, mod_lat, rope, ctx_kv)
    y_sample = xs

    return (y_prompt, y_sample, new_cache_k, new_cache_v)
```

```python
import functools
import math

import numpy as np
import jax
import jax.numpy as jnp
from jax import lax
from jax.experimental import pallas as pl
from jax.experimental.pallas import tpu as pltpu

D_MODEL = 1024
BATCH = 16
SEQ = 256
DEPTH = 2
DEC_BATCH = 2
DEC_SEQ = 4096
PAST_LEN = 512
GRID_W = 64
H_B = 4
DK_B = 64
DV_B = 2 * DK_B
D_A = D_MODEL // 2
D_B = H_B * DV_B
D_C = D_MODEL // 2
D_D = D_MODEL // 2
CONV_C = 31
POOL_WINDOWS = (2, 4, 8, 16)
D_DG = D_D // len(POOL_WINDOWS)
N_EXPERTS = 64
TOP_K = 8
N_GROUPS = 8
TOPK_GROUPS = 4
D_EXPERT = 256
D_SHARED = 256
ROUTED_SCALE = 2.5
ROPE_BASE = 10000.0
EPS = 1e-6

F32 = jnp.float32
BF16 = jnp.bfloat16

T_P = BATCH * SEQ
T_S = DEC_BATCH * DEC_SEQ
T_ALL = T_P + T_S
TILE = 256
N_TILES = T_ALL // TILE
P_TILES = T_P // TILE
S_TILES = DEC_SEQ // TILE
LANES = 128
VMEM_LIMIT = 56 * 1024 * 1024


def _cparams(*sem):
    return pltpu.CompilerParams(dimension_semantics=sem, vmem_limit_bytes=VMEM_LIMIT)


def _mod_row(i, tm):
    npt = T_P // tm
    per = DEC_SEQ // tm
    return jnp.where(i < npt, 0, 1 + (i - npt) // per)


def _seq_flags(i):
    j = (i - P_TILES) % S_TILES
    first = jnp.logical_or(i < P_TILES, j == 0)
    last = jnp.logical_or(i < P_TILES, j == S_TILES - 1)
    return first, last


def _seq_tile(i):
    return jnp.where(i < P_TILES, 0, (i - P_TILES) % S_TILES)


def _split_bf16(a):
    hi = a.astype(BF16)
    lo = (a - hi.astype(F32)).astype(BF16)
    return hi, lo


def _dot(a, b):
    return jnp.dot(a, b, preferred_element_type=F32)


def _dot_nt(a, b):
    return lax.dot_general(a, b, (((1,), (1,)), ((), ())), preferred_element_type=F32)


def _dot3(a, b):
    a_hi, a_lo = _split_bf16(a)
    b_hi, b_lo = _split_bf16(b)
    return _dot(a_hi, b_hi) + _dot(a_lo, b_hi) + _dot(a_hi, b_lo)


def _silu(x):
    return x * jax.nn.sigmoid(x)


MOD_TN = 1536


def _mod_kernel(c_ref, w_ref, b_ref, o_ref):
    o_ref[0] = _dot3(_silu(c_ref[...]), w_ref[0]) + b_ref[0]


def _modulation(cond, w_mod, b_mod):
    n = 6 * D_MODEL
    return pl.pallas_call(
        _mod_kernel,
        grid=(DEPTH, n // MOD_TN),
        in_specs=[
            pl.BlockSpec((8, D_MODEL), lambda l, j: (0, 0)),
            pl.BlockSpec((1, D_MODEL, MOD_TN), lambda l, j: (l, 0, j)),
            pl.BlockSpec((1, 1, MOD_TN), lambda l, j: (l, 0, j)),
        ],
        out_specs=pl.BlockSpec((1, 8, MOD_TN), lambda l, j: (l, 0, j)),
        out_shape=jax.ShapeDtypeStruct((DEPTH, 8, n), F32),
        compiler_params=_cparams("arbitrary", "arbitrary"),
        name="modulation",
    )(cond, w_mod, b_mod.reshape(DEPTH, 1, n))


IN_TM = 512


def _modulated_norm(x, g, shift, scale):
    ms = jnp.mean(x * x, axis=-1, keepdims=True)
    return (x * lax.rsqrt(ms + EPS) * g) * (1.0 + scale) + shift


def _norm_in_kernel(x_ref, g_ref, sh_ref, sc_ref, w_ref, o_ref):
    r = _mod_row(pl.program_id(0), IN_TM)
    h = _modulated_norm(x_ref[...], g_ref[...], sh_ref[pl.ds(r, 1), :], sc_ref[pl.ds(r, 1), :])
    o_ref[...] = _dot(h.astype(BF16), w_ref[...])


def _norm_in(x, g, mod, w_bf16):
    n = w_bf16.shape[1]
    return pl.pallas_call(
        _norm_in_kernel,
        grid=(T_ALL // IN_TM,),
        in_specs=[
            pl.BlockSpec((IN_TM, D_MODEL), lambda i: (i, 0)),
            pl.BlockSpec((1, D_MODEL), lambda i: (0, 0)),
            pl.BlockSpec((8, D_MODEL), lambda i: (0, 0)),
            pl.BlockSpec((8, D_MODEL), lambda i: (0, 1)),
            pl.BlockSpec((D_MODEL, n), lambda i: (0, 0)),
        ],
        out_specs=pl.BlockSpec((IN_TM, n), lambda i: (i, 0)),
        out_shape=jax.ShapeDtypeStruct((T_ALL, n), F32),
        compiler_params=_cparams("arbitrary"),
        name="norm_in_proj",
    )(x, g.reshape(1, D_MODEL), mod, mod, w_bf16)


def _rope_tables():
    half = DK_B // 2
    freqs = ROPE_BASE ** (-np.arange(0, half, 2, dtype=np.float64) / half)
    l = np.arange(DEC_SEQ)
    pos_r = (l // GRID_W).astype(np.float64)
    pos_c = (l % GRID_W).astype(np.float64)
    lane = np.arange(LANES)
    jj = lane % DK_B
    m = jj % half
    f = m % (half // 2)
    pos = np.where((jj < half)[None, :], pos_r[:, None], pos_c[:, None])
    ang = pos * freqs[f][None, :]
    sign = np.where(m < half // 2, -1.0, 1.0)[None, :]
    cos = np.concatenate([np.ones((TILE, LANES)), np.cos(ang)], axis=0)
    sin = np.concatenate([np.zeros((TILE, LANES)), sign * np.sin(ang)], axis=0)
    return cos.astype(np.float32), sin.astype(np.float32)


def _segment_mean_matrix():
    lane = np.arange(LANES)
    same = (lane[:, None] // DK_B) == (lane[None, :] // DK_B)
    return (same.astype(np.float32) / DK_B)


def _qk_prep(x, g, cos, sin, seg):
    x2 = x * x
    hi, lo = _split_bf16(x2)
    ms = _dot(hi, seg) + _dot(lo, seg)
    y = x * lax.rsqrt(ms + EPS) * g
    lane = lax.broadcasted_iota(jnp.int32, y.shape, 1)
    lower = (lane % (DK_B // 2)) < (DK_B // 4)
    partner = jnp.where(lower, pltpu.roll(y, LANES - DK_B // 4, 1), pltpu.roll(y, DK_B // 4, 1))
    return y * cos + partner * sin


def _qkv_prep_kernel(q_ref, k_ref, v_ref, qn_ref, kn_ref, cos_ref, sin_ref, seg_ref,
                     qo_ref, ko_ref, vo_ref):
    cos = cos_ref[...]
    sin = sin_ref[...]
    seg = seg_ref[...]
    scale = 1.0 / math.sqrt(DK_B)
    qo_ref[0] = _qk_prep(q_ref[...], qn_ref[...], cos, sin, seg) * scale
    ko_ref[0] = _qk_prep(k_ref[...], kn_ref[...], cos, sin, seg)
    vo_ref[0] = v_ref[...]


def _qkv_prep(u, qn, kn):
    cos, sin = _rope_tables()
    seg = jnp.asarray(_segment_mean_matrix(), BF16)
    qn2 = jnp.concatenate([qn, qn]).reshape(1, LANES)
    kn2 = jnp.concatenate([kn, kn]).reshape(1, LANES)
    col0 = 3 * D_A // LANES

    def tab_map(i, h):
        return (jnp.where(i < P_TILES, 0, 1 + (i - P_TILES) % S_TILES), 0)

    out = jax.ShapeDtypeStruct((H_B, T_ALL, LANES), F32)
    return pl.pallas_call(
        _qkv_prep_kernel,
        grid=(N_TILES, H_B),
        in_specs=[
            pl.BlockSpec((TILE, LANES), lambda i, h: (i, col0 + h)),
            pl.BlockSpec((TILE, LANES), lambda i, h: (i, col0 + H_B + h)),
            pl.BlockSpec((TILE, LANES), lambda i, h: (i, col0 + 2 * H_B + h)),
            pl.BlockSpec((1, LANES), lambda i, h: (0, 0)),
            pl.BlockSpec((1, LANES), lambda i, h: (0, 0)),
            pl.BlockSpec((TILE, LANES), tab_map),
            pl.BlockSpec((TILE, LANES), tab_map),
            pl.BlockSpec((LANES, LANES), lambda i, h: (0, 0)),
        ],
        out_specs=[pl.BlockSpec((1, TILE, LANES), lambda i, h: (h, i, 0))] * 3,
        out_shape=[out, out, out],
        compiler_params=_cparams("arbitrary", "arbitrary"),
        name="qkv_prep",
    )(u, u, u, qn2, kn2, jnp.asarray(cos), jnp.asarray(sin), seg)


def _lambda(lq1_ref, lk1_ref, lq2_ref, lk2_ref, lam_init):
    a = jnp.sum(lq1_ref[...] * lk1_ref[...], axis=-1, keepdims=True)
    b = jnp.sum(lq2_ref[...] * lk2_ref[...], axis=-1, keepdims=True)
    return jnp.exp(a) - jnp.exp(b) + lam_init


def _attn_body(q, keys, vals, lam, subg, lam_init):
    lane = lax.broadcasted_iota(jnp.int32, q.shape, 1)
    qa = jnp.where(lane < DK_B, q, 0.0).astype(BF16)
    qb = jnp.where(lane < DK_B, 0.0, q).astype(BF16)
    out = None
    for qq, coef_sign in ((qa, None), (qb, lam)):
        ss = [_dot_nt(qq, k) for k in keys]
        m = functools.reduce(jnp.maximum, [jnp.max(s, axis=-1, keepdims=True) for s in ss])
        ps = [jnp.exp(s - m) for s in ss]
        l = functools.reduce(jnp.add, [jnp.sum(p, axis=-1, keepdims=True) for p in ps])
        w = 1.0 / l if coef_sign is None else -coef_sign / l
        ps = [p * w for p in ps]
        out = ps if out is None else [a + b for a, b in zip(out, ps)]
    o = functools.reduce(jnp.add, [_dot(a.astype(BF16), v) for a, v in zip(out, vals)])
    ms = jnp.mean(o * o, axis=-1, keepdims=True)
    return (o * lax.rsqrt(ms + EPS) * subg) * (1.0 - lam_init)


def _attn_prompt_kernel(q_ref, k_ref, v_ref, lq1, lk1, lq2, lk2, sg_ref, o_ref, *, lam_init):
    lam = _lambda(lq1, lk1, lq2, lk2, lam_init)
    o_ref[...] = _attn_body(q_ref[0], [k_ref[0].astype(BF16)], [v_ref[0].astype(BF16)],
                            lam, sg_ref[...], lam_init)


def _attn_latent_kernel(q_ref, k_ref, v_ref, ck_ref, cv_ref, lq1, lk1, lq2, lk2, sg_ref, o_ref,
                        *, lam_init):
    lam = _lambda(lq1, lk1, lq2, lk2, lam_init)
    keys = [ck_ref[0, 0, 0].astype(BF16), k_ref[0].astype(BF16)]
    vals = [cv_ref[0, 0, 0].astype(BF16), v_ref[0].astype(BF16)]
    o_ref[...] = _attn_body(q_ref[0], keys, vals, lam, sg_ref[...], lam_init)


def _attention(q, k, v, cache_k, cache_v, e, lam_params, subg, lam_init):
    small = [p.reshape(1, DK_B) for p in lam_params] + [subg.reshape(1, DV_B)]
    small_specs2 = [pl.BlockSpec((1, DK_B), lambda b, h: (0, 0))] * 4 + \
                   [pl.BlockSpec((1, DV_B), lambda b, h: (0, 0))]
    small_specs3 = [pl.BlockSpec((1, DK_B), lambda s, h, j: (0, 0))] * 4 + \
                   [pl.BlockSpec((1, DV_B), lambda s, h, j: (0, 0))]
    o_prompt = pl.pallas_call(
        functools.partial(_attn_prompt_kernel, lam_init=lam_init),
        grid=(BATCH, H_B),
        in_specs=[pl.BlockSpec((1, SEQ, LANES), lambda b, h: (h, b, 0))] * 3 + small_specs2,
        out_specs=pl.BlockSpec((SEQ, LANES), lambda b, h: (b, h)),
        out_shape=jax.ShapeDtypeStruct((T_P, D_B), F32),
        compiler_params=_cparams("arbitrary", "arbitrary"),
        name="attn_prompt",
    )(q, k, v, *small)

    kv_spec = pl.BlockSpec((1, DEC_SEQ, LANES), lambda s, h, j: (h, T_P // DEC_SEQ + s, 0))
    c_spec = pl.BlockSpec((1, 1, 1, PAST_LEN, LANES), lambda s, h, j: (s, e, h, 0, 0))
    o_latent = pl.pallas_call(
        functools.partial(_attn_latent_kernel, lam_init=lam_init),
        grid=(DEC_BATCH, H_B, S_TILES),
        in_specs=[pl.BlockSpec((1, TILE, LANES), lambda s, h, j: (h, P_TILES + s * S_TILES + j, 0)),
                  kv_spec, kv_spec, c_spec, c_spec] + small_specs3,
        out_specs=pl.BlockSpec((TILE, LANES), lambda s, h, j: (s * S_TILES + j, h)),
        out_shape=jax.ShapeDtypeStruct((T_S, D_B), F32),
        compiler_params=_cparams("arbitrary", "arbitrary", "arbitrary"),
        name="attn_latent",
    )(q, k, v, cache_k, cache_v, *small)
    return jnp.concatenate([o_prompt, o_latent], axis=0)


HALO8 = 8
HALO16 = 16


def _prev_block(i, rows):
    return jnp.maximum(i * (TILE // rows) - 1, 0)


def _next_block(i, rows):
    return jnp.minimum((i + 1) * (TILE // rows), T_ALL // rows - 1)


def _even_out_kernel(x_ref, bg_ref, cg_ref, hin_ref, cgp_ref, hinp_ref, cgn_ref, hinn_ref,
                     o_ref, cw_ref, g1_ref, w_ref, out_ref):
    i = pl.program_id(0)
    first, last = _seq_flags(i)
    r = _mod_row(i, TILE)
    z = cg_ref[...] * hin_ref[...]
    zp = jnp.where(first, 0.0, cgp_ref[HALO8 - 1:HALO8, :] * hinp_ref[HALO8 - 1:HALO8, :])
    zn = jnp.where(last, 0.0, cgn_ref[0:1, :] * hinn_ref[0:1, :])
    row = lax.broadcasted_iota(jnp.int32, z.shape, 0)
    z_prev = jnp.where(row == 0, zp, pltpu.roll(z, 1, 0))
    z_next = jnp.where(row == TILE - 1, zn, pltpu.roll(z, TILE - 1, 0))
    cw = cw_ref[...]
    ya = bg_ref[...] * (cw[0:1, :] * z_prev + cw[1:2, :] * z + cw[2:3, :] * z_next)
    y = _dot(ya.astype(BF16), w_ref[0:D_A, :]) + _dot(o_ref[...].astype(BF16), w_ref[D_A:, :])
    out_ref[...] = x_ref[...] + g1_ref[pl.ds(r, 1), :] * y


def _even_out(x, u, o, conv_w, mod, w_out_bf16):
    tile_spec = lambda c: pl.BlockSpec((TILE, D_A), lambda i: (i, c))
    prev_spec = lambda c: pl.BlockSpec((HALO8, D_A), lambda i: (_prev_block(i, HALO8), c))
    next_spec = lambda c: pl.BlockSpec((HALO8, D_A), lambda i: (_next_block(i, HALO8), c))
    return pl.pallas_call(
        _even_out_kernel,
        grid=(N_TILES,),
        in_specs=[
            pl.BlockSpec((TILE, D_MODEL), lambda i: (i, 0)),
            tile_spec(0), tile_spec(1), tile_spec(2),
            prev_spec(1), prev_spec(2), next_spec(1), next_spec(2),
            pl.BlockSpec((TILE, D_B), lambda i: (i, 0)),
            pl.BlockSpec((3, D_A), lambda i: (0, 0)),
            pl.BlockSpec((8, D_MODEL), lambda i: (0, 2)),
            pl.BlockSpec((D_MODEL, D_MODEL), lambda i: (0, 0)),
        ],
        out_specs=pl.BlockSpec((TILE, D_MODEL), lambda i: (i, 0)),
        out_shape=jax.ShapeDtypeStruct((T_ALL, D_MODEL), F32),
        compiler_params=_cparams("arbitrary"),
        name="even_mixer_out",
    )(x, u, u, u, u, u, u, u, o, conv_w, mod, w_out_bf16)


def _odd_out_kernel(x_ref, a_ref, b_ref, pd_ref, ap_ref, bp_ref, an_ref, bn_ref, pp_ref, pn_ref,
                    cw_ref, cb_ref, lg_ref, lb_ref, wp_ref, ps_ref, g1_ref, w_ref, out_ref,
                    ext_ref, extp_ref):
    i = pl.program_id(0)
    first, last = _seq_flags(i)
    r = _mod_row(i, TILE)
    ext_ref[0:HALO16, :] = jnp.where(first, 0.0, ap_ref[...] * jax.nn.sigmoid(bp_ref[...]))
    ext_ref[HALO16:HALO16 + TILE, :] = a_ref[...] * jax.nn.sigmoid(b_ref[...])
    ext_ref[HALO16 + TILE:, :] = jnp.where(last, 0.0, an_ref[...] * jax.nn.sigmoid(bn_ref[...]))
    acc = jnp.zeros((TILE, D_C), F32)
    for j in range(CONV_C):
        acc = acc + cw_ref[j:j + 1, :] * ext_ref[pl.ds(HALO16 - CONV_C // 2 + j, TILE), :]
    g = acc + cb_ref[...]
    mu = jnp.mean(g, axis=-1, keepdims=True)
    var = jnp.mean(jnp.square(g - mu), axis=-1, keepdims=True)
    g = _silu(((g - mu) * lax.rsqrt(var + EPS)) * lg_ref[...] + lb_ref[...])
    extp_ref[0:HALO8, :] = jnp.where(first, 0.0, pp_ref[...])
    extp_ref[HALO8:HALO8 + TILE, :] = pd_ref[...]
    extp_ref[HALO8 + TILE:, :] = jnp.where(last, 0.0, pn_ref[...])
    seq_len = jnp.where(i < P_TILES, SEQ, DEC_SEQ)
    pos = _seq_tile(i) * TILE + lax.broadcasted_iota(jnp.int32, (TILE, 1), 0)
    yd = []
    for gi, w in enumerate(POOL_WINDOWS):
        cols = slice(gi * D_DG, (gi + 1) * D_DG)
        s = jnp.zeros((TILE, D_DG), F32)
        for d in range(-(w // 2), w - w // 2):
            s = s + extp_ref[pl.ds(HALO8 + d, TILE), cols]
        lo = jnp.maximum(pos - w // 2, 0)
        hi = jnp.minimum(pos - w // 2 + w, seq_len)
        pooled = s / (hi - lo).astype(F32) - pd_ref[:, cols]
        yd.append(_dot(pooled.astype(BF16), wp_ref[gi]))
    yd = jnp.concatenate(yd, axis=-1) * ps_ref[...]
    y = _dot(g.astype(BF16), w_ref[0:D_C, :]) + _dot(yd.astype(BF16), w_ref[D_C:, :])
    out_ref[...] = x_ref[...] + g1_ref[pl.ds(r, 1), :] * y


def _odd_out(x, u, conv_w, conv_b, ln_g, ln_b, w_pool_bf16, p_scale, mod, w_out_bf16):
    tile_spec = lambda c: pl.BlockSpec((TILE, D_C), lambda i: (i, c))
    prev_spec = lambda rows, c: pl.BlockSpec((rows, D_C), lambda i: (_prev_block(i, rows), c))
    next_spec = lambda rows, c: pl.BlockSpec((rows, D_C), lambda i: (_next_block(i, rows), c))
    vec = lambda: pl.BlockSpec((1, D_C), lambda i: (0, 0))
    return pl.pallas_call(
        _odd_out_kernel,
        grid=(N_TILES,),
        in_specs=[
            pl.BlockSpec((TILE, D_MODEL), lambda i: (i, 0)),
            tile_spec(0), tile_spec(1), tile_spec(2),
            prev_spec(HALO16, 0), prev_spec(HALO16, 1), next_spec(HALO16, 0), next_spec(HALO16, 1),
            prev_spec(HALO8, 2), next_spec(HALO8, 2),
            pl.BlockSpec((CONV_C, D_C), lambda i: (0, 0)),
            vec(), vec(), vec(),
            pl.BlockSpec((len(POOL_WINDOWS), D_DG, D_DG), lambda i: (0, 0, 0)),
            vec(),
            pl.BlockSpec((8, D_MODEL), lambda i: (0, 2)),
            pl.BlockSpec((D_MODEL, D_MODEL), lambda i: (0, 0)),
        ],
        out_specs=pl.BlockSpec((TILE, D_MODEL), lambda i: (i, 0)),
        out_shape=jax.ShapeDtypeStruct((T_ALL, D_MODEL), F32),
        scratch_shapes=[pltpu.VMEM((TILE + 2 * HALO16, D_C), F32),
                        pltpu.VMEM((TILE + 2 * HALO8, D_D), F32)],
        compiler_params=_cparams("arbitrary"),
        name="odd_mixer_out",
    )(x, u, u, u, u, u, u, u, u, u, conv_w, conv_b.reshape(1, D_C), ln_g.reshape(1, D_C),
      ln_b.reshape(1, D_C), w_pool_bf16, p_scale.reshape(1, D_D), mod, w_out_bf16)


GROUP = N_EXPERTS // N_GROUPS
NEG_INF = float("-inf")


def _first_argmax(v, idx, axis):
    m = jnp.max(v, axis=axis, keepdims=True)
    big = jnp.int32(2 ** 30)
    am = jnp.min(jnp.where(v == m, idx, big), axis=axis, keepdims=True)
    return m, am


def _route(scores, biased):
    shape = biased.shape
    member = lax.broadcasted_iota(jnp.int32, shape, 1)
    m1, a1 = _first_argmax(biased, member, 1)
    m2 = jnp.max(jnp.where(member == a1, NEG_INF, biased), axis=1, keepdims=True)
    gscore = m1 + m2
    gidx = lax.broadcasted_iota(jnp.int32, gscore.shape, 0)
    gsel = jnp.zeros(gscore.shape, jnp.bool_)
    for _ in range(TOPK_GROUPS):
        _, am = _first_argmax(gscore, gidx, 0)
        hit = gidx == am
        gsel = jnp.logical_or(gsel, hit)
        gscore = jnp.where(hit, NEG_INF, gscore)
    cand = jnp.where(gsel, biased, NEG_INF)
    eidx = lax.broadcasted_iota(jnp.int32, shape, 0) * GROUP + member
    sel = jnp.zeros(shape, jnp.bool_)
    for _ in range(TOP_K):
        m = jnp.max(jnp.max(cand, axis=1, keepdims=True), axis=0, keepdims=True)
        big = jnp.int32(2 ** 30)
        am = jnp.where(cand == m, eidx, big)
        am = jnp.min(jnp.min(am, axis=1, keepdims=True), axis=0, keepdims=True)
        hit = eidx == am
        sel = jnp.logical_or(sel, hit)
        cand = jnp.where(hit, NEG_INF, cand)
    wsel = jnp.where(sel, scores, 0.0)
    tot = jnp.sum(jnp.sum(wsel, axis=1, keepdims=True), axis=0, keepdims=True)
    return wsel / tot * ROUTED_SCALE


def _moe_pre_kernel(x_ref, g_ref, sh_ref, sc_ref, wr_ref, br_ref, wsg_ref, wsu_ref, wsd_ref,
                    h_ref, gate_ref, shared_ref):
    r = _mod_row(pl.program_id(0), TILE)
    h = _modulated_norm(x_ref[...], g_ref[...], sh_ref[pl.ds(r, 1), :], sc_ref[pl.ds(r, 1), :])
    hb = h.astype(BF16)
    h_ref[...] = hb
    h_hi, h_lo = hb, (h - hb.astype(F32)).astype(BF16)
    w_hi, w_lo = _split_bf16(wr_ref[...])
    logits = _dot_nt(w_hi, h_hi) + _dot_nt(w_lo, h_hi) + _dot_nt(w_hi, h_lo)
    scores = jax.nn.sigmoid(logits)
    biased = scores + br_ref[:, 0:1]
    shape3 = (N_GROUPS, GROUP, TILE)
    gate_t = _route(scores.reshape(shape3), biased.reshape(shape3)).reshape(N_EXPERTS, TILE)
    gate_pad = jnp.concatenate([gate_t, jnp.zeros((LANES - N_EXPERTS, TILE), F32)], axis=0)
    gate_ref[...] = gate_pad.T[:, :N_EXPERTS]
    a = _silu(_dot(hb, wsg_ref[...])) * _dot(hb, wsu_ref[...])
    shared_ref[...] = _dot(a.astype(BF16), wsd_ref[...])


def _moe_pre(x, g, mod, w_router_t, b_router, wsg, wsu, wsd):
    return pl.pallas_call(
        _moe_pre_kernel,
        grid=(N_TILES,),
        in_specs=[
            pl.BlockSpec((TILE, D_MODEL), lambda i: (i, 0)),
            pl.BlockSpec((1, D_MODEL), lambda i: (0, 0)),
            pl.BlockSpec((8, D_MODEL), lambda i: (0, 3)),
            pl.BlockSpec((8, D_MODEL), lambda i: (0, 4)),
            pl.BlockSpec((N_EXPERTS, D_MODEL), lambda i: (0, 0)),
            pl.BlockSpec((N_EXPERTS, LANES), lambda i: (0, 0)),
            pl.BlockSpec((D_MODEL, D_SHARED), lambda i: (0, 0)),
            pl.BlockSpec((D_MODEL, D_SHARED), lambda i: (0, 0)),
            pl.BlockSpec((D_SHARED, D_MODEL), lambda i: (0, 0)),
        ],
        out_specs=[
            pl.BlockSpec((TILE, D_MODEL), lambda i: (i, 0)),
            pl.BlockSpec((TILE, N_EXPERTS), lambda i: (i, 0)),
            pl.BlockSpec((TILE, D_MODEL), lambda i: (i, 0)),
        ],
        out_shape=[
            jax.ShapeDtypeStruct((T_ALL, D_MODEL), BF16),
            jax.ShapeDtypeStruct((T_ALL, N_EXPERTS), F32),
            jax.ShapeDtypeStruct((T_ALL, D_MODEL), F32),
        ],
        compiler_params=_cparams("arbitrary"),
        name="moe_pre",
    )(x, g.reshape(1, D_MODEL), mod, mod, w_router_t,
      jnp.broadcast_to(b_router.reshape(N_EXPERTS, 1), (N_EXPERTS, LANES)), wsg, wsu, wsd)


MOE_TM = 1024
MOE_EB = 2


def _moe_experts_kernel(h_ref, gate_ref, wg_ref, wu_ref, wd_ref, x_ref, sh_ref, g2_ref, out_ref):
    e = pl.program_id(1)

    @pl.when(e == 0)
    def _():
        out_ref[...] = jnp.zeros_like(out_ref)

    h = h_ref[...]
    gate = gate_ref[...]
    g_hi = gate.astype(BF16)
    g_mid = (gate - g_hi.astype(F32))
    g_lo = (g_mid - g_mid.astype(BF16).astype(F32)).astype(BF16)
    g_mid = g_mid.astype(BF16)
    acc = out_ref[...]
    for j in range(MOE_EB):
        eid = e * MOE_EB + j
        pick = (lax.broadcasted_iota(jnp.int32, (N_EXPERTS, D_EXPERT), 0) == eid)
        pick = jnp.where(pick, 1.0, 0.0).astype(BF16)
        gcol = _dot(g_hi, pick) + _dot(g_mid, pick) + _dot(g_lo, pick)
        hg = _dot(h, wg_ref[0, j].astype(BF16))
        hu = _dot(h, wu_ref[0, j].astype(BF16))
        a = _silu(hg) * hu * gcol
        acc = acc + _dot(a.astype(BF16), wd_ref[0, j].astype(BF16))
    out_ref[...] = acc

    @pl.when(e == pl.num_programs(1) - 1)
    def _():
        r = _mod_row(pl.program_id(0), MOE_TM)
        out_ref[...] = x_ref[...] + g2_ref[pl.ds(r, 1), :] * (out_ref[...] + sh_ref[...])


def _moe_experts(h, gate, w_gate, w_up, w_down, li, x, shared, mod):
    tok = lambda w: pl.BlockSpec((MOE_TM, w), lambda i, e: (i, 0))
    return pl.pallas_call(
        _moe_experts_kernel,
        grid=(T_ALL // MOE_TM, N_EXPERTS // MOE_EB),
        in_specs=[
            tok(D_MODEL), tok(N_EXPERTS),
            pl.BlockSpec((1, MOE_EB, D_MODEL, D_EXPERT), lambda i, e: (li, e, 0, 0)),
            pl.BlockSpec((1, MOE_EB, D_MODEL, D_EXPERT), lambda i, e: (li, e, 0, 0)),
            pl.BlockSpec((1, MOE_EB, D_EXPERT, D_MODEL), lambda i, e: (li, e, 0, 0)),
            tok(D_MODEL), tok(D_MODEL),
            pl.BlockSpec((8, D_MODEL), lambda i, e: (0, 5)),
        ],
        out_specs=tok(D_MODEL),
        out_shape=jax.ShapeDtypeStruct((T_ALL, D_MODEL), F32),
        compiler_params=_cparams("arbitrary", "arbitrary"),
        name="moe_experts",
    )(h, gate, w_gate, w_up, w_down, x, shared, mod)


def kernel(x_prompt, x_sample, cache_k, cache_v, c, c_ctx, w_mod, b_mod, norm1, norm2, w_in_even, conv_a, q_norm, k_norm, lam_q1, lam_k1, lam_q2, lam_k2, subln, w_out_even, w_in_odd, conv_c, conv_c_b, ln_c_g, ln_c_b, w_pool, pool_scale, w_out_odd, w_router, b_router, w_gate, w_up, w_down, ws_gate, ws_up, ws_down):
    x = jnp.concatenate([x_prompt.reshape(T_P, D_MODEL), x_sample.reshape(T_S, D_MODEL)], axis=0)
    cond = jnp.concatenate([c_ctx[None, :], c, jnp.zeros((8 - 1 - DEC_BATCH, D_MODEL), F32)], axis=0)
    mod_all = _modulation(cond, w_mod, b_mod)

    new_k = new_v = None
    for li in range(DEPTH):
        mod = mod_all[li]
        if li % 2 == 0:
            e = li // 2
            lam_init = 0.8 - 0.6 * math.exp(-0.3 * li)
            u = _norm_in(x, norm1[li], mod, w_in_even[e].astype(BF16))
            q, k, v = _qkv_prep(u, q_norm[e], k_norm[e])
            if new_k is None:
                new_k, new_v = k, v
            o = _attention(q, k, v, cache_k, cache_v, e,
                           (lam_q1[e], lam_k1[e], lam_q2[e], lam_k2[e]), subln[e], lam_init)
            x = _even_out(x, u, o, conv_a[e], mod, w_out_even[e].astype(BF16))
        else:
            o_ = li // 2
            u = _norm_in(x, norm1[li], mod, w_in_odd[o_].astype(BF16))
            x = _odd_out(x, u, conv_c[o_], conv_c_b[o_], ln_c_g[o_], ln_c_b[o_],
                         w_pool[o_].astype(BF16), pool_scale[o_], mod, w_out_odd[o_].astype(BF16))
        h, gate, shared = _moe_pre(x, norm2[li], mod, w_router[li].T, b_router[li],
                                   ws_gate[li].astype(BF16), ws_up[li].astype(BF16),
                                   ws_down[li].astype(BF16))
        x = _moe_experts(h, gate, w_gate, w_up, w_down, li, x, shared, mod)

    y_prompt = x[:T_P].reshape(BATCH, SEQ, D_MODEL)
    y_sample = x[T_P:].reshape(DEC_BATCH, DEC_SEQ, D_MODEL)

    def to_cache(a):
        a = a[:, :T_P].reshape(H_B, BATCH, SEQ, LANES)
        return jnp.transpose(a, (1, 0, 2, 3))[:, None]

    return (y_prompt, y_sample, to_cache(new_k), to_cache(new_v))
```

```python
import functools
import math

import numpy as np
import jax
import jax.numpy as jnp
from jax import lax
from jax.experimental import pallas as pl
from jax.experimental.pallas import tpu as pltpu

D_MODEL = 1024
BATCH = 16
SEQ = 256
DEPTH = 2
DEC_BATCH = 2
DEC_SEQ = 4096
PAST_LEN = 512
GRID_W = 64
H_B = 4
DK_B = 64
DV_B = 2 * DK_B
D_A = D_MODEL // 2
D_B = H_B * DV_B
D_C = D_MODEL // 2
D_D = D_MODEL // 2
CONV_C = 31
POOL_WINDOWS = (2, 4, 8, 16)
D_DG = D_D // len(POOL_WINDOWS)
N_EXPERTS = 64
TOP_K = 8
N_GROUPS = 8
TOPK_GROUPS = 4
D_EXPERT = 256
D_SHARED = 256
ROUTED_SCALE = 2.5
ROPE_BASE = 10000.0
EPS = 1e-6

F32 = jnp.float32
BF16 = jnp.bfloat16

T_P = BATCH * SEQ
T_S = DEC_BATCH * DEC_SEQ
T_ALL = T_P + T_S
TILE = 256
N_TILES = T_ALL // TILE
P_TILES = T_P // TILE
S_TILES = DEC_SEQ // TILE
LANES = 128
VMEM_LIMIT = 56 * 1024 * 1024


def _cparams(*sem):
    return pltpu.CompilerParams(dimension_semantics=sem, vmem_limit_bytes=VMEM_LIMIT)


def _mod_row(i, tm):
    npt = T_P // tm
    per = DEC_SEQ // tm
    return jnp.where(i < npt, 0, 1 + (i - npt) // per)


def _seq_flags(i):
    j = (i - P_TILES) % S_TILES
    first = jnp.logical_or(i < P_TILES, j == 0)
    last = jnp.logical_or(i < P_TILES, j == S_TILES - 1)
    return first, last


def _seq_tile(i):
    return jnp.where(i < P_TILES, 0, (i - P_TILES) % S_TILES)


def _split_bf16(a):
    hi = a.astype(BF16)
    lo = (a - hi.astype(F32)).astype(BF16)
    return hi, lo


def _dot(a, b):
    return jnp.dot(a, b, preferred_element_type=F32)


def _dot_nt(a, b):
    return lax.dot_general(a, b, (((1,), (1,)), ((), ())), preferred_element_type=F32)


def _dot3(a, b):
    a_hi, a_lo = _split_bf16(a)
    b_hi, b_lo = _split_bf16(b)
    return _dot(a_hi, b_hi) + _dot(a_lo, b_hi) + _dot(a_hi, b_lo)


def _silu(x):
    return x * jax.nn.sigmoid(x)


MOD_TN = 1536


def _mod_kernel(c_ref, w_ref, b_ref, o_ref):
    o_ref[0] = _dot3(_silu(c_ref[...]), w_ref[0]) + b_ref[0]


def _modulation(cond, w_mod, b_mod):
    n = 6 * D_MODEL
    return pl.pallas_call(
        _mod_kernel,
        grid=(DEPTH, n // MOD_TN),
        in_specs=[
            pl.BlockSpec((8, D_MODEL), lambda l, j: (0, 0)),
            pl.BlockSpec((1, D_MODEL, MOD_TN), lambda l, j: (l, 0, j)),
            pl.BlockSpec((1, 1, MOD_TN), lambda l, j: (l, 0, j)),
        ],
        out_specs=pl.BlockSpec((1, 8, MOD_TN), lambda l, j: (l, 0, j)),
        out_shape=jax.ShapeDtypeStruct((DEPTH, 8, n), F32),
        compiler_params=_cparams("arbitrary", "arbitrary"),
        name="modulation",
    )(cond, w_mod, b_mod.reshape(DEPTH, 1, n))


IN_TM = 512


def _modulated_norm(x, g, shift, scale):
    ms = jnp.mean(x * x, axis=-1, keepdims=True)
    return (x * lax.rsqrt(ms + EPS) * g) * (1.0 + scale) + shift


def _norm_in_kernel(x_ref, g_ref, sh_ref, sc_ref, w_ref, o_ref):
    r = _mod_row(pl.program_id(0), IN_TM)
    h = _modulated_norm(x_ref[...], g_ref[...], sh_ref[pl.ds(r, 1), :], sc_ref[pl.ds(r, 1), :])
    o_ref[...] = _dot(h.astype(BF16), w_ref[...])


def _norm_in(x, g, mod, w_bf16):
    n = w_bf16.shape[1]
    return pl.pallas_call(
        _norm_in_kernel,
        grid=(T_ALL // IN_TM,),
        in_specs=[
            pl.BlockSpec((IN_TM, D_MODEL), lambda i: (i, 0)),
            pl.BlockSpec((1, D_MODEL), lambda i: (0, 0)),
            pl.BlockSpec((8, D_MODEL), lambda i: (0, 0)),
            pl.BlockSpec((8, D_MODEL), lambda i: (0, 1)),
            pl.BlockSpec((D_MODEL, n), lambda i: (0, 0)),
        ],
        out_specs=pl.BlockSpec((IN_TM, n), lambda i: (i, 0)),
        out_shape=jax.ShapeDtypeStruct((T_ALL, n), F32),
        compiler_params=_cparams("arbitrary"),
        name="norm_in_proj",
    )(x, g.reshape(1, D_MODEL), mod, mod, w_bf16)


def _rope_tables():
    half = DK_B // 2
    freqs = ROPE_BASE ** (-np.arange(0, half, 2, dtype=np.float64) / half)
    l = np.arange(DEC_SEQ)
    pos_r = (l // GRID_W).astype(np.float64)
    pos_c = (l % GRID_W).astype(np.float64)
    lane = np.arange(LANES)
    jj = lane % DK_B
    m = jj % half
    f = m % (half // 2)
    pos = np.where((jj < half)[None, :], pos_r[:, None], pos_c[:, None])
    ang = pos * freqs[f][None, :]
    sign = np.where(m < half // 2, -1.0, 1.0)[None, :]
    cos = np.concatenate([np.ones((TILE, LANES)), np.cos(ang)], axis=0)
    sin = np.concatenate([np.zeros((TILE, LANES)), sign * np.sin(ang)], axis=0)
    return cos.astype(np.float32), sin.astype(np.float32)


def _segment_mean_matrix():
    lane = np.arange(LANES)
    same = (lane[:, None] // DK_B) == (lane[None, :] // DK_B)
    return (same.astype(np.float32) / DK_B)


def _qk_prep(x, g, cos, sin, seg):
    x2 = x * x
    hi, lo = _split_bf16(x2)
    ms = _dot(hi, seg) + _dot(lo, seg)
    y = x * lax.rsqrt(ms + EPS) * g
    lane = lax.broadcasted_iota(jnp.int32, y.shape, 1)
    lower = (lane % (DK_B // 2)) < (DK_B // 4)
    partner = jnp.where(lower, pltpu.roll(y, LANES - DK_B // 4, 1), pltpu.roll(y, DK_B // 4, 1))
    return y * cos + partner * sin


def _qkv_prep_kernel(q_ref, k_ref, v_ref, qn_ref, kn_ref, cos_ref, sin_ref, seg_ref,
                     qo_ref, ko_ref, vo_ref):
    cos = cos_ref[...]
    sin = sin_ref[...]
    seg = seg_ref[...]
    scale = 1.0 / math.sqrt(DK_B)
    qo_ref[0] = _qk_prep(q_ref[...], qn_ref[...], cos, sin, seg) * scale
    ko_ref[0] = _qk_prep(k_ref[...], kn_ref[...], cos, sin, seg)
    vo_ref[0] = v_ref[...]


def _qkv_prep(u, qn, kn):
    cos, sin = _rope_tables()
    seg = jnp.asarray(_segment_mean_matrix(), BF16)
    qn2 = jnp.concatenate([qn, qn]).reshape(1, LANES)
    kn2 = jnp.concatenate([kn, kn]).reshape(1, LANES)
    col0 = 3 * D_A // LANES

    def tab_map(i, h):
        return (jnp.where(i < P_TILES, 0, 1 + (i - P_TILES) % S_TILES), 0)

    out = jax.ShapeDtypeStruct((H_B, T_ALL, LANES), F32)
    return pl.pallas_call(
        _qkv_prep_kernel,
        grid=(N_TILES, H_B),
        in_specs=[
            pl.BlockSpec((TILE, LANES), lambda i, h: (i, col0 + h)),
            pl.BlockSpec((TILE, LANES), lambda i, h: (i, col0 + H_B + h)),
            pl.BlockSpec((TILE, LANES), lambda i, h: (i, col0 + 2 * H_B + h)),
            pl.BlockSpec((1, LANES), lambda i, h: (0, 0)),
            pl.BlockSpec((1, LANES), lambda i, h: (0, 0)),
            pl.BlockSpec((TILE, LANES), tab_map),
            pl.BlockSpec((TILE, LANES), tab_map),
            pl.BlockSpec((LANES, LANES), lambda i, h: (0, 0)),
        ],
        out_specs=[pl.BlockSpec((1, TILE, LANES), lambda i, h: (h, i, 0))] * 3,
        out_shape=[out, out, out],
        compiler_params=_cparams("arbitrary", "arbitrary"),
        name="qkv_prep",
    )(u, u, u, qn2, kn2, jnp.asarray(cos), jnp.asarray(sin), seg)


def _lambda(lq1_ref, lk1_ref, lq2_ref, lk2_ref, lam_init):
    a = jnp.sum(lq1_ref[...] * lk1_ref[...], axis=-1, keepdims=True)
    b = jnp.sum(lq2_ref[...] * lk2_ref[...], axis=-1, keepdims=True)
    return jnp.exp(a) - jnp.exp(b) + lam_init


def _attn_body(q, keys, vals, lam, subg, lam_init):
    lane = lax.broadcasted_iota(jnp.int32, q.shape, 1)
    qa = jnp.where(lane < DK_B, q, 0.0).astype(BF16)
    qb = jnp.where(lane < DK_B, 0.0, q).astype(BF16)
    out = None
    for qq, coef_sign in ((qa, None), (qb, lam)):
        ss = [_dot_nt(qq, k) for k in keys]
        m = functools.reduce(jnp.maximum, [jnp.max(s, axis=-1, keepdims=True) for s in ss])
        ps = [jnp.exp(s - m) for s in ss]
        l = functools.reduce(jnp.add, [jnp.sum(p, axis=-1, keepdims=True) for p in ps])
        w = 1.0 / l if coef_sign is None else -coef_sign / l
        ps = [p * w for p in ps]
        out = ps if out is None else [a + b for a, b in zip(out, ps)]
    o = functools.reduce(jnp.add, [_dot(a.astype(BF16), v) for a, v in zip(out, vals)])
    ms = jnp.mean(o * o, axis=-1, keepdims=True)
    return (o * lax.rsqrt(ms + EPS) * subg) * (1.0 - lam_init)


def _attn_prompt_kernel(q_ref, k_ref, v_ref, lq1, lk1, lq2, lk2, sg_ref, o_ref, *, lam_init):
    lam = _lambda(lq1, lk1, lq2, lk2, lam_init)
    o_ref[...] = _attn_body(q_ref[0], [k_ref[0].astype(BF16)], [v_ref[0].astype(BF16)],
                            lam, sg_ref[...], lam_init)


def _attn_latent_kernel(q_ref, k_ref, v_ref, ck_ref, cv_ref, lq1, lk1, lq2, lk2, sg_ref, o_ref,
                        *, lam_init):
    lam = _lambda(lq1, lk1, lq2, lk2, lam_init)
    keys = [ck_ref[0, 0, 0].astype(BF16), k_ref[0].astype(BF16)]
    vals = [cv_ref[0, 0, 0].astype(BF16), v_ref[0].astype(BF16)]
    o_ref[...] = _attn_body(q_ref[0], keys, vals, lam, sg_ref[...], lam_init)


def _attention(q, k, v, cache_k, cache_v, e, lam_params, subg, lam_init):
    small = [p.reshape(1, DK_B) for p in lam_params] + [subg.reshape(1, DV_B)]
    small_specs2 = [pl.BlockSpec((1, DK_B), lambda b, h: (0, 0))] * 4 + \
                   [pl.BlockSpec((1, DV_B), lambda b, h: (0, 0))]
    small_specs3 = [pl.BlockSpec((1, DK_B), lambda s, h, j: (0, 0))] * 4 + \
                   [pl.BlockSpec((1, DV_B), lambda s, h, j: (0, 0))]
    o_prompt = pl.pallas_call(
        functools.partial(_attn_prompt_kernel, lam_init=lam_init),
        grid=(BATCH, H_B),
        in_specs=[pl.BlockSpec((1, SEQ, LANES), lambda b, h: (h, b, 0))] * 3 + small_specs2,
        out_specs=pl.BlockSpec((SEQ, LANES), lambda b, h: (b, h)),
        out_shape=jax.ShapeDtypeStruct((T_P, D_B), F32),
        compiler_params=_cparams("arbitrary", "arbitrary"),
        name="attn_prompt",
    )(q, k, v, *small)

    kv_spec = pl.BlockSpec((1, DEC_SEQ, LANES), lambda s, h, j: (h, T_P // DEC_SEQ + s, 0))
    c_spec = pl.BlockSpec((1, 1, 1, PAST_LEN, LANES), lambda s, h, j: (s, e, h, 0, 0))
    o_latent = pl.pallas_call(
        functools.partial(_attn_latent_kernel, lam_init=lam_init),
        grid=(DEC_BATCH, H_B, S_TILES),
        in_specs=[pl.BlockSpec((1, TILE, LANES), lambda s, h, j: (h, P_TILES + s * S_TILES + j, 0)),
                  kv_spec, kv_spec, c_spec, c_spec] + small_specs3,
        out_specs=pl.BlockSpec((TILE, LANES), lambda s, h, j: (s * S_TILES + j, h)),
        out_shape=jax.ShapeDtypeStruct((T_S, D_B), F32),
        compiler_params=_cparams("arbitrary", "arbitrary", "arbitrary"),
        name="attn_latent",
    )(q, k, v, cache_k, cache_v, *small)
    return jnp.concatenate([o_prompt, o_latent], axis=0)


HALO8 = 8
HALO16 = 16


def _prev_block(i, rows):
    return jnp.maximum(i * (TILE // rows) - 1, 0)


def _next_block(i, rows):
    return jnp.minimum((i + 1) * (TILE // rows), T_ALL // rows - 1)


def _even_out_kernel(x_ref, bg_ref, cg_ref, hin_ref, cgp_ref, hinp_ref, cgn_ref, hinn_ref,
                     o_ref, cw_ref, g1_ref, w_ref, out_ref):
    i = pl.program_id(0)
    first, last = _seq_flags(i)
    r = _mod_row(i, TILE)
    z = cg_ref[...] * hin_ref[...]
    zp = jnp.where(first, 0.0, cgp_ref[HALO8 - 1:HALO8, :] * hinp_ref[HALO8 - 1:HALO8, :])
    zn = jnp.where(last, 0.0, cgn_ref[0:1, :] * hinn_ref[0:1, :])
    row = lax.broadcasted_iota(jnp.int32, z.shape, 0)
    z_prev = jnp.where(row == 0, zp, pltpu.roll(z, 1, 0))
    z_next = jnp.where(row == TILE - 1, zn, pltpu.roll(z, TILE - 1, 0))
    cw = cw_ref[...]
    ya = bg_ref[...] * (cw[0:1, :] * z_prev + cw[1:2, :] * z + cw[2:3, :] * z_next)
    y = _dot(ya.astype(BF16), w_ref[0:D_A, :]) + _dot(o_ref[...].astype(BF16), w_ref[D_A:, :])
    out_ref[...] = x_ref[...] + g1_ref[pl.ds(r, 1), :] * y


def _even_out(x, u, o, conv_w, mod, w_out_bf16):
    tile_spec = lambda c: pl.BlockSpec((TILE, D_A), lambda i: (i, c))
    prev_spec = lambda c: pl.BlockSpec((HALO8, D_A), lambda i: (_prev_block(i, HALO8), c))
    next_spec = lambda c: pl.BlockSpec((HALO8, D_A), lambda i: (_next_block(i, HALO8), c))
    return pl.pallas_call(
        _even_out_kernel,
        grid=(N_TILES,),
        in_specs=[
            pl.BlockSpec((TILE, D_MODEL), lambda i: (i, 0)),
            tile_spec(0), tile_spec(1), tile_spec(2),
            prev_spec(1), prev_spec(2), next_spec(1), next_spec(2),
            pl.BlockSpec((TILE, D_B), lambda i: (i, 0)),
            pl.BlockSpec((3, D_A), lambda i: (0, 0)),
            pl.BlockSpec((8, D_MODEL), lambda i: (0, 2)),
            pl.BlockSpec((D_MODEL, D_MODEL), lambda i: (0, 0)),
        ],
        out_specs=pl.BlockSpec((TILE, D_MODEL), lambda i: (i, 0)),
        out_shape=jax.ShapeDtypeStruct((T_ALL, D_MODEL), F32),
        compiler_params=_cparams("arbitrary"),
        name="even_mixer_out",
    )(x, u, u, u, u, u, u, u, o, conv_w, mod, w_out_bf16)


def _odd_out_kernel(x_ref, a_ref, b_ref, pd_ref, ap_ref, bp_ref, an_ref, bn_ref, pp_ref, pn_ref,
                    cw_ref, cb_ref, lg_ref, lb_ref, wp_ref, ps_ref, g1_ref, w_ref, out_ref,
                    ext_ref, extp_ref):
    i = pl.program_id(0)
    first, last = _seq_flags(i)
    r = _mod_row(i, TILE)
    ext_ref[0:HALO16, :] = jnp.where(first, 0.0, ap_ref[...] * jax.nn.sigmoid(bp_ref[...]))
    ext_ref[HALO16:HALO16 + TILE, :] = a_ref[...] * jax.nn.sigmoid(b_ref[...])
    ext_ref[HALO16 + TILE:, :] = jnp.where(last, 0.0, an_ref[...] * jax.nn.sigmoid(bn_ref[...]))
    acc = jnp.zeros((TILE, D_C), F32)
    for j in range(CONV_C):
        acc = acc + cw_ref[j:j + 1, :] * ext_ref[pl.ds(HALO16 - CONV_C // 2 + j, TILE), :]
    g = acc + cb_ref[...]
    mu = jnp.mean(g, axis=-1, keepdims=True)
    var = jnp.mean(jnp.square(g - mu), axis=-1, keepdims=True)
    g = _silu(((g - mu) * lax.rsqrt(var + EPS)) * lg_ref[...] + lb_ref[...])
    extp_ref[0:HALO8, :] = jnp.where(first, 0.0, pp_ref[...])
    extp_ref[HALO8:HALO8 + TILE, :] = pd_ref[...]
    extp_ref[HALO8 + TILE:, :] = jnp.where(last, 0.0, pn_ref[...])
    seq_len = jnp.where(i < P_TILES, SEQ, DEC_SEQ)
    pos = _seq_tile(i) * TILE + lax.broadcasted_iota(jnp.int32, (TILE, 1), 0)
    yd = []
    for gi, w in enumerate(POOL_WINDOWS):
        cols = slice(gi * D_DG, (gi + 1) * D_DG)
        s = jnp.zeros((TILE, D_DG), F32)
        for d in range(-(w // 2), w - w // 2):
            s = s + extp_ref[pl.ds(HALO8 + d, TILE), cols]
        lo = jnp.maximum(pos - w // 2, 0)
        hi = jnp.minimum(pos - w // 2 + w, seq_len)
        pooled = s / (hi - lo).astype(F32) - pd_ref[:, cols]
        yd.append(_dot(pooled.astype(BF16), wp_ref[gi]))
    yd = jnp.concatenate(yd, axis=-1) * ps_ref[...]
    y = _dot(g.astype(BF16), w_ref[0:D_C, :]) + _dot(yd.astype(BF16), w_ref[D_C:, :])
    out_ref[...] = x_ref[...] + g1_ref[pl.ds(r, 1), :] * y


def _odd_out(x, u, conv_w, conv_b, ln_g, ln_b, w_pool_bf16, p_scale, mod, w_out_bf16):
    tile_spec = lambda c: pl.BlockSpec((TILE, D_C), lambda i: (i, c))
    prev_spec = lambda rows, c: pl.BlockSpec((rows, D_C), lambda i: (_prev_block(i, rows), c))
    next_spec = lambda rows, c: pl.BlockSpec((rows, D_C), lambda i: (_next_block(i, rows), c))
    vec = lambda: pl.BlockSpec((1, D_C), lambda i: (0, 0))
    return pl.pallas_call(
        _odd_out_kernel,
        grid=(N_TILES,),
        in_specs=[
            pl.BlockSpec((TILE, D_MODEL), lambda i: (i, 0)),
            tile_spec(0), tile_spec(1), tile_spec(2),
            prev_spec(HALO16, 0), prev_spec(HALO16, 1), next_spec(HALO16, 0), next_spec(HALO16, 1),
            prev_spec(HALO8, 2), next_spec(HALO8, 2),
            pl.BlockSpec((CONV_C, D_C), lambda i: (0, 0)),
            vec(), vec(), vec(),
            pl.BlockSpec((len(POOL_WINDOWS), D_DG, D_DG), lambda i: (0, 0, 0)),
            vec(),
            pl.BlockSpec((8, D_MODEL), lambda i: (0, 2)),
            pl.BlockSpec((D_MODEL, D_MODEL), lambda i: (0, 0)),
        ],
        out_specs=pl.BlockSpec((TILE, D_MODEL), lambda i: (i, 0)),
        out_shape=jax.ShapeDtypeStruct((T_ALL, D_MODEL), F32),
        scratch_shapes=[pltpu.VMEM((TILE + 2 * HALO16, D_C), F32),
                        pltpu.VMEM((TILE + 2 * HALO8, D_D), F32)],
        compiler_params=_cparams("arbitrary"),
        name="odd_mixer_out",
    )(x, u, u, u, u, u, u, u, u, u, conv_w, conv_b.reshape(1, D_C), ln_g.reshape(1, D_C),
      ln_b.reshape(1, D_C), w_pool_bf16, p_scale.reshape(1, D_D), mod, w_out_bf16)


GROUP = N_EXPERTS // N_GROUPS
NEG_INF = float("-inf")


def _first_argmax(v, idx, axis):
    m = jnp.max(v, axis=axis, keepdims=True)
    big = jnp.int32(2 ** 30)
    am = jnp.min(jnp.where(v == m, idx, big), axis=axis, keepdims=True)
    return m, am


def _route(scores, biased):
    shape = biased.shape
    member = lax.broadcasted_iota(jnp.int32, shape, 1)
    m1, a1 = _first_argmax(biased, member, 1)
    m2 = jnp.max(jnp.where(member == a1, NEG_INF, biased), axis=1, keepdims=True)
    gscore = m1 + m2
    gidx = lax.broadcasted_iota(jnp.int32, gscore.shape, 0)
    gsel = jnp.zeros(gscore.shape, jnp.bool_)
    for _ in range(TOPK_GROUPS):
        _, am = _first_argmax(gscore, gidx, 0)
        hit = gidx == am
        gsel = jnp.logical_or(gsel, hit)
        gscore = jnp.where(hit, NEG_INF, gscore)
    cand = jnp.where(gsel, biased, NEG_INF)
    eidx = lax.broadcasted_iota(jnp.int32, shape, 0) * GROUP + member
    sel = jnp.zeros(shape, jnp.bool_)
    for _ in range(TOP_K):
        m = jnp.max(jnp.max(cand, axis=1, keepdims=True), axis=0, keepdims=True)
        big = jnp.int32(2 ** 30)
        am = jnp.where(cand == m, eidx, big)
        am = jnp.min(jnp.min(am, axis=1, keepdims=True), axis=0, keepdims=True)
        hit = eidx == am
        sel = jnp.logical_or(sel, hit)
        cand = jnp.where(hit, NEG_INF, cand)
    wsel = jnp.where(sel, scores, 0.0)
    tot = jnp.sum(jnp.sum(wsel, axis=1, keepdims=True), axis=0, keepdims=True)
    return wsel / tot * ROUTED_SCALE, sel


MOE_TC = 4096
N_CHUNKS = T_ALL // MOE_TC
ROW_TILE = 256
MAX_TILES = MOE_TC * TOP_K // ROW_TILE + N_EXPERTS
N_FLUSH = MOE_TC // TILE
N_BLK = MOE_TC // LANES
ROW_SUB = D_MODEL // LANES


def _moe_pre_kernel(x_ref, g_ref, sh_ref, sc_ref, wr_ref, br_ref, tri_ref, wsg_ref, wsu_ref,
                    wsd_ref, hrow_ref, gate_ref, rank_ref, shared_ref, carry_ref):
    i = pl.program_id(0)
    r = _mod_row(i, TILE)
    h = _modulated_norm(x_ref[...], g_ref[...], sh_ref[pl.ds(r, 1), :], sc_ref[pl.ds(r, 1), :])
    hb = h.astype(BF16)
    for s in range(ROW_SUB):
        hrow_ref[pl.ds(s, TILE, stride=ROW_SUB), :] = h[:, s * LANES:(s + 1) * LANES]
    h_hi, h_lo = hb, (h - hb.astype(F32)).astype(BF16)
    w_hi, w_lo = _split_bf16(wr_ref[...])
    logits = _dot_nt(w_hi, h_hi) + _dot_nt(w_lo, h_hi) + _dot_nt(w_hi, h_lo)
    scores = jax.nn.sigmoid(logits)
    biased = scores + br_ref[:, 0:1]
    shape3 = (N_GROUPS, GROUP, TILE)
    gate_t, sel = _route(scores.reshape(shape3), biased.reshape(shape3))
    gate_t = gate_t.reshape(N_EXPERTS, TILE)
    sel = jnp.where(sel.reshape(N_EXPERTS, TILE), 1.0, 0.0)

    @pl.when(i % N_FLUSH == 0)
    def _():
        carry_ref[...] = jnp.zeros_like(carry_ref)

    carry = carry_ref[...]
    local = _dot(sel.astype(BF16), tri_ref[...])
    rank = jnp.where(sel > 0.0, local + jnp.concatenate([carry] * (TILE // LANES), axis=1), -1.0)
    carry_ref[...] = carry + jnp.sum(sel, axis=1, keepdims=True)
    for b in range(TILE // LANES):
        gate_ref[b] = gate_t[:, b * LANES:(b + 1) * LANES]
        rank_ref[b] = rank[:, b * LANES:(b + 1) * LANES]
    a = _silu(_dot(hb, wsg_ref[...])) * _dot(hb, wsu_ref[...])
    shared_ref[...] = _dot(a.astype(BF16), wsd_ref[...])


def _moe_pre(x, g, mod, w_router_t, b_router, wsg, wsu, wsd):
    return pl.pallas_call(
        _moe_pre_kernel,
        grid=(N_TILES,),
        in_specs=[
            pl.BlockSpec((TILE, D_MODEL), lambda i: (i, 0)),
            pl.BlockSpec((1, D_MODEL), lambda i: (0, 0)),
            pl.BlockSpec((8, D_MODEL), lambda i: (0, 3)),
            pl.BlockSpec((8, D_MODEL), lambda i: (0, 4)),
            pl.BlockSpec((N_EXPERTS, D_MODEL), lambda i: (0, 0)),
            pl.BlockSpec((N_EXPERTS, LANES), lambda i: (0, 0)),
            pl.BlockSpec((TILE, TILE), lambda i: (0, 0)),
            pl.BlockSpec((D_MODEL, D_SHARED), lambda i: (0, 0)),
            pl.BlockSpec((D_MODEL, D_SHARED), lambda i: (0, 0)),
            pl.BlockSpec((D_SHARED, D_MODEL), lambda i: (0, 0)),
        ],
        out_specs=[
            pl.BlockSpec((TILE * ROW_SUB, LANES), lambda i: (i, 0)),
            pl.BlockSpec((TILE // LANES, N_EXPERTS, LANES), lambda i: (i, 0, 0)),
            pl.BlockSpec((TILE // LANES, N_EXPERTS, LANES), lambda i: (i, 0, 0)),
            pl.BlockSpec((TILE, D_MODEL), lambda i: (i, 0)),
        ],
        out_shape=[
            jax.ShapeDtypeStruct((T_ALL * ROW_SUB, LANES), F32),
            jax.ShapeDtypeStruct((T_ALL // LANES, N_EXPERTS, LANES), F32),
            jax.ShapeDtypeStruct((T_ALL // LANES, N_EXPERTS, LANES), F32),
            jax.ShapeDtypeStruct((T_ALL, D_MODEL), F32),
        ],
        scratch_shapes=[pltpu.VMEM((N_EXPERTS, LANES), F32)],
        compiler_params=_cparams("arbitrary"),
        name="moe_pre",
    )(x, g.reshape(1, D_MODEL), mod, mod, w_router_t,
      jnp.broadcast_to(b_router.reshape(N_EXPERTS, 1), (N_EXPERTS, LANES)),
      jnp.asarray(np.triu(np.ones((TILE, TILE), np.float32), 1), BF16), wsg, wsu, wsd)


def _tile_tables(rank_b):
    blk_cnt = jnp.sum((rank_b >= 0.0).astype(jnp.int32), axis=-1)
    blk_cnt = blk_cnt.reshape(N_CHUNKS, N_BLK, N_EXPERTS)
    cum_incl = jnp.cumsum(blk_cnt, axis=1)
    cum_excl = cum_incl - blk_cnt
    counts = cum_incl[:, -1, :]
    n_tile_e = (counts + ROW_TILE - 1) // ROW_TILE
    tile_end = jnp.cumsum(n_tile_e, axis=1)
    tile_start = tile_end - n_tile_e
    n_tiles = tile_end[:, -1:]
    j = jnp.arange(MAX_TILES, dtype=jnp.int32)[None, :]
    j_used = jnp.minimum(j, n_tiles - 1)
    t_exp = jnp.sum((tile_end[:, None, :] <= j_used[:, :, None]).astype(jnp.int32), axis=-1)
    q0 = (j_used - jnp.take_along_axis(tile_start, t_exp, axis=1)) * ROW_TILE
    cnt = jnp.clip(jnp.take_along_axis(counts, t_exp, axis=1) - q0, 0, ROW_TILE)
    cnt = jnp.where(j < n_tiles, cnt, 0)
    e_idx = jnp.broadcast_to(t_exp[:, None, :], (N_CHUNKS, N_BLK, MAX_TILES))
    ci = jnp.take_along_axis(cum_incl, e_idx, axis=2)
    ce = jnp.take_along_axis(cum_excl, e_idx, axis=2)
    b_lo = jnp.sum((ci <= q0[:, None, :]).astype(jnp.int32), axis=1)
    b_hi = jnp.sum((ce < (q0 + ROW_TILE)[:, None, :]).astype(jnp.int32), axis=1)
    flat = lambda a: a.reshape(-1).astype(jnp.int32)
    return flat(t_exp), flat(q0), flat(cnt), flat(b_lo), flat(b_hi)


SORT_HALF = 128


def _moe_sort_kernel(te_ref, q0_ref, cnt_ref, blo_ref, bhi_ref, rank_ref, gate_ref,
                     src_ref, w_ref):
    idx = pl.program_id(0) * MAX_TILES + pl.program_id(1)
    e = te_ref[idx]
    q0 = q0_ref[idx]
    cnt = cnt_ref[idx]
    lane = lax.broadcasted_iota(jnp.int32, (1, LANES), 1)
    for half in range(ROW_TILE // SORT_HALF):
        qcol = (q0 + half * SORT_HALF
                + lax.broadcasted_iota(jnp.int32, (SORT_HALF, LANES), 0)).astype(F32)

        def body(b, carry):
            s_acc, w_acc = carry
            rrow = rank_ref[b, pl.ds(e, 1), :]
            grow = gate_ref[b, pl.ds(e, 1), :]
            hit = rrow == qcol
            tok = (b * LANES + lane).astype(F32)
            return jnp.where(hit, tok, s_acc), jnp.where(hit, grow, w_acc)

        zero = jnp.zeros((SORT_HALF, LANES), F32)
        s_acc, w_acc = lax.fori_loop(blo_ref[idx], bhi_ref[idx], body, (zero, zero))
        s_row = jnp.sum(s_acc.T, axis=0, keepdims=True)
        w_row = jnp.sum(w_acc.T, axis=0, keepdims=True)
        valid = (half * SORT_HALF + lane) < cnt
        cols = slice(half * SORT_HALF, (half + 1) * SORT_HALF)
        src_ref[0, :, cols] = jnp.where(valid, s_row.astype(jnp.int32), MOE_TC)
        w_ref[0, :, cols] = jnp.where(valid, w_row, 0.0)


def _moe_sort(tables, rank_b, gate_b):
    chunk_spec = pl.BlockSpec((N_BLK, N_EXPERTS, LANES), lambda c, j, *_: (c, 0, 0))
    row_spec = pl.BlockSpec((1, 1, ROW_TILE), lambda c, j, *_: (c * MAX_TILES + j, 0, 0))
    n = N_CHUNKS * MAX_TILES
    return pl.pallas_call(
        _moe_sort_kernel,
        grid_spec=pltpu.PrefetchScalarGridSpec(
            num_scalar_prefetch=5,
            grid=(N_CHUNKS, MAX_TILES),
            in_specs=[chunk_spec, chunk_spec],
            out_specs=[row_spec, row_spec],
        ),
        out_shape=[jax.ShapeDtypeStruct((n, 1, ROW_TILE), jnp.int32),
                   jax.ShapeDtypeStruct((n, 1, ROW_TILE), F32)],
        compiler_params=_cparams("arbitrary", "arbitrary"),
        name="moe_sort",
    )(*tables, rank_b, gate_b)


SCATTER_BATCH = 16


def _moe_routed_kernel(te_ref, tc_ref, src_ref, wt_ref, h_ref, wg_ref, wu_ref, wd_ref,
                       x_ref, sh_ref, g2_ref, out_ref,
                       acc_ref, wgb_ref, wub_ref, wdb_ref, xs_ref, ys_ref):
    c = pl.program_id(0)
    j = pl.program_id(1)
    idx = c * MAX_TILES + jnp.minimum(j, MAX_TILES - 1)

    @pl.when(j == 0)
    def _():
        acc_ref[...] = jnp.zeros_like(acc_ref)

    @pl.when(jnp.logical_and(j < MAX_TILES, tc_ref[idx] > 0))
    def _():
        new_expert = jnp.logical_or(j == 0, te_ref[idx] != te_ref[jnp.maximum(idx - 1, 0)])

        @pl.when(new_expert)
        def _():
            wgb_ref[...] = wg_ref[0, 0].astype(BF16)
            wub_ref[...] = wu_ref[0, 0].astype(BF16)
            wdb_ref[...] = wd_ref[0, 0].astype(BF16)

        def gather(g, _):
            for k in range(8):
                t = jnp.minimum(src_ref[0, 0, g * 8 + k], MOE_TC - 1)
                row = h_ref[pl.ds(pl.multiple_of(t * ROW_SUB, ROW_SUB), ROW_SUB), :]
                xs_ref[g, pl.ds(k, ROW_SUB, stride=8), :] = row
            return 0

        lax.fori_loop(0, ROW_TILE // 8, gather, 0)
        xt = jnp.concatenate(
            [xs_ref[:, s * 8:(s + 1) * 8, :].reshape(ROW_TILE, LANES) for s in range(ROW_SUB)],
            axis=1).astype(BF16)
        a = _silu(_dot(xt, wgb_ref[...])) * _dot(xt, wub_ref[...])
        y = _dot(a.astype(BF16), wdb_ref[...])
        for s in range(ROW_SUB):
            ys_ref[:, s * 8:(s + 1) * 8, :] = y[:, s * LANES:(s + 1) * LANES].reshape(
                ROW_TILE // 8, 8, LANES)

        def scatter(bi, _):
            toks, olds, news = [], [], []
            for k in range(SCATTER_BATCH):
                r = bi * SCATTER_BATCH + k
                t = pl.multiple_of(src_ref[0, 0, r] * ROW_SUB, ROW_SUB)
                toks.append(t)
                olds.append(acc_ref[pl.ds(t, ROW_SUB), :])
            for k in range(SCATTER_BATCH):
                r = bi * SCATTER_BATCH + k
                yrow = ys_ref[bi * (SCATTER_BATCH // 8) + k // 8, pl.ds(k % 8, ROW_SUB, stride=8), :]
                news.append(olds[k] + wt_ref[0, 0, r] * yrow)
            for k in range(SCATTER_BATCH):
                acc_ref[pl.ds(toks[k], ROW_SUB), :] = news[k]
            return 0

        lax.fori_loop(0, ROW_TILE // SCATTER_BATCH, scatter, 0)

    @pl.when(j >= MAX_TILES)
    def _():
        base = (j - MAX_TILES) * (TILE * ROW_SUB)
        moe = jnp.concatenate(
            [acc_ref[pl.ds(base + s, TILE, stride=ROW_SUB), :] for s in range(ROW_SUB)], axis=1)
        out_ref[...] = x_ref[...] + g2_ref[pl.ds(c, 1), :] * (moe + sh_ref[...])


def _moe_routed(src, wts, t_exp, t_cnt, hrows, w_gate, w_up, w_down, li, x, shared, mod):
    def tile_idx(c, j):
        return c * MAX_TILES + jnp.minimum(j, MAX_TILES - 1)

    def flush_blk(c, j):
        return c * N_FLUSH + jnp.maximum(j - MAX_TILES, 0)

    smem_spec = pl.BlockSpec((1, 1, ROW_TILE), lambda c, j, te, tc: (tile_idx(c, j), 0, 0),
                             memory_space=pltpu.SMEM)
    w_in_spec = pl.BlockSpec((1, 1, D_MODEL, D_EXPERT),
                             lambda c, j, te, tc: (li, te[tile_idx(c, j)], 0, 0))
    w_out_spec = pl.BlockSpec((1, 1, D_EXPERT, D_MODEL),
                              lambda c, j, te, tc: (li, te[tile_idx(c, j)], 0, 0))
    tok_spec = pl.BlockSpec((TILE, D_MODEL), lambda c, j, te, tc: (flush_blk(c, j), 0))
    grid_spec = pltpu.PrefetchScalarGridSpec(
        num_scalar_prefetch=2,
        grid=(N_CHUNKS, MAX_TILES + N_FLUSH),
        in_specs=[
            smem_spec, smem_spec,
            pl.BlockSpec((MOE_TC * ROW_SUB, LANES), lambda c, j, te, tc: (c, 0),
                         pipeline_mode=pl.Buffered(1)),
            w_in_spec, w_in_spec, w_out_spec,
            tok_spec, tok_spec,
            pl.BlockSpec((8, D_MODEL), lambda c, j, te, tc: (0, 5)),
        ],
        out_specs=tok_spec,
        scratch_shapes=[
            pltpu.VMEM(((MOE_TC + 1) * ROW_SUB, LANES), F32),
            pltpu.VMEM((D_MODEL, D_EXPERT), BF16),
            pltpu.VMEM((D_MODEL, D_EXPERT), BF16),
            pltpu.VMEM((D_EXPERT, D_MODEL), BF16),
            pltpu.VMEM((ROW_TILE // 8, 8 * ROW_SUB, LANES), F32),
            pltpu.VMEM((ROW_TILE // 8, 8 * ROW_SUB, LANES), F32),
        ],
    )
    return pl.pallas_call(
        _moe_routed_kernel,
        grid_spec=grid_spec,
        out_shape=jax.ShapeDtypeStruct((T_ALL, D_MODEL), F32),
        compiler_params=_cparams("arbitrary", "arbitrary"),
        name="moe_routed",
    )(t_exp, t_cnt, src, wts, hrows, w_gate, w_up, w_down, x, shared, mod)


def kernel(x_prompt, x_sample, cache_k, cache_v, c, c_ctx, w_mod, b_mod, norm1, norm2, w_in_even, conv_a, q_norm, k_norm, lam_q1, lam_k1, lam_q2, lam_k2, subln, w_out_even, w_in_odd, conv_c, conv_c_b, ln_c_g, ln_c_b, w_pool, pool_scale, w_out_odd, w_router, b_router, w_gate, w_up, w_down, ws_gate, ws_up, ws_down):
    x = jnp.concatenate([x_prompt.reshape(T_P, D_MODEL), x_sample.reshape(T_S, D_MODEL)], axis=0)
    cond = jnp.concatenate([c_ctx[None, :], c, jnp.zeros((8 - 1 - DEC_BATCH, D_MODEL), F32)], axis=0)
    mod_all = _modulation(cond, w_mod, b_mod)

    new_k = new_v = None
    for li in range(DEPTH):
        mod = mod_all[li]
        if li % 2 == 0:
            e = li // 2
            lam_init = 0.8 - 0.6 * math.exp(-0.3 * li)
            u = _norm_in(x, norm1[li], mod, w_in_even[e].astype(BF16))
            q, k, v = _qkv_prep(u, q_norm[e], k_norm[e])
            if new_k is None:
                new_k, new_v = k, v
            o = _attention(q, k, v, cache_k, cache_v, e,
                           (lam_q1[e], lam_k1[e], lam_q2[e], lam_k2[e]), subln[e], lam_init)
            x = _even_out(x, u, o, conv_a[e], mod, w_out_even[e].astype(BF16))
        else:
            o_ = li // 2
            u = _norm_in(x, norm1[li], mod, w_in_odd[o_].astype(BF16))
            x = _odd_out(x, u, conv_c[o_], conv_c_b[o_], ln_c_g[o_], ln_c_b[o_],
                         w_pool[o_].astype(BF16), pool_scale[o_], mod, w_out_odd[o_].astype(BF16))
        hrows, gate_b, rank_b, shared = _moe_pre(
            x, norm2[li], mod, w_router[li].T, b_router[li], ws_gate[li].astype(BF16),
            ws_up[li].astype(BF16), ws_down[li].astype(BF16))
        tables = _tile_tables(rank_b)
        src, wts = _moe_sort(tables, rank_b, gate_b)
        x = _moe_routed(src, wts, tables[0], tables[2], hrows, w_gate, w_up, w_down, li,
                        x, shared, mod)

    y_prompt = x[:T_P].reshape(BATCH, SEQ, D_MODEL)
    y_sample = x[T_P:].reshape(DEC_BATCH, DEC_SEQ, D_MODEL)

    def to_cache(a):
        a = a[:, :T_P].reshape(H_B, BATCH, SEQ, LANES)
        return jnp.transpose(a, (1, 0, 2, 3))[:, None]

    return (y_prompt, y_sample, to_cache(new_k), to_cache(new_v))
```

```python
import functools
import math

import numpy as np
import jax
import jax.numpy as jnp
from jax import lax
from jax.experimental import pallas as pl
from jax.experimental.pallas import tpu as pltpu

D_MODEL = 1024
BATCH = 16
SEQ = 256
DEPTH = 2
DEC_BATCH = 2
DEC_SEQ = 4096
PAST_LEN = 512
GRID_W = 64
H_B = 4
DK_B = 64
DV_B = 2 * DK_B
D_A = D_MODEL // 2
D_B = H_B * DV_B
D_C = D_MODEL // 2
D_D = D_MODEL // 2
CONV_C = 31
POOL_WINDOWS = (2, 4, 8, 16)
D_DG = D_D // len(POOL_WINDOWS)
N_EXPERTS = 64
TOP_K = 8
N_GROUPS = 8
TOPK_GROUPS = 4
D_EXPERT = 256
D_SHARED = 256
ROUTED_SCALE = 2.5
ROPE_BASE = 10000.0
EPS = 1e-6

F32 = jnp.float32
BF16 = jnp.bfloat16

T_P = BATCH * SEQ
T_S = DEC_BATCH * DEC_SEQ
T_ALL = T_P + T_S
TILE = 256
N_TILES = T_ALL // TILE
P_TILES = T_P // TILE
S_TILES = DEC_SEQ // TILE
LANES = 128
VMEM_LIMIT = 56 * 1024 * 1024


def _cparams(*sem):
    return pltpu.CompilerParams(dimension_semantics=sem, vmem_limit_bytes=VMEM_LIMIT)


def _mod_row(i, tm):
    npt = T_P // tm
    per = DEC_SEQ // tm
    return jnp.where(i < npt, 0, 1 + (i - npt) // per)


def _seq_flags(i):
    j = (i - P_TILES) % S_TILES
    first = jnp.logical_or(i < P_TILES, j == 0)
    last = jnp.logical_or(i < P_TILES, j == S_TILES - 1)
    return first, last


def _seq_tile(i):
    return jnp.where(i < P_TILES, 0, (i - P_TILES) % S_TILES)


def _split_bf16(a):
    hi = a.astype(BF16)
    lo = (a - hi.astype(F32)).astype(BF16)
    return hi, lo


def _dot(a, b):
    return jnp.dot(a, b, preferred_element_type=F32)


def _dot_nt(a, b):
    return lax.dot_general(a, b, (((1,), (1,)), ((), ())), preferred_element_type=F32)


def _dot3(a, b):
    a_hi, a_lo = _split_bf16(a)
    b_hi, b_lo = _split_bf16(b)
    return _dot(a_hi, b_hi) + _dot(a_lo, b_hi) + _dot(a_hi, b_lo)


def _silu(x):
    return x * jax.nn.sigmoid(x)


MOD_TN = 1536


def _mod_kernel(c_ref, w_ref, b_ref, o_ref):
    o_ref[0] = _dot3(_silu(c_ref[...]), w_ref[0]) + b_ref[0]


def _modulation(cond, w_mod, b_mod):
    n = 6 * D_MODEL
    return pl.pallas_call(
        _mod_kernel,
        grid=(DEPTH, n // MOD_TN),
        in_specs=[
            pl.BlockSpec((8, D_MODEL), lambda l, j: (0, 0)),
            pl.BlockSpec((1, D_MODEL, MOD_TN), lambda l, j: (l, 0, j)),
            pl.BlockSpec((1, 1, MOD_TN), lambda l, j: (l, 0, j)),
        ],
        out_specs=pl.BlockSpec((1, 8, MOD_TN), lambda l, j: (l, 0, j)),
        out_shape=jax.ShapeDtypeStruct((DEPTH, 8, n), F32),
        compiler_params=_cparams("arbitrary", "arbitrary"),
        name="modulation",
    )(cond, w_mod, b_mod.reshape(DEPTH, 1, n))


IN_TM = 512


def _modulated_norm(x, g, shift, scale):
    ms = jnp.mean(x * x, axis=-1, keepdims=True)
    return (x * lax.rsqrt(ms + EPS) * g) * (1.0 + scale) + shift


def _norm_in_kernel(x_ref, g_ref, sh_ref, sc_ref, w_ref, o_ref):
    r = _mod_row(pl.program_id(0), IN_TM)
    h = _modulated_norm(x_ref[...], g_ref[...], sh_ref[pl.ds(r, 1), :], sc_ref[pl.ds(r, 1), :])
    o_ref[...] = _dot(h.astype(BF16), w_ref[...])


def _norm_in(x, g, mod, w_bf16):
    n = w_bf16.shape[1]
    return pl.pallas_call(
        _norm_in_kernel,
        grid=(T_ALL // IN_TM,),
        in_specs=[
            pl.BlockSpec((IN_TM, D_MODEL), lambda i: (i, 0)),
            pl.BlockSpec((1, D_MODEL), lambda i: (0, 0)),
            pl.BlockSpec((8, D_MODEL), lambda i: (0, 0)),
            pl.BlockSpec((8, D_MODEL), lambda i: (0, 1)),
            pl.BlockSpec((D_MODEL, n), lambda i: (0, 0)),
        ],
        out_specs=pl.BlockSpec((IN_TM, n), lambda i: (i, 0)),
        out_shape=jax.ShapeDtypeStruct((T_ALL, n), F32),
        compiler_params=_cparams("arbitrary"),
        name="norm_in_proj",
    )(x, g.reshape(1, D_MODEL), mod, mod, w_bf16)


def _rope_tables():
    half = DK_B // 2
    freqs = ROPE_BASE ** (-np.arange(0, half, 2, dtype=np.float64) / half)
    l = np.arange(DEC_SEQ)
    pos_r = (l // GRID_W).astype(np.float64)
    pos_c = (l % GRID_W).astype(np.float64)
    lane = np.arange(LANES)
    jj = lane % DK_B
    m = jj % half
    f = m % (half // 2)
    pos = np.where((jj < half)[None, :], pos_r[:, None], pos_c[:, None])
    ang = pos * freqs[f][None, :]
    sign = np.where(m < half // 2, -1.0, 1.0)[None, :]
    cos = np.concatenate([np.ones((TILE, LANES)), np.cos(ang)], axis=0)
    sin = np.concatenate([np.zeros((TILE, LANES)), sign * np.sin(ang)], axis=0)
    return cos.astype(np.float32), sin.astype(np.float32)


def _segment_mean_matrix():
    lane = np.arange(LANES)
    same = (lane[:, None] // DK_B) == (lane[None, :] // DK_B)
    return (same.astype(np.float32) / DK_B)


def _qk_prep(x, g, cos, sin, seg):
    x2 = x * x
    hi, lo = _split_bf16(x2)
    ms = _dot(hi, seg) + _dot(lo, seg)
    y = x * lax.rsqrt(ms + EPS) * g
    lane = lax.broadcasted_iota(jnp.int32, y.shape, 1)
    lower = (lane % (DK_B // 2)) < (DK_B // 4)
    partner = jnp.where(lower, pltpu.roll(y, LANES - DK_B // 4, 1), pltpu.roll(y, DK_B // 4, 1))
    return y * cos + partner * sin


def _qkv_prep_kernel(q_ref, k_ref, v_ref, qn_ref, kn_ref, cos_ref, sin_ref, seg_ref,
                     qo_ref, ko_ref, vo_ref):
    cos = cos_ref[...]
    sin = sin_ref[...]
    seg = seg_ref[...]
    scale = 1.0 / math.sqrt(DK_B)
    qo_ref[0] = _qk_prep(q_ref[...], qn_ref[...], cos, sin, seg) * scale
    ko_ref[0] = _qk_prep(k_ref[...], kn_ref[...], cos, sin, seg)
    vo_ref[0] = v_ref[...]


def _qkv_prep(u, qn, kn):
    cos, sin = _rope_tables()
    seg = jnp.asarray(_segment_mean_matrix(), BF16)
    qn2 = jnp.concatenate([qn, qn]).reshape(1, LANES)
    kn2 = jnp.concatenate([kn, kn]).reshape(1, LANES)
    col0 = 3 * D_A // LANES

    def tab_map(i, h):
        return (jnp.where(i < P_TILES, 0, 1 + (i - P_TILES) % S_TILES), 0)

    out = jax.ShapeDtypeStruct((H_B, T_ALL, LANES), F32)
    return pl.pallas_call(
        _qkv_prep_kernel,
        grid=(N_TILES, H_B),
        in_specs=[
            pl.BlockSpec((TILE, LANES), lambda i, h: (i, col0 + h)),
            pl.BlockSpec((TILE, LANES), lambda i, h: (i, col0 + H_B + h)),
            pl.BlockSpec((TILE, LANES), lambda i, h: (i, col0 + 2 * H_B + h)),
            pl.BlockSpec((1, LANES), lambda i, h: (0, 0)),
            pl.BlockSpec((1, LANES), lambda i, h: (0, 0)),
            pl.BlockSpec((TILE, LANES), tab_map),
            pl.BlockSpec((TILE, LANES), tab_map),
            pl.BlockSpec((LANES, LANES), lambda i, h: (0, 0)),
        ],
        out_specs=[pl.BlockSpec((1, TILE, LANES), lambda i, h: (h, i, 0))] * 3,
        out_shape=[out, out, out],
        compiler_params=_cparams("arbitrary", "arbitrary"),
        name="qkv_prep",
    )(u, u, u, qn2, kn2, jnp.asarray(cos), jnp.asarray(sin), seg)


def _lambda(lq1_ref, lk1_ref, lq2_ref, lk2_ref, lam_init):
    a = jnp.sum(lq1_ref[...] * lk1_ref[...], axis=-1, keepdims=True)
    b = jnp.sum(lq2_ref[...] * lk2_ref[...], axis=-1, keepdims=True)
    return jnp.exp(a) - jnp.exp(b) + lam_init


def _attn_body(q, keys, vals, lam, subg, lam_init):
    lane = lax.broadcasted_iota(jnp.int32, q.shape, 1)
    qa = jnp.where(lane < DK_B, q, 0.0).astype(BF16)
    qb = jnp.where(lane < DK_B, 0.0, q).astype(BF16)
    out = None
    for qq, coef_sign in ((qa, None), (qb, lam)):
        ss = [_dot_nt(qq, k) for k in keys]
        m = functools.reduce(jnp.maximum, [jnp.max(s, axis=-1, keepdims=True) for s in ss])
        ps = [jnp.exp(s - m) for s in ss]
        l = functools.reduce(jnp.add, [jnp.sum(p, axis=-1, keepdims=True) for p in ps])
        w = 1.0 / l if coef_sign is None else -coef_sign / l
        ps = [p * w for p in ps]
        out = ps if out is None else [a + b for a, b in zip(out, ps)]
    o = functools.reduce(jnp.add, [_dot(a.astype(BF16), v) for a, v in zip(out, vals)])
    ms = jnp.mean(o * o, axis=-1, keepdims=True)
    return (o * lax.rsqrt(ms + EPS) * subg) * (1.0 - lam_init)


def _attn_prompt_kernel(q_ref, k_ref, v_ref, lq1, lk1, lq2, lk2, sg_ref, o_ref, *, lam_init):
    lam = _lambda(lq1, lk1, lq2, lk2, lam_init)
    o_ref[...] = _attn_body(q_ref[0], [k_ref[0].astype(BF16)], [v_ref[0].astype(BF16)],
                            lam, sg_ref[...], lam_init)


def _attn_latent_kernel(q_ref, k_ref, v_ref, ck_ref, cv_ref, lq1, lk1, lq2, lk2, sg_ref, o_ref,
                        *, lam_init):
    lam = _lambda(lq1, lk1, lq2, lk2, lam_init)
    keys = [ck_ref[0, 0, 0].astype(BF16), k_ref[0].astype(BF16)]
    vals = [cv_ref[0, 0, 0].astype(BF16), v_ref[0].astype(BF16)]
    o_ref[...] = _attn_body(q_ref[0], keys, vals, lam, sg_ref[...], lam_init)


def _attention(q, k, v, cache_k, cache_v, e, lam_params, subg, lam_init):
    small = [p.reshape(1, DK_B) for p in lam_params] + [subg.reshape(1, DV_B)]
    small_specs2 = [pl.BlockSpec((1, DK_B), lambda b, h: (0, 0))] * 4 + \
                   [pl.BlockSpec((1, DV_B), lambda b, h: (0, 0))]
    small_specs3 = [pl.BlockSpec((1, DK_B), lambda s, h, j: (0, 0))] * 4 + \
                   [pl.BlockSpec((1, DV_B), lambda s, h, j: (0, 0))]
    o_prompt = pl.pallas_call(
        functools.partial(_attn_prompt_kernel, lam_init=lam_init),
        grid=(BATCH, H_B),
        in_specs=[pl.BlockSpec((1, SEQ, LANES), lambda b, h: (h, b, 0))] * 3 + small_specs2,
        out_specs=pl.BlockSpec((SEQ, LANES), lambda b, h: (b, h)),
        out_shape=jax.ShapeDtypeStruct((T_P, D_B), F32),
        compiler_params=_cparams("arbitrary", "arbitrary"),
        name="attn_prompt",
    )(q, k, v, *small)

    kv_spec = pl.BlockSpec((1, DEC_SEQ, LANES), lambda s, h, j: (h, T_P // DEC_SEQ + s, 0))
    c_spec = pl.BlockSpec((1, 1, 1, PAST_LEN, LANES), lambda s, h, j: (s, e, h, 0, 0))
    o_latent = pl.pallas_call(
        functools.partial(_attn_latent_kernel, lam_init=lam_init),
        grid=(DEC_BATCH, H_B, S_TILES),
        in_specs=[pl.BlockSpec((1, TILE, LANES), lambda s, h, j: (h, P_TILES + s * S_TILES + j, 0)),
                  kv_spec, kv_spec, c_spec, c_spec] + small_specs3,
        out_specs=pl.BlockSpec((TILE, LANES), lambda s, h, j: (s * S_TILES + j, h)),
        out_shape=jax.ShapeDtypeStruct((T_S, D_B), F32),
        compiler_params=_cparams("arbitrary", "arbitrary", "arbitrary"),
        name="attn_latent",
    )(q, k, v, cache_k, cache_v, *small)
    return jnp.concatenate([o_prompt, o_latent], axis=0)


HALO8 = 8
HALO16 = 16


def _prev_block(i, rows):
    return jnp.maximum(i * (TILE // rows) - 1, 0)


def _next_block(i, rows):
    return jnp.minimum((i + 1) * (TILE // rows), T_ALL // rows - 1)


def _even_out_kernel(x_ref, bg_ref, cg_ref, hin_ref, cgp_ref, hinp_ref, cgn_ref, hinn_ref,
                     o_ref, cw_ref, g1_ref, w_ref, out_ref):
    i = pl.program_id(0)
    first, last = _seq_flags(i)
    r = _mod_row(i, TILE)
    z = cg_ref[...] * hin_ref[...]
    zp = jnp.where(first, 0.0, cgp_ref[HALO8 - 1:HALO8, :] * hinp_ref[HALO8 - 1:HALO8, :])
    zn = jnp.where(last, 0.0, cgn_ref[0:1, :] * hinn_ref[0:1, :])
    row = lax.broadcasted_iota(jnp.int32, z.shape, 0)
    z_prev = jnp.where(row == 0, zp, pltpu.roll(z, 1, 0))
    z_next = jnp.where(row == TILE - 1, zn, pltpu.roll(z, TILE - 1, 0))
    cw = cw_ref[...]
    ya = bg_ref[...] * (cw[0:1, :] * z_prev + cw[1:2, :] * z + cw[2:3, :] * z_next)
    y = _dot(ya.astype(BF16), w_ref[0:D_A, :]) + _dot(o_ref[...].astype(BF16), w_ref[D_A:, :])
    out_ref[...] = x_ref[...] + g1_ref[pl.ds(r, 1), :] * y


def _even_out(x, u, o, conv_w, mod, w_out_bf16):
    tile_spec = lambda c: pl.BlockSpec((TILE, D_A), lambda i: (i, c))
    prev_spec = lambda c: pl.BlockSpec((HALO8, D_A), lambda i: (_prev_block(i, HALO8), c))
    next_spec = lambda c: pl.BlockSpec((HALO8, D_A), lambda i: (_next_block(i, HALO8), c))
    return pl.pallas_call(
        _even_out_kernel,
        grid=(N_TILES,),
        in_specs=[
            pl.BlockSpec((TILE, D_MODEL), lambda i: (i, 0)),
            tile_spec(0), tile_spec(1), tile_spec(2),
            prev_spec(1), prev_spec(2), next_spec(1), next_spec(2),
            pl.BlockSpec((TILE, D_B), lambda i: (i, 0)),
            pl.BlockSpec((3, D_A), lambda i: (0, 0)),
            pl.BlockSpec((8, D_MODEL), lambda i: (0, 2)),
            pl.BlockSpec((D_MODEL, D_MODEL), lambda i: (0, 0)),
        ],
        out_specs=pl.BlockSpec((TILE, D_MODEL), lambda i: (i, 0)),
        out_shape=jax.ShapeDtypeStruct((T_ALL, D_MODEL), F32),
        compiler_params=_cparams("arbitrary"),
        name="even_mixer_out",
    )(x, u, u, u, u, u, u, u, o, conv_w, mod, w_out_bf16)


def _odd_out_kernel(x_ref, a_ref, b_ref, pd_ref, ap_ref, bp_ref, an_ref, bn_ref, pp_ref, pn_ref,
                    cw_ref, cb_ref, lg_ref, lb_ref, wp_ref, ps_ref, g1_ref, w_ref, out_ref,
                    ext_ref, extp_ref):
    i = pl.program_id(0)
    first, last = _seq_flags(i)
    r = _mod_row(i, TILE)
    ext_ref[0:HALO16, :] = jnp.where(first, 0.0, ap_ref[...] * jax.nn.sigmoid(bp_ref[...]))
    ext_ref[HALO16:HALO16 + TILE, :] = a_ref[...] * jax.nn.sigmoid(b_ref[...])
    ext_ref[HALO16 + TILE:, :] = jnp.where(last, 0.0, an_ref[...] * jax.nn.sigmoid(bn_ref[...]))
    acc = jnp.zeros((TILE, D_C), F32)
    for j in range(CONV_C):
        acc = acc + cw_ref[j:j + 1, :] * ext_ref[pl.ds(HALO16 - CONV_C // 2 + j, TILE), :]
    g = acc + cb_ref[...]
    mu = jnp.mean(g, axis=-1, keepdims=True)
    var = jnp.mean(jnp.square(g - mu), axis=-1, keepdims=True)
    g = _silu(((g - mu) * lax.rsqrt(var + EPS)) * lg_ref[...] + lb_ref[...])
    extp_ref[0:HALO8, :] = jnp.where(first, 0.0, pp_ref[...])
    extp_ref[HALO8:HALO8 + TILE, :] = pd_ref[...]
    extp_ref[HALO8 + TILE:, :] = jnp.where(last, 0.0, pn_ref[...])
    seq_len = jnp.where(i < P_TILES, SEQ, DEC_SEQ)
    pos = _seq_tile(i) * TILE + lax.broadcasted_iota(jnp.int32, (TILE, 1), 0)
    yd = []
    for gi, w in enumerate(POOL_WINDOWS):
        cols = slice(gi * D_DG, (gi + 1) * D_DG)
        s = jnp.zeros((TILE, D_DG), F32)
        for d in range(-(w // 2), w - w // 2):
            s = s + extp_ref[pl.ds(HALO8 + d, TILE), cols]
        lo = jnp.maximum(pos - w // 2, 0)
        hi = jnp.minimum(pos - w // 2 + w, seq_len)
        pooled = s / (hi - lo).astype(F32) - pd_ref[:, cols]
        yd.append(_dot(pooled.astype(BF16), wp_ref[gi]))
    yd = jnp.concatenate(yd, axis=-1) * ps_ref[...]
    y = _dot(g.astype(BF16), w_ref[0:D_C, :]) + _dot(yd.astype(BF16), w_ref[D_C:, :])
    out_ref[...] = x_ref[...] + g1_ref[pl.ds(r, 1), :] * y


def _odd_out(x, u, conv_w, conv_b, ln_g, ln_b, w_pool_bf16, p_scale, mod, w_out_bf16):
    tile_spec = lambda c: pl.BlockSpec((TILE, D_C), lambda i: (i, c))
    prev_spec = lambda rows, c: pl.BlockSpec((rows, D_C), lambda i: (_prev_block(i, rows), c))
    next_spec = lambda rows, c: pl.BlockSpec((rows, D_C), lambda i: (_next_block(i, rows), c))
    vec = lambda: pl.BlockSpec((1, D_C), lambda i: (0, 0))
    return pl.pallas_call(
        _odd_out_kernel,
        grid=(N_TILES,),
        in_specs=[
            pl.BlockSpec((TILE, D_MODEL), lambda i: (i, 0)),
            tile_spec(0), tile_spec(1), tile_spec(2),
            prev_spec(HALO16, 0), prev_spec(HALO16, 1), next_spec(HALO16, 0), next_spec(HALO16, 1),
            prev_spec(HALO8, 2), next_spec(HALO8, 2),
            pl.BlockSpec((CONV_C, D_C), lambda i: (0, 0)),
            vec(), vec(), vec(),
            pl.BlockSpec((len(POOL_WINDOWS), D_DG, D_DG), lambda i: (0, 0, 0)),
            vec(),
            pl.BlockSpec((8, D_MODEL), lambda i: (0, 2)),
            pl.BlockSpec((D_MODEL, D_MODEL), lambda i: (0, 0)),
        ],
        out_specs=pl.BlockSpec((TILE, D_MODEL), lambda i: (i, 0)),
        out_shape=jax.ShapeDtypeStruct((T_ALL, D_MODEL), F32),
        scratch_shapes=[pltpu.VMEM((TILE + 2 * HALO16, D_C), F32),
                        pltpu.VMEM((TILE + 2 * HALO8, D_D), F32)],
        compiler_params=_cparams("arbitrary"),
        name="odd_mixer_out",
    )(x, u, u, u, u, u, u, u, u, u, conv_w, conv_b.reshape(1, D_C), ln_g.reshape(1, D_C),
      ln_b.reshape(1, D_C), w_pool_bf16, p_scale.reshape(1, D_D), mod, w_out_bf16)


GROUP = N_EXPERTS // N_GROUPS
NEG_INF = float("-inf")


def _first_argmax(v, idx, axis):
    m = jnp.max(v, axis=axis, keepdims=True)
    big = jnp.int32(2 ** 30)
    am = jnp.min(jnp.where(v == m, idx, big), axis=axis, keepdims=True)
    return m, am


def _route(scores, biased):
    shape = biased.shape
    member = lax.broadcasted_iota(jnp.int32, shape, 1)
    m1, a1 = _first_argmax(biased, member, 1)
    m2 = jnp.max(jnp.where(member == a1, NEG_INF, biased), axis=1, keepdims=True)
    gscore = m1 + m2
    gidx = lax.broadcasted_iota(jnp.int32, gscore.shape, 0)
    gsel = jnp.zeros(gscore.shape, jnp.bool_)
    for _ in range(TOPK_GROUPS):
        _, am = _first_argmax(gscore, gidx, 0)
        hit = gidx == am
        gsel = jnp.logical_or(gsel, hit)
        gscore = jnp.where(hit, NEG_INF, gscore)
    cand = jnp.where(gsel, biased, NEG_INF)
    eidx = lax.broadcasted_iota(jnp.int32, shape, 0) * GROUP + member
    sel = jnp.zeros(shape, jnp.bool_)
    for _ in range(TOP_K):
        m = jnp.max(jnp.max(cand, axis=1, keepdims=True), axis=0, keepdims=True)
        big = jnp.int32(2 ** 30)
        am = jnp.where(cand == m, eidx, big)
        am = jnp.min(jnp.min(am, axis=1, keepdims=True), axis=0, keepdims=True)
        hit = eidx == am
        sel = jnp.logical_or(sel, hit)
        cand = jnp.where(hit, NEG_INF, cand)
    wsel = jnp.where(sel, scores, 0.0)
    tot = jnp.sum(jnp.sum(wsel, axis=1, keepdims=True), axis=0, keepdims=True)
    return wsel / tot * ROUTED_SCALE, sel


MOE_TC = 4096
N_CHUNKS = T_ALL // MOE_TC
ROW_TILE = 256
MAX_TILES = MOE_TC * TOP_K // ROW_TILE + N_EXPERTS
N_FLUSH = MOE_TC // TILE
N_BLK = MOE_TC // LANES
ROW_SUB = D_MODEL // LANES


def _moe_pre_kernel(x_ref, g_ref, sh_ref, sc_ref, wr_ref, br_ref, tri_ref, wsg_ref, wsu_ref,
                    wsd_ref, hrow_ref, gate_ref, rank_ref, shared_ref, carry_ref):
    i = pl.program_id(0)
    r = _mod_row(i, TILE)
    h = _modulated_norm(x_ref[...], g_ref[...], sh_ref[pl.ds(r, 1), :], sc_ref[pl.ds(r, 1), :])
    hb = h.astype(BF16)
    for s in range(ROW_SUB):
        hrow_ref[pl.ds(s, TILE, stride=ROW_SUB), :] = h[:, s * LANES:(s + 1) * LANES]
    h_hi, h_lo = hb, (h - hb.astype(F32)).astype(BF16)
    w_hi, w_lo = _split_bf16(wr_ref[...])
    logits = _dot_nt(w_hi, h_hi) + _dot_nt(w_lo, h_hi) + _dot_nt(w_hi, h_lo)
    scores = jax.nn.sigmoid(logits)
    biased = scores + br_ref[:, 0:1]
    shape3 = (N_GROUPS, GROUP, TILE)
    gate_t, sel = _route(scores.reshape(shape3), biased.reshape(shape3))
    gate_t = gate_t.reshape(N_EXPERTS, TILE)
    sel = jnp.where(sel.reshape(N_EXPERTS, TILE), 1.0, 0.0)

    @pl.when(i % N_FLUSH == 0)
    def _():
        carry_ref[...] = jnp.zeros_like(carry_ref)

    carry = carry_ref[...]
    local = _dot(sel.astype(BF16), tri_ref[...])
    rank = jnp.where(sel > 0.0, local + jnp.concatenate([carry] * (TILE // LANES), axis=1), -1.0)
    carry_ref[...] = carry + jnp.sum(sel, axis=1, keepdims=True)
    gate_ref[...] = gate_t
    rank_ref[...] = rank
    a = _silu(_dot(hb, wsg_ref[...])) * _dot(hb, wsu_ref[...])
    shared_ref[...] = _dot(a.astype(BF16), wsd_ref[...])


def _moe_pre(x, g, mod, w_router_t, b_router, wsg, wsu, wsd):
    return pl.pallas_call(
        _moe_pre_kernel,
        grid=(N_TILES,),
        in_specs=[
            pl.BlockSpec((TILE, D_MODEL), lambda i: (i, 0)),
            pl.BlockSpec((1, D_MODEL), lambda i: (0, 0)),
            pl.BlockSpec((8, D_MODEL), lambda i: (0, 3)),
            pl.BlockSpec((8, D_MODEL), lambda i: (0, 4)),
            pl.BlockSpec((N_EXPERTS, D_MODEL), lambda i: (0, 0)),
            pl.BlockSpec((N_EXPERTS, LANES), lambda i: (0, 0)),
            pl.BlockSpec((TILE, TILE), lambda i: (0, 0)),
            pl.BlockSpec((D_MODEL, D_SHARED), lambda i: (0, 0)),
            pl.BlockSpec((D_MODEL, D_SHARED), lambda i: (0, 0)),
            pl.BlockSpec((D_SHARED, D_MODEL), lambda i: (0, 0)),
        ],
        out_specs=[
            pl.BlockSpec((TILE * ROW_SUB, LANES), lambda i: (i, 0)),
            pl.BlockSpec((N_EXPERTS, TILE), lambda i: (0, i)),
            pl.BlockSpec((N_EXPERTS, TILE), lambda i: (0, i)),
            pl.BlockSpec((TILE, D_MODEL), lambda i: (i, 0)),
        ],
        out_shape=[
            jax.ShapeDtypeStruct((T_ALL * ROW_SUB, LANES), F32),
            jax.ShapeDtypeStruct((N_EXPERTS, T_ALL), F32),
            jax.ShapeDtypeStruct((N_EXPERTS, T_ALL), F32),
            jax.ShapeDtypeStruct((T_ALL, D_MODEL), F32),
        ],
        scratch_shapes=[pltpu.VMEM((N_EXPERTS, LANES), F32)],
        compiler_params=_cparams("arbitrary"),
        name="moe_pre",
    )(x, g.reshape(1, D_MODEL), mod, mod, w_router_t,
      jnp.broadcast_to(b_router.reshape(N_EXPERTS, 1), (N_EXPERTS, LANES)),
      jnp.asarray(np.triu(np.ones((TILE, TILE), np.float32), 1), BF16), wsg, wsu, wsd)


LIST_TILES = MOE_TC // ROW_TILE


def _tile_tables(rank_t):
    sel = (rank_t >= 0.0).astype(jnp.int32).reshape(N_EXPERTS, N_CHUNKS, MOE_TC)
    counts = jnp.sum(sel, axis=-1).T
    n_tile_e = (counts + ROW_TILE - 1) // ROW_TILE
    tile_end = jnp.cumsum(n_tile_e, axis=1)
    tile_start = tile_end - n_tile_e
    n_tiles = tile_end[:, -1:]
    j = jnp.arange(MAX_TILES, dtype=jnp.int32)[None, :]
    j_used = jnp.minimum(j, n_tiles - 1)
    t_exp = jnp.sum((tile_end[:, None, :] <= j_used[:, :, None]).astype(jnp.int32), axis=-1)
    pick = t_exp[:, :, None] == jnp.arange(N_EXPERTS, dtype=jnp.int32)[None, None, :]
    lookup = lambda tab: jnp.sum(jnp.where(pick, tab[:, None, :], 0), axis=-1)
    q_tile = j_used - lookup(tile_start)
    cnt = jnp.clip(lookup(counts) - q_tile * ROW_TILE, 0, ROW_TILE)
    cnt = jnp.where(j < n_tiles, cnt, 0)
    chunk = jnp.arange(N_CHUNKS, dtype=jnp.int32)[:, None]
    list_tile = (chunk * N_EXPERTS + t_exp) * LIST_TILES + q_tile
    flat = lambda a: a.reshape(-1).astype(jnp.int32)
    return flat(t_exp), flat(list_tile), flat(cnt), counts.astype(jnp.int32)


def _moe_sort_kernel(rank_ref, gate_ref, cnt_ref, src_ref, w_ref):
    rank = rank_ref[...]
    lane = lax.broadcasted_iota(jnp.int32, rank.shape, 1)
    d = jnp.where(rank >= 0.0, lane - rank.astype(jnp.int32), 0)
    w = gate_ref[...]
    for s in range(MOE_TC.bit_length() - 1):
        k = 1 << s
        d_in = pltpu.roll(d, MOE_TC - k, 1)
        w_in = pltpu.roll(w, MOE_TC - k, 1)
        take = (d_in & k) != 0
        leave = (d & k) != 0
        d = jnp.where(take, d_in, jnp.where(leave, 0, d))
        w = jnp.where(take, w_in, w)
    valid = lane < cnt_ref[0][:, 0:1]
    src_ref[...] = jnp.where(valid, (lane + d) * ROW_SUB, MOE_TC * ROW_SUB)
    w_ref[...] = jnp.where(valid, w, 0.0)


def _moe_sort(rank_t, gate_t, counts):
    chunk_spec = pl.BlockSpec((N_EXPERTS, MOE_TC), lambda c: (0, c))
    cnt_b = jnp.broadcast_to(counts[:, :, None], (N_CHUNKS, N_EXPERTS, LANES))
    src, wts = pl.pallas_call(
        _moe_sort_kernel,
        grid=(N_CHUNKS,),
        in_specs=[chunk_spec, chunk_spec,
                  pl.BlockSpec((1, N_EXPERTS, LANES), lambda c: (c, 0, 0))],
        out_specs=[pl.BlockSpec((N_EXPERTS, MOE_TC), lambda c: (c, 0))] * 2,
        out_shape=[jax.ShapeDtypeStruct((N_CHUNKS * N_EXPERTS, MOE_TC), jnp.int32),
                   jax.ShapeDtypeStruct((N_CHUNKS * N_EXPERTS, MOE_TC), F32)],
        compiler_params=_cparams("arbitrary"),
        name="moe_sort",
    )(rank_t, gate_t, cnt_b)
    n = N_CHUNKS * N_EXPERTS * LIST_TILES
    return src.reshape(n, 1, ROW_TILE), wts.reshape(n, 1, ROW_TILE)


SCATTER_BATCH = 16


def _moe_routed_kernel(te_ref, tl_ref, tc_ref, src_ref, wt_ref, h_ref, wg_ref, wu_ref, wd_ref,
                       x_ref, sh_ref, g2_ref, out_ref,
                       acc_ref, wgb_ref, wub_ref, wdb_ref, xs_ref, ys_ref):
    c = pl.program_id(0)
    j = pl.program_id(1)
    idx = c * MAX_TILES + jnp.minimum(j, MAX_TILES - 1)

    @pl.when(j == 0)
    def _():
        acc_ref[...] = jnp.zeros_like(acc_ref)

    @pl.when(jnp.logical_and(j < MAX_TILES, tc_ref[idx] > 0))
    def _():
        new_expert = jnp.logical_or(j == 0, te_ref[idx] != te_ref[jnp.maximum(idx - 1, 0)])

        @pl.when(new_expert)
        def _():
            wgb_ref[...] = wg_ref[0, 0].astype(BF16)
            wub_ref[...] = wu_ref[0, 0].astype(BF16)
            wdb_ref[...] = wd_ref[0, 0].astype(BF16)

        last_row = (MOE_TC - 1) * ROW_SUB
        for r in range(ROW_TILE):
            t = pl.multiple_of(jnp.minimum(src_ref[0, 0, r], last_row), ROW_SUB)
            xs_ref[r // 8, pl.ds(r % 8, ROW_SUB, stride=8), :] = h_ref[pl.ds(t, ROW_SUB), :]
        xt = jnp.concatenate(
            [xs_ref[:, s * 8:(s + 1) * 8, :].reshape(ROW_TILE, LANES) for s in range(ROW_SUB)],
            axis=1).astype(BF16)
        eye = (lax.broadcasted_iota(jnp.int32, (ROW_TILE, ROW_TILE), 0)
               == lax.broadcasted_iota(jnp.int32, (ROW_TILE, ROW_TILE), 1))
        wcol = jnp.sum(jnp.where(eye, wt_ref[0], 0.0), axis=1, keepdims=True)
        a = _silu(_dot(xt, wgb_ref[...])) * _dot(xt, wub_ref[...]) * wcol
        y = _dot(a.astype(BF16), wdb_ref[...])
        for s in range(ROW_SUB):
            ys_ref[:, s * 8:(s + 1) * 8, :] = y[:, s * LANES:(s + 1) * LANES].reshape(
                ROW_TILE // 8, 8, LANES)

        for r0 in range(0, ROW_TILE, SCATTER_BATCH):
            rows = range(r0, r0 + SCATTER_BATCH)
            toks = [pl.multiple_of(src_ref[0, 0, r], ROW_SUB) for r in rows]
            olds = [acc_ref[pl.ds(t, ROW_SUB), :] for t in toks]
            news = [old + ys_ref[r // 8, pl.ds(r % 8, ROW_SUB, stride=8), :]
                    for old, r in zip(olds, rows)]
            for t, new in zip(toks, news):
                acc_ref[pl.ds(t, ROW_SUB), :] = new

    @pl.when(j >= MAX_TILES)
    def _():
        base = (j - MAX_TILES) * (TILE * ROW_SUB)
        moe = jnp.concatenate(
            [acc_ref[pl.ds(base + s, TILE, stride=ROW_SUB), :] for s in range(ROW_SUB)], axis=1)
        out_ref[...] = x_ref[...] + g2_ref[pl.ds(c, 1), :] * (moe + sh_ref[...])


def _moe_routed(src, wts, t_exp, t_list, t_cnt, hrows, w_gate, w_up, w_down, li, x, shared, mod):
    def tile_idx(c, j):
        return c * MAX_TILES + jnp.minimum(j, MAX_TILES - 1)

    def flush_blk(c, j):
        return c * N_FLUSH + jnp.maximum(j - MAX_TILES, 0)

    list_map = lambda c, j, te, tl, tc: (tl[tile_idx(c, j)], 0, 0)
    w_map = lambda c, j, te, tl, tc: (li, te[tile_idx(c, j)], 0, 0)
    w_in_spec = pl.BlockSpec((1, 1, D_MODEL, D_EXPERT), w_map)
    w_out_spec = pl.BlockSpec((1, 1, D_EXPERT, D_MODEL), w_map)
    tok_spec = pl.BlockSpec((TILE, D_MODEL), lambda c, j, te, tl, tc: (flush_blk(c, j), 0))
    grid_spec = pltpu.PrefetchScalarGridSpec(
        num_scalar_prefetch=3,
        grid=(N_CHUNKS, MAX_TILES + N_FLUSH),
        in_specs=[
            pl.BlockSpec((1, 1, ROW_TILE), list_map, memory_space=pltpu.SMEM),
            pl.BlockSpec((1, 1, ROW_TILE), list_map),
            pl.BlockSpec((MOE_TC * ROW_SUB, LANES), lambda c, j, te, tl, tc: (c, 0),
                         pipeline_mode=pl.Buffered(1)),
            w_in_spec, w_in_spec, w_out_spec,
            tok_spec, tok_spec,
            pl.BlockSpec((8, D_MODEL), lambda c, j, te, tl, tc: (0, 5)),
        ],
        out_specs=tok_spec,
        scratch_shapes=[
            pltpu.VMEM(((MOE_TC + 1) * ROW_SUB, LANES), F32),
            pltpu.VMEM((D_MODEL, D_EXPERT), BF16),
            pltpu.VMEM((D_MODEL, D_EXPERT), BF16),
            pltpu.VMEM((D_EXPERT, D_MODEL), BF16),
            pltpu.VMEM((ROW_TILE // 8, 8 * ROW_SUB, LANES), F32),
            pltpu.VMEM((ROW_TILE // 8, 8 * ROW_SUB, LANES), F32),
        ],
    )
    return pl.pallas_call(
        _moe_routed_kernel,
        grid_spec=grid_spec,
        out_shape=jax.ShapeDtypeStruct((T_ALL, D_MODEL), F32),
        compiler_params=_cparams("arbitrary", "arbitrary"),
        name="moe_routed",
    )(t_exp, t_list, t_cnt, src, wts, hrows, w_gate, w_up, w_down, x, shared, mod)


def kernel(x_prompt, x_sample, cache_k, cache_v, c, c_ctx, w_mod, b_mod, norm1, norm2, w_in_even, conv_a, q_norm, k_norm, lam_q1, lam_k1, lam_q2, lam_k2, subln, w_out_even, w_in_odd, conv_c, conv_c_b, ln_c_g, ln_c_b, w_pool, pool_scale, w_out_odd, w_router, b_router, w_gate, w_up, w_down, ws_gate, ws_up, ws_down):
    x = jnp.concatenate([x_prompt.reshape(T_P, D_MODEL), x_sample.reshape(T_S, D_MODEL)], axis=0)
    cond = jnp.concatenate([c_ctx[None, :], c, jnp.zeros((8 - 1 - DEC_BATCH, D_MODEL), F32)], axis=0)
    mod_all = _modulation(cond, w_mod, b_mod)

    new_k = new_v = None
    for li in range(DEPTH):
        mod = mod_all[li]
        if li % 2 == 0:
            e = li // 2
            lam_init = 0.8 - 0.6 * math.exp(-0.3 * li)
            u = _norm_in(x, norm1[li], mod, w_in_even[e].astype(BF16))
            q, k, v = _qkv_prep(u, q_norm[e], k_norm[e])
            if new_k is None:
                new_k, new_v = k, v
            o = _attention(q, k, v, cache_k, cache_v, e,
                           (lam_q1[e], lam_k1[e], lam_q2[e], lam_k2[e]), subln[e], lam_init)
            x = _even_out(x, u, o, conv_a[e], mod, w_out_even[e].astype(BF16))
        else:
            o_ = li // 2
            u = _norm_in(x, norm1[li], mod, w_in_odd[o_].astype(BF16))
            x = _odd_out(x, u, conv_c[o_], conv_c_b[o_], ln_c_g[o_], ln_c_b[o_],
                         w_pool[o_].astype(BF16), pool_scale[o_], mod, w_out_odd[o_].astype(BF16))
        hrows, gate_b, rank_b, shared = _moe_pre(
            x, norm2[li], mod, w_router[li].T, b_router[li], ws_gate[li].astype(BF16),
            ws_up[li].astype(BF16), ws_down[li].astype(BF16))
        t_exp, t_list, t_cnt, counts = _tile_tables(rank_b)
        src, wts = _moe_sort(rank_b, gate_b, counts)
        x = _moe_routed(src, wts, t_exp, t_list, t_cnt, hrows, w_gate, w_up, w_down, li,
                        x, shared, mod)

    y_prompt = x[:T_P].reshape(BATCH, SEQ, D_MODEL)
    y_sample = x[T_P:].reshape(DEC_BATCH, DEC_SEQ, D_MODEL)

    def to_cache(a):
        a = a[:, :T_P].reshape(H_B, BATCH, SEQ, LANES)
        return jnp.transpose(a, (1, 0, 2, 3))[:, None]

    return (y_prompt, y_sample, to_cache(new_k), to_cache(new_v))
```

```python
import functools
import math

import numpy as np
import jax
import jax.numpy as jnp
from jax import lax
from jax.experimental import pallas as pl
from jax.experimental.pallas import tpu as pltpu

D_MODEL = 1024
BATCH = 16
SEQ = 256
DEPTH = 2
DEC_BATCH = 2
DEC_SEQ = 4096
PAST_LEN = 512
GRID_W = 64
H_B = 4
DK_B = 64
DV_B = 2 * DK_B
D_A = D_MODEL // 2
D_B = H_B * DV_B
D_C = D_MODEL // 2
D_D = D_MODEL // 2
CONV_C = 31
POOL_WINDOWS = (2, 4, 8, 16)
D_DG = D_D // len(POOL_WINDOWS)
N_EXPERTS = 64
TOP_K = 8
N_GROUPS = 8
TOPK_GROUPS = 4
D_EXPERT = 256
D_SHARED = 256
ROUTED_SCALE = 2.5
ROPE_BASE = 10000.0
EPS = 1e-6

F32 = jnp.float32
BF16 = jnp.bfloat16

T_P = BATCH * SEQ
T_S = DEC_BATCH * DEC_SEQ
T_ALL = T_P + T_S
TILE = 256
N_TILES = T_ALL // TILE
P_TILES = T_P // TILE
S_TILES = DEC_SEQ // TILE
LANES = 128
VMEM_LIMIT = 56 * 1024 * 1024


def _cparams(*sem):
    return pltpu.CompilerParams(dimension_semantics=sem, vmem_limit_bytes=VMEM_LIMIT)


def _mod_row(i, tm):
    npt = T_P // tm
    per = DEC_SEQ // tm
    return jnp.where(i < npt, 0, 1 + (i - npt) // per)


def _seq_flags(i):
    j = (i - P_TILES) % S_TILES
    first = jnp.logical_or(i < P_TILES, j == 0)
    last = jnp.logical_or(i < P_TILES, j == S_TILES - 1)
    return first, last


def _seq_tile(i):
    return jnp.where(i < P_TILES, 0, (i - P_TILES) % S_TILES)


def _split_bf16(a):
    hi = a.astype(BF16)
    lo = (a - hi.astype(F32)).astype(BF16)
    return hi, lo


def _dot(a, b):
    return jnp.dot(a, b, preferred_element_type=F32)


def _dot_nt(a, b):
    return lax.dot_general(a, b, (((1,), (1,)), ((), ())), preferred_element_type=F32)


def _dot3(a, b):
    a_hi, a_lo = _split_bf16(a)
    b_hi, b_lo = _split_bf16(b)
    return _dot(a_hi, b_hi) + _dot(a_lo, b_hi) + _dot(a_hi, b_lo)


def _silu(x):
    return x * jax.nn.sigmoid(x)


MOD_TN = 1536


def _mod_kernel(c_ref, w_ref, b_ref, o_ref):
    o_ref[0] = _dot3(_silu(c_ref[...]), w_ref[0]) + b_ref[0]


def _modulation(cond, w_mod, b_mod):
    n = 6 * D_MODEL
    return pl.pallas_call(
        _mod_kernel,
        grid=(DEPTH, n // MOD_TN),
        in_specs=[
            pl.BlockSpec((8, D_MODEL), lambda l, j: (0, 0)),
            pl.BlockSpec((1, D_MODEL, MOD_TN), lambda l, j: (l, 0, j)),
            pl.BlockSpec((1, 1, MOD_TN), lambda l, j: (l, 0, j)),
        ],
        out_specs=pl.BlockSpec((1, 8, MOD_TN), lambda l, j: (l, 0, j)),
        out_shape=jax.ShapeDtypeStruct((DEPTH, 8, n), F32),
        compiler_params=_cparams("arbitrary", "arbitrary"),
        name="modulation",
    )(cond, w_mod, b_mod.reshape(DEPTH, 1, n))


IN_TM = 512


def _modulated_norm(x, g, shift, scale):
    ms = jnp.mean(x * x, axis=-1, keepdims=True)
    return (x * lax.rsqrt(ms + EPS) * g) * (1.0 + scale) + shift


def _norm_in_kernel(x_ref, g_ref, sh_ref, sc_ref, w_ref, o_ref):
    r = _mod_row(pl.program_id(0), IN_TM)
    h = _modulated_norm(x_ref[...], g_ref[...], sh_ref[pl.ds(r, 1), :], sc_ref[pl.ds(r, 1), :])
    o_ref[...] = _dot(h.astype(BF16), w_ref[...])


def _norm_in(x, g, mod, w_bf16):
    n = w_bf16.shape[1]
    return pl.pallas_call(
        _norm_in_kernel,
        grid=(T_ALL // IN_TM,),
        in_specs=[
            pl.BlockSpec((IN_TM, D_MODEL), lambda i: (i, 0)),
            pl.BlockSpec((1, D_MODEL), lambda i: (0, 0)),
            pl.BlockSpec((8, D_MODEL), lambda i: (0, 0)),
            pl.BlockSpec((8, D_MODEL), lambda i: (0, 1)),
            pl.BlockSpec((D_MODEL, n), lambda i: (0, 0)),
        ],
        out_specs=pl.BlockSpec((IN_TM, n), lambda i: (i, 0)),
        out_shape=jax.ShapeDtypeStruct((T_ALL, n), F32),
        compiler_params=_cparams("arbitrary"),
        name="norm_in_proj",
    )(x, g.reshape(1, D_MODEL), mod, mod, w_bf16)


def _rope_tables():
    half = DK_B // 2
    freqs = ROPE_BASE ** (-np.arange(0, half, 2, dtype=np.float64) / half)
    l = np.arange(DEC_SEQ)
    pos_r = (l // GRID_W).astype(np.float64)
    pos_c = (l % GRID_W).astype(np.float64)
    lane = np.arange(LANES)
    jj = lane % DK_B
    m = jj % half
    f = m % (half // 2)
    pos = np.where((jj < half)[None, :], pos_r[:, None], pos_c[:, None])
    ang = pos * freqs[f][None, :]
    sign = np.where(m < half // 2, -1.0, 1.0)[None, :]
    cos = np.concatenate([np.ones((TILE, LANES)), np.cos(ang)], axis=0)
    sin = np.concatenate([np.zeros((TILE, LANES)), sign * np.sin(ang)], axis=0)
    return cos.astype(np.float32), sin.astype(np.float32)


def _segment_mean_matrix():
    lane = np.arange(LANES)
    same = (lane[:, None] // DK_B) == (lane[None, :] // DK_B)
    return (same.astype(np.float32) / DK_B)


def _qk_prep(x, g, cos, sin, seg):
    x2 = x * x
    hi, lo = _split_bf16(x2)
    ms = _dot(hi, seg) + _dot(lo, seg)
    y = x * lax.rsqrt(ms + EPS) * g
    lane = lax.broadcasted_iota(jnp.int32, y.shape, 1)
    lower = (lane % (DK_B // 2)) < (DK_B // 4)
    partner = jnp.where(lower, pltpu.roll(y, LANES - DK_B // 4, 1), pltpu.roll(y, DK_B // 4, 1))
    return y * cos + partner * sin


def _qkv_prep_kernel(q_ref, k_ref, v_ref, qn_ref, kn_ref, cos_ref, sin_ref, seg_ref,
                     qo_ref, ko_ref, vo_ref):
    cos = cos_ref[...]
    sin = sin_ref[...]
    seg = seg_ref[...]
    scale = 1.0 / math.sqrt(DK_B)
    qo_ref[0] = _qk_prep(q_ref[...], qn_ref[...], cos, sin, seg) * scale
    ko_ref[0] = _qk_prep(k_ref[...], kn_ref[...], cos, sin, seg)
    vo_ref[0] = v_ref[...]


def _qkv_prep(u, qn, kn):
    cos, sin = _rope_tables()
    seg = jnp.asarray(_segment_mean_matrix(), BF16)
    qn2 = jnp.concatenate([qn, qn]).reshape(1, LANES)
    kn2 = jnp.concatenate([kn, kn]).reshape(1, LANES)
    col0 = 3 * D_A // LANES

    def tab_map(i, h):
        return (jnp.where(i < P_TILES, 0, 1 + (i - P_TILES) % S_TILES), 0)

    out = jax.ShapeDtypeStruct((H_B, T_ALL, LANES), F32)
    return pl.pallas_call(
        _qkv_prep_kernel,
        grid=(N_TILES, H_B),
        in_specs=[
            pl.BlockSpec((TILE, LANES), lambda i, h: (i, col0 + h)),
            pl.BlockSpec((TILE, LANES), lambda i, h: (i, col0 + H_B + h)),
            pl.BlockSpec((TILE, LANES), lambda i, h: (i, col0 + 2 * H_B + h)),
            pl.BlockSpec((1, LANES), lambda i, h: (0, 0)),
            pl.BlockSpec((1, LANES), lambda i, h: (0, 0)),
            pl.BlockSpec((TILE, LANES), tab_map),
            pl.BlockSpec((TILE, LANES), tab_map),
            pl.BlockSpec((LANES, LANES), lambda i, h: (0, 0)),
        ],
        out_specs=[pl.BlockSpec((1, TILE, LANES), lambda i, h: (h, i, 0))] * 3,
        out_shape=[out, out, out],
        compiler_params=_cparams("arbitrary", "arbitrary"),
        name="qkv_prep",
    )(u, u, u, qn2, kn2, jnp.asarray(cos), jnp.asarray(sin), seg)


def _lambda(lq1_ref, lk1_ref, lq2_ref, lk2_ref, lam_init):
    a = jnp.sum(lq1_ref[...] * lk1_ref[...], axis=-1, keepdims=True)
    b = jnp.sum(lq2_ref[...] * lk2_ref[...], axis=-1, keepdims=True)
    return jnp.exp(a) - jnp.exp(b) + lam_init


def _attn_body(q, keys, vals, lam, subg, lam_init):
    lane = lax.broadcasted_iota(jnp.int32, q.shape, 1)
    qa = jnp.where(lane < DK_B, q, 0.0).astype(BF16)
    qb = jnp.where(lane < DK_B, 0.0, q).astype(BF16)
    out = None
    for qq, coef_sign in ((qa, None), (qb, lam)):
        ss = [_dot_nt(qq, k) for k in keys]
        m = functools.reduce(jnp.maximum, [jnp.max(s, axis=-1, keepdims=True) for s in ss])
        ps = [jnp.exp(s - m) for s in ss]
        l = functools.reduce(jnp.add, [jnp.sum(p, axis=-1, keepdims=True) for p in ps])
        w = 1.0 / l if coef_sign is None else -coef_sign / l
        ps = [p * w for p in ps]
        out = ps if out is None else [a + b for a, b in zip(out, ps)]
    o = functools.reduce(jnp.add, [_dot(a.astype(BF16), v) for a, v in zip(out, vals)])
    ms = jnp.mean(o * o, axis=-1, keepdims=True)
    return (o * lax.rsqrt(ms + EPS) * subg) * (1.0 - lam_init)


def _attn_prompt_kernel(q_ref, k_ref, v_ref, lq1, lk1, lq2, lk2, sg_ref, o_ref, *, lam_init):
    lam = _lambda(lq1, lk1, lq2, lk2, lam_init)
    o_ref[...] = _attn_body(q_ref[0], [k_ref[0].astype(BF16)], [v_ref[0].astype(BF16)],
                            lam, sg_ref[...], lam_init)


def _attn_latent_kernel(q_ref, k_ref, v_ref, ck_ref, cv_ref, lq1, lk1, lq2, lk2, sg_ref, o_ref,
                        *, lam_init):
    lam = _lambda(lq1, lk1, lq2, lk2, lam_init)
    keys = [ck_ref[0, 0, 0].astype(BF16), k_ref[0].astype(BF16)]
    vals = [cv_ref[0, 0, 0].astype(BF16), v_ref[0].astype(BF16)]
    o_ref[...] = _attn_body(q_ref[0], keys, vals, lam, sg_ref[...], lam_init)


def _attention(q, k, v, cache_k, cache_v, e, lam_params, subg, lam_init):
    small = [p.reshape(1, DK_B) for p in lam_params] + [subg.reshape(1, DV_B)]
    small_specs2 = [pl.BlockSpec((1, DK_B), lambda b, h: (0, 0))] * 4 + \
                   [pl.BlockSpec((1, DV_B), lambda b, h: (0, 0))]
    small_specs3 = [pl.BlockSpec((1, DK_B), lambda s, h, j: (0, 0))] * 4 + \
                   [pl.BlockSpec((1, DV_B), lambda s, h, j: (0, 0))]
    o_prompt = pl.pallas_call(
        functools.partial(_attn_prompt_kernel, lam_init=lam_init),
        grid=(BATCH, H_B),
        in_specs=[pl.BlockSpec((1, SEQ, LANES), lambda b, h: (h, b, 0))] * 3 + small_specs2,
        out_specs=pl.BlockSpec((SEQ, LANES), lambda b, h: (b, h)),
        out_shape=jax.ShapeDtypeStruct((T_P, D_B), F32),
        compiler_params=_cparams("arbitrary", "arbitrary"),
        name="attn_prompt",
    )(q, k, v, *small)

    kv_spec = pl.BlockSpec((1, DEC_SEQ, LANES), lambda s, h, j: (h, T_P // DEC_SEQ + s, 0))
    c_spec = pl.BlockSpec((1, 1, 1, PAST_LEN, LANES), lambda s, h, j: (s, e, h, 0, 0))
    o_latent = pl.pallas_call(
        functools.partial(_attn_latent_kernel, lam_init=lam_init),
        grid=(DEC_BATCH, H_B, S_TILES),
        in_specs=[pl.BlockSpec((1, TILE, LANES), lambda s, h, j: (h, P_TILES + s * S_TILES + j, 0)),
                  kv_spec, kv_spec, c_spec, c_spec] + small_specs3,
        out_specs=pl.BlockSpec((TILE, LANES), lambda s, h, j: (s * S_TILES + j, h)),
        out_shape=jax.ShapeDtypeStruct((T_S, D_B), F32),
        compiler_params=_cparams("arbitrary", "arbitrary", "arbitrary"),
        name="attn_latent",
    )(q, k, v, cache_k, cache_v, *small)
    return jnp.concatenate([o_prompt, o_latent], axis=0)


HALO8 = 8
HALO16 = 16


def _prev_block(i, rows):
    return jnp.maximum(i * (TILE // rows) - 1, 0)


def _next_block(i, rows):
    return jnp.minimum((i + 1) * (TILE // rows), T_ALL // rows - 1)


def _even_out_kernel(x_ref, bg_ref, cg_ref, hin_ref, cgp_ref, hinp_ref, cgn_ref, hinn_ref,
                     o_ref, cw_ref, g1_ref, w_ref, out_ref):
    i = pl.program_id(0)
    first, last = _seq_flags(i)
    r = _mod_row(i, TILE)
    z = cg_ref[...] * hin_ref[...]
    zp = jnp.where(first, 0.0, cgp_ref[HALO8 - 1:HALO8, :] * hinp_ref[HALO8 - 1:HALO8, :])
    zn = jnp.where(last, 0.0, cgn_ref[0:1, :] * hinn_ref[0:1, :])
    row = lax.broadcasted_iota(jnp.int32, z.shape, 0)
    z_prev = jnp.where(row == 0, zp, pltpu.roll(z, 1, 0))
    z_next = jnp.where(row == TILE - 1, zn, pltpu.roll(z, TILE - 1, 0))
    cw = cw_ref[...]
    ya = bg_ref[...] * (cw[0:1, :] * z_prev + cw[1:2, :] * z + cw[2:3, :] * z_next)
    y = _dot(ya.astype(BF16), w_ref[0:D_A, :]) + _dot(o_ref[...].astype(BF16), w_ref[D_A:, :])
    out_ref[...] = x_ref[...] + g1_ref[pl.ds(r, 1), :] * y


def _even_out(x, u, o, conv_w, mod, w_out_bf16):
    tile_spec = lambda c: pl.BlockSpec((TILE, D_A), lambda i: (i, c))
    prev_spec = lambda c: pl.BlockSpec((HALO8, D_A), lambda i: (_prev_block(i, HALO8), c))
    next_spec = lambda c: pl.BlockSpec((HALO8, D_A), lambda i: (_next_block(i, HALO8), c))
    return pl.pallas_call(
        _even_out_kernel,
        grid=(N_TILES,),
        in_specs=[
            pl.BlockSpec((TILE, D_MODEL), lambda i: (i, 0)),
            tile_spec(0), tile_spec(1), tile_spec(2),
            prev_spec(1), prev_spec(2), next_spec(1), next_spec(2),
            pl.BlockSpec((TILE, D_B), lambda i: (i, 0)),
            pl.BlockSpec((3, D_A), lambda i: (0, 0)),
            pl.BlockSpec((8, D_MODEL), lambda i: (0, 2)),
            pl.BlockSpec((D_MODEL, D_MODEL), lambda i: (0, 0)),
        ],
        out_specs=pl.BlockSpec((TILE, D_MODEL), lambda i: (i, 0)),
        out_shape=jax.ShapeDtypeStruct((T_ALL, D_MODEL), F32),
        compiler_params=_cparams("arbitrary"),
        name="even_mixer_out",
    )(x, u, u, u, u, u, u, u, o, conv_w, mod, w_out_bf16)


def _odd_out_kernel(x_ref, a_ref, b_ref, pd_ref, ap_ref, bp_ref, an_ref, bn_ref, pp_ref, pn_ref,
                    cw_ref, cb_ref, lg_ref, lb_ref, wp_ref, ps_ref, g1_ref, w_ref, out_ref,
                    ext_ref, extp_ref):
    i = pl.program_id(0)
    first, last = _seq_flags(i)
    r = _mod_row(i, TILE)
    ext_ref[0:HALO16, :] = jnp.where(first, 0.0, ap_ref[...] * jax.nn.sigmoid(bp_ref[...]))
    ext_ref[HALO16:HALO16 + TILE, :] = a_ref[...] * jax.nn.sigmoid(b_ref[...])
    ext_ref[HALO16 + TILE:, :] = jnp.where(last, 0.0, an_ref[...] * jax.nn.sigmoid(bn_ref[...]))
    acc = jnp.zeros((TILE, D_C), F32)
    for j in range(CONV_C):
        acc = acc + cw_ref[j:j + 1, :] * ext_ref[pl.ds(HALO16 - CONV_C // 2 + j, TILE), :]
    g = acc + cb_ref[...]
    mu = jnp.mean(g, axis=-1, keepdims=True)
    var = jnp.mean(jnp.square(g - mu), axis=-1, keepdims=True)
    g = _silu(((g - mu) * lax.rsqrt(var + EPS)) * lg_ref[...] + lb_ref[...])
    extp_ref[0:HALO8, :] = jnp.where(first, 0.0, pp_ref[...])
    extp_ref[HALO8:HALO8 + TILE, :] = pd_ref[...]
    extp_ref[HALO8 + TILE:, :] = jnp.where(last, 0.0, pn_ref[...])
    seq_len = jnp.where(i < P_TILES, SEQ, DEC_SEQ)
    pos = _seq_tile(i) * TILE + lax.broadcasted_iota(jnp.int32, (TILE, 1), 0)
    yd = []
    for gi, w in enumerate(POOL_WINDOWS):
        cols = slice(gi * D_DG, (gi + 1) * D_DG)
        s = jnp.zeros((TILE, D_DG), F32)
        for d in range(-(w // 2), w - w // 2):
            s = s + extp_ref[pl.ds(HALO8 + d, TILE), cols]
        lo = jnp.maximum(pos - w // 2, 0)
        hi = jnp.minimum(pos - w // 2 + w, seq_len)
        pooled = s / (hi - lo).astype(F32) - pd_ref[:, cols]
        yd.append(_dot(pooled.astype(BF16), wp_ref[gi]))
    yd = jnp.concatenate(yd, axis=-1) * ps_ref[...]
    y = _dot(g.astype(BF16), w_ref[0:D_C, :]) + _dot(yd.astype(BF16), w_ref[D_C:, :])
    out_ref[...] = x_ref[...] + g1_ref[pl.ds(r, 1), :] * y


def _odd_out(x, u, conv_w, conv_b, ln_g, ln_b, w_pool_bf16, p_scale, mod, w_out_bf16):
    tile_spec = lambda c: pl.BlockSpec((TILE, D_C), lambda i: (i, c))
    prev_spec = lambda rows, c: pl.BlockSpec((rows, D_C), lambda i: (_prev_block(i, rows), c))
    next_spec = lambda rows, c: pl.BlockSpec((rows, D_C), lambda i: (_next_block(i, rows), c))
    vec = lambda: pl.BlockSpec((1, D_C), lambda i: (0, 0))
    return pl.pallas_call(
        _odd_out_kernel,
        grid=(N_TILES,),
        in_specs=[
            pl.BlockSpec((TILE, D_MODEL), lambda i: (i, 0)),
            tile_spec(0), tile_spec(1), tile_spec(2),
            prev_spec(HALO16, 0), prev_spec(HALO16, 1), next_spec(HALO16, 0), next_spec(HALO16, 1),
            prev_spec(HALO8, 2), next_spec(HALO8, 2),
            pl.BlockSpec((CONV_C, D_C), lambda i: (0, 0)),
            vec(), vec(), vec(),
            pl.BlockSpec((len(POOL_WINDOWS), D_DG, D_DG), lambda i: (0, 0, 0)),
            vec(),
            pl.BlockSpec((8, D_MODEL), lambda i: (0, 2)),
            pl.BlockSpec((D_MODEL, D_MODEL), lambda i: (0, 0)),
        ],
        out_specs=pl.BlockSpec((TILE, D_MODEL), lambda i: (i, 0)),
        out_shape=jax.ShapeDtypeStruct((T_ALL, D_MODEL), F32),
        scratch_shapes=[pltpu.VMEM((TILE + 2 * HALO16, D_C), F32),
                        pltpu.VMEM((TILE + 2 * HALO8, D_D), F32)],
        compiler_params=_cparams("arbitrary"),
        name="odd_mixer_out",
    )(x, u, u, u, u, u, u, u, u, u, conv_w, conv_b.reshape(1, D_C), ln_g.reshape(1, D_C),
      ln_b.reshape(1, D_C), w_pool_bf16, p_scale.reshape(1, D_D), mod, w_out_bf16)


GROUP = N_EXPERTS // N_GROUPS
NEG_INF = float("-inf")


def _first_argmax(v, idx, axis):
    m = jnp.max(v, axis=axis, keepdims=True)
    big = jnp.int32(2 ** 30)
    am = jnp.min(jnp.where(v == m, idx, big), axis=axis, keepdims=True)
    return m, am


def _route(scores, biased):
    shape = biased.shape
    member = lax.broadcasted_iota(jnp.int32, shape, 1)
    m1, a1 = _first_argmax(biased, member, 1)
    m2 = jnp.max(jnp.where(member == a1, NEG_INF, biased), axis=1, keepdims=True)
    gscore = m1 + m2
    gidx = lax.broadcasted_iota(jnp.int32, gscore.shape, 0)
    gsel = jnp.zeros(gscore.shape, jnp.bool_)
    for _ in range(TOPK_GROUPS):
        _, am = _first_argmax(gscore, gidx, 0)
        hit = gidx == am
        gsel = jnp.logical_or(gsel, hit)
        gscore = jnp.where(hit, NEG_INF, gscore)
    cand = jnp.where(gsel, biased, NEG_INF)
    eidx = lax.broadcasted_iota(jnp.int32, shape, 0) * GROUP + member
    sel = jnp.zeros(shape, jnp.bool_)
    for _ in range(TOP_K):
        m = jnp.max(jnp.max(cand, axis=1, keepdims=True), axis=0, keepdims=True)
        big = jnp.int32(2 ** 30)
        am = jnp.where(cand == m, eidx, big)
        am = jnp.min(jnp.min(am, axis=1, keepdims=True), axis=0, keepdims=True)
        hit = eidx == am
        sel = jnp.logical_or(sel, hit)
        cand = jnp.where(hit, NEG_INF, cand)
    wsel = jnp.where(sel, scores, 0.0)
    tot = jnp.sum(jnp.sum(wsel, axis=1, keepdims=True), axis=0, keepdims=True)
    return wsel / tot * ROUTED_SCALE, sel


MOE_TC = 4096
N_CHUNKS = T_ALL // MOE_TC
ROW_TILE = 256
MAX_TILES = MOE_TC * TOP_K // ROW_TILE + N_EXPERTS
N_FLUSH = MOE_TC // TILE
N_BLK = MOE_TC // LANES
ROW_SUB = D_MODEL // LANES


def _moe_pre_kernel(x_ref, g_ref, sh_ref, sc_ref, wr_ref, br_ref, tri_ref, wsg_ref, wsu_ref,
                    wsd_ref, hrow_ref, gate_ref, rank_ref, shared_ref, carry_ref):
    i = pl.program_id(0)
    r = _mod_row(i, TILE)
    h = _modulated_norm(x_ref[...], g_ref[...], sh_ref[pl.ds(r, 1), :], sc_ref[pl.ds(r, 1), :])
    hb = h.astype(BF16)
    for s in range(ROW_SUB):
        hrow_ref[pl.ds(s, TILE, stride=ROW_SUB), :] = h[:, s * LANES:(s + 1) * LANES]
    h_hi, h_lo = hb, (h - hb.astype(F32)).astype(BF16)
    w_hi, w_lo = _split_bf16(wr_ref[...])
    logits = _dot_nt(w_hi, h_hi) + _dot_nt(w_lo, h_hi) + _dot_nt(w_hi, h_lo)
    scores = jax.nn.sigmoid(logits)
    biased = scores + br_ref[:, 0:1]
    shape3 = (N_GROUPS, GROUP, TILE)
    gate_t, sel = _route(scores.reshape(shape3), biased.reshape(shape3))
    gate_t = gate_t.reshape(N_EXPERTS, TILE)
    sel = jnp.where(sel.reshape(N_EXPERTS, TILE), 1.0, 0.0)

    @pl.when(i % N_FLUSH == 0)
    def _():
        carry_ref[...] = jnp.zeros_like(carry_ref)

    carry = carry_ref[...]
    local = _dot(sel.astype(BF16), tri_ref[...])
    rank = jnp.where(sel > 0.0, local + jnp.concatenate([carry] * (TILE // LANES), axis=1), -1.0)
    carry_ref[...] = carry + jnp.sum(sel, axis=1, keepdims=True)
    gate_ref[...] = gate_t
    rank_ref[...] = rank
    a = _silu(_dot(hb, wsg_ref[...])) * _dot(hb, wsu_ref[...])
    shared_ref[...] = _dot(a.astype(BF16), wsd_ref[...])


def _moe_pre(x, g, mod, w_router_t, b_router, wsg, wsu, wsd):
    return pl.pallas_call(
        _moe_pre_kernel,
        grid=(N_TILES,),
        in_specs=[
            pl.BlockSpec((TILE, D_MODEL), lambda i: (i, 0)),
            pl.BlockSpec((1, D_MODEL), lambda i: (0, 0)),
            pl.BlockSpec((8, D_MODEL), lambda i: (0, 3)),
            pl.BlockSpec((8, D_MODEL), lambda i: (0, 4)),
            pl.BlockSpec((N_EXPERTS, D_MODEL), lambda i: (0, 0)),
            pl.BlockSpec((N_EXPERTS, LANES), lambda i: (0, 0)),
            pl.BlockSpec((TILE, TILE), lambda i: (0, 0)),
            pl.BlockSpec((D_MODEL, D_SHARED), lambda i: (0, 0)),
            pl.BlockSpec((D_MODEL, D_SHARED), lambda i: (0, 0)),
            pl.BlockSpec((D_SHARED, D_MODEL), lambda i: (0, 0)),
        ],
        out_specs=[
            pl.BlockSpec((TILE * ROW_SUB, LANES), lambda i: (i, 0)),
            pl.BlockSpec((N_EXPERTS, TILE), lambda i: (0, i)),
            pl.BlockSpec((N_EXPERTS, TILE), lambda i: (0, i)),
            pl.BlockSpec((TILE, D_MODEL), lambda i: (i, 0)),
        ],
        out_shape=[
            jax.ShapeDtypeStruct((T_ALL * ROW_SUB, LANES), F32),
            jax.ShapeDtypeStruct((N_EXPERTS, T_ALL), F32),
            jax.ShapeDtypeStruct((N_EXPERTS, T_ALL), F32),
            jax.ShapeDtypeStruct((T_ALL, D_MODEL), F32),
        ],
        scratch_shapes=[pltpu.VMEM((N_EXPERTS, LANES), F32)],
        compiler_params=_cparams("arbitrary"),
        name="moe_pre",
    )(x, g.reshape(1, D_MODEL), mod, mod, w_router_t,
      jnp.broadcast_to(b_router.reshape(N_EXPERTS, 1), (N_EXPERTS, LANES)),
      jnp.asarray(np.triu(np.ones((TILE, TILE), np.float32), 1), BF16), wsg, wsu, wsd)


LIST_TILES = MOE_TC // ROW_TILE
DUMMY_TILE = N_CHUNKS * N_EXPERTS * LIST_TILES
DUMMY_ROW = MOE_TC * ROW_SUB


def _tile_tables(rank_t):
    sel = (rank_t >= 0.0).astype(jnp.int32).reshape(N_EXPERTS, N_CHUNKS, MOE_TC)
    counts = jnp.sum(sel, axis=-1).T
    n_tile_e = (counts + ROW_TILE - 1) // ROW_TILE
    tile_end = jnp.cumsum(n_tile_e, axis=1)
    tile_start = tile_end - n_tile_e
    n_tiles = tile_end[:, -1:]
    j = jnp.arange(MAX_TILES, dtype=jnp.int32)[None, :]
    j_used = jnp.minimum(j, n_tiles - 1)
    t_exp = jnp.sum((tile_end[:, None, :] <= j_used[:, :, None]).astype(jnp.int32), axis=-1)
    pick = t_exp[:, :, None] == jnp.arange(N_EXPERTS, dtype=jnp.int32)[None, None, :]
    lookup = lambda tab: jnp.sum(jnp.where(pick, tab[:, None, :], 0), axis=-1)
    q_tile = j_used - lookup(tile_start)
    cnt = jnp.clip(lookup(counts) - q_tile * ROW_TILE, 0, ROW_TILE)
    cnt = jnp.where(j < n_tiles, cnt, 0)
    chunk = jnp.arange(N_CHUNKS, dtype=jnp.int32)[:, None]
    list_tile = jnp.where(j < n_tiles, (chunk * N_EXPERTS + t_exp) * LIST_TILES + q_tile,
                          DUMMY_TILE)
    flat = lambda a: a.reshape(-1).astype(jnp.int32)
    return flat(t_exp), flat(list_tile), flat(cnt), counts.astype(jnp.int32)


def _moe_sort_kernel(rank_ref, gate_ref, cnt_ref, gsrc_ref, ssrc_ref, w_ref):
    rank = rank_ref[...]
    lane = lax.broadcasted_iota(jnp.int32, rank.shape, 1)
    d = jnp.where(rank >= 0.0, lane - rank.astype(jnp.int32), 0)
    w = gate_ref[...]
    for s in range(MOE_TC.bit_length() - 1):
        k = 1 << s
        d_in = pltpu.roll(d, MOE_TC - k, 1)
        w_in = pltpu.roll(w, MOE_TC - k, 1)
        take = (d_in & k) != 0
        leave = (d & k) != 0
        d = jnp.where(take, d_in, jnp.where(leave, 0, d))
        w = jnp.where(take, w_in, w)
    cnt = jnp.where(pl.program_id(0) < N_CHUNKS, cnt_ref[0][:, 0:1], 0)
    valid = lane < cnt
    row = (lane + d) * ROW_SUB
    gsrc_ref[...] = jnp.where(valid, row, 0)
    ssrc_ref[...] = jnp.where(valid, row, DUMMY_ROW)
    w_ref[...] = jnp.where(valid, w, 0.0)


def _moe_sort(rank_t, gate_t, counts):
    last = N_CHUNKS - 1
    chunk_spec = pl.BlockSpec((N_EXPERTS, MOE_TC), lambda c: (0, jnp.minimum(c, last)))
    cnt_b = jnp.broadcast_to(counts[:, :, None], (N_CHUNKS, N_EXPERTS, LANES))
    n_rows = (N_CHUNKS + 1) * N_EXPERTS
    outs = pl.pallas_call(
        _moe_sort_kernel,
        grid=(N_CHUNKS + 1,),
        in_specs=[chunk_spec, chunk_spec,
                  pl.BlockSpec((1, N_EXPERTS, LANES), lambda c: (jnp.minimum(c, last), 0, 0))],
        out_specs=[pl.BlockSpec((N_EXPERTS, MOE_TC), lambda c: (c, 0))] * 3,
        out_shape=[jax.ShapeDtypeStruct((n_rows, MOE_TC), jnp.int32),
                   jax.ShapeDtypeStruct((n_rows, MOE_TC), jnp.int32),
                   jax.ShapeDtypeStruct((n_rows, MOE_TC), F32)],
        compiler_params=_cparams("arbitrary"),
        name="moe_sort",
    )(rank_t, gate_t, cnt_b)
    return [a.reshape(n_rows * LIST_TILES, 1, ROW_TILE) for a in outs]


SCATTER_BATCH = 16


PIPE_STEPS = MAX_TILES + 2
N_PIECES = 8


def _routed_step(gsrc_ref, ssrc_ref, wt_ref, h_ref, acc_ref, wgb_ref, wub_ref, wdb_ref,
                 xs_g, xs_f, ys_f, ys_s):
    def scatter(r0):
        rows = range(r0, r0 + SCATTER_BATCH)
        toks = [pl.multiple_of(ssrc_ref[0, 0, r], ROW_SUB) for r in rows]
        olds = [acc_ref[pl.ds(t, ROW_SUB), :] for t in toks]
        news = [old + ys_s[r // 8, pl.ds(r % 8, ROW_SUB, stride=8), :]
                for old, r in zip(olds, rows)]
        for t, new in zip(toks, news):
            acc_ref[pl.ds(t, ROW_SUB), :] = new

    def gather(r0):
        for r in range(r0, r0 + ROW_TILE // N_PIECES):
            t = pl.multiple_of(gsrc_ref[0, 0, r], ROW_SUB)
            xs_g[r // 8, pl.ds(r % 8, ROW_SUB, stride=8), :] = h_ref[pl.ds(t, ROW_SUB), :]

    def side_work(piece):
        per = ROW_TILE // N_PIECES
        for r0 in range(piece * per, (piece + 1) * per, SCATTER_BATCH):
            scatter(r0)
        gather(piece * per)

    kc = D_MODEL // (N_PIECES // 2)
    hg = hu = None
    for p in range(N_PIECES // 2):
        side_work(p)
        xk = jnp.concatenate(
            [xs_f[:, s * 8:(s + 1) * 8, :].reshape(ROW_TILE, LANES)
             for s in range(p * kc // LANES, (p + 1) * kc // LANES)], axis=1).astype(BF16)
        dg = _dot(xk, wgb_ref[p * kc:(p + 1) * kc, :])
        du = _dot(xk, wub_ref[p * kc:(p + 1) * kc, :])
        hg = dg if hg is None else hg + dg
        hu = du if hu is None else hu + du
    eye = (lax.broadcasted_iota(jnp.int32, (ROW_TILE, ROW_TILE), 0)
           == lax.broadcasted_iota(jnp.int32, (ROW_TILE, ROW_TILE), 1))
    wcol = jnp.sum(jnp.where(eye, wt_ref[0], 0.0), axis=1, keepdims=True)
    a = (_silu(hg) * hu * wcol).astype(BF16)
    for p in range(N_PIECES // 2):
        side_work(N_PIECES // 2 + p)
        y = _dot(a, wdb_ref[:, p * kc:(p + 1) * kc])
        for q in range(kc // LANES):
            s = p * kc // LANES + q
            ys_f[:, s * 8:(s + 1) * 8, :] = y[:, q * LANES:(q + 1) * LANES].reshape(
                ROW_TILE // 8, 8, LANES)


def _moe_routed_kernel(te_ref, tl_ref, tc_ref, gsrc_ref, ssrc_ref, wt_ref, h_ref,
                       wg_ref, wu_ref, wd_ref, x_ref, sh_ref, g2_ref, out_ref,
                       acc_ref, wgb_ref, wub_ref, wdb_ref, xs0_ref, xs1_ref, ys0_ref, ys1_ref):
    c = pl.program_id(0)
    j = pl.program_id(1)
    tile = lambda jj: c * MAX_TILES + jnp.clip(jj, 0, MAX_TILES - 1)

    @pl.when(j == 0)
    def _():
        for ref in (acc_ref, xs0_ref, xs1_ref, ys0_ref, ys1_ref):
            ref[...] = jnp.zeros_like(ref)

    active = jnp.logical_and(j < PIPE_STEPS, jnp.logical_or(j < 2, tc_ref[tile(j - 2)] > 0))

    @pl.when(jnp.logical_and(active, jnp.logical_or(j < 2, te_ref[tile(j - 1)] != te_ref[tile(j - 2)])))
    def _():
        wgb_ref[...] = wg_ref[0, 0].astype(BF16)
        wub_ref[...] = wu_ref[0, 0].astype(BF16)
        wdb_ref[...] = wd_ref[0, 0].astype(BF16)

    step = functools.partial(_routed_step, gsrc_ref, ssrc_ref, wt_ref, h_ref, acc_ref,
                             wgb_ref, wub_ref, wdb_ref)

    @pl.when(jnp.logical_and(active, j % 2 == 0))
    def _():
        step(xs0_ref, xs1_ref, ys1_ref, ys0_ref)

    @pl.when(jnp.logical_and(active, j % 2 == 1))
    def _():
        step(xs1_ref, xs0_ref, ys0_ref, ys1_ref)

    @pl.when(j >= PIPE_STEPS)
    def _():
        base = (j - PIPE_STEPS) * (TILE * ROW_SUB)
        moe = jnp.concatenate(
            [acc_ref[pl.ds(base + s, TILE, stride=ROW_SUB), :] for s in range(ROW_SUB)], axis=1)
        out_ref[...] = x_ref[...] + g2_ref[pl.ds(c, 1), :] * (moe + sh_ref[...])


def _moe_routed(gsrc, ssrc, wts, t_exp, t_list, t_cnt, hrows, w_gate, w_up, w_down, li,
                x, shared, mod):
    def tile_idx(c, jj):
        return c * MAX_TILES + jnp.clip(jj, 0, MAX_TILES - 1)

    def flush_blk(c, j):
        return c * N_FLUSH + jnp.maximum(j - PIPE_STEPS, 0)

    def list_map(lag):
        def index_map(c, j, te, tl, tc):
            jj = j - lag
            real = jnp.logical_and(jj >= 0, jj < MAX_TILES)
            return (jnp.where(real, tl[tile_idx(c, jj)], DUMMY_TILE), 0, 0)
        return index_map

    w_map = lambda c, j, te, tl, tc: (li, te[tile_idx(c, j - 1)], 0, 0)
    w_in_spec = pl.BlockSpec((1, 1, D_MODEL, D_EXPERT), w_map)
    w_out_spec = pl.BlockSpec((1, 1, D_EXPERT, D_MODEL), w_map)
    tok_spec = pl.BlockSpec((TILE, D_MODEL), lambda c, j, te, tl, tc: (flush_blk(c, j), 0))
    stage = pltpu.VMEM((ROW_TILE // 8, 8 * ROW_SUB, LANES), F32)
    grid_spec = pltpu.PrefetchScalarGridSpec(
        num_scalar_prefetch=3,
        grid=(N_CHUNKS, PIPE_STEPS + N_FLUSH),
        in_specs=[
            pl.BlockSpec((1, 1, ROW_TILE), list_map(0), memory_space=pltpu.SMEM),
            pl.BlockSpec((1, 1, ROW_TILE), list_map(2), memory_space=pltpu.SMEM),
            pl.BlockSpec((1, 1, ROW_TILE), list_map(1)),
            pl.BlockSpec((MOE_TC * ROW_SUB, LANES), lambda c, j, te, tl, tc: (c, 0),
                         pipeline_mode=pl.Buffered(1)),
            w_in_spec, w_in_spec, w_out_spec,
            tok_spec, tok_spec,
            pl.BlockSpec((8, D_MODEL), lambda c, j, te, tl, tc: (0, 5)),
        ],
        out_specs=tok_spec,
        scratch_shapes=[
            pltpu.VMEM(((MOE_TC + 1) * ROW_SUB, LANES), F32),
            pltpu.VMEM((D_MODEL, D_EXPERT), BF16),
            pltpu.VMEM((D_MODEL, D_EXPERT), BF16),
            pltpu.VMEM((D_EXPERT, D_MODEL), BF16),
            stage, stage, stage, stage,
        ],
    )
    return pl.pallas_call(
        _moe_routed_kernel,
        grid_spec=grid_spec,
        out_shape=jax.ShapeDtypeStruct((T_ALL, D_MODEL), F32),
        compiler_params=_cparams("arbitrary", "arbitrary"),
        name="moe_routed",
    )(t_exp, t_list, t_cnt, gsrc, ssrc, wts, hrows, w_gate, w_up, w_down, x, shared, mod)


def kernel(x_prompt, x_sample, cache_k, cache_v, c, c_ctx, w_mod, b_mod, norm1, norm2, w_in_even, conv_a, q_norm, k_norm, lam_q1, lam_k1, lam_q2, lam_k2, subln, w_out_even, w_in_odd, conv_c, conv_c_b, ln_c_g, ln_c_b, w_pool, pool_scale, w_out_odd, w_router, b_router, w_gate, w_up, w_down, ws_gate, ws_up, ws_down):
    x = jnp.concatenate([x_prompt.reshape(T_P, D_MODEL), x_sample.reshape(T_S, D_MODEL)], axis=0)
    cond = jnp.concatenate([c_ctx[None, :], c, jnp.zeros((8 - 1 - DEC_BATCH, D_MODEL), F32)], axis=0)
    mod_all = _modulation(cond, w_mod, b_mod)

    new_k = new_v = None
    for li in range(DEPTH):
        mod = mod_all[li]
        if li % 2 == 0:
            e = li // 2
            lam_init = 0.8 - 0.6 * math.exp(-0.3 * li)
            u = _norm_in(x, norm1[li], mod, w_in_even[e].astype(BF16))
            q, k, v = _qkv_prep(u, q_norm[e], k_norm[e])
            if new_k is None:
                new_k, new_v = k, v
            o = _attention(q, k, v, cache_k, cache_v, e,
                           (lam_q1[e], lam_k1[e], lam_q2[e], lam_k2[e]), subln[e], lam_init)
            x = _even_out(x, u, o, conv_a[e], mod, w_out_even[e].astype(BF16))
        else:
            o_ = li // 2
            u = _norm_in(x, norm1[li], mod, w_in_odd[o_].astype(BF16))
            x = _odd_out(x, u, conv_c[o_], conv_c_b[o_], ln_c_g[o_], ln_c_b[o_],
                         w_pool[o_].astype(BF16), pool_scale[o_], mod, w_out_odd[o_].astype(BF16))
        hrows, gate_b, rank_b, shared = _moe_pre(
            x, norm2[li], mod, w_router[li].T, b_router[li], ws_gate[li].astype(BF16),
            ws_up[li].astype(BF16), ws_down[li].astype(BF16))
        t_exp, t_list, t_cnt, counts = _tile_tables(rank_b)
        gsrc, ssrc, wts = _moe_sort(rank_b, gate_b, counts)
        x = _moe_routed(gsrc, ssrc, wts, t_exp, t_list, t_cnt, hrows, w_gate, w_up, w_down, li,
                        x, shared, mod)

    y_prompt = x[:T_P].reshape(BATCH, SEQ, D_MODEL)
    y_sample = x[T_P:].reshape(DEC_BATCH, DEC_SEQ, D_MODEL)

    def to_cache(a):
        a = a[:, :T_P].reshape(H_B, BATCH, SEQ, LANES)
        return jnp.transpose(a, (1, 0, 2, 3))[:, None]

    return (y_prompt, y_sample, to_cache(new_k), to_cache(new_v))
```

```python
import functools
import math

import numpy as np
import jax
import jax.numpy as jnp
from jax import lax
from jax.experimental import pallas as pl
from jax.experimental.pallas import tpu as pltpu

D_MODEL = 1024
BATCH = 16
SEQ = 256
DEPTH = 2
DEC_BATCH = 2
DEC_SEQ = 4096
PAST_LEN = 512
GRID_W = 64
H_B = 4
DK_B = 64
DV_B = 2 * DK_B
D_A = D_MODEL // 2
D_B = H_B * DV_B
D_C = D_MODEL // 2
D_D = D_MODEL // 2
CONV_C = 31
POOL_WINDOWS = (2, 4, 8, 16)
D_DG = D_D // len(POOL_WINDOWS)
N_EXPERTS = 64
TOP_K = 8
N_GROUPS = 8
TOPK_GROUPS = 4
D_EXPERT = 256
D_SHARED = 256
ROUTED_SCALE = 2.5
ROPE_BASE = 10000.0
EPS = 1e-6

F32 = jnp.float32
BF16 = jnp.bfloat16

T_P = BATCH * SEQ
T_S = DEC_BATCH * DEC_SEQ
T_ALL = T_P + T_S
TILE = 256
N_TILES = T_ALL // TILE
P_TILES = T_P // TILE
S_TILES = DEC_SEQ // TILE
LANES = 128
VMEM_LIMIT = 56 * 1024 * 1024


def _cparams(*sem):
    return pltpu.CompilerParams(dimension_semantics=sem, vmem_limit_bytes=VMEM_LIMIT)


def _mod_row(i, tm):
    npt = T_P // tm
    per = DEC_SEQ // tm
    return jnp.where(i < npt, 0, 1 + (i - npt) // per)


def _seq_flags(i):
    j = (i - P_TILES) % S_TILES
    first = jnp.logical_or(i < P_TILES, j == 0)
    last = jnp.logical_or(i < P_TILES, j == S_TILES - 1)
    return first, last


def _seq_tile(i):
    return jnp.where(i < P_TILES, 0, (i - P_TILES) % S_TILES)


def _split_bf16(a):
    hi = a.astype(BF16)
    lo = (a - hi.astype(F32)).astype(BF16)
    return hi, lo


def _dot(a, b):
    return jnp.dot(a, b, preferred_element_type=F32)


def _dot_nt(a, b):
    return lax.dot_general(a, b, (((1,), (1,)), ((), ())), preferred_element_type=F32)


def _dot3(a, b):
    a_hi, a_lo = _split_bf16(a)
    b_hi, b_lo = _split_bf16(b)
    return _dot(a_hi, b_hi) + _dot(a_lo, b_hi) + _dot(a_hi, b_lo)


def _silu(x):
    return x * jax.nn.sigmoid(x)


MOD_TN = 1536


def _mod_kernel(c_ref, w_ref, b_ref, o_ref):
    o_ref[0] = _dot3(_silu(c_ref[...]), w_ref[0]) + b_ref[0]


def _modulation(cond, w_mod, b_mod):
    n = 6 * D_MODEL
    return pl.pallas_call(
        _mod_kernel,
        grid=(DEPTH, n // MOD_TN),
        in_specs=[
            pl.BlockSpec((8, D_MODEL), lambda l, j: (0, 0)),
            pl.BlockSpec((1, D_MODEL, MOD_TN), lambda l, j: (l, 0, j)),
            pl.BlockSpec((1, 1, MOD_TN), lambda l, j: (l, 0, j)),
        ],
        out_specs=pl.BlockSpec((1, 8, MOD_TN), lambda l, j: (l, 0, j)),
        out_shape=jax.ShapeDtypeStruct((DEPTH, 8, n), F32),
        compiler_params=_cparams("arbitrary", "arbitrary"),
        name="modulation",
    )(cond, w_mod, b_mod.reshape(DEPTH, 1, n))


IN_TM = 512


def _modulated_norm(x, g, shift, scale):
    ms = jnp.mean(x * x, axis=-1, keepdims=True)
    return (x * lax.rsqrt(ms + EPS) * g) * (1.0 + scale) + shift


def _tok_specs(x, tm, width):
    if isinstance(x, tuple):
        n_p = T_P // tm
        return ([pl.BlockSpec((tm, width), lambda i: (jnp.minimum(i, n_p - 1), 0)),
                 pl.BlockSpec((tm, width), lambda i: (jnp.maximum(i - n_p, 0), 0))], list(x))
    return [pl.BlockSpec((tm, width), lambda i: (i, 0))], [x]


def _tok_load(refs, i, tm):
    if len(refs) == 2:
        return jnp.where(i < T_P // tm, refs[0][...], refs[1][...])
    return refs[0][...]


def _norm_in_kernel(*refs, n_x):
    x_refs, (g_ref, sh_ref, sc_ref, w_ref, o_ref) = refs[:n_x], refs[n_x:]
    i = pl.program_id(0)
    r = _mod_row(i, IN_TM)
    h = _modulated_norm(_tok_load(x_refs, i, IN_TM), g_ref[...], sh_ref[pl.ds(r, 1), :],
                        sc_ref[pl.ds(r, 1), :])
    o_ref[...] = _dot(h.astype(BF16), w_ref[...])


def _norm_in(x, g, mod, w_bf16):
    n = w_bf16.shape[1]
    x_specs, x_ops = _tok_specs(x, IN_TM, D_MODEL)
    return pl.pallas_call(
        functools.partial(_norm_in_kernel, n_x=len(x_ops)),
        grid=(T_ALL // IN_TM,),
        in_specs=x_specs + [
            pl.BlockSpec((1, D_MODEL), lambda i: (0, 0)),
            pl.BlockSpec((8, D_MODEL), lambda i: (0, 0)),
            pl.BlockSpec((8, D_MODEL), lambda i: (0, 1)),
            pl.BlockSpec((D_MODEL, n), lambda i: (0, 0)),
        ],
        out_specs=pl.BlockSpec((IN_TM, n), lambda i: (i, 0)),
        out_shape=jax.ShapeDtypeStruct((T_ALL, n), F32),
        compiler_params=_cparams("arbitrary"),
        name="norm_in_proj",
    )(*x_ops, g.reshape(1, D_MODEL), mod, mod, w_bf16)


QKV_TM = 1024


def _rope_tables():
    half = DK_B // 2
    freqs = ROPE_BASE ** (-np.arange(0, half, 2, dtype=np.float64) / half)
    l = np.arange(DEC_SEQ)
    pos_r = (l // GRID_W).astype(np.float64)
    pos_c = (l % GRID_W).astype(np.float64)
    lane = np.arange(LANES)
    jj = lane % DK_B
    m = jj % half
    f = m % (half // 2)
    pos = np.where((jj < half)[None, :], pos_r[:, None], pos_c[:, None])
    ang = pos * freqs[f][None, :]
    sign = np.where(m < half // 2, -1.0, 1.0)[None, :]
    cos = np.concatenate([np.ones((QKV_TM, LANES)), np.cos(ang)], axis=0)
    sin = np.concatenate([np.zeros((QKV_TM, LANES)), sign * np.sin(ang)], axis=0)
    return cos.astype(np.float32), sin.astype(np.float32)


def _segment_mean_matrix():
    lane = np.arange(LANES)
    same = (lane[:, None] // DK_B) == (lane[None, :] // DK_B)
    return (same.astype(np.float32) / DK_B)


def _qk_prep(x, g, cos, sin, seg):
    x2 = x * x
    hi, lo = _split_bf16(x2)
    ms = _dot(hi, seg) + _dot(lo, seg)
    y = x * lax.rsqrt(ms + EPS) * g
    lane = lax.broadcasted_iota(jnp.int32, y.shape, 1)
    lower = (lane % (DK_B // 2)) < (DK_B // 4)
    partner = jnp.where(lower, pltpu.roll(y, LANES - DK_B // 4, 1), pltpu.roll(y, DK_B // 4, 1))
    return y * cos + partner * sin


def _qkv_prep_kernel(q_ref, k_ref, v_ref, qn_ref, kn_ref, cos_ref, sin_ref, seg_ref,
                     qo_ref, ko_ref, vo_ref, kf_ref):
    cos = cos_ref[...]
    sin = sin_ref[...]
    seg = seg_ref[...]
    scale = math.log2(math.e) / math.sqrt(DK_B)
    for h in range(H_B):
        cols = slice(h * LANES, (h + 1) * LANES)
        qo_ref[h] = (_qk_prep(q_ref[:, cols], qn_ref[...], cos, sin, seg) * scale).astype(BF16)
        k = _qk_prep(k_ref[:, cols], kn_ref[...], cos, sin, seg)
        kf_ref[h] = k
        ko_ref[h] = k.astype(BF16)
        vo_ref[h] = v_ref[:, cols].astype(BF16)


def _qkv_prep(u, qn, kn):
    cos, sin = _rope_tables()
    seg = jnp.asarray(_segment_mean_matrix(), BF16)
    qn2 = jnp.concatenate([qn, qn]).reshape(1, LANES)
    kn2 = jnp.concatenate([kn, kn]).reshape(1, LANES)
    col0 = 3 * D_A // D_B
    p_steps = T_P // QKV_TM
    s_steps = DEC_SEQ // QKV_TM

    def tab_map(i):
        return (jnp.where(i < p_steps, 0, 1 + (i - p_steps) % s_steps), 0)

    out_b = jax.ShapeDtypeStruct((H_B, T_ALL, LANES), BF16)
    out_spec = pl.BlockSpec((H_B, QKV_TM, LANES), lambda i: (0, i, 0))
    return pl.pallas_call(
        _qkv_prep_kernel,
        grid=(T_ALL // QKV_TM,),
        in_specs=[
            pl.BlockSpec((QKV_TM, D_B), lambda i: (i, col0)),
            pl.BlockSpec((QKV_TM, D_B), lambda i: (i, col0 + 1)),
            pl.BlockSpec((QKV_TM, D_B), lambda i: (i, col0 + 2)),
            pl.BlockSpec((1, LANES), lambda i: (0, 0)),
            pl.BlockSpec((1, LANES), lambda i: (0, 0)),
            pl.BlockSpec((QKV_TM, LANES), tab_map),
            pl.BlockSpec((QKV_TM, LANES), tab_map),
            pl.BlockSpec((LANES, LANES), lambda i: (0, 0)),
        ],
        out_specs=[out_spec] * 4,
        out_shape=[out_b, out_b, out_b, jax.ShapeDtypeStruct((H_B, T_ALL, LANES), F32)],
        compiler_params=_cparams("arbitrary"),
        name="qkv_prep",
    )(u, u, u, qn2, kn2, jnp.asarray(cos), jnp.asarray(sin), seg)


def _lambda(lq1_ref, lk1_ref, lq2_ref, lk2_ref, lam_init):
    a = jnp.sum(lq1_ref[...] * lk1_ref[...], axis=-1, keepdims=True)
    b = jnp.sum(lq2_ref[...] * lk2_ref[...], axis=-1, keepdims=True)
    return jnp.exp(a) - jnp.exp(b) + lam_init


def _attn_body(q, keys, vals, lam, subg, lam_init):
    lane = lax.broadcasted_iota(jnp.int32, q.shape, 1)
    zero = jnp.zeros_like(q)
    qa = jnp.where(lane < DK_B, q, zero)
    qb = jnp.where(lane < DK_B, zero, q)
    scores = [[_dot_nt(qq, k) for k in keys] for qq in (qa, qb)]
    outs = []
    for ss in scores:
        m = functools.reduce(jnp.maximum, [jnp.max(s, axis=-1, keepdims=True) for s in ss])
        ps = [jnp.exp2(s - m) for s in ss]
        l = functools.reduce(jnp.add, [jnp.sum(p, axis=-1, keepdims=True) for p in ps])
        pv = functools.reduce(jnp.add, [_dot(p.astype(BF16), v) for p, v in zip(ps, vals)])
        outs.append(pv / l)
    o = outs[0] - lam * outs[1]
    ms = jnp.mean(o * o, axis=-1, keepdims=True)
    return (o * lax.rsqrt(ms + EPS) * subg) * (1.0 - lam_init)


def _attn_prompt_kernel(q_ref, k_ref, v_ref, lq1, lk1, lq2, lk2, sg_ref, o_ref, *, lam_init):
    lam = _lambda(lq1, lk1, lq2, lk2, lam_init)
    o_ref[...] = _attn_body(q_ref[0], [k_ref[0]], [v_ref[0]], lam, sg_ref[...], lam_init)


def _attn_latent_kernel(q_ref, k_ref, v_ref, ck_ref, cv_ref, lq1, lk1, lq2, lk2, sg_ref, o_ref,
                        *, lam_init):
    lam = _lambda(lq1, lk1, lq2, lk2, lam_init)
    keys = [ck_ref[0, 0, 0].astype(BF16), k_ref[0]]
    vals = [cv_ref[0, 0, 0].astype(BF16), v_ref[0]]
    o_ref[...] = _attn_body(q_ref[0], keys, vals, lam, sg_ref[...], lam_init)


def _attention(q, k, v, cache_k, cache_v, e, lam_params, subg, lam_init):
    small = [p.reshape(1, DK_B) for p in lam_params] + [subg.reshape(1, DV_B)]
    small_specs2 = [pl.BlockSpec((1, DK_B), lambda b, h: (0, 0))] * 4 + \
                   [pl.BlockSpec((1, DV_B), lambda b, h: (0, 0))]
    small_specs3 = [pl.BlockSpec((1, DK_B), lambda s, h, j: (0, 0))] * 4 + \
                   [pl.BlockSpec((1, DV_B), lambda s, h, j: (0, 0))]
    o_prompt = pl.pallas_call(
        functools.partial(_attn_prompt_kernel, lam_init=lam_init),
        grid=(BATCH, H_B),
        in_specs=[pl.BlockSpec((1, SEQ, LANES), lambda b, h: (h, b, 0))] * 3 + small_specs2,
        out_specs=pl.BlockSpec((SEQ, LANES), lambda b, h: (b, h)),
        out_shape=jax.ShapeDtypeStruct((T_P, D_B), F32),
        compiler_params=_cparams("arbitrary", "arbitrary"),
        name="attn_prompt",
    )(q, k, v, *small)

    kv_spec = pl.BlockSpec((1, DEC_SEQ, LANES), lambda s, h, j: (h, T_P // DEC_SEQ + s, 0))
    c_spec = pl.BlockSpec((1, 1, 1, PAST_LEN, LANES), lambda s, h, j: (s, e, h, 0, 0))
    o_latent = pl.pallas_call(
        functools.partial(_attn_latent_kernel, lam_init=lam_init),
        grid=(DEC_BATCH, H_B, S_TILES),
        in_specs=[pl.BlockSpec((1, TILE, LANES), lambda s, h, j: (h, P_TILES + s * S_TILES + j, 0)),
                  kv_spec, kv_spec, c_spec, c_spec] + small_specs3,
        out_specs=pl.BlockSpec((TILE, LANES), lambda s, h, j: (s * S_TILES + j, h)),
        out_shape=jax.ShapeDtypeStruct((T_S, D_B), F32),
        compiler_params=_cparams("arbitrary", "arbitrary", "arbitrary"),
        name="attn_latent",
    )(q, k, v, cache_k, cache_v, *small)
    return o_prompt, o_latent


HALO8 = 8
HALO16 = 16


def _prev_block(i, rows):
    return jnp.maximum(i * (TILE // rows) - 1, 0)


def _next_block(i, rows):
    return jnp.minimum((i + 1) * (TILE // rows), T_ALL // rows - 1)


def _even_out_kernel(*refs, n_x, n_o):
    x_refs, o_refs = refs[:n_x], refs[n_x:n_x + n_o]
    (bg_ref, cg_ref, hin_ref, cgp_ref, hinp_ref, cgn_ref, hinn_ref, cw_ref, g1_ref, w_ref,
     out_ref) = refs[n_x + n_o:]
    i = pl.program_id(0)
    first, last = _seq_flags(i)
    r = _mod_row(i, TILE)
    z = cg_ref[...] * hin_ref[...]
    zp = jnp.where(first, 0.0, cgp_ref[HALO8 - 1:HALO8, :] * hinp_ref[HALO8 - 1:HALO8, :])
    zn = jnp.where(last, 0.0, cgn_ref[0:1, :] * hinn_ref[0:1, :])
    row = lax.broadcasted_iota(jnp.int32, z.shape, 0)
    z_prev = jnp.where(row == 0, zp, pltpu.roll(z, 1, 0))
    z_next = jnp.where(row == TILE - 1, zn, pltpu.roll(z, TILE - 1, 0))
    cw = cw_ref[...]
    ya = bg_ref[...] * (cw[0:1, :] * z_prev + cw[1:2, :] * z + cw[2:3, :] * z_next)
    o = _tok_load(o_refs, i, TILE)
    y = _dot(ya.astype(BF16), w_ref[0:D_A, :]) + _dot(o.astype(BF16), w_ref[D_A:, :])
    out_ref[...] = _tok_load(x_refs, i, TILE) + g1_ref[pl.ds(r, 1), :] * y


def _even_out(x, u, o, conv_w, mod, w_out_bf16):
    tile_spec = lambda c: pl.BlockSpec((TILE, D_A), lambda i: (i, c))
    prev_spec = lambda c: pl.BlockSpec((HALO8, D_A), lambda i: (_prev_block(i, HALO8), c))
    next_spec = lambda c: pl.BlockSpec((HALO8, D_A), lambda i: (_next_block(i, HALO8), c))
    x_specs, x_ops = _tok_specs(x, TILE, D_MODEL)
    o_specs, o_ops = _tok_specs(o, TILE, D_B)
    return pl.pallas_call(
        functools.partial(_even_out_kernel, n_x=len(x_ops), n_o=len(o_ops)),
        grid=(N_TILES,),
        in_specs=x_specs + o_specs + [
            tile_spec(0), tile_spec(1), tile_spec(2),
            prev_spec(1), prev_spec(2), next_spec(1), next_spec(2),
            pl.BlockSpec((3, D_A), lambda i: (0, 0)),
            pl.BlockSpec((8, D_MODEL), lambda i: (0, 2)),
            pl.BlockSpec((D_MODEL, D_MODEL), lambda i: (0, 0)),
        ],
        out_specs=pl.BlockSpec((TILE, D_MODEL), lambda i: (i, 0)),
        out_shape=jax.ShapeDtypeStruct((T_ALL, D_MODEL), F32),
        compiler_params=_cparams("arbitrary"),
        name="even_mixer_out",
    )(*x_ops, *o_ops, u, u, u, u, u, u, u, conv_w, mod, w_out_bf16)


def _odd_out_kernel(x_ref, a_ref, b_ref, pd_ref, ap_ref, bp_ref, an_ref, bn_ref, pp_ref, pn_ref,
                    cw_ref, cb_ref, lg_ref, lb_ref, wp_ref, ps_ref, g1_ref, w_ref, out_ref,
                    ext_ref, extp_ref, shift_ref):
    i = pl.program_id(0)
    first, last = _seq_flags(i)
    r = _mod_row(i, TILE)
    ext_ref[0:HALO16, :] = jnp.where(first, 0.0, ap_ref[...] * jax.nn.sigmoid(bp_ref[...]))
    ext_ref[HALO16:HALO16 + TILE, :] = a_ref[...] * jax.nn.sigmoid(b_ref[...])
    ext_ref[HALO16 + TILE:, :] = jnp.where(last, 0.0, an_ref[...] * jax.nn.sigmoid(bn_ref[...]))
    base = HALO16 - CONV_C // 2
    parts = []
    for cb in range(D_C // LANES):
        cols = slice(cb * LANES, (cb + 1) * LANES)
        acc = jnp.zeros((TILE, LANES), F32)
        for phase in range(8):
            taps = [j for j in range(CONV_C) if (base + j) % 8 == phase]
            reach = max((base + j) // 8 for j in taps)
            rows = TILE + 8 * reach
            shift_ref[0:rows, :] = ext_ref[pl.ds(phase, rows), cols]
            for j in taps:
                a = (base + j) // 8
                acc = acc + cw_ref[j:j + 1, cols] * shift_ref[8 * a:8 * a + TILE, :]
        parts.append(acc)
    g = jnp.concatenate(parts, axis=-1) + cb_ref[...]
    mu = jnp.mean(g, axis=-1, keepdims=True)
    var = jnp.mean(jnp.square(g - mu), axis=-1, keepdims=True)
    g = _silu(((g - mu) * lax.rsqrt(var + EPS)) * lg_ref[...] + lb_ref[...])
    extp_ref[0:HALO8, :] = jnp.where(first, 0.0, pp_ref[...])
    extp_ref[HALO8:HALO8 + TILE, :] = pd_ref[...]
    extp_ref[HALO8 + TILE:, :] = jnp.where(last, 0.0, pn_ref[...])
    seq_len = jnp.where(i < P_TILES, SEQ, DEC_SEQ)
    pos = _seq_tile(i) * TILE + lax.broadcasted_iota(jnp.int32, (TILE, 1), 0)
    yd = []
    for gi, w in enumerate(POOL_WINDOWS):
        cols = slice(gi * D_DG, (gi + 1) * D_DG)
        s = jnp.zeros((TILE, D_DG), F32)
        for d in range(-(w // 2), w - w // 2):
            s = s + extp_ref[pl.ds(HALO8 + d, TILE), cols]
        lo = jnp.maximum(pos - w // 2, 0)
        hi = jnp.minimum(pos - w // 2 + w, seq_len)
        pooled = s / (hi - lo).astype(F32) - pd_ref[:, cols]
        yd.append(_dot(pooled.astype(BF16), wp_ref[gi]))
    yd = jnp.concatenate(yd, axis=-1) * ps_ref[...]
    y = _dot(g.astype(BF16), w_ref[0:D_C, :]) + _dot(yd.astype(BF16), w_ref[D_C:, :])
    out_ref[...] = x_ref[...] + g1_ref[pl.ds(r, 1), :] * y


def _odd_out(x, u, conv_w, conv_b, ln_g, ln_b, w_pool_bf16, p_scale, mod, w_out_bf16):
    tile_spec = lambda c: pl.BlockSpec((TILE, D_C), lambda i: (i, c))
    prev_spec = lambda rows, c: pl.BlockSpec((rows, D_C), lambda i: (_prev_block(i, rows), c))
    next_spec = lambda rows, c: pl.BlockSpec((rows, D_C), lambda i: (_next_block(i, rows), c))
    vec = lambda: pl.BlockSpec((1, D_C), lambda i: (0, 0))
    return pl.pallas_call(
        _odd_out_kernel,
        grid=(N_TILES,),
        in_specs=[
            pl.BlockSpec((TILE, D_MODEL), lambda i: (i, 0)),
            tile_spec(0), tile_spec(1), tile_spec(2),
            prev_spec(HALO16, 0), prev_spec(HALO16, 1), next_spec(HALO16, 0), next_spec(HALO16, 1),
            prev_spec(HALO8, 2), next_spec(HALO8, 2),
            pl.BlockSpec((CONV_C, D_C), lambda i: (0, 0)),
            vec(), vec(), vec(),
            pl.BlockSpec((len(POOL_WINDOWS), D_DG, D_DG), lambda i: (0, 0, 0)),
            vec(),
            pl.BlockSpec((8, D_MODEL), lambda i: (0, 2)),
            pl.BlockSpec((D_MODEL, D_MODEL), lambda i: (0, 0)),
        ],
        out_specs=pl.BlockSpec((TILE, D_MODEL), lambda i: (i, 0)),
        out_shape=jax.ShapeDtypeStruct((T_ALL, D_MODEL), F32),
        scratch_shapes=[pltpu.VMEM((TILE + 2 * HALO16, D_C), F32),
                        pltpu.VMEM((TILE + 2 * HALO8, D_D), F32),
                        pltpu.VMEM((TILE + 2 * HALO16, LANES), F32)],
        compiler_params=_cparams("arbitrary"),
        name="odd_mixer_out",
    )(x, u, u, u, u, u, u, u, u, u, conv_w, conv_b.reshape(1, D_C), ln_g.reshape(1, D_C),
      ln_b.reshape(1, D_C), w_pool_bf16, p_scale.reshape(1, D_D), mod, w_out_bf16)


GROUP = N_EXPERTS // N_GROUPS
NEG_INF = float("-inf")


def _first_argmax(v, idx, axis):
    m = jnp.max(v, axis=axis, keepdims=True)
    big = jnp.int32(2 ** 30)
    am = jnp.min(jnp.where(v == m, idx, big), axis=axis, keepdims=True)
    return m, am


def _route(scores, biased):
    shape = biased.shape
    member = lax.broadcasted_iota(jnp.int32, shape, 1)
    m1, a1 = _first_argmax(biased, member, 1)
    m2 = jnp.max(jnp.where(member == a1, NEG_INF, biased), axis=1, keepdims=True)
    gscore = m1 + m2
    gidx = lax.broadcasted_iota(jnp.int32, gscore.shape, 0)
    gsel = jnp.zeros(gscore.shape, jnp.bool_)
    for _ in range(TOPK_GROUPS):
        _, am = _first_argmax(gscore, gidx, 0)
        hit = gidx == am
        gsel = jnp.logical_or(gsel, hit)
        gscore = jnp.where(hit, NEG_INF, gscore)
    cand = jnp.where(gsel, biased, NEG_INF)
    eidx = lax.broadcasted_iota(jnp.int32, shape, 0) * GROUP + member
    sel = jnp.zeros(shape, jnp.bool_)
    for _ in range(TOP_K):
        m = jnp.max(jnp.max(cand, axis=1, keepdims=True), axis=0, keepdims=True)
        big = jnp.int32(2 ** 30)
        am = jnp.where(cand == m, eidx, big)
        am = jnp.min(jnp.min(am, axis=1, keepdims=True), axis=0, keepdims=True)
        hit = eidx == am
        sel = jnp.logical_or(sel, hit)
        cand = jnp.where(hit, NEG_INF, cand)
    wsel = jnp.where(sel, scores, 0.0)
    tot = jnp.sum(jnp.sum(wsel, axis=1, keepdims=True), axis=0, keepdims=True)
    return wsel / tot * ROUTED_SCALE, sel


MOE_TC = 4096
N_CHUNKS = T_ALL // MOE_TC
ROW_TILE = 256
MAX_TILES = MOE_TC * TOP_K // ROW_TILE + N_EXPERTS
N_FLUSH = MOE_TC // TILE
N_BLK = MOE_TC // LANES
ROW_SUB = D_MODEL // LANES


def _moe_pre_kernel(x_ref, g_ref, sh_ref, sc_ref, wr_ref, br_ref, tri_ref, wsg_ref, wsu_ref,
                    wsd_ref, hrow_ref, gate_ref, rank_ref, shared_ref, carry_ref):
    i = pl.program_id(0)
    r = _mod_row(i, TILE)
    h = _modulated_norm(x_ref[...], g_ref[...], sh_ref[pl.ds(r, 1), :], sc_ref[pl.ds(r, 1), :])
    hb = h.astype(BF16)
    for s in range(ROW_SUB):
        hrow_ref[pl.ds(s, TILE, stride=ROW_SUB), :] = h[:, s * LANES:(s + 1) * LANES]
    h_hi, h_lo = hb, (h - hb.astype(F32)).astype(BF16)
    w_hi, w_lo = _split_bf16(wr_ref[...])
    logits = _dot_nt(w_hi, h_hi) + _dot_nt(w_lo, h_hi) + _dot_nt(w_hi, h_lo)
    scores = jax.nn.sigmoid(logits)
    biased = scores + br_ref[:, 0:1]
    shape3 = (N_GROUPS, GROUP, TILE)
    gate_t, sel = _route(scores.reshape(shape3), biased.reshape(shape3))
    gate_t = gate_t.reshape(N_EXPERTS, TILE)
    sel = jnp.where(sel.reshape(N_EXPERTS, TILE), 1.0, 0.0)

    @pl.when(i % N_FLUSH == 0)
    def _():
        carry_ref[...] = jnp.zeros_like(carry_ref)

    carry = carry_ref[...]
    local = _dot(sel.astype(BF16), tri_ref[...])
    rank = jnp.where(sel > 0.0, local + jnp.concatenate([carry] * (TILE // LANES), axis=1), -1.0)
    carry_ref[...] = carry + jnp.sum(sel, axis=1, keepdims=True)
    gate_ref[...] = gate_t
    rank_ref[...] = rank
    a = _silu(_dot(hb, wsg_ref[...])) * _dot(hb, wsu_ref[...])
    shared_ref[...] = _dot(a.astype(BF16), wsd_ref[...])


def _moe_pre(x, g, mod, w_router_t, b_router, wsg, wsu, wsd):
    return pl.pallas_call(
        _moe_pre_kernel,
        grid=(N_TILES,),
        in_specs=[
            pl.BlockSpec((TILE, D_MODEL), lambda i: (i, 0)),
            pl.BlockSpec((1, D_MODEL), lambda i: (0, 0)),
            pl.BlockSpec((8, D_MODEL), lambda i: (0, 3)),
            pl.BlockSpec((8, D_MODEL), lambda i: (0, 4)),
            pl.BlockSpec((N_EXPERTS, D_MODEL), lambda i: (0, 0)),
            pl.BlockSpec((N_EXPERTS, LANES), lambda i: (0, 0)),
            pl.BlockSpec((TILE, TILE), lambda i: (0, 0)),
            pl.BlockSpec((D_MODEL, D_SHARED), lambda i: (0, 0)),
            pl.BlockSpec((D_MODEL, D_SHARED), lambda i: (0, 0)),
            pl.BlockSpec((D_SHARED, D_MODEL), lambda i: (0, 0)),
        ],
        out_specs=[
            pl.BlockSpec((TILE * ROW_SUB, LANES), lambda i: (i, 0)),
            pl.BlockSpec((N_EXPERTS, TILE), lambda i: (0, i)),
            pl.BlockSpec((N_EXPERTS, TILE), lambda i: (0, i)),
            pl.BlockSpec((TILE, D_MODEL), lambda i: (i, 0)),
        ],
        out_shape=[
            jax.ShapeDtypeStruct((T_ALL * ROW_SUB, LANES), F32),
            jax.ShapeDtypeStruct((N_EXPERTS, T_ALL), F32),
            jax.ShapeDtypeStruct((N_EXPERTS, T_ALL), F32),
            jax.ShapeDtypeStruct((T_ALL, D_MODEL), F32),
        ],
        scratch_shapes=[pltpu.VMEM((N_EXPERTS, LANES), F32)],
        compiler_params=_cparams("arbitrary"),
        name="moe_pre",
    )(x, g.reshape(1, D_MODEL), mod, mod, w_router_t,
      jnp.broadcast_to(b_router.reshape(N_EXPERTS, 1), (N_EXPERTS, LANES)),
      jnp.asarray(np.triu(np.ones((TILE, TILE), np.float32), 1), BF16), wsg, wsu, wsd)


LIST_TILES = MOE_TC // ROW_TILE
DUMMY_TILE = N_CHUNKS * N_EXPERTS * LIST_TILES
DUMMY_ROW = MOE_TC * ROW_SUB


def _tile_tables(rank_t):
    sel = (rank_t >= 0.0).astype(jnp.int32).reshape(N_EXPERTS, N_CHUNKS, MOE_TC)
    counts = jnp.sum(sel, axis=-1).T
    n_tile_e = (counts + ROW_TILE - 1) // ROW_TILE
    tile_end = jnp.cumsum(n_tile_e, axis=1)
    tile_start = tile_end - n_tile_e
    n_tiles = tile_end[:, -1:]
    j = jnp.arange(MAX_TILES, dtype=jnp.int32)[None, :]
    j_used = jnp.minimum(j, n_tiles - 1)
    t_exp = jnp.sum((tile_end[:, None, :] <= j_used[:, :, None]).astype(jnp.int32), axis=-1)
    pick = t_exp[:, :, None] == jnp.arange(N_EXPERTS, dtype=jnp.int32)[None, None, :]
    lookup = lambda tab: jnp.sum(jnp.where(pick, tab[:, None, :], 0), axis=-1)
    q_tile = j_used - lookup(tile_start)
    cnt = jnp.clip(lookup(counts) - q_tile * ROW_TILE, 0, ROW_TILE)
    cnt = jnp.where(j < n_tiles, cnt, 0)
    chunk = jnp.arange(N_CHUNKS, dtype=jnp.int32)[:, None]
    list_tile = jnp.where(j < n_tiles, (chunk * N_EXPERTS + t_exp) * LIST_TILES + q_tile,
                          DUMMY_TILE)
    flat = lambda a: a.reshape(-1).astype(jnp.int32)
    return flat(t_exp), flat(list_tile), flat(cnt), counts.astype(jnp.int32)


def _moe_sort_kernel(rank_ref, gate_ref, cnt_ref, gsrc_ref, ssrc_ref, w_ref):
    rank = rank_ref[...]
    lane = lax.broadcasted_iota(jnp.int32, rank.shape, 1)
    d = jnp.where(rank >= 0.0, lane - rank.astype(jnp.int32), 0)
    w = gate_ref[...]
    for s in range(MOE_TC.bit_length() - 1):
        k = 1 << s
        d_in = pltpu.roll(d, MOE_TC - k, 1)
        w_in = pltpu.roll(w, MOE_TC - k, 1)
        take = (d_in & k) != 0
        leave = (d & k) != 0
        d = jnp.where(take, d_in, jnp.where(leave, 0, d))
        w = jnp.where(take, w_in, w)
    cnt = jnp.where(pl.program_id(0) < N_CHUNKS, cnt_ref[0][:, 0:1], 0)
    valid = lane < cnt
    row = (lane + d) * ROW_SUB
    gsrc_ref[...] = jnp.where(valid, row, 0)
    ssrc_ref[...] = jnp.where(valid, row, DUMMY_ROW)
    w_ref[...] = jnp.where(valid, w, 0.0)


def _moe_sort(rank_t, gate_t, counts):
    last = N_CHUNKS - 1
    chunk_spec = pl.BlockSpec((N_EXPERTS, MOE_TC), lambda c: (0, jnp.minimum(c, last)))
    cnt_b = jnp.broadcast_to(counts[:, :, None], (N_CHUNKS, N_EXPERTS, LANES))
    n_rows = (N_CHUNKS + 1) * N_EXPERTS
    outs = pl.pallas_call(
        _moe_sort_kernel,
        grid=(N_CHUNKS + 1,),
        in_specs=[chunk_spec, chunk_spec,
                  pl.BlockSpec((1, N_EXPERTS, LANES), lambda c: (jnp.minimum(c, last), 0, 0))],
        out_specs=[pl.BlockSpec((N_EXPERTS, MOE_TC), lambda c: (c, 0))] * 3,
        out_shape=[jax.ShapeDtypeStruct((n_rows, MOE_TC), jnp.int32),
                   jax.ShapeDtypeStruct((n_rows, MOE_TC), jnp.int32),
                   jax.ShapeDtypeStruct((n_rows, MOE_TC), F32)],
        compiler_params=_cparams("arbitrary"),
        name="moe_sort",
    )(rank_t, gate_t, cnt_b)
    return [a.reshape(n_rows * LIST_TILES, 1, ROW_TILE) for a in outs]


SCATTER_BATCH = 16


PIPE_STEPS = MAX_TILES + 2
N_PIECES = 8


def _routed_step(gsrc_ref, ssrc_ref, wt_ref, h_ref, acc_ref, wgb_ref, wub_ref, wdb_ref,
                 xs_g, xs_f, ys_f, ys_s):
    def scatter(r0):
        rows = range(r0, r0 + SCATTER_BATCH)
        toks = [pl.multiple_of(ssrc_ref[0, 0, r], ROW_SUB) for r in rows]
        olds = [acc_ref[pl.ds(t, ROW_SUB), :] for t in toks]
        news = [old + ys_s[r // 8, pl.ds(r % 8, ROW_SUB, stride=8), :]
                for old, r in zip(olds, rows)]
        for t, new in zip(toks, news):
            acc_ref[pl.ds(t, ROW_SUB), :] = new

    def gather(r0):
        for r in range(r0, r0 + ROW_TILE // N_PIECES):
            t = pl.multiple_of(gsrc_ref[0, 0, r], ROW_SUB)
            xs_g[r // 8, pl.ds(r % 8, ROW_SUB, stride=8), :] = h_ref[pl.ds(t, ROW_SUB), :]

    def side_work(piece):
        per = ROW_TILE // N_PIECES
        for r0 in range(piece * per, (piece + 1) * per, SCATTER_BATCH):
            scatter(r0)
        gather(piece * per)

    kc = D_MODEL // (N_PIECES // 2)
    hg = hu = None
    for p in range(N_PIECES // 2):
        side_work(p)
        xk = jnp.concatenate(
            [xs_f[:, s * 8:(s + 1) * 8, :].reshape(ROW_TILE, LANES)
             for s in range(p * kc // LANES, (p + 1) * kc // LANES)], axis=1).astype(BF16)
        dg = _dot(xk, wgb_ref[p * kc:(p + 1) * kc, :])
        du = _dot(xk, wub_ref[p * kc:(p + 1) * kc, :])
        hg = dg if hg is None else hg + dg
        hu = du if hu is None else hu + du
    eye = (lax.broadcasted_iota(jnp.int32, (ROW_TILE, ROW_TILE), 0)
           == lax.broadcasted_iota(jnp.int32, (ROW_TILE, ROW_TILE), 1))
    wcol = jnp.sum(jnp.where(eye, wt_ref[0], 0.0), axis=1, keepdims=True)
    a = (_silu(hg) * hu * wcol).astype(BF16)
    for p in range(N_PIECES // 2):
        side_work(N_PIECES // 2 + p)
        y = _dot(a, wdb_ref[:, p * kc:(p + 1) * kc])
        for q in range(kc // LANES):
            s = p * kc // LANES + q
            ys_f[:, s * 8:(s + 1) * 8, :] = y[:, q * LANES:(q + 1) * LANES].reshape(
                ROW_TILE // 8, 8, LANES)


def _moe_routed_kernel(te_ref, tl_ref, tc_ref, gsrc_ref, ssrc_ref, wt_ref, h_ref,
                       wg_ref, wu_ref, wd_ref, x_ref, sh_ref, g2_ref, out_ref,
                       acc_ref, wgb_ref, wub_ref, wdb_ref, xs0_ref, xs1_ref, ys0_ref, ys1_ref):
    c = pl.program_id(0)
    j = pl.program_id(1)
    tile = lambda jj: c * MAX_TILES + jnp.clip(jj, 0, MAX_TILES - 1)

    @pl.when(j == 0)
    def _():
        for ref in (acc_ref, xs0_ref, xs1_ref, ys0_ref, ys1_ref):
            ref[...] = jnp.zeros_like(ref)

    active = jnp.logical_and(j < PIPE_STEPS, jnp.logical_or(j < 2, tc_ref[tile(j - 2)] > 0))

    @pl.when(jnp.logical_and(active, jnp.logical_or(j < 2, te_ref[tile(j - 1)] != te_ref[tile(j - 2)])))
    def _():
        wgb_ref[...] = wg_ref[0, 0].astype(BF16)
        wub_ref[...] = wu_ref[0, 0].astype(BF16)
        wdb_ref[...] = wd_ref[0, 0].astype(BF16)

    step = functools.partial(_routed_step, gsrc_ref, ssrc_ref, wt_ref, h_ref, acc_ref,
                             wgb_ref, wub_ref, wdb_ref)

    @pl.when(jnp.logical_and(active, j % 2 == 0))
    def _():
        step(xs0_ref, xs1_ref, ys1_ref, ys0_ref)

    @pl.when(jnp.logical_and(active, j % 2 == 1))
    def _():
        step(xs1_ref, xs0_ref, ys0_ref, ys1_ref)

    @pl.when(j >= PIPE_STEPS)
    def _():
        base = (j - PIPE_STEPS) * (TILE * ROW_SUB)
        moe = jnp.concatenate(
            [acc_ref[pl.ds(base + s, TILE, stride=ROW_SUB), :] for s in range(ROW_SUB)], axis=1)
        out_ref[...] = x_ref[...] + g2_ref[pl.ds(c, 1), :] * (moe + sh_ref[...])


def _moe_routed(gsrc, ssrc, wts, t_exp, t_list, t_cnt, hrows, w_gate, w_up, w_down, li,
                x, shared, mod):
    def tile_idx(c, jj):
        return c * MAX_TILES + jnp.clip(jj, 0, MAX_TILES - 1)

    def flush_blk(c, j):
        return c * N_FLUSH + jnp.maximum(j - PIPE_STEPS, 0)

    def list_map(lag):
        def index_map(c, j, te, tl, tc):
            jj = j - lag
            real = jnp.logical_and(jj >= 0, jj < MAX_TILES)
            return (jnp.where(real, tl[tile_idx(c, jj)], DUMMY_TILE), 0, 0)
        return index_map

    w_map = lambda c, j, te, tl, tc: (li, te[tile_idx(c, j - 1)], 0, 0)
    w_in_spec = pl.BlockSpec((1, 1, D_MODEL, D_EXPERT), w_map)
    w_out_spec = pl.BlockSpec((1, 1, D_EXPERT, D_MODEL), w_map)
    tok_spec = pl.BlockSpec((TILE, D_MODEL), lambda c, j, te, tl, tc: (flush_blk(c, j), 0))
    stage = pltpu.VMEM((ROW_TILE // 8, 8 * ROW_SUB, LANES), F32)
    grid_spec = pltpu.PrefetchScalarGridSpec(
        num_scalar_prefetch=3,
        grid=(N_CHUNKS, PIPE_STEPS + N_FLUSH),
        in_specs=[
            pl.BlockSpec((1, 1, ROW_TILE), list_map(0), memory_space=pltpu.SMEM),
            pl.BlockSpec((1, 1, ROW_TILE), list_map(2), memory_space=pltpu.SMEM),
            pl.BlockSpec((1, 1, ROW_TILE), list_map(1)),
            pl.BlockSpec((MOE_TC * ROW_SUB, LANES), lambda c, j, te, tl, tc: (c, 0),
                         pipeline_mode=pl.Buffered(1)),
            w_in_spec, w_in_spec, w_out_spec,
            tok_spec, tok_spec,
            pl.BlockSpec((8, D_MODEL), lambda c, j, te, tl, tc: (0, 5)),
        ],
        out_specs=tok_spec,
        scratch_shapes=[
            pltpu.VMEM(((MOE_TC + 1) * ROW_SUB, LANES), F32),
            pltpu.VMEM((D_MODEL, D_EXPERT), BF16),
            pltpu.VMEM((D_MODEL, D_EXPERT), BF16),
            pltpu.VMEM((D_EXPERT, D_MODEL), BF16),
            stage, stage, stage, stage,
        ],
    )
    return pl.pallas_call(
        _moe_routed_kernel,
        grid_spec=grid_spec,
        out_shape=jax.ShapeDtypeStruct((T_ALL, D_MODEL), F32),
        compiler_params=_cparams("arbitrary", "arbitrary"),
        name="moe_routed",
    )(t_exp, t_list, t_cnt, gsrc, ssrc, wts, hrows, w_gate, w_up, w_down, x, shared, mod)


def kernel(x_prompt, x_sample, cache_k, cache_v, c, c_ctx, w_mod, b_mod, norm1, norm2, w_in_even, conv_a, q_norm, k_norm, lam_q1, lam_k1, lam_q2, lam_k2, subln, w_out_even, w_in_odd, conv_c, conv_c_b, ln_c_g, ln_c_b, w_pool, pool_scale, w_out_odd, w_router, b_router, w_gate, w_up, w_down, ws_gate, ws_up, ws_down):
    x = (x_prompt.reshape(T_P, D_MODEL), x_sample.reshape(T_S, D_MODEL))
    cond = jnp.concatenate([c_ctx[None, :], c, jnp.zeros((8 - 1 - DEC_BATCH, D_MODEL), F32)], axis=0)
    mod_all = _modulation(cond, w_mod, b_mod)

    new_k, new_v = [], []
    for li in range(DEPTH):
        mod = mod_all[li]
        if li % 2 == 0:
            e = li // 2
            lam_init = 0.8 - 0.6 * math.exp(-0.3 * li)
            u = _norm_in(x, norm1[li], mod, w_in_even[e].astype(BF16))
            q, k, v, k_f32 = _qkv_prep(u, q_norm[e], k_norm[e])
            new_k.append(jnp.transpose(k_f32[:, :T_P].reshape(H_B, BATCH, SEQ, LANES), (1, 0, 2, 3)))
            new_v.append(jnp.transpose(
                u[:T_P, 3 * D_A + 2 * D_B:].reshape(BATCH, SEQ, H_B, DV_B), (0, 2, 1, 3)))
            o = _attention(q, k, v, cache_k, cache_v, e,
                           (lam_q1[e], lam_k1[e], lam_q2[e], lam_k2[e]), subln[e], lam_init)
            x = _even_out(x, u, o, conv_a[e], mod, w_out_even[e].astype(BF16))
        else:
            o_ = li // 2
            u = _norm_in(x, norm1[li], mod, w_in_odd[o_].astype(BF16))
            x = _odd_out(x, u, conv_c[o_], conv_c_b[o_], ln_c_g[o_], ln_c_b[o_],
                         w_pool[o_].astype(BF16), pool_scale[o_], mod, w_out_odd[o_].astype(BF16))
        hrows, gate_b, rank_b, shared = _moe_pre(
            x, norm2[li], mod, w_router[li].T, b_router[li], ws_gate[li].astype(BF16),
            ws_up[li].astype(BF16), ws_down[li].astype(BF16))
        t_exp, t_list, t_cnt, counts = _tile_tables(rank_b)
        gsrc, ssrc, wts = _moe_sort(rank_b, gate_b, counts)
        x = _moe_routed(gsrc, ssrc, wts, t_exp, t_list, t_cnt, hrows, w_gate, w_up, w_down, li,
                        x, shared, mod)

    y_prompt = x[:T_P].reshape(BATCH, SEQ, D_MODEL)
    y_sample = x[T_P:].reshape(DEC_BATCH, DEC_SEQ, D_MODEL)

    return (y_prompt, y_sample, jnp.stack(new_k, axis=1), jnp.stack(new_v, axis=1))
```

```python
import functools
import math

import numpy as np
import jax
import jax.numpy as jnp
from jax import lax
from jax.experimental import pallas as pl
from jax.experimental.pallas import tpu as pltpu

D_MODEL = 1024
BATCH = 16
SEQ = 256
DEPTH = 2
DEC_BATCH = 2
DEC_SEQ = 4096
PAST_LEN = 512
GRID_W = 64
H_B = 4
DK_B = 64
DV_B = 2 * DK_B
D_A = D_MODEL // 2
D_B = H_B * DV_B
D_C = D_MODEL // 2
D_D = D_MODEL // 2
CONV_C = 31
POOL_WINDOWS = (2, 4, 8, 16)
D_DG = D_D // len(POOL_WINDOWS)
N_EXPERTS = 64
TOP_K = 8
N_GROUPS = 8
TOPK_GROUPS = 4
D_EXPERT = 256
D_SHARED = 256
ROUTED_SCALE = 2.5
ROPE_BASE = 10000.0
EPS = 1e-6

F32 = jnp.float32
BF16 = jnp.bfloat16

T_P = BATCH * SEQ
T_S = DEC_BATCH * DEC_SEQ
T_ALL = T_P + T_S
TILE = 256
N_TILES = T_ALL // TILE
P_TILES = T_P // TILE
S_TILES = DEC_SEQ // TILE
LANES = 128
VMEM_LIMIT = 56 * 1024 * 1024


def _cparams(*sem):
    return pltpu.CompilerParams(dimension_semantics=sem, vmem_limit_bytes=VMEM_LIMIT)


def _mod_row(i, tm):
    npt = T_P // tm
    per = DEC_SEQ // tm
    return jnp.where(i < npt, 0, 1 + (i - npt) // per)


def _seq_flags(i):
    j = (i - P_TILES) % S_TILES
    first = jnp.logical_or(i < P_TILES, j == 0)
    last = jnp.logical_or(i < P_TILES, j == S_TILES - 1)
    return first, last


def _seq_tile(i):
    return jnp.where(i < P_TILES, 0, (i - P_TILES) % S_TILES)


def _split_bf16(a):
    hi = a.astype(BF16)
    lo = (a - hi.astype(F32)).astype(BF16)
    return hi, lo


def _dot(a, b):
    return jnp.dot(a, b, preferred_element_type=F32)


def _dot_nt(a, b):
    return lax.dot_general(a, b, (((1,), (1,)), ((), ())), preferred_element_type=F32)


def _dot3(a, b):
    a_hi, a_lo = _split_bf16(a)
    b_hi, b_lo = _split_bf16(b)
    return _dot(a_hi, b_hi) + _dot(a_lo, b_hi) + _dot(a_hi, b_lo)


def _silu(x):
    return x * jax.nn.sigmoid(x)


MOD_TN = 1536


def _mod_kernel(c_ref, w_ref, b_ref, o_ref):
    o_ref[0] = _dot3(_silu(c_ref[...]), w_ref[0]) + b_ref[0]


def _modulation(cond, w_mod, b_mod):
    n = 6 * D_MODEL
    return pl.pallas_call(
        _mod_kernel,
        grid=(DEPTH, n // MOD_TN),
        in_specs=[
            pl.BlockSpec((8, D_MODEL), lambda l, j: (0, 0)),
            pl.BlockSpec((1, D_MODEL, MOD_TN), lambda l, j: (l, 0, j)),
            pl.BlockSpec((1, 1, MOD_TN), lambda l, j: (l, 0, j)),
        ],
        out_specs=pl.BlockSpec((1, 8, MOD_TN), lambda l, j: (l, 0, j)),
        out_shape=jax.ShapeDtypeStruct((DEPTH, 8, n), F32),
        compiler_params=_cparams("arbitrary", "arbitrary"),
        name="modulation",
    )(cond, w_mod, b_mod.reshape(DEPTH, 1, n))


IN_TM = 512


def _modulated_norm(x, g, shift, scale):
    ms = jnp.mean(x * x, axis=-1, keepdims=True)
    return (x * lax.rsqrt(ms + EPS) * g) * (1.0 + scale) + shift


def _tok_specs(x, tm, width):
    if isinstance(x, tuple):
        n_p = T_P // tm
        return ([pl.BlockSpec((tm, width), lambda i: (jnp.minimum(i, n_p - 1), 0)),
                 pl.BlockSpec((tm, width), lambda i: (jnp.maximum(i - n_p, 0), 0))], list(x))
    return [pl.BlockSpec((tm, width), lambda i: (i, 0))], [x]


def _tok_load(refs, i, tm):
    if len(refs) == 2:
        return jnp.where(i < T_P // tm, refs[0][...], refs[1][...])
    return refs[0][...]


def _norm_in_kernel(*refs, n_x):
    x_refs, (g_ref, sh_ref, sc_ref, w_ref, o_ref) = refs[:n_x], refs[n_x:]
    i = pl.program_id(0)
    r = _mod_row(i, IN_TM)
    h = _modulated_norm(_tok_load(x_refs, i, IN_TM), g_ref[...], sh_ref[pl.ds(r, 1), :],
                        sc_ref[pl.ds(r, 1), :])
    o_ref[...] = _dot(h.astype(BF16), w_ref[...])


def _norm_in(x, g, mod, w_bf16):
    n = w_bf16.shape[1]
    x_specs, x_ops = _tok_specs(x, IN_TM, D_MODEL)
    return pl.pallas_call(
        functools.partial(_norm_in_kernel, n_x=len(x_ops)),
        grid=(T_ALL // IN_TM,),
        in_specs=x_specs + [
            pl.BlockSpec((1, D_MODEL), lambda i: (0, 0)),
            pl.BlockSpec((8, D_MODEL), lambda i: (0, 0)),
            pl.BlockSpec((8, D_MODEL), lambda i: (0, 1)),
            pl.BlockSpec((D_MODEL, n), lambda i: (0, 0)),
        ],
        out_specs=pl.BlockSpec((IN_TM, n), lambda i: (i, 0)),
        out_shape=jax.ShapeDtypeStruct((T_ALL, n), F32),
        compiler_params=_cparams("arbitrary"),
        name="norm_in_proj",
    )(*x_ops, g.reshape(1, D_MODEL), mod, mod, w_bf16)


QKV_TM = 1024


def _rope_tables():
    half = DK_B // 2
    freqs = ROPE_BASE ** (-np.arange(0, half, 2, dtype=np.float64) / half)
    l = np.arange(DEC_SEQ)
    pos_r = (l // GRID_W).astype(np.float64)
    pos_c = (l % GRID_W).astype(np.float64)
    lane = np.arange(LANES)
    jj = lane % DK_B
    m = jj % half
    f = m % (half // 2)
    pos = np.where((jj < half)[None, :], pos_r[:, None], pos_c[:, None])
    ang = pos * freqs[f][None, :]
    sign = np.where(m < half // 2, -1.0, 1.0)[None, :]
    cos = np.concatenate([np.ones((QKV_TM, LANES)), np.cos(ang)], axis=0)
    sin = np.concatenate([np.zeros((QKV_TM, LANES)), sign * np.sin(ang)], axis=0)
    return cos.astype(np.float32), sin.astype(np.float32)


def _segment_mean_matrix():
    lane = np.arange(LANES)
    same = (lane[:, None] // DK_B) == (lane[None, :] // DK_B)
    return (same.astype(np.float32) / DK_B)


def _qk_prep(x, g, cos, sin, seg):
    x2 = x * x
    hi, lo = _split_bf16(x2)
    ms = _dot(hi, seg) + _dot(lo, seg)
    y = x * lax.rsqrt(ms + EPS) * g
    lane = lax.broadcasted_iota(jnp.int32, y.shape, 1)
    lower = (lane % (DK_B // 2)) < (DK_B // 4)
    partner = jnp.where(lower, pltpu.roll(y, LANES - DK_B // 4, 1), pltpu.roll(y, DK_B // 4, 1))
    return y * cos + partner * sin


def _qkv_prep_kernel(q_ref, k_ref, v_ref, qn_ref, kn_ref, cos_ref, sin_ref, seg_ref,
                     qo_ref, ko_ref, vo_ref, kf_ref):
    cos = cos_ref[...]
    sin = sin_ref[...]
    seg = seg_ref[...]
    scale = math.log2(math.e) / math.sqrt(DK_B)
    for h in range(H_B):
        cols = slice(h * LANES, (h + 1) * LANES)
        qo_ref[h] = (_qk_prep(q_ref[:, cols], qn_ref[...], cos, sin, seg) * scale).astype(BF16)
        k = _qk_prep(k_ref[:, cols], kn_ref[...], cos, sin, seg)
        kf_ref[h] = k
        ko_ref[h] = k.astype(BF16)
        vo_ref[h] = v_ref[:, cols].astype(BF16)


def _qkv_prep(u, qn, kn):
    cos, sin = _rope_tables()
    seg = jnp.asarray(_segment_mean_matrix(), BF16)
    qn2 = jnp.concatenate([qn, qn]).reshape(1, LANES)
    kn2 = jnp.concatenate([kn, kn]).reshape(1, LANES)
    col0 = 3 * D_A // D_B
    p_steps = T_P // QKV_TM
    s_steps = DEC_SEQ // QKV_TM

    def tab_map(i):
        return (jnp.where(i < p_steps, 0, 1 + (i - p_steps) % s_steps), 0)

    out_b = jax.ShapeDtypeStruct((H_B, T_ALL, LANES), BF16)
    out_spec = pl.BlockSpec((H_B, QKV_TM, LANES), lambda i: (0, i, 0))
    return pl.pallas_call(
        _qkv_prep_kernel,
        grid=(T_ALL // QKV_TM,),
        in_specs=[
            pl.BlockSpec((QKV_TM, D_B), lambda i: (i, col0)),
            pl.BlockSpec((QKV_TM, D_B), lambda i: (i, col0 + 1)),
            pl.BlockSpec((QKV_TM, D_B), lambda i: (i, col0 + 2)),
            pl.BlockSpec((1, LANES), lambda i: (0, 0)),
            pl.BlockSpec((1, LANES), lambda i: (0, 0)),
            pl.BlockSpec((QKV_TM, LANES), tab_map),
            pl.BlockSpec((QKV_TM, LANES), tab_map),
            pl.BlockSpec((LANES, LANES), lambda i: (0, 0)),
        ],
        out_specs=[out_spec] * 4,
        out_shape=[out_b, out_b, out_b, jax.ShapeDtypeStruct((H_B, T_ALL, LANES), F32)],
        compiler_params=_cparams("arbitrary"),
        name="qkv_prep",
    )(u, u, u, qn2, kn2, jnp.asarray(cos), jnp.asarray(sin), seg)


def _lambda(lq1_ref, lk1_ref, lq2_ref, lk2_ref, lam_init):
    a = jnp.sum(lq1_ref[...] * lk1_ref[...], axis=-1, keepdims=True)
    b = jnp.sum(lq2_ref[...] * lk2_ref[...], axis=-1, keepdims=True)
    return jnp.exp(a) - jnp.exp(b) + lam_init


def _attn_body(q, keys, vals, lam, subg, lam_init):
    lane = lax.broadcasted_iota(jnp.int32, q.shape, 1)
    zero = jnp.zeros_like(q)
    qa = jnp.where(lane < DK_B, q, zero)
    qb = jnp.where(lane < DK_B, zero, q)
    scores = [[_dot_nt(qq, k) for k in keys] for qq in (qa, qb)]
    outs = []
    for ss in scores:
        m = functools.reduce(jnp.maximum, [jnp.max(s, axis=-1, keepdims=True) for s in ss])
        ps = [jnp.exp2(s - m) for s in ss]
        l = functools.reduce(jnp.add, [jnp.sum(p, axis=-1, keepdims=True) for p in ps])
        pv = functools.reduce(jnp.add, [_dot(p.astype(BF16), v) for p, v in zip(ps, vals)])
        outs.append(pv / l)
    o = outs[0] - lam * outs[1]
    ms = jnp.mean(o * o, axis=-1, keepdims=True)
    return (o * lax.rsqrt(ms + EPS) * subg) * (1.0 - lam_init)


def _attn_prompt_kernel(q_ref, k_ref, v_ref, lq1, lk1, lq2, lk2, sg_ref, o_ref, *, lam_init):
    lam = _lambda(lq1, lk1, lq2, lk2, lam_init)
    o_ref[...] = _attn_body(q_ref[0], [k_ref[0]], [v_ref[0]], lam, sg_ref[...], lam_init)


def _attn_latent_kernel(q_ref, k_ref, v_ref, ck_ref, cv_ref, lq1, lk1, lq2, lk2, sg_ref, o_ref,
                        *, lam_init):
    lam = _lambda(lq1, lk1, lq2, lk2, lam_init)
    keys = [ck_ref[0, 0, 0].astype(BF16), k_ref[0]]
    vals = [cv_ref[0, 0, 0].astype(BF16), v_ref[0]]
    o_ref[...] = _attn_body(q_ref[0], keys, vals, lam, sg_ref[...], lam_init)


def _attention(q, k, v, cache_k, cache_v, e, lam_params, subg, lam_init):
    small = [p.reshape(1, DK_B) for p in lam_params] + [subg.reshape(1, DV_B)]
    small_specs2 = [pl.BlockSpec((1, DK_B), lambda b, h: (0, 0))] * 4 + \
                   [pl.BlockSpec((1, DV_B), lambda b, h: (0, 0))]
    small_specs3 = [pl.BlockSpec((1, DK_B), lambda s, h, j: (0, 0))] * 4 + \
                   [pl.BlockSpec((1, DV_B), lambda s, h, j: (0, 0))]
    o_prompt = pl.pallas_call(
        functools.partial(_attn_prompt_kernel, lam_init=lam_init),
        grid=(BATCH, H_B),
        in_specs=[pl.BlockSpec((1, SEQ, LANES), lambda b, h: (h, b, 0))] * 3 + small_specs2,
        out_specs=pl.BlockSpec((SEQ, LANES), lambda b, h: (b, h)),
        out_shape=jax.ShapeDtypeStruct((T_P, D_B), F32),
        compiler_params=_cparams("arbitrary", "arbitrary"),
        name="attn_prompt",
    )(q, k, v, *small)

    kv_spec = pl.BlockSpec((1, DEC_SEQ, LANES), lambda s, h, j: (h, T_P // DEC_SEQ + s, 0))
    c_spec = pl.BlockSpec((1, 1, 1, PAST_LEN, LANES), lambda s, h, j: (s, e, h, 0, 0))
    o_latent = pl.pallas_call(
        functools.partial(_attn_latent_kernel, lam_init=lam_init),
        grid=(DEC_BATCH, H_B, S_TILES),
        in_specs=[pl.BlockSpec((1, TILE, LANES), lambda s, h, j: (h, P_TILES + s * S_TILES + j, 0)),
                  kv_spec, kv_spec, c_spec, c_spec] + small_specs3,
        out_specs=pl.BlockSpec((TILE, LANES), lambda s, h, j: (s * S_TILES + j, h)),
        out_shape=jax.ShapeDtypeStruct((T_S, D_B), F32),
        compiler_params=_cparams("arbitrary", "arbitrary", "arbitrary"),
        name="attn_latent",
    )(q, k, v, cache_k, cache_v, *small)
    return o_prompt, o_latent


HALO8 = 8
HALO16 = 16


def _prev_block(i, rows):
    return jnp.maximum(i * (TILE // rows) - 1, 0)


def _next_block(i, rows):
    return jnp.minimum((i + 1) * (TILE // rows), T_ALL // rows - 1)


def _even_out_kernel(*refs, n_x, n_o):
    x_refs, o_refs = refs[:n_x], refs[n_x:n_x + n_o]
    (bg_ref, cg_ref, hin_ref, cgp_ref, hinp_ref, cgn_ref, hinn_ref, cw_ref, g1_ref, w_ref,
     out_ref) = refs[n_x + n_o:]
    i = pl.program_id(0)
    first, last = _seq_flags(i)
    r = _mod_row(i, TILE)
    z = cg_ref[...] * hin_ref[...]
    zp = jnp.where(first, 0.0, cgp_ref[HALO8 - 1:HALO8, :] * hinp_ref[HALO8 - 1:HALO8, :])
    zn = jnp.where(last, 0.0, cgn_ref[0:1, :] * hinn_ref[0:1, :])
    row = lax.broadcasted_iota(jnp.int32, z.shape, 0)
    z_prev = jnp.where(row == 0, zp, pltpu.roll(z, 1, 0))
    z_next = jnp.where(row == TILE - 1, zn, pltpu.roll(z, TILE - 1, 0))
    cw = cw_ref[...]
    ya = bg_ref[...] * (cw[0:1, :] * z_prev + cw[1:2, :] * z + cw[2:3, :] * z_next)
    o = _tok_load(o_refs, i, TILE)
    y = _dot(ya.astype(BF16), w_ref[0:D_A, :]) + _dot(o.astype(BF16), w_ref[D_A:, :])
    out_ref[...] = _tok_load(x_refs, i, TILE) + g1_ref[pl.ds(r, 1), :] * y


def _even_out(x, u, o, conv_w, mod, w_out_bf16):
    tile_spec = lambda c: pl.BlockSpec((TILE, D_A), lambda i: (i, c))
    prev_spec = lambda c: pl.BlockSpec((HALO8, D_A), lambda i: (_prev_block(i, HALO8), c))
    next_spec = lambda c: pl.BlockSpec((HALO8, D_A), lambda i: (_next_block(i, HALO8), c))
    x_specs, x_ops = _tok_specs(x, TILE, D_MODEL)
    o_specs, o_ops = _tok_specs(o, TILE, D_B)
    return pl.pallas_call(
        functools.partial(_even_out_kernel, n_x=len(x_ops), n_o=len(o_ops)),
        grid=(N_TILES,),
        in_specs=x_specs + o_specs + [
            tile_spec(0), tile_spec(1), tile_spec(2),
            prev_spec(1), prev_spec(2), next_spec(1), next_spec(2),
            pl.BlockSpec((3, D_A), lambda i: (0, 0)),
            pl.BlockSpec((8, D_MODEL), lambda i: (0, 2)),
            pl.BlockSpec((D_MODEL, D_MODEL), lambda i: (0, 0)),
        ],
        out_specs=pl.BlockSpec((TILE, D_MODEL), lambda i: (i, 0)),
        out_shape=jax.ShapeDtypeStruct((T_ALL, D_MODEL), F32),
        compiler_params=_cparams("arbitrary"),
        name="even_mixer_out",
    )(*x_ops, *o_ops, u, u, u, u, u, u, u, conv_w, mod, w_out_bf16)


def _odd_out_kernel(x_ref, a_ref, b_ref, pd_ref, ap_ref, bp_ref, an_ref, bn_ref, pp_ref, pn_ref,
                    cw_ref, cb_ref, lg_ref, lb_ref, wp_ref, ps_ref, g1_ref, w_ref, out_ref,
                    ext_ref, extp_ref, shift_ref):
    i = pl.program_id(0)
    first, last = _seq_flags(i)
    r = _mod_row(i, TILE)
    ext_ref[0:HALO16, :] = jnp.where(first, 0.0, ap_ref[...] * jax.nn.sigmoid(bp_ref[...]))
    ext_ref[HALO16:HALO16 + TILE, :] = a_ref[...] * jax.nn.sigmoid(b_ref[...])
    ext_ref[HALO16 + TILE:, :] = jnp.where(last, 0.0, an_ref[...] * jax.nn.sigmoid(bn_ref[...]))
    base = HALO16 - CONV_C // 2
    parts = []
    for cb in range(D_C // LANES):
        cols = slice(cb * LANES, (cb + 1) * LANES)
        acc = jnp.zeros((TILE, LANES), F32)
        for phase in range(8):
            taps = [j for j in range(CONV_C) if (base + j) % 8 == phase]
            reach = max((base + j) // 8 for j in taps)
            rows = TILE + 8 * reach
            shift_ref[0:rows, :] = ext_ref[pl.ds(phase, rows), cols]
            for j in taps:
                a = (base + j) // 8
                acc = acc + cw_ref[j:j + 1, cols] * shift_ref[8 * a:8 * a + TILE, :]
        parts.append(acc)
    g = jnp.concatenate(parts, axis=-1) + cb_ref[...]
    mu = jnp.mean(g, axis=-1, keepdims=True)
    var = jnp.mean(jnp.square(g - mu), axis=-1, keepdims=True)
    g = _silu(((g - mu) * lax.rsqrt(var + EPS)) * lg_ref[...] + lb_ref[...])
    extp_ref[0:HALO8, :] = jnp.where(first, 0.0, pp_ref[...])
    extp_ref[HALO8:HALO8 + TILE, :] = pd_ref[...]
    extp_ref[HALO8 + TILE:, :] = jnp.where(last, 0.0, pn_ref[...])
    seq_len = jnp.where(i < P_TILES, SEQ, DEC_SEQ)
    pos = _seq_tile(i) * TILE + lax.broadcasted_iota(jnp.int32, (TILE, 1), 0)
    yd = []
    for gi, w in enumerate(POOL_WINDOWS):
        cols = slice(gi * D_DG, (gi + 1) * D_DG)
        s = jnp.zeros((TILE, D_DG), F32)
        for d in range(-(w // 2), w - w // 2):
            s = s + extp_ref[pl.ds(HALO8 + d, TILE), cols]
        lo = jnp.maximum(pos - w // 2, 0)
        hi = jnp.minimum(pos - w // 2 + w, seq_len)
        pooled = s / (hi - lo).astype(F32) - pd_ref[:, cols]
        yd.append(_dot(pooled.astype(BF16), wp_ref[gi]))
    yd = jnp.concatenate(yd, axis=-1) * ps_ref[...]
    y = _dot(g.astype(BF16), w_ref[0:D_C, :]) + _dot(yd.astype(BF16), w_ref[D_C:, :])
    out_ref[...] = x_ref[...] + g1_ref[pl.ds(r, 1), :] * y


def _odd_out(x, u, conv_w, conv_b, ln_g, ln_b, w_pool_bf16, p_scale, mod, w_out_bf16):
    tile_spec = lambda c: pl.BlockSpec((TILE, D_C), lambda i: (i, c))
    prev_spec = lambda rows, c: pl.BlockSpec((rows, D_C), lambda i: (_prev_block(i, rows), c))
    next_spec = lambda rows, c: pl.BlockSpec((rows, D_C), lambda i: (_next_block(i, rows), c))
    vec = lambda: pl.BlockSpec((1, D_C), lambda i: (0, 0))
    return pl.pallas_call(
        _odd_out_kernel,
        grid=(N_TILES,),
        in_specs=[
            pl.BlockSpec((TILE, D_MODEL), lambda i: (i, 0)),
            tile_spec(0), tile_spec(1), tile_spec(2),
            prev_spec(HALO16, 0), prev_spec(HALO16, 1), next_spec(HALO16, 0), next_spec(HALO16, 1),
            prev_spec(HALO8, 2), next_spec(HALO8, 2),
            pl.BlockSpec((CONV_C, D_C), lambda i: (0, 0)),
            vec(), vec(), vec(),
            pl.BlockSpec((len(POOL_WINDOWS), D_DG, D_DG), lambda i: (0, 0, 0)),
            vec(),
            pl.BlockSpec((8, D_MODEL), lambda i: (0, 2)),
            pl.BlockSpec((D_MODEL, D_MODEL), lambda i: (0, 0)),
        ],
        out_specs=pl.BlockSpec((TILE, D_MODEL), lambda i: (i, 0)),
        out_shape=jax.ShapeDtypeStruct((T_ALL, D_MODEL), F32),
        scratch_shapes=[pltpu.VMEM((TILE + 2 * HALO16, D_C), F32),
                        pltpu.VMEM((TILE + 2 * HALO8, D_D), F32),
                        pltpu.VMEM((TILE + 2 * HALO16, LANES), F32)],
        compiler_params=_cparams("arbitrary"),
        name="odd_mixer_out",
    )(x, u, u, u, u, u, u, u, u, u, conv_w, conv_b.reshape(1, D_C), ln_g.reshape(1, D_C),
      ln_b.reshape(1, D_C), w_pool_bf16, p_scale.reshape(1, D_D), mod, w_out_bf16)


GROUP = N_EXPERTS // N_GROUPS
NEG_INF = float("-inf")


def _first_argmax(v, idx, axis):
    m = jnp.max(v, axis=axis, keepdims=True)
    big = jnp.int32(2 ** 30)
    am = jnp.min(jnp.where(v == m, idx, big), axis=axis, keepdims=True)
    return m, am


def _route(scores, biased):
    shape = biased.shape
    member = lax.broadcasted_iota(jnp.int32, shape, 1)
    m1, a1 = _first_argmax(biased, member, 1)
    m2 = jnp.max(jnp.where(member == a1, NEG_INF, biased), axis=1, keepdims=True)
    gscore = m1 + m2
    gidx = lax.broadcasted_iota(jnp.int32, gscore.shape, 0)
    gsel = jnp.zeros(gscore.shape, jnp.bool_)
    for _ in range(TOPK_GROUPS):
        _, am = _first_argmax(gscore, gidx, 0)
        hit = gidx == am
        gsel = jnp.logical_or(gsel, hit)
        gscore = jnp.where(hit, NEG_INF, gscore)
    cand = jnp.where(gsel, biased, NEG_INF)
    eidx = lax.broadcasted_iota(jnp.int32, shape, 0) * GROUP + member
    sel = jnp.zeros(shape, jnp.bool_)
    for _ in range(TOP_K):
        m = jnp.max(jnp.max(cand, axis=1, keepdims=True), axis=0, keepdims=True)
        big = jnp.int32(2 ** 30)
        am = jnp.where(cand == m, eidx, big)
        am = jnp.min(jnp.min(am, axis=1, keepdims=True), axis=0, keepdims=True)
        hit = eidx == am
        sel = jnp.logical_or(sel, hit)
        cand = jnp.where(hit, NEG_INF, cand)
    wsel = jnp.where(sel, scores, 0.0)
    tot = jnp.sum(jnp.sum(wsel, axis=1, keepdims=True), axis=0, keepdims=True)
    return wsel / tot * ROUTED_SCALE, sel


MOE_TC = 4096
N_CHUNKS = T_ALL // MOE_TC
ROW_TILE = 256
N_FLUSH = MOE_TC // TILE
ROW_SUB = D_MODEL // LANES


def _moe_pre_kernel(x_ref, g_ref, sh_ref, sc_ref, wr_ref, br_ref, tri_ref, wsg_ref, wsu_ref,
                    wsd_ref, hrow_ref, gate_ref, rank_ref, shared_ref, carry_ref):
    i = pl.program_id(0)
    r = _mod_row(i, TILE)
    h = _modulated_norm(x_ref[...], g_ref[...], sh_ref[pl.ds(r, 1), :], sc_ref[pl.ds(r, 1), :])
    hb = h.astype(BF16)
    for s in range(ROW_SUB):
        hrow_ref[pl.ds(s, TILE, stride=ROW_SUB), :] = h[:, s * LANES:(s + 1) * LANES]
    h_hi, h_lo = hb, (h - hb.astype(F32)).astype(BF16)
    w_hi, w_lo = _split_bf16(wr_ref[...])
    logits = _dot_nt(w_hi, h_hi) + _dot_nt(w_lo, h_hi) + _dot_nt(w_hi, h_lo)
    scores = jax.nn.sigmoid(logits)
    biased = scores + br_ref[:, 0:1]
    shape3 = (N_GROUPS, GROUP, TILE)
    gate_t, sel = _route(scores.reshape(shape3), biased.reshape(shape3))
    gate_t = gate_t.reshape(N_EXPERTS, TILE)
    sel = jnp.where(sel.reshape(N_EXPERTS, TILE), 1.0, 0.0)

    @pl.when(i % N_FLUSH == 0)
    def _():
        carry_ref[...] = jnp.zeros_like(carry_ref)

    carry = carry_ref[...]
    local = _dot(sel.astype(BF16), tri_ref[...])
    rank = jnp.where(sel > 0.0, local + jnp.concatenate([carry] * (TILE // LANES), axis=1), -1.0)
    carry_ref[...] = carry + jnp.sum(sel, axis=1, keepdims=True)
    gate_ref[...] = gate_t
    rank_ref[...] = rank
    a = _silu(_dot(hb, wsg_ref[...])) * _dot(hb, wsu_ref[...])
    shared_ref[...] = _dot(a.astype(BF16), wsd_ref[...])


def _moe_pre(x, g, mod, w_router_t, b_router, wsg, wsu, wsd):
    return pl.pallas_call(
        _moe_pre_kernel,
        grid=(N_TILES,),
        in_specs=[
            pl.BlockSpec((TILE, D_MODEL), lambda i: (i, 0)),
            pl.BlockSpec((1, D_MODEL), lambda i: (0, 0)),
            pl.BlockSpec((8, D_MODEL), lambda i: (0, 3)),
            pl.BlockSpec((8, D_MODEL), lambda i: (0, 4)),
            pl.BlockSpec((N_EXPERTS, D_MODEL), lambda i: (0, 0)),
            pl.BlockSpec((N_EXPERTS, LANES), lambda i: (0, 0)),
            pl.BlockSpec((TILE, TILE), lambda i: (0, 0)),
            pl.BlockSpec((D_MODEL, D_SHARED), lambda i: (0, 0)),
            pl.BlockSpec((D_MODEL, D_SHARED), lambda i: (0, 0)),
            pl.BlockSpec((D_SHARED, D_MODEL), lambda i: (0, 0)),
        ],
        out_specs=[
            pl.BlockSpec((TILE * ROW_SUB, LANES), lambda i: (i, 0)),
            pl.BlockSpec((N_EXPERTS, TILE), lambda i: (0, i)),
            pl.BlockSpec((N_EXPERTS, TILE), lambda i: (0, i)),
            pl.BlockSpec((TILE, D_MODEL), lambda i: (i, 0)),
        ],
        out_shape=[
            jax.ShapeDtypeStruct((T_ALL * ROW_SUB, LANES), F32),
            jax.ShapeDtypeStruct((N_EXPERTS, T_ALL), F32),
            jax.ShapeDtypeStruct((N_EXPERTS, T_ALL), F32),
            jax.ShapeDtypeStruct((T_ALL, D_MODEL), F32),
        ],
        scratch_shapes=[pltpu.VMEM((N_EXPERTS, LANES), F32)],
        compiler_params=_cparams("arbitrary"),
        name="moe_pre",
    )(x, g.reshape(1, D_MODEL), mod, mod, w_router_t,
      jnp.broadcast_to(b_router.reshape(N_EXPERTS, 1), (N_EXPERTS, LANES)),
      jnp.asarray(np.triu(np.ones((TILE, TILE), np.float32), 1), BF16), wsg, wsu, wsd)


LIST_ROWS = MOE_TC // LANES
TILE_ROWS = ROW_TILE // LANES
DUMMY_ROW = MOE_TC * ROW_SUB


def _expert_counts(rank_t):
    sel = (rank_t >= 0.0).astype(jnp.int32).reshape(N_EXPERTS, N_CHUNKS, MOE_TC)
    return jnp.sum(sel, axis=-1).T


def _moe_sort_kernel(rank_ref, gate_ref, cnt_ref, gsrc_ref, ssrc_ref, w_ref):
    rank = rank_ref[...]
    lane = lax.broadcasted_iota(jnp.int32, rank.shape, 1)
    d = jnp.where(rank >= 0.0, lane - rank.astype(jnp.int32), 0)
    w = gate_ref[...]
    for s in range(MOE_TC.bit_length() - 1):
        k = 1 << s
        d_in = pltpu.roll(d, MOE_TC - k, 1)
        w_in = pltpu.roll(w, MOE_TC - k, 1)
        take = (d_in & k) != 0
        leave = (d & k) != 0
        d = jnp.where(take, d_in, jnp.where(leave, 0, d))
        w = jnp.where(take, w_in, w)
    valid = lane < cnt_ref[0][:, 0:1]
    row = (lane + d) * ROW_SUB
    gsrc = jnp.where(valid, row, 0)
    ssrc = jnp.where(valid, row, DUMMY_ROW)
    w = jnp.where(valid, w, 0.0)
    for b in range(LIST_ROWS):
        cols = slice(b * LANES, (b + 1) * LANES)
        dst = pl.ds(b, N_EXPERTS, stride=LIST_ROWS)
        gsrc_ref[dst, :] = gsrc[:, cols]
        ssrc_ref[dst, :] = ssrc[:, cols]
        w_ref[dst, :] = w[:, cols]


def _moe_sort(rank_t, gate_t, counts):
    chunk_spec = pl.BlockSpec((N_EXPERTS, MOE_TC), lambda c: (0, c))
    cnt_b = jnp.broadcast_to(counts[:, :, None], (N_CHUNKS, N_EXPERTS, LANES))
    n_rows = N_CHUNKS * N_EXPERTS * LIST_ROWS
    return pl.pallas_call(
        _moe_sort_kernel,
        grid=(N_CHUNKS,),
        in_specs=[chunk_spec, chunk_spec,
                  pl.BlockSpec((1, N_EXPERTS, LANES), lambda c: (c, 0, 0))],
        out_specs=[pl.BlockSpec((N_EXPERTS * LIST_ROWS, LANES), lambda c: (c, 0))] * 3,
        out_shape=[jax.ShapeDtypeStruct((n_rows, LANES), jnp.int32),
                   jax.ShapeDtypeStruct((n_rows, LANES), jnp.int32),
                   jax.ShapeDtypeStruct((n_rows, LANES), F32)],
        compiler_params=_cparams("arbitrary"),
        name="moe_sort",
    )(rank_t, gate_t, cnt_b)


SCATTER_BATCH = 16


def _expert_tile(t, gsrc_ref, ssrc_ref, wt_ref, h_ref, acc_ref, wgb_ref, wub_ref, wdb_ref,
                 xs_ref, ys_ref):
    base = t * TILE_ROWS
    for r in range(ROW_TILE):
        tok = pl.multiple_of(gsrc_ref[base + r // LANES, r % LANES], ROW_SUB)
        xs_ref[r // 8, pl.ds(r % 8, ROW_SUB, stride=8), :] = h_ref[pl.ds(tok, ROW_SUB), :]
    xt = jnp.concatenate(
        [xs_ref[:, s * 8:(s + 1) * 8, :].reshape(ROW_TILE, LANES) for s in range(ROW_SUB)],
        axis=1).astype(BF16)
    eye = (lax.broadcasted_iota(jnp.int32, (ROW_TILE, LANES), 0) % LANES
           == lax.broadcasted_iota(jnp.int32, (ROW_TILE, LANES), 1))
    row_blk = lax.broadcasted_iota(jnp.int32, (ROW_TILE, LANES), 0) // LANES
    wrows = functools.reduce(
        lambda a, b: a + b,
        [jnp.where(row_blk == k, wt_ref[pl.ds(base + k, 1), :], 0.0) for k in range(TILE_ROWS)])
    wcol = jnp.sum(jnp.where(eye, wrows, 0.0), axis=1, keepdims=True)
    a = _silu(_dot(xt, wgb_ref[...])) * _dot(xt, wub_ref[...]) * wcol
    y = _dot(a.astype(BF16), wdb_ref[...])
    for s in range(ROW_SUB):
        ys_ref[:, s * 8:(s + 1) * 8, :] = y[:, s * LANES:(s + 1) * LANES].reshape(
            ROW_TILE // 8, 8, LANES)
    for r0 in range(0, ROW_TILE, SCATTER_BATCH):
        rows = range(r0, r0 + SCATTER_BATCH)
        toks = [pl.multiple_of(ssrc_ref[base + r // LANES, r % LANES], ROW_SUB) for r in rows]
        olds = [acc_ref[pl.ds(tok, ROW_SUB), :] for tok in toks]
        news = [old + ys_ref[r // 8, pl.ds(r % 8, ROW_SUB, stride=8), :]
                for old, r in zip(olds, rows)]
        for tok, new in zip(toks, news):
            acc_ref[pl.ds(tok, ROW_SUB), :] = new


def _moe_routed_kernel(nt_ref, gsrc_ref, ssrc_ref, wt_ref, h_ref, wg_ref, wu_ref, wd_ref,
                       x_ref, sh_ref, g2_ref, out_ref,
                       acc_ref, wgb_ref, wub_ref, wdb_ref, xs_ref, ys_ref):
    c = pl.program_id(0)
    j = pl.program_id(1)

    @pl.when(j == 0)
    def _():
        acc_ref[...] = jnp.zeros_like(acc_ref)

    @pl.when(j < N_EXPERTS)
    def _():
        wgb_ref[...] = wg_ref[0, 0].astype(BF16)
        wub_ref[...] = wu_ref[0, 0].astype(BF16)
        wdb_ref[...] = wd_ref[0, 0].astype(BF16)

        def tile(t, carry):
            _expert_tile(t, gsrc_ref, ssrc_ref, wt_ref, h_ref, acc_ref, wgb_ref, wub_ref,
                         wdb_ref, xs_ref, ys_ref)
            return carry

        lax.fori_loop(0, nt_ref[c * N_EXPERTS + j], tile, 0)

    @pl.when(j >= N_EXPERTS)
    def _():
        base = (j - N_EXPERTS) * (TILE * ROW_SUB)
        moe = jnp.concatenate(
            [acc_ref[pl.ds(base + s, TILE, stride=ROW_SUB), :] for s in range(ROW_SUB)], axis=1)
        out_ref[...] = x_ref[...] + g2_ref[pl.ds(c, 1), :] * (moe + sh_ref[...])


def _moe_routed(gsrc, ssrc, wts, counts, hrows, w_gate, w_up, w_down, li, x, shared, mod):
    n_tiles = ((counts + ROW_TILE - 1) // ROW_TILE).reshape(-1).astype(jnp.int32)
    expert = lambda j: jnp.minimum(j, N_EXPERTS - 1)
    list_map = lambda c, j, nt: (c * N_EXPERTS + expert(j), 0)
    w_map = lambda c, j, nt: (li, expert(j), 0, 0)
    w_in_spec = pl.BlockSpec((1, 1, D_MODEL, D_EXPERT), w_map)
    w_out_spec = pl.BlockSpec((1, 1, D_EXPERT, D_MODEL), w_map)
    tok_spec = pl.BlockSpec(
        (TILE, D_MODEL), lambda c, j, nt: (c * N_FLUSH + jnp.maximum(j - N_EXPERTS, 0), 0))
    stage = pltpu.VMEM((ROW_TILE // 8, 8 * ROW_SUB, LANES), F32)
    grid_spec = pltpu.PrefetchScalarGridSpec(
        num_scalar_prefetch=1,
        grid=(N_CHUNKS, N_EXPERTS + N_FLUSH),
        in_specs=[
            pl.BlockSpec((LIST_ROWS, LANES), list_map, memory_space=pltpu.SMEM),
            pl.BlockSpec((LIST_ROWS, LANES), list_map, memory_space=pltpu.SMEM),
            pl.BlockSpec((LIST_ROWS, LANES), list_map),
            pl.BlockSpec((MOE_TC * ROW_SUB, LANES), lambda c, j, nt: (c, 0),
                         pipeline_mode=pl.Buffered(1)),
            w_in_spec, w_in_spec, w_out_spec,
            tok_spec, tok_spec,
            pl.BlockSpec((8, D_MODEL), lambda c, j, nt: (0, 5)),
        ],
        out_specs=tok_spec,
        scratch_shapes=[
            pltpu.VMEM(((MOE_TC + 1) * ROW_SUB, LANES), F32),
            pltpu.VMEM((D_MODEL, D_EXPERT), BF16),
            pltpu.VMEM((D_MODEL, D_EXPERT), BF16),
            pltpu.VMEM((D_EXPERT, D_MODEL), BF16),
            stage, stage,
        ],
    )
    return pl.pallas_call(
        _moe_routed_kernel,
        grid_spec=grid_spec,
        out_shape=jax.ShapeDtypeStruct((T_ALL, D_MODEL), F32),
        compiler_params=_cparams("arbitrary", "arbitrary"),
        name="moe_routed",
    )(n_tiles, gsrc, ssrc, wts, hrows, w_gate, w_up, w_down, x, shared, mod)


def kernel(x_prompt, x_sample, cache_k, cache_v, c, c_ctx, w_mod, b_mod, norm1, norm2, w_in_even, conv_a, q_norm, k_norm, lam_q1, lam_k1, lam_q2, lam_k2, subln, w_out_even, w_in_odd, conv_c, conv_c_b, ln_c_g, ln_c_b, w_pool, pool_scale, w_out_odd, w_router, b_router, w_gate, w_up, w_down, ws_gate, ws_up, ws_down):
    x = (x_prompt.reshape(T_P, D_MODEL), x_sample.reshape(T_S, D_MODEL))
    cond = jnp.concatenate([c_ctx[None, :], c, jnp.zeros((8 - 1 - DEC_BATCH, D_MODEL), F32)], axis=0)
    mod_all = _modulation(cond, w_mod, b_mod)

    new_k, new_v = [], []
    for li in range(DEPTH):
        mod = mod_all[li]
        if li % 2 == 0:
            e = li // 2
            lam_init = 0.8 - 0.6 * math.exp(-0.3 * li)
            u = _norm_in(x, norm1[li], mod, w_in_even[e].astype(BF16))
            q, k, v, k_f32 = _qkv_prep(u, q_norm[e], k_norm[e])
            new_k.append(jnp.transpose(k_f32[:, :T_P].reshape(H_B, BATCH, SEQ, LANES), (1, 0, 2, 3)))
            new_v.append(jnp.transpose(
                u[:T_P, 3 * D_A + 2 * D_B:].reshape(BATCH, SEQ, H_B, DV_B), (0, 2, 1, 3)))
            o = _attention(q, k, v, cache_k, cache_v, e,
                           (lam_q1[e], lam_k1[e], lam_q2[e], lam_k2[e]), subln[e], lam_init)
            x = _even_out(x, u, o, conv_a[e], mod, w_out_even[e].astype(BF16))
        else:
            o_ = li // 2
            u = _norm_in(x, norm1[li], mod, w_in_odd[o_].astype(BF16))
            x = _odd_out(x, u, conv_c[o_], conv_c_b[o_], ln_c_g[o_], ln_c_b[o_],
                         w_pool[o_].astype(BF16), pool_scale[o_], mod, w_out_odd[o_].astype(BF16))
        hrows, gate_b, rank_b, shared = _moe_pre(
            x, norm2[li], mod, w_router[li].T, b_router[li], ws_gate[li].astype(BF16),
            ws_up[li].astype(BF16), ws_down[li].astype(BF16))
        counts = _expert_counts(rank_b)
        gsrc, ssrc, wts = _moe_sort(rank_b, gate_b, counts)
        x = _moe_routed(gsrc, ssrc, wts, counts, hrows, w_gate, w_up, w_down, li, x, shared, mod)

    y_prompt = x[:T_P].reshape(BATCH, SEQ, D_MODEL)
    y_sample = x[T_P:].reshape(DEC_BATCH, DEC_SEQ, D_MODEL)

    return (y_prompt, y_sample, jnp.stack(new_k, axis=1), jnp.stack(new_v, axis=1))
```

```python
import functools
import math

import numpy as np
import jax
import jax.numpy as jnp
from jax import lax
from jax.experimental import pallas as pl
from jax.experimental.pallas import tpu as pltpu

D_MODEL = 1024
BATCH = 16
SEQ = 256
DEPTH = 2
DEC_BATCH = 2
DEC_SEQ = 4096
PAST_LEN = 512
GRID_W = 64
H_B = 4
DK_B = 64
DV_B = 2 * DK_B
D_A = D_MODEL // 2
D_B = H_B * DV_B
D_C = D_MODEL // 2
D_D = D_MODEL // 2
CONV_C = 31
POOL_WINDOWS = (2, 4, 8, 16)
D_DG = D_D // len(POOL_WINDOWS)
N_EXPERTS = 64
TOP_K = 8
N_GROUPS = 8
TOPK_GROUPS = 4
D_EXPERT = 256
D_SHARED = 256
ROUTED_SCALE = 2.5
ROPE_BASE = 10000.0
EPS = 1e-6

F32 = jnp.float32
BF16 = jnp.bfloat16

T_P = BATCH * SEQ
T_S = DEC_BATCH * DEC_SEQ
T_ALL = T_P + T_S
TILE = 256
N_TILES = T_ALL // TILE
P_TILES = T_P // TILE
S_TILES = DEC_SEQ // TILE
LANES = 128
VMEM_LIMIT = 56 * 1024 * 1024


def _cparams(*sem):
    return pltpu.CompilerParams(dimension_semantics=sem, vmem_limit_bytes=VMEM_LIMIT)


def _mod_row(i, tm):
    npt = T_P // tm
    per = DEC_SEQ // tm
    return jnp.where(i < npt, 0, 1 + (i - npt) // per)


def _seq_flags(i):
    j = (i - P_TILES) % S_TILES
    first = jnp.logical_or(i < P_TILES, j == 0)
    last = jnp.logical_or(i < P_TILES, j == S_TILES - 1)
    return first, last


def _seq_tile(i):
    return jnp.where(i < P_TILES, 0, (i - P_TILES) % S_TILES)


def _split_bf16(a):
    hi = a.astype(BF16)
    lo = (a - hi.astype(F32)).astype(BF16)
    return hi, lo


def _dot(a, b):
    return jnp.dot(a, b, preferred_element_type=F32)


def _dot_nt(a, b):
    return lax.dot_general(a, b, (((1,), (1,)), ((), ())), preferred_element_type=F32)


def _dot3(a, b):
    a_hi, a_lo = _split_bf16(a)
    b_hi, b_lo = _split_bf16(b)
    return _dot(a_hi, b_hi) + _dot(a_lo, b_hi) + _dot(a_hi, b_lo)


def _silu(x):
    return x * jax.nn.sigmoid(x)


MOD_TN = 1536


def _mod_kernel(c_ref, w_ref, b_ref, o_ref):
    o_ref[0] = _dot3(_silu(c_ref[...]), w_ref[0]) + b_ref[0]


def _modulation(cond, w_mod, b_mod):
    n = 6 * D_MODEL
    return pl.pallas_call(
        _mod_kernel,
        grid=(DEPTH, n // MOD_TN),
        in_specs=[
            pl.BlockSpec((8, D_MODEL), lambda l, j: (0, 0)),
            pl.BlockSpec((1, D_MODEL, MOD_TN), lambda l, j: (l, 0, j)),
            pl.BlockSpec((1, 1, MOD_TN), lambda l, j: (l, 0, j)),
        ],
        out_specs=pl.BlockSpec((1, 8, MOD_TN), lambda l, j: (l, 0, j)),
        out_shape=jax.ShapeDtypeStruct((DEPTH, 8, n), F32),
        compiler_params=_cparams("arbitrary", "arbitrary"),
        name="modulation",
    )(cond, w_mod, b_mod.reshape(DEPTH, 1, n))


IN_TM = 512


def _modulated_norm(x, g, shift, scale):
    ms = jnp.mean(x * x, axis=-1, keepdims=True)
    return (x * lax.rsqrt(ms + EPS) * g) * (1.0 + scale) + shift


def _tok_specs(x, tm, width):
    if isinstance(x, tuple):
        n_p = T_P // tm
        return ([pl.BlockSpec((tm, width), lambda i: (jnp.minimum(i, n_p - 1), 0)),
                 pl.BlockSpec((tm, width), lambda i: (jnp.maximum(i - n_p, 0), 0))], list(x))
    return [pl.BlockSpec((tm, width), lambda i: (i, 0))], [x]


def _tok_load(refs, i, tm):
    if len(refs) == 2:
        return jnp.where(i < T_P // tm, refs[0][...], refs[1][...])
    return refs[0][...]


def _norm_in_kernel(*refs, n_x):
    x_refs, (g_ref, sh_ref, sc_ref, w_ref, o_ref) = refs[:n_x], refs[n_x:]
    i = pl.program_id(0)
    r = _mod_row(i, IN_TM)
    h = _modulated_norm(_tok_load(x_refs, i, IN_TM), g_ref[...], sh_ref[pl.ds(r, 1), :],
                        sc_ref[pl.ds(r, 1), :])
    o_ref[...] = _dot(h.astype(BF16), w_ref[...])


def _norm_in(x, g, mod, w_bf16):
    n = w_bf16.shape[1]
    x_specs, x_ops = _tok_specs(x, IN_TM, D_MODEL)
    return pl.pallas_call(
        functools.partial(_norm_in_kernel, n_x=len(x_ops)),
        grid=(T_ALL // IN_TM,),
        in_specs=x_specs + [
            pl.BlockSpec((1, D_MODEL), lambda i: (0, 0)),
            pl.BlockSpec((8, D_MODEL), lambda i: (0, 0)),
            pl.BlockSpec((8, D_MODEL), lambda i: (0, 1)),
            pl.BlockSpec((D_MODEL, n), lambda i: (0, 0)),
        ],
        out_specs=pl.BlockSpec((IN_TM, n), lambda i: (i, 0)),
        out_shape=jax.ShapeDtypeStruct((T_ALL, n), F32),
        compiler_params=_cparams("arbitrary"),
        name="norm_in_proj",
    )(*x_ops, g.reshape(1, D_MODEL), mod, mod, w_bf16)


QKV_TM = 1024


def _rope_tables():
    half = DK_B // 2
    freqs = ROPE_BASE ** (-np.arange(0, half, 2, dtype=np.float64) / half)
    l = np.arange(DEC_SEQ)
    pos_r = (l // GRID_W).astype(np.float64)
    pos_c = (l % GRID_W).astype(np.float64)
    lane = np.arange(LANES)
    jj = lane % DK_B
    m = jj % half
    f = m % (half // 2)
    pos = np.where((jj < half)[None, :], pos_r[:, None], pos_c[:, None])
    ang = pos * freqs[f][None, :]
    sign = np.where(m < half // 2, -1.0, 1.0)[None, :]
    cos = np.concatenate([np.ones((QKV_TM, LANES)), np.cos(ang)], axis=0)
    sin = np.concatenate([np.zeros((QKV_TM, LANES)), sign * np.sin(ang)], axis=0)
    return cos.astype(np.float32), sin.astype(np.float32)


def _segment_mean_matrix():
    lane = np.arange(LANES)
    same = (lane[:, None] // DK_B) == (lane[None, :] // DK_B)
    return (same.astype(np.float32) / DK_B)


def _qk_prep(x, g, cos, sin, seg):
    x2 = x * x
    hi, lo = _split_bf16(x2)
    ms = _dot(hi, seg) + _dot(lo, seg)
    y = x * lax.rsqrt(ms + EPS) * g
    lane = lax.broadcasted_iota(jnp.int32, y.shape, 1)
    lower = (lane % (DK_B // 2)) < (DK_B // 4)
    partner = jnp.where(lower, pltpu.roll(y, LANES - DK_B // 4, 1), pltpu.roll(y, DK_B // 4, 1))
    return y * cos + partner * sin


def _qkv_prep_kernel(q_ref, k_ref, v_ref, qn_ref, kn_ref, cos_ref, sin_ref, seg_ref,
                     qo_ref, ko_ref, vo_ref, kf_ref):
    cos = cos_ref[...]
    sin = sin_ref[...]
    seg = seg_ref[...]
    scale = math.log2(math.e) / math.sqrt(DK_B)
    for h in range(H_B):
        cols = slice(h * LANES, (h + 1) * LANES)
        qo_ref[h] = (_qk_prep(q_ref[:, cols], qn_ref[...], cos, sin, seg) * scale).astype(BF16)
        k = _qk_prep(k_ref[:, cols], kn_ref[...], cos, sin, seg)
        kf_ref[h] = k
        ko_ref[h] = k.astype(BF16)
        vo_ref[h] = v_ref[:, cols].astype(BF16)


def _qkv_prep(u, qn, kn):
    cos, sin = _rope_tables()
    seg = jnp.asarray(_segment_mean_matrix(), BF16)
    qn2 = jnp.concatenate([qn, qn]).reshape(1, LANES)
    kn2 = jnp.concatenate([kn, kn]).reshape(1, LANES)
    col0 = 3 * D_A // D_B
    p_steps = T_P // QKV_TM
    s_steps = DEC_SEQ // QKV_TM

    def tab_map(i):
        return (jnp.where(i < p_steps, 0, 1 + (i - p_steps) % s_steps), 0)

    out_b = jax.ShapeDtypeStruct((H_B, T_ALL, LANES), BF16)
    out_spec = pl.BlockSpec((H_B, QKV_TM, LANES), lambda i: (0, i, 0))
    return pl.pallas_call(
        _qkv_prep_kernel,
        grid=(T_ALL // QKV_TM,),
        in_specs=[
            pl.BlockSpec((QKV_TM, D_B), lambda i: (i, col0)),
            pl.BlockSpec((QKV_TM, D_B), lambda i: (i, col0 + 1)),
            pl.BlockSpec((QKV_TM, D_B), lambda i: (i, col0 + 2)),
            pl.BlockSpec((1, LANES), lambda i: (0, 0)),
            pl.BlockSpec((1, LANES), lambda i: (0, 0)),
            pl.BlockSpec((QKV_TM, LANES), tab_map),
            pl.BlockSpec((QKV_TM, LANES), tab_map),
            pl.BlockSpec((LANES, LANES), lambda i: (0, 0)),
        ],
        out_specs=[out_spec] * 4,
        out_shape=[out_b, out_b, out_b, jax.ShapeDtypeStruct((H_B, T_ALL, LANES), F32)],
        compiler_params=_cparams("arbitrary"),
        name="qkv_prep",
    )(u, u, u, qn2, kn2, jnp.asarray(cos), jnp.asarray(sin), seg)


def _lambda(lq1_ref, lk1_ref, lq2_ref, lk2_ref, lam_init):
    a = jnp.sum(lq1_ref[...] * lk1_ref[...], axis=-1, keepdims=True)
    b = jnp.sum(lq2_ref[...] * lk2_ref[...], axis=-1, keepdims=True)
    return jnp.exp(a) - jnp.exp(b) + lam_init


def _attn_body(q, keys, vals, lam, subg, lam_init):
    lane = lax.broadcasted_iota(jnp.int32, q.shape, 1)
    zero = jnp.zeros_like(q)
    qa = jnp.where(lane < DK_B, q, zero)
    qb = jnp.where(lane < DK_B, zero, q)
    scores = [[_dot_nt(qq, k) for k in keys] for qq in (qa, qb)]
    outs = []
    for ss in scores:
        m = functools.reduce(jnp.maximum, [jnp.max(s, axis=-1, keepdims=True) for s in ss])
        ps = [jnp.exp2(s - m) for s in ss]
        l = functools.reduce(jnp.add, [jnp.sum(p, axis=-1, keepdims=True) for p in ps])
        pv = functools.reduce(jnp.add, [_dot(p.astype(BF16), v) for p, v in zip(ps, vals)])
        outs.append(pv / l)
    o = outs[0] - lam * outs[1]
    ms = jnp.mean(o * o, axis=-1, keepdims=True)
    return (o * lax.rsqrt(ms + EPS) * subg) * (1.0 - lam_init)


def _attn_prompt_kernel(q_ref, k_ref, v_ref, lq1, lk1, lq2, lk2, sg_ref, o_ref, *, lam_init):
    lam = _lambda(lq1, lk1, lq2, lk2, lam_init)
    o_ref[...] = _attn_body(q_ref[0], [k_ref[0]], [v_ref[0]], lam, sg_ref[...], lam_init)


def _attn_latent_kernel(q_ref, k_ref, v_ref, ck_ref, cv_ref, lq1, lk1, lq2, lk2, sg_ref, o_ref,
                        *, lam_init):
    lam = _lambda(lq1, lk1, lq2, lk2, lam_init)
    keys = [ck_ref[0, 0, 0].astype(BF16), k_ref[0]]
    vals = [cv_ref[0, 0, 0].astype(BF16), v_ref[0]]
    o_ref[...] = _attn_body(q_ref[0], keys, vals, lam, sg_ref[...], lam_init)


def _attention(q, k, v, cache_k, cache_v, e, lam_params, subg, lam_init):
    small = [p.reshape(1, DK_B) for p in lam_params] + [subg.reshape(1, DV_B)]
    small_specs2 = [pl.BlockSpec((1, DK_B), lambda b, h: (0, 0))] * 4 + \
                   [pl.BlockSpec((1, DV_B), lambda b, h: (0, 0))]
    small_specs3 = [pl.BlockSpec((1, DK_B), lambda s, h, j: (0, 0))] * 4 + \
                   [pl.BlockSpec((1, DV_B), lambda s, h, j: (0, 0))]
    o_prompt = pl.pallas_call(
        functools.partial(_attn_prompt_kernel, lam_init=lam_init),
        grid=(BATCH, H_B),
        in_specs=[pl.BlockSpec((1, SEQ, LANES), lambda b, h: (h, b, 0))] * 3 + small_specs2,
        out_specs=pl.BlockSpec((SEQ, LANES), lambda b, h: (b, h)),
        out_shape=jax.ShapeDtypeStruct((T_P, D_B), F32),
        compiler_params=_cparams("arbitrary", "arbitrary"),
        name="attn_prompt",
    )(q, k, v, *small)

    kv_spec = pl.BlockSpec((1, DEC_SEQ, LANES), lambda s, h, j: (h, T_P // DEC_SEQ + s, 0))
    c_spec = pl.BlockSpec((1, 1, 1, PAST_LEN, LANES), lambda s, h, j: (s, e, h, 0, 0))
    o_latent = pl.pallas_call(
        functools.partial(_attn_latent_kernel, lam_init=lam_init),
        grid=(DEC_BATCH, H_B, S_TILES),
        in_specs=[pl.BlockSpec((1, TILE, LANES), lambda s, h, j: (h, P_TILES + s * S_TILES + j, 0)),
                  kv_spec, kv_spec, c_spec, c_spec] + small_specs3,
        out_specs=pl.BlockSpec((TILE, LANES), lambda s, h, j: (s * S_TILES + j, h)),
        out_shape=jax.ShapeDtypeStruct((T_S, D_B), F32),
        compiler_params=_cparams("arbitrary", "arbitrary", "arbitrary"),
        name="attn_latent",
    )(q, k, v, cache_k, cache_v, *small)
    return o_prompt, o_latent


HALO8 = 8
HALO16 = 16


def _prev_block(i, rows):
    return jnp.maximum(i * (TILE // rows) - 1, 0)


def _next_block(i, rows):
    return jnp.minimum((i + 1) * (TILE // rows), T_ALL // rows - 1)


def _even_out_kernel(*refs, n_x, n_o):
    x_refs, o_refs = refs[:n_x], refs[n_x:n_x + n_o]
    (bg_ref, cg_ref, hin_ref, cgp_ref, hinp_ref, cgn_ref, hinn_ref, cw_ref, g1_ref, w_ref,
     out_ref) = refs[n_x + n_o:]
    i = pl.program_id(0)
    first, last = _seq_flags(i)
    r = _mod_row(i, TILE)
    z = cg_ref[...] * hin_ref[...]
    zp = jnp.where(first, 0.0, cgp_ref[HALO8 - 1:HALO8, :] * hinp_ref[HALO8 - 1:HALO8, :])
    zn = jnp.where(last, 0.0, cgn_ref[0:1, :] * hinn_ref[0:1, :])
    row = lax.broadcasted_iota(jnp.int32, z.shape, 0)
    z_prev = jnp.where(row == 0, zp, pltpu.roll(z, 1, 0))
    z_next = jnp.where(row == TILE - 1, zn, pltpu.roll(z, TILE - 1, 0))
    cw = cw_ref[...]
    ya = bg_ref[...] * (cw[0:1, :] * z_prev + cw[1:2, :] * z + cw[2:3, :] * z_next)
    o = _tok_load(o_refs, i, TILE)
    y = _dot(ya.astype(BF16), w_ref[0:D_A, :]) + _dot(o.astype(BF16), w_ref[D_A:, :])
    out_ref[...] = _tok_load(x_refs, i, TILE) + g1_ref[pl.ds(r, 1), :] * y


def _even_out(x, u, o, conv_w, mod, w_out_bf16):
    tile_spec = lambda c: pl.BlockSpec((TILE, D_A), lambda i: (i, c))
    prev_spec = lambda c: pl.BlockSpec((HALO8, D_A), lambda i: (_prev_block(i, HALO8), c))
    next_spec = lambda c: pl.BlockSpec((HALO8, D_A), lambda i: (_next_block(i, HALO8), c))
    x_specs, x_ops = _tok_specs(x, TILE, D_MODEL)
    o_specs, o_ops = _tok_specs(o, TILE, D_B)
    return pl.pallas_call(
        functools.partial(_even_out_kernel, n_x=len(x_ops), n_o=len(o_ops)),
        grid=(N_TILES,),
        in_specs=x_specs + o_specs + [
            tile_spec(0), tile_spec(1), tile_spec(2),
            prev_spec(1), prev_spec(2), next_spec(1), next_spec(2),
            pl.BlockSpec((3, D_A), lambda i: (0, 0)),
            pl.BlockSpec((8, D_MODEL), lambda i: (0, 2)),
            pl.BlockSpec((D_MODEL, D_MODEL), lambda i: (0, 0)),
        ],
        out_specs=pl.BlockSpec((TILE, D_MODEL), lambda i: (i, 0)),
        out_shape=jax.ShapeDtypeStruct((T_ALL, D_MODEL), F32),
        compiler_params=_cparams("arbitrary"),
        name="even_mixer_out",
    )(*x_ops, *o_ops, u, u, u, u, u, u, u, conv_w, mod, w_out_bf16)


def _odd_out_kernel(x_ref, a_ref, b_ref, pd_ref, ap_ref, bp_ref, an_ref, bn_ref, pp_ref, pn_ref,
                    cw_ref, cb_ref, lg_ref, lb_ref, wp_ref, ps_ref, g1_ref, w_ref, out_ref,
                    ext_ref, extp_ref, shift_ref):
    i = pl.program_id(0)
    first, last = _seq_flags(i)
    r = _mod_row(i, TILE)
    ext_ref[0:HALO16, :] = jnp.where(first, 0.0, ap_ref[...] * jax.nn.sigmoid(bp_ref[...]))
    ext_ref[HALO16:HALO16 + TILE, :] = a_ref[...] * jax.nn.sigmoid(b_ref[...])
    ext_ref[HALO16 + TILE:, :] = jnp.where(last, 0.0, an_ref[...] * jax.nn.sigmoid(bn_ref[...]))
    base = HALO16 - CONV_C // 2
    parts = []
    for cb in range(D_C // LANES):
        cols = slice(cb * LANES, (cb + 1) * LANES)
        acc = jnp.zeros((TILE, LANES), F32)
        for phase in range(8):
            taps = [j for j in range(CONV_C) if (base + j) % 8 == phase]
            reach = max((base + j) // 8 for j in taps)
            rows = TILE + 8 * reach
            shift_ref[0:rows, :] = ext_ref[pl.ds(phase, rows), cols]
            for j in taps:
                a = (base + j) // 8
                acc = acc + cw_ref[j:j + 1, cols] * shift_ref[8 * a:8 * a + TILE, :]
        parts.append(acc)
    g = jnp.concatenate(parts, axis=-1) + cb_ref[...]
    mu = jnp.mean(g, axis=-1, keepdims=True)
    var = jnp.mean(jnp.square(g - mu), axis=-1, keepdims=True)
    g = _silu(((g - mu) * lax.rsqrt(var + EPS)) * lg_ref[...] + lb_ref[...])
    extp_ref[0:HALO8, :] = jnp.where(first, 0.0, pp_ref[...])
    extp_ref[HALO8:HALO8 + TILE, :] = pd_ref[...]
    extp_ref[HALO8 + TILE:, :] = jnp.where(last, 0.0, pn_ref[...])
    seq_len = jnp.where(i < P_TILES, SEQ, DEC_SEQ)
    pos = _seq_tile(i) * TILE + lax.broadcasted_iota(jnp.int32, (TILE, 1), 0)
    yd = []
    for gi, w in enumerate(POOL_WINDOWS):
        cols = slice(gi * D_DG, (gi + 1) * D_DG)
        s = jnp.zeros((TILE, D_DG), F32)
        for d in range(-(w // 2), w - w // 2):
            s = s + extp_ref[pl.ds(HALO8 + d, TILE), cols]
        lo = jnp.maximum(pos - w // 2, 0)
        hi = jnp.minimum(pos - w // 2 + w, seq_len)
        pooled = s / (hi - lo).astype(F32) - pd_ref[:, cols]
        yd.append(_dot(pooled.astype(BF16), wp_ref[gi]))
    yd = jnp.concatenate(yd, axis=-1) * ps_ref[...]
    y = _dot(g.astype(BF16), w_ref[0:D_C, :]) + _dot(yd.astype(BF16), w_ref[D_C:, :])
    out_ref[...] = x_ref[...] + g1_ref[pl.ds(r, 1), :] * y


def _odd_out(x, u, conv_w, conv_b, ln_g, ln_b, w_pool_bf16, p_scale, mod, w_out_bf16):
    tile_spec = lambda c: pl.BlockSpec((TILE, D_C), lambda i: (i, c))
    prev_spec = lambda rows, c: pl.BlockSpec((rows, D_C), lambda i: (_prev_block(i, rows), c))
    next_spec = lambda rows, c: pl.BlockSpec((rows, D_C), lambda i: (_next_block(i, rows), c))
    vec = lambda: pl.BlockSpec((1, D_C), lambda i: (0, 0))
    return pl.pallas_call(
        _odd_out_kernel,
        grid=(N_TILES,),
        in_specs=[
            pl.BlockSpec((TILE, D_MODEL), lambda i: (i, 0)),
            tile_spec(0), tile_spec(1), tile_spec(2),
            prev_spec(HALO16, 0), prev_spec(HALO16, 1), next_spec(HALO16, 0), next_spec(HALO16, 1),
            prev_spec(HALO8, 2), next_spec(HALO8, 2),
            pl.BlockSpec((CONV_C, D_C), lambda i: (0, 0)),
            vec(), vec(), vec(),
            pl.BlockSpec((len(POOL_WINDOWS), D_DG, D_DG), lambda i: (0, 0, 0)),
            vec(),
            pl.BlockSpec((8, D_MODEL), lambda i: (0, 2)),
            pl.BlockSpec((D_MODEL, D_MODEL), lambda i: (0, 0)),
        ],
        out_specs=pl.BlockSpec((TILE, D_MODEL), lambda i: (i, 0)),
        out_shape=jax.ShapeDtypeStruct((T_ALL, D_MODEL), F32),
        scratch_shapes=[pltpu.VMEM((TILE + 2 * HALO16, D_C), F32),
                        pltpu.VMEM((TILE + 2 * HALO8, D_D), F32),
                        pltpu.VMEM((TILE + 2 * HALO16, LANES), F32)],
        compiler_params=_cparams("arbitrary"),
        name="odd_mixer_out",
    )(x, u, u, u, u, u, u, u, u, u, conv_w, conv_b.reshape(1, D_C), ln_g.reshape(1, D_C),
      ln_b.reshape(1, D_C), w_pool_bf16, p_scale.reshape(1, D_D), mod, w_out_bf16)


GROUP = N_EXPERTS // N_GROUPS
NEG_INF = float("-inf")


def _first_argmax(v, idx, axis):
    m = jnp.max(v, axis=axis, keepdims=True)
    big = jnp.int32(2 ** 30)
    am = jnp.min(jnp.where(v == m, idx, big), axis=axis, keepdims=True)
    return m, am


def _route(scores, biased):
    shape = biased.shape
    member = lax.broadcasted_iota(jnp.int32, shape, 1)
    m1, a1 = _first_argmax(biased, member, 1)
    m2 = jnp.max(jnp.where(member == a1, NEG_INF, biased), axis=1, keepdims=True)
    gscore = m1 + m2
    gidx = lax.broadcasted_iota(jnp.int32, gscore.shape, 0)
    gsel = jnp.zeros(gscore.shape, jnp.bool_)
    for _ in range(TOPK_GROUPS):
        _, am = _first_argmax(gscore, gidx, 0)
        hit = gidx == am
        gsel = jnp.logical_or(gsel, hit)
        gscore = jnp.where(hit, NEG_INF, gscore)
    cand = jnp.where(gsel, biased, NEG_INF)
    eidx = lax.broadcasted_iota(jnp.int32, shape, 0) * GROUP + member
    sel = jnp.zeros(shape, jnp.bool_)
    for _ in range(TOP_K):
        m = jnp.max(jnp.max(cand, axis=1, keepdims=True), axis=0, keepdims=True)
        big = jnp.int32(2 ** 30)
        am = jnp.where(cand == m, eidx, big)
        am = jnp.min(jnp.min(am, axis=1, keepdims=True), axis=0, keepdims=True)
        hit = eidx == am
        sel = jnp.logical_or(sel, hit)
        cand = jnp.where(hit, NEG_INF, cand)
    wsel = jnp.where(sel, scores, 0.0)
    tot = jnp.sum(jnp.sum(wsel, axis=1, keepdims=True), axis=0, keepdims=True)
    return wsel / tot * ROUTED_SCALE, sel


MOE_TC = 4096
N_CHUNKS = T_ALL // MOE_TC
ROW_TILE = 256
N_FLUSH = MOE_TC // TILE
PRE_TM = 1024
ROW_SUB = D_MODEL // LANES


def _moe_pre_kernel(x_ref, g_ref, sh_ref, sc_ref, wr_ref, br_ref, tri_ref, wsg_ref, wsu_ref,
                    wsd_ref, hrow_ref, gate_ref, rank_ref, shared_ref, carry_ref):
    i = pl.program_id(0)
    r = _mod_row(i, PRE_TM)
    h = _modulated_norm(x_ref[...], g_ref[...], sh_ref[pl.ds(r, 1), :], sc_ref[pl.ds(r, 1), :])
    hb = h.astype(BF16)
    for s in range(ROW_SUB):
        hrow_ref[pl.ds(s, PRE_TM, stride=ROW_SUB), :] = h[:, s * LANES:(s + 1) * LANES]
    h_hi, h_lo = hb, (h - hb.astype(F32)).astype(BF16)
    w_hi, w_lo = _split_bf16(wr_ref[...])
    logits = _dot_nt(w_hi, h_hi) + _dot_nt(w_lo, h_hi) + _dot_nt(w_hi, h_lo)
    scores = jax.nn.sigmoid(logits)
    biased = scores + br_ref[:, 0:1]
    shape3 = (N_GROUPS, GROUP, PRE_TM)
    gate_t, sel = _route(scores.reshape(shape3), biased.reshape(shape3))
    gate_t = gate_t.reshape(N_EXPERTS, PRE_TM)
    sel = jnp.where(sel.reshape(N_EXPERTS, PRE_TM), 1.0, 0.0)

    @pl.when(i % (MOE_TC // PRE_TM) == 0)
    def _():
        carry_ref[...] = jnp.zeros_like(carry_ref)

    carry = carry_ref[...]
    local = _dot(sel.astype(BF16), tri_ref[...])
    rank = jnp.where(sel > 0.0, local + jnp.concatenate([carry] * (PRE_TM // LANES), axis=1), -1.0)
    carry_ref[...] = carry + jnp.sum(sel, axis=1, keepdims=True)
    gate_ref[...] = gate_t
    rank_ref[...] = rank
    a = _silu(_dot(hb, wsg_ref[...])) * _dot(hb, wsu_ref[...])
    shared_ref[...] = _dot(a.astype(BF16), wsd_ref[...])


def _moe_pre(x, g, mod, w_router_t, b_router, wsg, wsu, wsd):
    return pl.pallas_call(
        _moe_pre_kernel,
        grid=(T_ALL // PRE_TM,),
        in_specs=[
            pl.BlockSpec((PRE_TM, D_MODEL), lambda i: (i, 0)),
            pl.BlockSpec((1, D_MODEL), lambda i: (0, 0)),
            pl.BlockSpec((8, D_MODEL), lambda i: (0, 3)),
            pl.BlockSpec((8, D_MODEL), lambda i: (0, 4)),
            pl.BlockSpec((N_EXPERTS, D_MODEL), lambda i: (0, 0)),
            pl.BlockSpec((N_EXPERTS, LANES), lambda i: (0, 0)),
            pl.BlockSpec((PRE_TM, PRE_TM), lambda i: (0, 0)),
            pl.BlockSpec((D_MODEL, D_SHARED), lambda i: (0, 0)),
            pl.BlockSpec((D_MODEL, D_SHARED), lambda i: (0, 0)),
            pl.BlockSpec((D_SHARED, D_MODEL), lambda i: (0, 0)),
        ],
        out_specs=[
            pl.BlockSpec((PRE_TM * ROW_SUB, LANES), lambda i: (i, 0)),
            pl.BlockSpec((N_EXPERTS, PRE_TM), lambda i: (0, i)),
            pl.BlockSpec((N_EXPERTS, PRE_TM), lambda i: (0, i)),
            pl.BlockSpec((PRE_TM, D_MODEL), lambda i: (i, 0)),
        ],
        out_shape=[
            jax.ShapeDtypeStruct((T_ALL * ROW_SUB, LANES), F32),
            jax.ShapeDtypeStruct((N_EXPERTS, T_ALL), F32),
            jax.ShapeDtypeStruct((N_EXPERTS, T_ALL), F32),
            jax.ShapeDtypeStruct((T_ALL, D_MODEL), F32),
        ],
        scratch_shapes=[pltpu.VMEM((N_EXPERTS, LANES), F32)],
        compiler_params=_cparams("arbitrary"),
        name="moe_pre",
    )(x, g.reshape(1, D_MODEL), mod, mod, w_router_t,
      jnp.broadcast_to(b_router.reshape(N_EXPERTS, 1), (N_EXPERTS, LANES)),
      jnp.asarray(np.triu(np.ones((PRE_TM, PRE_TM), np.float32), 1), BF16), wsg, wsu, wsd)


LIST_ROWS = MOE_TC // LANES
TILE_ROWS = ROW_TILE // LANES
DUMMY_ROW = MOE_TC * ROW_SUB


def _expert_counts(rank_t):
    sel = (rank_t >= 0.0).astype(jnp.int32).reshape(N_EXPERTS, N_CHUNKS, MOE_TC)
    return jnp.sum(sel, axis=-1).T


def _moe_sort_kernel(rank_ref, gate_ref, cnt_ref, list_ref, w_ref):
    rank = rank_ref[...]
    lane = lax.broadcasted_iota(jnp.int32, rank.shape, 1)
    d = jnp.where(rank >= 0.0, lane - rank.astype(jnp.int32), 0)
    w = gate_ref[...]
    for s in range(MOE_TC.bit_length() - 1):
        k = 1 << s
        d_in = pltpu.roll(d, MOE_TC - k, 1)
        w_in = pltpu.roll(w, MOE_TC - k, 1)
        take = (d_in & k) != 0
        leave = (d & k) != 0
        d = jnp.where(take, d_in, jnp.where(leave, 0, d))
        w = jnp.where(take, w_in, w)
    valid = lane < cnt_ref[0][:, 0:1]
    row = (lane + d) * ROW_SUB
    gsrc = jnp.where(valid, row, 0)
    ssrc = jnp.where(valid, row, DUMMY_ROW)
    w = jnp.where(valid, w, 0.0)
    for b in range(LIST_ROWS):
        cols = slice(b * LANES, (b + 1) * LANES)
        list_ref[pl.ds(b, N_EXPERTS, stride=2 * LIST_ROWS), :] = gsrc[:, cols]
        list_ref[pl.ds(LIST_ROWS + b, N_EXPERTS, stride=2 * LIST_ROWS), :] = ssrc[:, cols]
        w_ref[pl.ds(b, N_EXPERTS, stride=LIST_ROWS), :] = w[:, cols]


def _moe_sort(rank_t, gate_t, counts):
    chunk_spec = pl.BlockSpec((N_EXPERTS, MOE_TC), lambda c: (0, c))
    cnt_b = jnp.broadcast_to(counts[:, :, None], (N_CHUNKS, N_EXPERTS, LANES))
    n_rows = N_CHUNKS * N_EXPERTS * LIST_ROWS
    return pl.pallas_call(
        _moe_sort_kernel,
        grid=(N_CHUNKS,),
        in_specs=[chunk_spec, chunk_spec,
                  pl.BlockSpec((1, N_EXPERTS, LANES), lambda c: (c, 0, 0))],
        out_specs=[pl.BlockSpec((N_EXPERTS * 2 * LIST_ROWS, LANES), lambda c: (c, 0)),
                   pl.BlockSpec((N_EXPERTS * LIST_ROWS, LANES), lambda c: (c, 0))],
        out_shape=[jax.ShapeDtypeStruct((2 * n_rows, LANES), jnp.int32),
                   jax.ShapeDtypeStruct((n_rows, LANES), F32)],
        compiler_params=_cparams("arbitrary"),
        name="moe_sort",
    )(rank_t, gate_t, cnt_b)


SCATTER_BATCH = 16


EXPERTS_PER_STEP = 2
EXPERT_STEPS = N_EXPERTS // EXPERTS_PER_STEP


def _expert_tile(t, list0, wt0, list_ref, wt_ref, h_ref, acc_ref, wgb_ref, wub_ref, wdb_ref,
                 xs_ref, ys_ref):
    gbase = list0 + t * TILE_ROWS
    sbase = gbase + LIST_ROWS
    for r in range(ROW_TILE):
        tok = pl.multiple_of(list_ref[gbase + r // LANES, r % LANES], ROW_SUB)
        xs_ref[r // 8, pl.ds(r % 8, ROW_SUB, stride=8), :] = h_ref[pl.ds(tok, ROW_SUB), :]
    xt = jnp.concatenate(
        [xs_ref[:, s * 8:(s + 1) * 8, :].reshape(ROW_TILE, LANES) for s in range(ROW_SUB)],
        axis=1).astype(BF16)
    eye = (lax.broadcasted_iota(jnp.int32, (ROW_TILE, LANES), 0) % LANES
           == lax.broadcasted_iota(jnp.int32, (ROW_TILE, LANES), 1))
    row_blk = lax.broadcasted_iota(jnp.int32, (ROW_TILE, LANES), 0) // LANES
    wrows = functools.reduce(
        lambda a, b: a + b,
        [jnp.where(row_blk == k, wt_ref[pl.ds(wt0 + t * TILE_ROWS + k, 1), :], 0.0)
         for k in range(TILE_ROWS)])
    wcol = jnp.sum(jnp.where(eye, wrows, 0.0), axis=1, keepdims=True)
    a = _silu(_dot(xt, wgb_ref[...])) * _dot(xt, wub_ref[...]) * wcol
    y = _dot(a.astype(BF16), wdb_ref[...])
    for s in range(ROW_SUB):
        ys_ref[:, s * 8:(s + 1) * 8, :] = y[:, s * LANES:(s + 1) * LANES].reshape(
            ROW_TILE // 8, 8, LANES)
    for r0 in range(0, ROW_TILE, SCATTER_BATCH):
        rows = range(r0, r0 + SCATTER_BATCH)
        toks = [pl.multiple_of(list_ref[sbase + r // LANES, r % LANES], ROW_SUB) for r in rows]
        olds = [acc_ref[pl.ds(tok, ROW_SUB), :] for tok in toks]
        news = [old + ys_ref[r // 8, pl.ds(r % 8, ROW_SUB, stride=8), :]
                for old, r in zip(olds, rows)]
        for tok, new in zip(toks, news):
            acc_ref[pl.ds(tok, ROW_SUB), :] = new


def _moe_routed_kernel(nt_ref, list_ref, wt_ref, h_ref, wg_ref, wu_ref, wd_ref,
                       x_ref, sh_ref, g2_ref, out_ref,
                       acc_ref, wgb_ref, wub_ref, wdb_ref, xs_ref, ys_ref):
    c = pl.program_id(0)
    j = pl.program_id(1)

    @pl.when(j == 0)
    def _():
        acc_ref[...] = jnp.zeros_like(acc_ref)

    for k in range(EXPERTS_PER_STEP):
        expert = jnp.minimum(j, EXPERT_STEPS - 1) * EXPERTS_PER_STEP + k
        n_tiles = nt_ref[c * N_EXPERTS + expert]

        @pl.when(jnp.logical_and(j < EXPERT_STEPS, n_tiles > 0))
        def _():
            wgb_ref[...] = wg_ref[0, k].astype(BF16)
            wub_ref[...] = wu_ref[0, k].astype(BF16)
            wdb_ref[...] = wd_ref[0, k].astype(BF16)

            def tile(t, carry):
                _expert_tile(t, k * 2 * LIST_ROWS, k * LIST_ROWS, list_ref, wt_ref, h_ref,
                             acc_ref, wgb_ref, wub_ref, wdb_ref, xs_ref, ys_ref)
                return carry

            lax.fori_loop(0, n_tiles, tile, 0)

    @pl.when(j >= EXPERT_STEPS)
    def _():
        base = (j - EXPERT_STEPS) * (TILE * ROW_SUB)
        moe = jnp.concatenate(
            [acc_ref[pl.ds(base + s, TILE, stride=ROW_SUB), :] for s in range(ROW_SUB)], axis=1)
        out_ref[...] = x_ref[...] + g2_ref[pl.ds(c, 1), :] * (moe + sh_ref[...])


def _moe_routed(lists, wts, counts, hrows, w_gate, w_up, w_down, li, x, shared, mod):
    n_tiles = ((counts + ROW_TILE - 1) // ROW_TILE).reshape(-1).astype(jnp.int32)
    group = lambda j: jnp.minimum(j, EXPERT_STEPS - 1)
    list_map = lambda c, j, nt: (c * EXPERT_STEPS + group(j), 0)
    w_map = lambda c, j, nt: (li, group(j), 0, 0)
    w_in_spec = pl.BlockSpec((1, EXPERTS_PER_STEP, D_MODEL, D_EXPERT), w_map)
    w_out_spec = pl.BlockSpec((1, EXPERTS_PER_STEP, D_EXPERT, D_MODEL), w_map)
    tok_spec = pl.BlockSpec(
        (TILE, D_MODEL), lambda c, j, nt: (c * N_FLUSH + jnp.maximum(j - EXPERT_STEPS, 0), 0))
    stage = pltpu.VMEM((ROW_TILE // 8, 8 * ROW_SUB, LANES), F32)
    grid_spec = pltpu.PrefetchScalarGridSpec(
        num_scalar_prefetch=1,
        grid=(N_CHUNKS, EXPERT_STEPS + N_FLUSH),
        in_specs=[
            pl.BlockSpec((EXPERTS_PER_STEP * 2 * LIST_ROWS, LANES), list_map,
                         memory_space=pltpu.SMEM),
            pl.BlockSpec((EXPERTS_PER_STEP * LIST_ROWS, LANES), list_map),
            pl.BlockSpec((MOE_TC * ROW_SUB, LANES), lambda c, j, nt: (c, 0),
                         pipeline_mode=pl.Buffered(1)),
            w_in_spec, w_in_spec, w_out_spec,
            tok_spec, tok_spec,
            pl.BlockSpec((8, D_MODEL), lambda c, j, nt: (0, 5)),
        ],
        out_specs=tok_spec,
        scratch_shapes=[
            pltpu.VMEM(((MOE_TC + 1) * ROW_SUB, LANES), F32),
            pltpu.VMEM((D_MODEL, D_EXPERT), BF16),
            pltpu.VMEM((D_MODEL, D_EXPERT), BF16),
            pltpu.VMEM((D_EXPERT, D_MODEL), BF16),
            stage, stage,
        ],
    )
    return pl.pallas_call(
        _moe_routed_kernel,
        grid_spec=grid_spec,
        out_shape=jax.ShapeDtypeStruct((T_ALL, D_MODEL), F32),
        compiler_params=_cparams("arbitrary", "arbitrary"),
        name="moe_routed",
    )(n_tiles, lists, wts, hrows, w_gate, w_up, w_down, x, shared, mod)


def kernel(x_prompt, x_sample, cache_k, cache_v, c, c_ctx, w_mod, b_mod, norm1, norm2, w_in_even, conv_a, q_norm, k_norm, lam_q1, lam_k1, lam_q2, lam_k2, subln, w_out_even, w_in_odd, conv_c, conv_c_b, ln_c_g, ln_c_b, w_pool, pool_scale, w_out_odd, w_router, b_router, w_gate, w_up, w_down, ws_gate, ws_up, ws_down):
    x = (x_prompt.reshape(T_P, D_MODEL), x_sample.reshape(T_S, D_MODEL))
    cond = jnp.concatenate([c_ctx[None, :], c, jnp.zeros((8 - 1 - DEC_BATCH, D_MODEL), F32)], axis=0)
    mod_all = _modulation(cond, w_mod, b_mod)

    new_k, new_v = [], []
    for li in range(DEPTH):
        mod = mod_all[li]
        if li % 2 == 0:
            e = li // 2
            lam_init = 0.8 - 0.6 * math.exp(-0.3 * li)
            u = _norm_in(x, norm1[li], mod, w_in_even[e].astype(BF16))
            q, k, v, k_f32 = _qkv_prep(u, q_norm[e], k_norm[e])
            new_k.append(jnp.transpose(k_f32[:, :T_P].reshape(H_B, BATCH, SEQ, LANES), (1, 0, 2, 3)))
            new_v.append(jnp.transpose(
                u[:T_P, 3 * D_A + 2 * D_B:].reshape(BATCH, SEQ, H_B, DV_B), (0, 2, 1, 3)))
            o = _attention(q, k, v, cache_k, cache_v, e,
                           (lam_q1[e], lam_k1[e], lam_q2[e], lam_k2[e]), subln[e], lam_init)
            x = _even_out(x, u, o, conv_a[e], mod, w_out_even[e].astype(BF16))
        else:
            o_ = li // 2
            u = _norm_in(x, norm1[li], mod, w_in_odd[o_].astype(BF16))
            x = _odd_out(x, u, conv_c[o_], conv_c_b[o_], ln_c_g[o_], ln_c_b[o_],
                         w_pool[o_].astype(BF16), pool_scale[o_], mod, w_out_odd[o_].astype(BF16))
        hrows, gate_b, rank_b, shared = _moe_pre(
            x, norm2[li], mod, w_router[li].T, b_router[li], ws_gate[li].astype(BF16),
            ws_up[li].astype(BF16), ws_down[li].astype(BF16))
        counts = _expert_counts(rank_b)
        lists, wts = _moe_sort(rank_b, gate_b, counts)
        x = _moe_routed(lists, wts, counts, hrows, w_gate, w_up, w_down, li, x, shared, mod)

    y_prompt = x[:T_P].reshape(BATCH, SEQ, D_MODEL)
    y_sample = x[T_P:].reshape(DEC_BATCH, DEC_SEQ, D_MODEL)

    return (y_prompt, y_sample, jnp.stack(new_k, axis=1), jnp.stack(new_v, axis=1))
```

```python
import functools
import math

import numpy as np
import jax
import jax.numpy as jnp
from jax import lax
from jax.experimental import pallas as pl
from jax.experimental.pallas import tpu as pltpu

D_MODEL = 1024
BATCH = 16
SEQ = 256
DEPTH = 2
DEC_BATCH = 2
DEC_SEQ = 4096
PAST_LEN = 512
GRID_W = 64
H_B = 4
DK_B = 64
DV_B = 2 * DK_B
D_A = D_MODEL // 2
D_B = H_B * DV_B
D_C = D_MODEL // 2
D_D = D_MODEL // 2
CONV_C = 31
POOL_WINDOWS = (2, 4, 8, 16)
D_DG = D_D // len(POOL_WINDOWS)
N_EXPERTS = 64
TOP_K = 8
N_GROUPS = 8
TOPK_GROUPS = 4
D_EXPERT = 256
D_SHARED = 256
ROUTED_SCALE = 2.5
ROPE_BASE = 10000.0
EPS = 1e-6

F32 = jnp.float32
BF16 = jnp.bfloat16

T_P = BATCH * SEQ
T_S = DEC_BATCH * DEC_SEQ
T_ALL = T_P + T_S
TILE = 256
N_TILES = T_ALL // TILE
P_TILES = T_P // TILE
S_TILES = DEC_SEQ // TILE
LANES = 128
VMEM_LIMIT = 56 * 1024 * 1024
VMEM_LIMIT_ROUTED = 60 * 1024 * 1024


def _cparams(*sem, vmem_limit=VMEM_LIMIT):
    return pltpu.CompilerParams(dimension_semantics=sem, vmem_limit_bytes=vmem_limit)


def _mod_row(i, tm):
    npt = T_P // tm
    per = DEC_SEQ // tm
    return jnp.where(i < npt, 0, 1 + (i - npt) // per)


def _seq_flags(i):
    j = (i - P_TILES) % S_TILES
    first = jnp.logical_or(i < P_TILES, j == 0)
    last = jnp.logical_or(i < P_TILES, j == S_TILES - 1)
    return first, last


def _seq_tile(i):
    return jnp.where(i < P_TILES, 0, (i - P_TILES) % S_TILES)


def _split_bf16(a):
    hi = a.astype(BF16)
    lo = (a - hi.astype(F32)).astype(BF16)
    return hi, lo


def _dot(a, b):
    return jnp.dot(a, b, preferred_element_type=F32)


def _dot_nt(a, b):
    return lax.dot_general(a, b, (((1,), (1,)), ((), ())), preferred_element_type=F32)


def _dot3(a, b):
    a_hi, a_lo = _split_bf16(a)
    b_hi, b_lo = _split_bf16(b)
    return _dot(a_hi, b_hi) + _dot(a_lo, b_hi) + _dot(a_hi, b_lo)


def _silu(x):
    return x * jax.nn.sigmoid(x)


MOD_TN = 1536


def _mod_kernel(c_ref, w_ref, b_ref, o_ref):
    o_ref[0] = _dot3(_silu(c_ref[...]), w_ref[0]) + b_ref[0]


def _modulation(cond, w_mod, b_mod):
    n = 6 * D_MODEL
    return pl.pallas_call(
        _mod_kernel,
        grid=(DEPTH, n // MOD_TN),
        in_specs=[
            pl.BlockSpec((8, D_MODEL), lambda l, j: (0, 0)),
            pl.BlockSpec((1, D_MODEL, MOD_TN), lambda l, j: (l, 0, j)),
            pl.BlockSpec((1, 1, MOD_TN), lambda l, j: (l, 0, j)),
        ],
        out_specs=pl.BlockSpec((1, 8, MOD_TN), lambda l, j: (l, 0, j)),
        out_shape=jax.ShapeDtypeStruct((DEPTH, 8, n), F32),
        compiler_params=_cparams("arbitrary", "arbitrary"),
        name="modulation",
    )(cond, w_mod, b_mod.reshape(DEPTH, 1, n))


IN_TM = 512


def _modulated_norm(x, g, shift, scale):
    ms = jnp.mean(x * x, axis=-1, keepdims=True)
    return (x * lax.rsqrt(ms + EPS) * g) * (1.0 + scale) + shift


def _tok_specs(x, tm, width):
    if isinstance(x, tuple):
        n_p = T_P // tm
        return ([pl.BlockSpec((tm, width), lambda i: (jnp.minimum(i, n_p - 1), 0)),
                 pl.BlockSpec((tm, width), lambda i: (jnp.maximum(i - n_p, 0), 0))], list(x))
    return [pl.BlockSpec((tm, width), lambda i: (i, 0))], [x]


def _tok_load(refs, i, tm):
    if len(refs) == 2:
        return jnp.where(i < T_P // tm, refs[0][...], refs[1][...])
    return refs[0][...]


def _norm_in_kernel(*refs, n_x):
    x_refs, (g_ref, sh_ref, sc_ref, w_ref, o_ref) = refs[:n_x], refs[n_x:]
    i = pl.program_id(0)
    r = _mod_row(i, IN_TM)
    h = _modulated_norm(_tok_load(x_refs, i, IN_TM), g_ref[...], sh_ref[pl.ds(r, 1), :],
                        sc_ref[pl.ds(r, 1), :])
    o_ref[...] = _dot(h.astype(BF16), w_ref[...])


def _norm_in(x, g, mod, w_bf16):
    n = w_bf16.shape[1]
    x_specs, x_ops = _tok_specs(x, IN_TM, D_MODEL)
    return pl.pallas_call(
        functools.partial(_norm_in_kernel, n_x=len(x_ops)),
        grid=(T_ALL // IN_TM,),
        in_specs=x_specs + [
            pl.BlockSpec((1, D_MODEL), lambda i: (0, 0)),
            pl.BlockSpec((8, D_MODEL), lambda i: (0, 0)),
            pl.BlockSpec((8, D_MODEL), lambda i: (0, 1)),
            pl.BlockSpec((D_MODEL, n), lambda i: (0, 0)),
        ],
        out_specs=pl.BlockSpec((IN_TM, n), lambda i: (i, 0)),
        out_shape=jax.ShapeDtypeStruct((T_ALL, n), F32),
        compiler_params=_cparams("arbitrary"),
        name="norm_in_proj",
    )(*x_ops, g.reshape(1, D_MODEL), mod, mod, w_bf16)


QKV_TM = 1024


def _rope_tables():
    half = DK_B // 2
    freqs = ROPE_BASE ** (-np.arange(0, half, 2, dtype=np.float64) / half)
    l = np.arange(DEC_SEQ)
    pos_r = (l // GRID_W).astype(np.float64)
    pos_c = (l % GRID_W).astype(np.float64)
    lane = np.arange(LANES)
    jj = lane % DK_B
    m = jj % half
    f = m % (half // 2)
    pos = np.where((jj < half)[None, :], pos_r[:, None], pos_c[:, None])
    ang = pos * freqs[f][None, :]
    sign = np.where(m < half // 2, -1.0, 1.0)[None, :]
    cos = np.concatenate([np.ones((QKV_TM, LANES)), np.cos(ang)], axis=0)
    sin = np.concatenate([np.zeros((QKV_TM, LANES)), sign * np.sin(ang)], axis=0)
    return cos.astype(np.float32), sin.astype(np.float32)


def _segment_mean_matrix():
    lane = np.arange(LANES)
    same = (lane[:, None] // DK_B) == (lane[None, :] // DK_B)
    return (same.astype(np.float32) / DK_B)


def _qk_prep(x, g, cos, sin, seg):
    x2 = x * x
    hi, lo = _split_bf16(x2)
    ms = _dot(hi, seg) + _dot(lo, seg)
    y = x * lax.rsqrt(ms + EPS) * g
    lane = lax.broadcasted_iota(jnp.int32, y.shape, 1)
    lower = (lane % (DK_B // 2)) < (DK_B // 4)
    partner = jnp.where(lower, pltpu.roll(y, LANES - DK_B // 4, 1), pltpu.roll(y, DK_B // 4, 1))
    return y * cos + partner * sin


def _qkv_prep_kernel(q_ref, k_ref, v_ref, qn_ref, kn_ref, cos_ref, sin_ref, seg_ref,
                     qo_ref, ko_ref, vo_ref, kf_ref):
    cos = cos_ref[...]
    sin = sin_ref[...]
    seg = seg_ref[...]
    scale = math.log2(math.e) / math.sqrt(DK_B)
    for h in range(H_B):
        cols = slice(h * LANES, (h + 1) * LANES)
        qo_ref[h] = (_qk_prep(q_ref[:, cols], qn_ref[...], cos, sin, seg) * scale).astype(BF16)
        k = _qk_prep(k_ref[:, cols], kn_ref[...], cos, sin, seg)
        kf_ref[h] = k
        ko_ref[h] = k.astype(BF16)
        vo_ref[h] = v_ref[:, cols].astype(BF16)


def _qkv_prep(u, qn, kn):
    cos, sin = _rope_tables()
    seg = jnp.asarray(_segment_mean_matrix(), BF16)
    qn2 = jnp.concatenate([qn, qn]).reshape(1, LANES)
    kn2 = jnp.concatenate([kn, kn]).reshape(1, LANES)
    col0 = 3 * D_A // D_B
    p_steps = T_P // QKV_TM
    s_steps = DEC_SEQ // QKV_TM

    def tab_map(i):
        return (jnp.where(i < p_steps, 0, 1 + (i - p_steps) % s_steps), 0)

    out_b = jax.ShapeDtypeStruct((H_B, T_ALL, LANES), BF16)
    out_spec = pl.BlockSpec((H_B, QKV_TM, LANES), lambda i: (0, i, 0))
    return pl.pallas_call(
        _qkv_prep_kernel,
        grid=(T_ALL // QKV_TM,),
        in_specs=[
            pl.BlockSpec((QKV_TM, D_B), lambda i: (i, col0)),
            pl.BlockSpec((QKV_TM, D_B), lambda i: (i, col0 + 1)),
            pl.BlockSpec((QKV_TM, D_B), lambda i: (i, col0 + 2)),
            pl.BlockSpec((1, LANES), lambda i: (0, 0)),
            pl.BlockSpec((1, LANES), lambda i: (0, 0)),
            pl.BlockSpec((QKV_TM, LANES), tab_map),
            pl.BlockSpec((QKV_TM, LANES), tab_map),
            pl.BlockSpec((LANES, LANES), lambda i: (0, 0)),
        ],
        out_specs=[out_spec] * 4,
        out_shape=[out_b, out_b, out_b, jax.ShapeDtypeStruct((H_B, T_ALL, LANES), F32)],
        compiler_params=_cparams("arbitrary"),
        name="qkv_prep",
    )(u, u, u, qn2, kn2, jnp.asarray(cos), jnp.asarray(sin), seg)


def _lambda(lq1_ref, lk1_ref, lq2_ref, lk2_ref, lam_init):
    a = jnp.sum(lq1_ref[...] * lk1_ref[...], axis=-1, keepdims=True)
    b = jnp.sum(lq2_ref[...] * lk2_ref[...], axis=-1, keepdims=True)
    return jnp.exp(a) - jnp.exp(b) + lam_init


def _attn_body(q, keys, vals, lam, subg, lam_init):
    lane = lax.broadcasted_iota(jnp.int32, q.shape, 1)
    zero = jnp.zeros_like(q)
    qa = jnp.where(lane < DK_B, q, zero)
    qb = jnp.where(lane < DK_B, zero, q)
    scores = [[_dot_nt(qq, k) for k in keys] for qq in (qa, qb)]
    outs = []
    for ss in scores:
        m = functools.reduce(jnp.maximum, [jnp.max(s, axis=-1, keepdims=True) for s in ss])
        ps = [jnp.exp2(s - m) for s in ss]
        l = functools.reduce(jnp.add, [jnp.sum(p, axis=-1, keepdims=True) for p in ps])
        pv = functools.reduce(jnp.add, [_dot(p.astype(BF16), v) for p, v in zip(ps, vals)])
        outs.append(pv / l)
    o = outs[0] - lam * outs[1]
    ms = jnp.mean(o * o, axis=-1, keepdims=True)
    return (o * lax.rsqrt(ms + EPS) * subg) * (1.0 - lam_init)


def _attn_prompt_kernel(q_ref, k_ref, v_ref, lq1, lk1, lq2, lk2, sg_ref, o_ref, *, lam_init):
    lam = _lambda(lq1, lk1, lq2, lk2, lam_init)
    for h in range(H_B):
        o_ref[:, h * DV_B:(h + 1) * DV_B] = _attn_body(
            q_ref[h], [k_ref[h]], [v_ref[h]], lam, sg_ref[...], lam_init)


def _attn_latent_kernel(q_ref, k_ref, v_ref, ck_ref, cv_ref, lq1, lk1, lq2, lk2, sg_ref, o_ref,
                        *, lam_init):
    lam = _lambda(lq1, lk1, lq2, lk2, lam_init)
    keys = [ck_ref[0, 0, 0].astype(BF16), k_ref[0]]
    vals = [cv_ref[0, 0, 0].astype(BF16), v_ref[0]]
    o_ref[...] = _attn_body(q_ref[0], keys, vals, lam, sg_ref[...], lam_init)


def _attention(q, k, v, cache_k, cache_v, e, lam_params, subg, lam_init):
    small = [p.reshape(1, DK_B) for p in lam_params] + [subg.reshape(1, DV_B)]
    small_specs2 = [pl.BlockSpec((1, DK_B), lambda b: (0, 0))] * 4 + \
                   [pl.BlockSpec((1, DV_B), lambda b: (0, 0))]
    small_specs3 = [pl.BlockSpec((1, DK_B), lambda s, h, j: (0, 0))] * 4 + \
                   [pl.BlockSpec((1, DV_B), lambda s, h, j: (0, 0))]
    o_prompt = pl.pallas_call(
        functools.partial(_attn_prompt_kernel, lam_init=lam_init),
        grid=(BATCH,),
        in_specs=[pl.BlockSpec((H_B, SEQ, LANES), lambda b: (0, b, 0))] * 3 + small_specs2,
        out_specs=pl.BlockSpec((SEQ, D_B), lambda b: (b, 0)),
        out_shape=jax.ShapeDtypeStruct((T_P, D_B), F32),
        compiler_params=_cparams("arbitrary"),
        name="attn_prompt",
    )(q, k, v, *small)

    kv_spec = pl.BlockSpec((1, DEC_SEQ, LANES), lambda s, h, j: (h, T_P // DEC_SEQ + s, 0))
    c_spec = pl.BlockSpec((1, 1, 1, PAST_LEN, LANES), lambda s, h, j: (s, e, h, 0, 0))
    o_latent = pl.pallas_call(
        functools.partial(_attn_latent_kernel, lam_init=lam_init),
        grid=(DEC_BATCH, H_B, S_TILES),
        in_specs=[pl.BlockSpec((1, TILE, LANES), lambda s, h, j: (h, P_TILES + s * S_TILES + j, 0)),
                  kv_spec, kv_spec, c_spec, c_spec] + small_specs3,
        out_specs=pl.BlockSpec((TILE, LANES), lambda s, h, j: (s * S_TILES + j, h)),
        out_shape=jax.ShapeDtypeStruct((T_S, D_B), F32),
        compiler_params=_cparams("arbitrary", "arbitrary", "arbitrary"),
        name="attn_latent",
    )(q, k, v, cache_k, cache_v, *small)
    return o_prompt, o_latent


HALO8 = 8
HALO16 = 16


def _prev_block(i, rows):
    return jnp.maximum(i * (TILE // rows) - 1, 0)


def _next_block(i, rows):
    return jnp.minimum((i + 1) * (TILE // rows), T_ALL // rows - 1)


def _even_out_kernel(*refs, n_x, n_o):
    x_refs, o_refs = refs[:n_x], refs[n_x:n_x + n_o]
    (bg_ref, cg_ref, hin_ref, cgp_ref, hinp_ref, cgn_ref, hinn_ref, cw_ref, g1_ref, w_ref,
     out_ref) = refs[n_x + n_o:]
    i = pl.program_id(0)
    first, last = _seq_flags(i)
    r = _mod_row(i, TILE)
    z = cg_ref[...] * hin_ref[...]
    zp = jnp.where(first, 0.0, cgp_ref[HALO8 - 1:HALO8, :] * hinp_ref[HALO8 - 1:HALO8, :])
    zn = jnp.where(last, 0.0, cgn_ref[0:1, :] * hinn_ref[0:1, :])
    row = lax.broadcasted_iota(jnp.int32, z.shape, 0)
    z_prev = jnp.where(row == 0, zp, pltpu.roll(z, 1, 0))
    z_next = jnp.where(row == TILE - 1, zn, pltpu.roll(z, TILE - 1, 0))
    cw = cw_ref[...]
    ya = bg_ref[...] * (cw[0:1, :] * z_prev + cw[1:2, :] * z + cw[2:3, :] * z_next)
    o = _tok_load(o_refs, i, TILE)
    y = _dot(ya.astype(BF16), w_ref[0:D_A, :]) + _dot(o.astype(BF16), w_ref[D_A:, :])
    out_ref[...] = _tok_load(x_refs, i, TILE) + g1_ref[pl.ds(r, 1), :] * y


def _even_out(x, u, o, conv_w, mod, w_out_bf16):
    tile_spec = lambda c: pl.BlockSpec((TILE, D_A), lambda i: (i, c))
    prev_spec = lambda c: pl.BlockSpec((HALO8, D_A), lambda i: (_prev_block(i, HALO8), c))
    next_spec = lambda c: pl.BlockSpec((HALO8, D_A), lambda i: (_next_block(i, HALO8), c))
    x_specs, x_ops = _tok_specs(x, TILE, D_MODEL)
    o_specs, o_ops = _tok_specs(o, TILE, D_B)
    return pl.pallas_call(
        functools.partial(_even_out_kernel, n_x=len(x_ops), n_o=len(o_ops)),
        grid=(N_TILES,),
        in_specs=x_specs + o_specs + [
            tile_spec(0), tile_spec(1), tile_spec(2),
            prev_spec(1), prev_spec(2), next_spec(1), next_spec(2),
            pl.BlockSpec((3, D_A), lambda i: (0, 0)),
            pl.BlockSpec((8, D_MODEL), lambda i: (0, 2)),
            pl.BlockSpec((D_MODEL, D_MODEL), lambda i: (0, 0)),
        ],
        out_specs=pl.BlockSpec((TILE, D_MODEL), lambda i: (i, 0)),
        out_shape=jax.ShapeDtypeStruct((T_ALL, D_MODEL), F32),
        compiler_params=_cparams("arbitrary"),
        name="even_mixer_out",
    )(*x_ops, *o_ops, u, u, u, u, u, u, u, conv_w, mod, w_out_bf16)


def _odd_out_kernel(x_ref, a_ref, b_ref, pd_ref, ap_ref, bp_ref, an_ref, bn_ref, pp_ref, pn_ref,
                    cw_ref, cb_ref, lg_ref, lb_ref, wp_ref, ps_ref, g1_ref, w_ref, out_ref,
                    ext_ref, extp_ref, shift_ref):
    i = pl.program_id(0)
    first, last = _seq_flags(i)
    r = _mod_row(i, TILE)
    ext_ref[0:HALO16, :] = jnp.where(first, 0.0, ap_ref[...] * jax.nn.sigmoid(bp_ref[...]))
    ext_ref[HALO16:HALO16 + TILE, :] = a_ref[...] * jax.nn.sigmoid(b_ref[...])
    ext_ref[HALO16 + TILE:, :] = jnp.where(last, 0.0, an_ref[...] * jax.nn.sigmoid(bn_ref[...]))
    base = HALO16 - CONV_C // 2
    parts = []
    for cb in range(D_C // LANES):
        cols = slice(cb * LANES, (cb + 1) * LANES)
        acc = jnp.zeros((TILE, LANES), F32)
        for phase in range(8):
            taps = [j for j in range(CONV_C) if (base + j) % 8 == phase]
            reach = max((base + j) // 8 for j in taps)
            rows = TILE + 8 * reach
            shift_ref[0:rows, :] = ext_ref[pl.ds(phase, rows), cols]
            for j in taps:
                a = (base + j) // 8
                acc = acc + cw_ref[j:j + 1, cols] * shift_ref[8 * a:8 * a + TILE, :]
        parts.append(acc)
    g = jnp.concatenate(parts, axis=-1) + cb_ref[...]
    mu = jnp.mean(g, axis=-1, keepdims=True)
    var = jnp.mean(jnp.square(g - mu), axis=-1, keepdims=True)
    g = _silu(((g - mu) * lax.rsqrt(var + EPS)) * lg_ref[...] + lb_ref[...])
    extp_ref[0:HALO8, :] = jnp.where(first, 0.0, pp_ref[...])
    extp_ref[HALO8:HALO8 + TILE, :] = pd_ref[...]
    extp_ref[HALO8 + TILE:, :] = jnp.where(last, 0.0, pn_ref[...])
    seq_len = jnp.where(i < P_TILES, SEQ, DEC_SEQ)
    pos = _seq_tile(i) * TILE + lax.broadcasted_iota(jnp.int32, (TILE, 1), 0)
    yd = []
    for gi, w in enumerate(POOL_WINDOWS):
        cols = slice(gi * D_DG, (gi + 1) * D_DG)
        s = jnp.zeros((TILE, D_DG), F32)
        for d in range(-(w // 2), w - w // 2):
            s = s + extp_ref[pl.ds(HALO8 + d, TILE), cols]
        lo = jnp.maximum(pos - w // 2, 0)
        hi = jnp.minimum(pos - w // 2 + w, seq_len)
        pooled = s / (hi - lo).astype(F32) - pd_ref[:, cols]
        yd.append(_dot(pooled.astype(BF16), wp_ref[gi]))
    yd = jnp.concatenate(yd, axis=-1) * ps_ref[...]
    y = _dot(g.astype(BF16), w_ref[0:D_C, :]) + _dot(yd.astype(BF16), w_ref[D_C:, :])
    out_ref[...] = x_ref[...] + g1_ref[pl.ds(r, 1), :] * y


def _odd_out(x, u, conv_w, conv_b, ln_g, ln_b, w_pool_bf16, p_scale, mod, w_out_bf16):
    tile_spec = lambda c: pl.BlockSpec((TILE, D_C), lambda i: (i, c))
    prev_spec = lambda rows, c: pl.BlockSpec((rows, D_C), lambda i: (_prev_block(i, rows), c))
    next_spec = lambda rows, c: pl.BlockSpec((rows, D_C), lambda i: (_next_block(i, rows), c))
    vec = lambda: pl.BlockSpec((1, D_C), lambda i: (0, 0))
    return pl.pallas_call(
        _odd_out_kernel,
        grid=(N_TILES,),
        in_specs=[
            pl.BlockSpec((TILE, D_MODEL), lambda i: (i, 0)),
            tile_spec(0), tile_spec(1), tile_spec(2),
            prev_spec(HALO16, 0), prev_spec(HALO16, 1), next_spec(HALO16, 0), next_spec(HALO16, 1),
            prev_spec(HALO8, 2), next_spec(HALO8, 2),
            pl.BlockSpec((CONV_C, D_C), lambda i: (0, 0)),
            vec(), vec(), vec(),
            pl.BlockSpec((len(POOL_WINDOWS), D_DG, D_DG), lambda i: (0, 0, 0)),
            vec(),
            pl.BlockSpec((8, D_MODEL), lambda i: (0, 2)),
            pl.BlockSpec((D_MODEL, D_MODEL), lambda i: (0, 0)),
        ],
        out_specs=pl.BlockSpec((TILE, D_MODEL), lambda i: (i, 0)),
        out_shape=jax.ShapeDtypeStruct((T_ALL, D_MODEL), F32),
        scratch_shapes=[pltpu.VMEM((TILE + 2 * HALO16, D_C), F32),
                        pltpu.VMEM((TILE + 2 * HALO8, D_D), F32),
                        pltpu.VMEM((TILE + 2 * HALO16, LANES), F32)],
        compiler_params=_cparams("arbitrary"),
        name="odd_mixer_out",
    )(x, u, u, u, u, u, u, u, u, u, conv_w, conv_b.reshape(1, D_C), ln_g.reshape(1, D_C),
      ln_b.reshape(1, D_C), w_pool_bf16, p_scale.reshape(1, D_D), mod, w_out_bf16)


GROUP = N_EXPERTS // N_GROUPS
NEG_INF = float("-inf")


def _first_argmax(v, idx, axis):
    m = jnp.max(v, axis=axis, keepdims=True)
    big = jnp.int32(2 ** 30)
    am = jnp.min(jnp.where(v == m, idx, big), axis=axis, keepdims=True)
    return m, am


def _route(scores, biased):
    shape = biased.shape
    member = lax.broadcasted_iota(jnp.int32, shape, 1)
    m1, a1 = _first_argmax(biased, member, 1)
    m2 = jnp.max(jnp.where(member == a1, NEG_INF, biased), axis=1, keepdims=True)
    gscore = m1 + m2
    gidx = lax.broadcasted_iota(jnp.int32, gscore.shape, 0)
    gsel = jnp.zeros(gscore.shape, jnp.bool_)
    for _ in range(TOPK_GROUPS):
        _, am = _first_argmax(gscore, gidx, 0)
        hit = gidx == am
        gsel = jnp.logical_or(gsel, hit)
        gscore = jnp.where(hit, NEG_INF, gscore)
    cand = jnp.where(gsel, biased, NEG_INF)
    eidx = lax.broadcasted_iota(jnp.int32, shape, 0) * GROUP + member
    sel = jnp.zeros(shape, jnp.bool_)
    for _ in range(TOP_K):
        m = jnp.max(jnp.max(cand, axis=1, keepdims=True), axis=0, keepdims=True)
        big = jnp.int32(2 ** 30)
        am = jnp.where(cand == m, eidx, big)
        am = jnp.min(jnp.min(am, axis=1, keepdims=True), axis=0, keepdims=True)
        hit = eidx == am
        sel = jnp.logical_or(sel, hit)
        cand = jnp.where(hit, NEG_INF, cand)
    wsel = jnp.where(sel, scores, 0.0)
    tot = jnp.sum(jnp.sum(wsel, axis=1, keepdims=True), axis=0, keepdims=True)
    return wsel / tot * ROUTED_SCALE, sel


MOE_TC = 4096
N_CHUNKS = T_ALL // MOE_TC
ROW_TILE = 256
N_FLUSH = MOE_TC // TILE
PRE_TM = 1024
ROW_SUB = D_MODEL // LANES


def _moe_pre_kernel(x_ref, g_ref, sh_ref, sc_ref, wr_ref, br_ref, tri_ref, wsg_ref, wsu_ref,
                    wsd_ref, hrow_ref, gate_ref, rank_ref, shared_ref, carry_ref):
    i = pl.program_id(0)
    r = _mod_row(i, PRE_TM)
    h = _modulated_norm(x_ref[...], g_ref[...], sh_ref[pl.ds(r, 1), :], sc_ref[pl.ds(r, 1), :])
    hb = h.astype(BF16)
    for s in range(ROW_SUB):
        hrow_ref[pl.ds(s, PRE_TM, stride=ROW_SUB), :] = h[:, s * LANES:(s + 1) * LANES]
    h_hi, h_lo = hb, (h - hb.astype(F32)).astype(BF16)
    w_hi, w_lo = _split_bf16(wr_ref[...])
    logits = _dot_nt(w_hi, h_hi) + _dot_nt(w_lo, h_hi) + _dot_nt(w_hi, h_lo)
    scores = jax.nn.sigmoid(logits)
    biased = scores + br_ref[:, 0:1]
    shape3 = (N_GROUPS, GROUP, PRE_TM)
    gate_t, sel = _route(scores.reshape(shape3), biased.reshape(shape3))
    gate_t = gate_t.reshape(N_EXPERTS, PRE_TM)
    sel = jnp.where(sel.reshape(N_EXPERTS, PRE_TM), 1.0, 0.0)

    @pl.when(i % (MOE_TC // PRE_TM) == 0)
    def _():
        carry_ref[...] = jnp.zeros_like(carry_ref)

    carry = carry_ref[...]
    local = _dot(sel.astype(BF16), tri_ref[...])
    rank = jnp.where(sel > 0.0, local + jnp.concatenate([carry] * (PRE_TM // LANES), axis=1), -1.0)
    carry_ref[...] = carry + jnp.sum(sel, axis=1, keepdims=True)
    gate_ref[...] = gate_t
    rank_ref[...] = rank
    a = _silu(_dot(hb, wsg_ref[...])) * _dot(hb, wsu_ref[...])
    shared_ref[...] = _dot(a.astype(BF16), wsd_ref[...])


def _moe_pre(x, g, mod, w_router_t, b_router, wsg, wsu, wsd):
    return pl.pallas_call(
        _moe_pre_kernel,
        grid=(T_ALL // PRE_TM,),
        in_specs=[
            pl.BlockSpec((PRE_TM, D_MODEL), lambda i: (i, 0)),
            pl.BlockSpec((1, D_MODEL), lambda i: (0, 0)),
            pl.BlockSpec((8, D_MODEL), lambda i: (0, 3)),
            pl.BlockSpec((8, D_MODEL), lambda i: (0, 4)),
            pl.BlockSpec((N_EXPERTS, D_MODEL), lambda i: (0, 0)),
            pl.BlockSpec((N_EXPERTS, LANES), lambda i: (0, 0)),
            pl.BlockSpec((PRE_TM, PRE_TM), lambda i: (0, 0)),
            pl.BlockSpec((D_MODEL, D_SHARED), lambda i: (0, 0)),
            pl.BlockSpec((D_MODEL, D_SHARED), lambda i: (0, 0)),
            pl.BlockSpec((D_SHARED, D_MODEL), lambda i: (0, 0)),
        ],
        out_specs=[
            pl.BlockSpec((PRE_TM * ROW_SUB, LANES), lambda i: (i, 0)),
            pl.BlockSpec((N_EXPERTS, PRE_TM), lambda i: (0, i)),
            pl.BlockSpec((N_EXPERTS, PRE_TM), lambda i: (0, i)),
            pl.BlockSpec((PRE_TM, D_MODEL), lambda i: (i, 0)),
        ],
        out_shape=[
            jax.ShapeDtypeStruct((T_ALL * ROW_SUB, LANES), F32),
            jax.ShapeDtypeStruct((N_EXPERTS, T_ALL), F32),
            jax.ShapeDtypeStruct((N_EXPERTS, T_ALL), F32),
            jax.ShapeDtypeStruct((T_ALL, D_MODEL), F32),
        ],
        scratch_shapes=[pltpu.VMEM((N_EXPERTS, LANES), F32)],
        compiler_params=_cparams("arbitrary"),
        name="moe_pre",
    )(x, g.reshape(1, D_MODEL), mod, mod, w_router_t,
      jnp.broadcast_to(b_router.reshape(N_EXPERTS, 1), (N_EXPERTS, LANES)),
      jnp.asarray(np.triu(np.ones((PRE_TM, PRE_TM), np.float32), 1), BF16), wsg, wsu, wsd)


LIST_ROWS = MOE_TC // LANES
TILE_ROWS = ROW_TILE // LANES
DUMMY_ROW = MOE_TC * ROW_SUB


def _expert_counts(rank_t):
    sel = (rank_t >= 0.0).astype(jnp.int32).reshape(N_EXPERTS, N_CHUNKS, MOE_TC)
    return jnp.sum(sel, axis=-1).T


def _moe_sort_kernel(rank_ref, gate_ref, cnt_ref, list_ref, w_ref):
    rank = rank_ref[...]
    lane = lax.broadcasted_iota(jnp.int32, rank.shape, 1)
    d = jnp.where(rank >= 0.0, lane - rank.astype(jnp.int32), 0)
    w = gate_ref[...]
    for s in range(MOE_TC.bit_length() - 1):
        k = 1 << s
        d_in = pltpu.roll(d, MOE_TC - k, 1)
        w_in = pltpu.roll(w, MOE_TC - k, 1)
        take = (d_in & k) != 0
        leave = (d & k) != 0
        d = jnp.where(take, d_in, jnp.where(leave, 0, d))
        w = jnp.where(take, w_in, w)
    valid = lane < cnt_ref[0][:, 0:1]
    row = (lane + d) * ROW_SUB
    gsrc = jnp.where(valid, row, 0)
    ssrc = jnp.where(valid, row, DUMMY_ROW)
    w = jnp.where(valid, w, 0.0)
    for b in range(LIST_ROWS):
        cols = slice(b * LANES, (b + 1) * LANES)
        list_ref[pl.ds(b, N_EXPERTS, stride=2 * LIST_ROWS), :] = gsrc[:, cols]
        list_ref[pl.ds(LIST_ROWS + b, N_EXPERTS, stride=2 * LIST_ROWS), :] = ssrc[:, cols]
        w_ref[pl.ds(b, N_EXPERTS, stride=LIST_ROWS), :] = w[:, cols]


def _moe_sort(rank_t, gate_t, counts):
    chunk_spec = pl.BlockSpec((N_EXPERTS, MOE_TC), lambda c: (0, c))
    cnt_b = jnp.broadcast_to(counts[:, :, None], (N_CHUNKS, N_EXPERTS, LANES))
    n_rows = N_CHUNKS * N_EXPERTS * LIST_ROWS
    return pl.pallas_call(
        _moe_sort_kernel,
        grid=(N_CHUNKS,),
        in_specs=[chunk_spec, chunk_spec,
                  pl.BlockSpec((1, N_EXPERTS, LANES), lambda c: (c, 0, 0))],
        out_specs=[pl.BlockSpec((N_EXPERTS * 2 * LIST_ROWS, LANES), lambda c: (c, 0)),
                   pl.BlockSpec((N_EXPERTS * LIST_ROWS, LANES), lambda c: (c, 0))],
        out_shape=[jax.ShapeDtypeStruct((2 * n_rows, LANES), jnp.int32),
                   jax.ShapeDtypeStruct((n_rows, LANES), F32)],
        compiler_params=_cparams("arbitrary"),
        name="moe_sort",
    )(rank_t, gate_t, cnt_b)


SCATTER_BATCH = 16


EXPERTS_PER_STEP = 2
EXPERT_STEPS = N_EXPERTS // EXPERTS_PER_STEP


def _expert_tile(t, list0, wt0, list_ref, wt_ref, h_ref, acc_ref, wgb_ref, wub_ref,
                 wdb_ref, xs_ref, ys_ref):
    gbase = list0 + t * TILE_ROWS
    sbase = gbase + LIST_ROWS
    for r in range(ROW_TILE):
        tok = pl.multiple_of(list_ref[gbase + r // LANES, r % LANES], ROW_SUB)
        xs_ref[r // 8, pl.ds(r % 8, ROW_SUB, stride=8), :] = h_ref[pl.ds(tok, ROW_SUB), :]
    xt = jnp.concatenate(
        [xs_ref[:, s * 8:(s + 1) * 8, :].reshape(ROW_TILE, LANES) for s in range(ROW_SUB)],
        axis=1).astype(BF16)
    eye = (lax.broadcasted_iota(jnp.int32, (ROW_TILE, LANES), 0) % LANES
           == lax.broadcasted_iota(jnp.int32, (ROW_TILE, LANES), 1))
    row_blk = lax.broadcasted_iota(jnp.int32, (ROW_TILE, LANES), 0) // LANES
    wrows = functools.reduce(
        lambda a, b: a + b,
        [jnp.where(row_blk == k, wt_ref[pl.ds(wt0 + t * TILE_ROWS + k, 1), :], 0.0)
         for k in range(TILE_ROWS)])
    wcol = jnp.sum(jnp.where(eye, wrows, 0.0), axis=1, keepdims=True)
    a = _silu(_dot(xt, wgb_ref[...])) * _dot(xt, wub_ref[...]) * wcol
    y = _dot(a.astype(BF16), wdb_ref[...])
    for s in range(ROW_SUB):
        ys_ref[:, s * 8:(s + 1) * 8, :] = y[:, s * LANES:(s + 1) * LANES].reshape(
            ROW_TILE // 8, 8, LANES)
    for r0 in range(0, ROW_TILE, SCATTER_BATCH):
        rows = range(r0, r0 + SCATTER_BATCH)
        dsts = [acc_ref.at[pl.ds(pl.multiple_of(list_ref[sbase + r // LANES, r % LANES], ROW_SUB),
                                 ROW_SUB), :] for r in rows]
        news = [dst[...] + ys_ref[r // 8, pl.ds(r % 8, ROW_SUB, stride=8), :]
                for dst, r in zip(dsts, rows)]
        for dst, new in zip(dsts, news):
            dst[...] = new


def _moe_routed_kernel(cnt_ref, list_ref, wt_ref, h_ref, wg_ref, wu_ref, wd_ref,
                       x_ref, sh_ref, g2_ref, *refs):
    out_refs, (acc_ref, wgb_ref, wub_ref, wdb_ref, xs_ref, ys_ref) = refs[:-6], refs[-6:]
    c = pl.program_id(0)
    j = pl.program_id(1)

    @pl.when(j == 0)
    def _():
        acc_ref[...] = jnp.zeros_like(acc_ref)

    for k in range(EXPERTS_PER_STEP):
        expert = jnp.minimum(j, EXPERT_STEPS - 1) * EXPERTS_PER_STEP + k
        count = cnt_ref[c * N_EXPERTS + expert]

        @pl.when(jnp.logical_and(j < EXPERT_STEPS, count > 0))
        def _():
            wgb_ref[...] = wg_ref[0, k].astype(BF16)
            wub_ref[...] = wu_ref[0, k].astype(BF16)
            wdb_ref[...] = wd_ref[0, k].astype(BF16)

            def tile(t, carry):
                _expert_tile(t, k * 2 * LIST_ROWS, k * LIST_ROWS, list_ref, wt_ref, h_ref,
                             acc_ref, wgb_ref, wub_ref, wdb_ref, xs_ref, ys_ref)
                return carry

            lax.fori_loop(0, (count + ROW_TILE - 1) // ROW_TILE, tile, 0)

    @pl.when(j >= EXPERT_STEPS)
    def _():
        base = (j - EXPERT_STEPS) * (TILE * ROW_SUB)
        moe = jnp.concatenate(
            [acc_ref[pl.ds(base + s, TILE, stride=ROW_SUB), :] for s in range(ROW_SUB)], axis=1)
        new_x = x_ref[...] + g2_ref[pl.ds(c, 1), :] * (moe + sh_ref[...])
        if len(out_refs) == 1:
            out_refs[0][...] = new_x
        else:
            prompt_ref, latent_ref = out_refs

            @pl.when(c < T_P // MOE_TC)
            def _():
                prompt_ref[...] = new_x

            @pl.when(c >= T_P // MOE_TC)
            def _():
                latent_ref[...] = new_x


def _moe_routed(lists, wts, counts, hrows, w_gate, w_up, w_down, li, x, shared, mod,
                split_output):
    group = lambda j: jnp.minimum(j, EXPERT_STEPS - 1)
    list_map = lambda c, j, nt: (c * EXPERT_STEPS + group(j), 0)
    w_map = lambda c, j, nt: (li, group(j), 0, 0)
    w_in_spec = pl.BlockSpec((1, EXPERTS_PER_STEP, D_MODEL, D_EXPERT), w_map)
    w_out_spec = pl.BlockSpec((1, EXPERTS_PER_STEP, D_EXPERT, D_MODEL), w_map)
    out_blk = lambda c, j: c * N_FLUSH + jnp.maximum(j - EXPERT_STEPS, 0)
    tok_spec = pl.BlockSpec((TILE, D_MODEL), lambda c, j, nt: (out_blk(c, j), 0))
    if split_output:
        out_specs = [
            pl.BlockSpec((TILE, D_MODEL), lambda c, j, nt: (jnp.minimum(out_blk(c, j), P_TILES - 1), 0)),
            pl.BlockSpec((TILE, D_MODEL), lambda c, j, nt: (jnp.maximum(out_blk(c, j) - P_TILES, 0), 0)),
        ]
        out_shape = [jax.ShapeDtypeStruct((T_P, D_MODEL), F32),
                     jax.ShapeDtypeStruct((T_S, D_MODEL), F32)]
    else:
        out_specs = tok_spec
        out_shape = jax.ShapeDtypeStruct((T_ALL, D_MODEL), F32)
    stage = pltpu.VMEM((ROW_TILE // 8, 8 * ROW_SUB, LANES), F32)
    grid_spec = pltpu.PrefetchScalarGridSpec(
        num_scalar_prefetch=1,
        grid=(N_CHUNKS, EXPERT_STEPS + N_FLUSH),
        in_specs=[
            pl.BlockSpec((EXPERTS_PER_STEP * 2 * LIST_ROWS, LANES), list_map,
                         memory_space=pltpu.SMEM),
            pl.BlockSpec((EXPERTS_PER_STEP * LIST_ROWS, LANES), list_map),
            pl.BlockSpec((MOE_TC * ROW_SUB, LANES), lambda c, j, nt: (c, 0),
                         pipeline_mode=pl.Buffered(1)),
            w_in_spec, w_in_spec, w_out_spec,
            tok_spec, tok_spec,
            pl.BlockSpec((8, D_MODEL), lambda c, j, nt: (0, 5)),
        ],
        out_specs=out_specs,
        scratch_shapes=[
            pltpu.VMEM(((MOE_TC + 1) * ROW_SUB, LANES), F32),
            pltpu.VMEM((D_MODEL, D_EXPERT), BF16),
            pltpu.VMEM((D_MODEL, D_EXPERT), BF16),
            pltpu.VMEM((D_EXPERT, D_MODEL), BF16),
            stage, stage,
        ],
    )
    return pl.pallas_call(
        _moe_routed_kernel,
        grid_spec=grid_spec,
        out_shape=out_shape,
        compiler_params=_cparams("arbitrary", "arbitrary", vmem_limit=VMEM_LIMIT_ROUTED),
        name="moe_routed",
    )(counts.reshape(-1), lists, wts, hrows, w_gate, w_up, w_down, x, shared, mod)


def kernel(x_prompt, x_sample, cache_k, cache_v, c, c_ctx, w_mod, b_mod, norm1, norm2, w_in_even, conv_a, q_norm, k_norm, lam_q1, lam_k1, lam_q2, lam_k2, subln, w_out_even, w_in_odd, conv_c, conv_c_b, ln_c_g, ln_c_b, w_pool, pool_scale, w_out_odd, w_router, b_router, w_gate, w_up, w_down, ws_gate, ws_up, ws_down):
    x = (x_prompt.reshape(T_P, D_MODEL), x_sample.reshape(T_S, D_MODEL))
    cond = jnp.concatenate([c_ctx[None, :], c, jnp.zeros((8 - 1 - DEC_BATCH, D_MODEL), F32)], axis=0)
    mod_all = _modulation(cond, w_mod, b_mod)

    new_k, new_v = [], []
    for li in range(DEPTH):
        mod = mod_all[li]
        if li % 2 == 0:
            e = li // 2
            lam_init = 0.8 - 0.6 * math.exp(-0.3 * li)
            u = _norm_in(x, norm1[li], mod, w_in_even[e].astype(BF16))
            q, k, v, k_f32 = _qkv_prep(u, q_norm[e], k_norm[e])
            new_k.append(jnp.transpose(k_f32[:, :T_P].reshape(H_B, BATCH, SEQ, LANES), (1, 0, 2, 3)))
            new_v.append(jnp.transpose(
                u[:T_P, 3 * D_A + 2 * D_B:].reshape(BATCH, SEQ, H_B, DV_B), (0, 2, 1, 3)))
            o = _attention(q, k, v, cache_k, cache_v, e,
                           (lam_q1[e], lam_k1[e], lam_q2[e], lam_k2[e]), subln[e], lam_init)
            x = _even_out(x, u, o, conv_a[e], mod, w_out_even[e].astype(BF16))
        else:
            o_ = li // 2
            u = _norm_in(x, norm1[li], mod, w_in_odd[o_].astype(BF16))
            x = _odd_out(x, u, conv_c[o_], conv_c_b[o_], ln_c_g[o_], ln_c_b[o_],
                         w_pool[o_].astype(BF16), pool_scale[o_], mod, w_out_odd[o_].astype(BF16))
        hrows, gate_b, rank_b, shared = _moe_pre(
            x, norm2[li], mod, w_router[li].T, b_router[li], ws_gate[li].astype(BF16),
            ws_up[li].astype(BF16), ws_down[li].astype(BF16))
        counts = _expert_counts(rank_b)
        lists, wts = _moe_sort(rank_b, gate_b, counts)
        x = _moe_routed(lists, wts, counts, hrows, w_gate, w_up, w_down, li, x, shared, mod,
                        split_output=(li == DEPTH - 1))

    y_prompt = x[0].reshape(BATCH, SEQ, D_MODEL)
    y_sample = x[1].reshape(DEC_BATCH, DEC_SEQ, D_MODEL)
    return (y_prompt, y_sample, jnp.stack(new_k, axis=1), jnp.stack(new_v, axis=1))
```

```python
import functools
import math

import numpy as np
import jax
import jax.numpy as jnp
from jax import lax
from jax.experimental import pallas as pl
from jax.experimental.pallas import tpu as pltpu

D_MODEL = 1024
BATCH = 16
SEQ = 256
DEPTH = 2
DEC_BATCH = 2
DEC_SEQ = 4096
PAST_LEN = 512
GRID_W = 64
H_B = 4
DK_B = 64
DV_B = 2 * DK_B
D_A = D_MODEL // 2
D_B = H_B * DV_B
D_C = D_MODEL // 2
D_D = D_MODEL // 2
CONV_C = 31
POOL_WINDOWS = (2, 4, 8, 16)
D_DG = D_D // len(POOL_WINDOWS)
N_EXPERTS = 64
TOP_K = 8
N_GROUPS = 8
TOPK_GROUPS = 4
D_EXPERT = 256
D_SHARED = 256
ROUTED_SCALE = 2.5
ROPE_BASE = 10000.0
EPS = 1e-6

F32 = jnp.float32
BF16 = jnp.bfloat16

T_P = BATCH * SEQ
T_S = DEC_BATCH * DEC_SEQ
T_ALL = T_P + T_S
TILE = 256
N_TILES = T_ALL // TILE
P_TILES = T_P // TILE
S_TILES = DEC_SEQ // TILE
LANES = 128
VMEM_LIMIT = 56 * 1024 * 1024
VMEM_LIMIT_ROUTED = 60 * 1024 * 1024


def _cparams(*sem, vmem_limit=VMEM_LIMIT):
    return pltpu.CompilerParams(dimension_semantics=sem, vmem_limit_bytes=vmem_limit)


def _mod_row(i, tm):
    npt = T_P // tm
    per = DEC_SEQ // tm
    return jnp.where(i < npt, 0, 1 + (i - npt) // per)


def _seq_flags(i):
    j = (i - P_TILES) % S_TILES
    first = jnp.logical_or(i < P_TILES, j == 0)
    last = jnp.logical_or(i < P_TILES, j == S_TILES - 1)
    return first, last


def _seq_tile(i):
    return jnp.where(i < P_TILES, 0, (i - P_TILES) % S_TILES)


def _split_bf16(a):
    hi = a.astype(BF16)
    lo = (a - hi.astype(F32)).astype(BF16)
    return hi, lo


def _dot(a, b):
    return jnp.dot(a, b, preferred_element_type=F32)


def _dot_nt(a, b):
    return lax.dot_general(a, b, (((1,), (1,)), ((), ())), preferred_element_type=F32)


def _dot3(a, b):
    a_hi, a_lo = _split_bf16(a)
    b_hi, b_lo = _split_bf16(b)
    return _dot(a_hi, b_hi) + _dot(a_lo, b_hi) + _dot(a_hi, b_lo)


def _silu(x):
    return x * jax.nn.sigmoid(x)


MOD_TN = 1536


def _mod_kernel(c_ref, w_ref, b_ref, o_ref):
    o_ref[0] = _dot3(_silu(c_ref[...]), w_ref[0]) + b_ref[0]


def _modulation(cond, w_mod, b_mod):
    n = 6 * D_MODEL
    return pl.pallas_call(
        _mod_kernel,
        grid=(DEPTH, n // MOD_TN),
        in_specs=[
            pl.BlockSpec((8, D_MODEL), lambda l, j: (0, 0)),
            pl.BlockSpec((1, D_MODEL, MOD_TN), lambda l, j: (l, 0, j)),
            pl.BlockSpec((1, 1, MOD_TN), lambda l, j: (l, 0, j)),
        ],
        out_specs=pl.BlockSpec((1, 8, MOD_TN), lambda l, j: (l, 0, j)),
        out_shape=jax.ShapeDtypeStruct((DEPTH, 8, n), F32),
        compiler_params=_cparams("arbitrary", "arbitrary"),
        name="modulation",
    )(cond, w_mod, b_mod.reshape(DEPTH, 1, n))


IN_TM = 512


def _modulated_norm(x, g, shift, scale):
    ms = jnp.mean(x * x, axis=-1, keepdims=True)
    return (x * lax.rsqrt(ms + EPS) * g) * (1.0 + scale) + shift


def _tok_specs(x, tm, width):
    if isinstance(x, tuple):
        n_p = T_P // tm
        return ([pl.BlockSpec((tm, width), lambda i: (jnp.minimum(i, n_p - 1), 0)),
                 pl.BlockSpec((tm, width), lambda i: (jnp.maximum(i - n_p, 0), 0))], list(x))
    return [pl.BlockSpec((tm, width), lambda i: (i, 0))], [x]


def _tok_load(refs, i, tm):
    if len(refs) == 2:
        return jnp.where(i < T_P // tm, refs[0][...], refs[1][...])
    return refs[0][...]


def _norm_in_kernel(*refs, n_x):
    x_refs, (g_ref, sh_ref, sc_ref, w_ref, o_ref) = refs[:n_x], refs[n_x:]
    i = pl.program_id(0)
    r = _mod_row(i, IN_TM)
    h = _modulated_norm(_tok_load(x_refs, i, IN_TM), g_ref[...], sh_ref[pl.ds(r, 1), :],
                        sc_ref[pl.ds(r, 1), :])
    o_ref[...] = _dot(h.astype(BF16), w_ref[...])


def _norm_in(x, g, mod, w_bf16):
    n = w_bf16.shape[1]
    x_specs, x_ops = _tok_specs(x, IN_TM, D_MODEL)
    return pl.pallas_call(
        functools.partial(_norm_in_kernel, n_x=len(x_ops)),
        grid=(T_ALL // IN_TM,),
        in_specs=x_specs + [
            pl.BlockSpec((1, D_MODEL), lambda i: (0, 0)),
            pl.BlockSpec((8, D_MODEL), lambda i: (0, 0)),
            pl.BlockSpec((8, D_MODEL), lambda i: (0, 1)),
            pl.BlockSpec((D_MODEL, n), lambda i: (0, 0)),
        ],
        out_specs=pl.BlockSpec((IN_TM, n), lambda i: (i, 0)),
        out_shape=jax.ShapeDtypeStruct((T_ALL, n), F32),
        compiler_params=_cparams("arbitrary"),
        name="norm_in_proj",
    )(*x_ops, g.reshape(1, D_MODEL), mod, mod, w_bf16)


QKV_TM = 1024


def _rope_tables():
    half = DK_B // 2
    freqs = ROPE_BASE ** (-np.arange(0, half, 2, dtype=np.float64) / half)
    l = np.arange(DEC_SEQ)
    pos_r = (l // GRID_W).astype(np.float64)
    pos_c = (l % GRID_W).astype(np.float64)
    lane = np.arange(LANES)
    jj = lane % DK_B
    m = jj % half
    f = m % (half // 2)
    pos = np.where((jj < half)[None, :], pos_r[:, None], pos_c[:, None])
    ang = pos * freqs[f][None, :]
    sign = np.where(m < half // 2, -1.0, 1.0)[None, :]
    cos = np.concatenate([np.ones((QKV_TM, LANES)), np.cos(ang)], axis=0)
    sin = np.concatenate([np.zeros((QKV_TM, LANES)), sign * np.sin(ang)], axis=0)
    return cos.astype(np.float32), sin.astype(np.float32)


def _segment_mean_matrix():
    lane = np.arange(LANES)
    same = (lane[:, None] // DK_B) == (lane[None, :] // DK_B)
    return (same.astype(np.float32) / DK_B)


def _qk_prep(x, g, cos, sin, seg):
    x2 = x * x
    hi, lo = _split_bf16(x2)
    ms = _dot(hi, seg) + _dot(lo, seg)
    y = x * lax.rsqrt(ms + EPS) * g
    lane = lax.broadcasted_iota(jnp.int32, y.shape, 1)
    lower = (lane % (DK_B // 2)) < (DK_B // 4)
    partner = jnp.where(lower, pltpu.roll(y, LANES - DK_B // 4, 1), pltpu.roll(y, DK_B // 4, 1))
    return y * cos + partner * sin


def _qkv_prep_kernel(q_ref, k_ref, v_ref, qn_ref, kn_ref, cos_ref, sin_ref, seg_ref,
                     qo_ref, ko_ref, vo_ref, kf_ref):
    cos = cos_ref[...]
    sin = sin_ref[...]
    seg = seg_ref[...]
    scale = math.log2(math.e) / math.sqrt(DK_B)
    for h in range(H_B):
        cols = slice(h * LANES, (h + 1) * LANES)
        qo_ref[h] = (_qk_prep(q_ref[:, cols], qn_ref[...], cos, sin, seg) * scale).astype(BF16)
        k = _qk_prep(k_ref[:, cols], kn_ref[...], cos, sin, seg)
        kf_ref[h] = k
        ko_ref[h] = k.astype(BF16)
        vo_ref[h] = v_ref[:, cols].astype(BF16)


def _qkv_prep(u, qn, kn):
    cos, sin = _rope_tables()
    seg = jnp.asarray(_segment_mean_matrix(), BF16)
    qn2 = jnp.concatenate([qn, qn]).reshape(1, LANES)
    kn2 = jnp.concatenate([kn, kn]).reshape(1, LANES)
    col0 = 3 * D_A // D_B
    p_steps = T_P // QKV_TM
    s_steps = DEC_SEQ // QKV_TM

    def tab_map(i):
        return (jnp.where(i < p_steps, 0, 1 + (i - p_steps) % s_steps), 0)

    out_b = jax.ShapeDtypeStruct((H_B, T_ALL, LANES), BF16)
    out_spec = pl.BlockSpec((H_B, QKV_TM, LANES), lambda i: (0, i, 0))
    return pl.pallas_call(
        _qkv_prep_kernel,
        grid=(T_ALL // QKV_TM,),
        in_specs=[
            pl.BlockSpec((QKV_TM, D_B), lambda i: (i, col0)),
            pl.BlockSpec((QKV_TM, D_B), lambda i: (i, col0 + 1)),
            pl.BlockSpec((QKV_TM, D_B), lambda i: (i, col0 + 2)),
            pl.BlockSpec((1, LANES), lambda i: (0, 0)),
            pl.BlockSpec((1, LANES), lambda i: (0, 0)),
            pl.BlockSpec((QKV_TM, LANES), tab_map),
            pl.BlockSpec((QKV_TM, LANES), tab_map),
            pl.BlockSpec((LANES, LANES), lambda i: (0, 0)),
        ],
        out_specs=[out_spec] * 4,
        out_shape=[out_b, out_b, out_b, jax.ShapeDtypeStruct((H_B, T_ALL, LANES), F32)],
        compiler_params=_cparams("arbitrary"),
        name="qkv_prep",
    )(u, u, u, qn2, kn2, jnp.asarray(cos), jnp.asarray(sin), seg)


def _lambda(lq1_ref, lk1_ref, lq2_ref, lk2_ref, lam_init):
    a = jnp.sum(lq1_ref[...] * lk1_ref[...], axis=-1, keepdims=True)
    b = jnp.sum(lq2_ref[...] * lk2_ref[...], axis=-1, keepdims=True)
    return jnp.exp(a) - jnp.exp(b) + lam_init


ATTN_TQ = 512


def _attn_body(q, keys, vals, lam, subg, lam_init):
    lane = lax.broadcasted_iota(jnp.int32, q.shape, 1)
    zero = jnp.zeros_like(q)
    qa = jnp.where(lane < DK_B, q, zero)
    qb = jnp.where(lane < DK_B, zero, q)
    scores = [[_dot_nt(qq, k) for k in keys] for qq in (qa, qb)]
    outs = []
    for ss in scores:
        m = functools.reduce(jnp.maximum, [jnp.max(s, axis=-1, keepdims=True) for s in ss])
        ps = [jnp.exp2(s - m) for s in ss]
        l = functools.reduce(jnp.add, [jnp.sum(p, axis=-1, keepdims=True) for p in ps])
        pv = functools.reduce(jnp.add, [_dot(p.astype(BF16), v) for p, v in zip(ps, vals)])
        outs.append(pv / l)
    o = outs[0] - lam * outs[1]
    ms = jnp.mean(o * o, axis=-1, keepdims=True)
    return (o * lax.rsqrt(ms + EPS) * subg) * (1.0 - lam_init)


def _attn_prompt_kernel(q_ref, k_ref, v_ref, lq1, lk1, lq2, lk2, sg_ref, o_ref, *, lam_init):
    lam = _lambda(lq1, lk1, lq2, lk2, lam_init)
    for h in range(H_B):
        o_ref[:, h * DV_B:(h + 1) * DV_B] = _attn_body(
            q_ref[h], [k_ref[h]], [v_ref[h]], lam, sg_ref[...], lam_init)


def _attn_latent_kernel(q_ref, k_ref, v_ref, ck_ref, cv_ref, lq1, lk1, lq2, lk2, sg_ref, o_ref,
                        *, lam_init):
    lam = _lambda(lq1, lk1, lq2, lk2, lam_init)
    keys = [ck_ref[0, 0, 0].astype(BF16), k_ref[0]]
    vals = [cv_ref[0, 0, 0].astype(BF16), v_ref[0]]
    o_ref[...] = _attn_body(q_ref[0], keys, vals, lam, sg_ref[...], lam_init)


def _attention(q, k, v, cache_k, cache_v, e, lam_params, subg, lam_init):
    small = [p.reshape(1, DK_B) for p in lam_params] + [subg.reshape(1, DV_B)]
    small_specs2 = [pl.BlockSpec((1, DK_B), lambda b: (0, 0))] * 4 + \
                   [pl.BlockSpec((1, DV_B), lambda b: (0, 0))]
    small_specs3 = [pl.BlockSpec((1, DK_B), lambda s, h, j: (0, 0))] * 4 + \
                   [pl.BlockSpec((1, DV_B), lambda s, h, j: (0, 0))]
    o_prompt = pl.pallas_call(
        functools.partial(_attn_prompt_kernel, lam_init=lam_init),
        grid=(BATCH,),
        in_specs=[pl.BlockSpec((H_B, SEQ, LANES), lambda b: (0, b, 0))] * 3 + small_specs2,
        out_specs=pl.BlockSpec((SEQ, D_B), lambda b: (b, 0)),
        out_shape=jax.ShapeDtypeStruct((T_P, D_B), F32),
        compiler_params=_cparams("arbitrary"),
        name="attn_prompt",
    )(q, k, v, *small)

    q_tiles = DEC_SEQ // ATTN_TQ
    kv_spec = pl.BlockSpec((1, DEC_SEQ, LANES), lambda s, h, j: (h, T_P // DEC_SEQ + s, 0))
    c_spec = pl.BlockSpec((1, 1, 1, PAST_LEN, LANES), lambda s, h, j: (s, e, h, 0, 0))
    o_latent = pl.pallas_call(
        functools.partial(_attn_latent_kernel, lam_init=lam_init),
        grid=(DEC_BATCH, H_B, q_tiles),
        in_specs=[pl.BlockSpec((1, ATTN_TQ, LANES),
                               lambda s, h, j: (h, T_P // ATTN_TQ + s * q_tiles + j, 0)),
                  kv_spec, kv_spec, c_spec, c_spec] + small_specs3,
        out_specs=pl.BlockSpec((ATTN_TQ, LANES), lambda s, h, j: (s * q_tiles + j, h)),
        out_shape=jax.ShapeDtypeStruct((T_S, D_B), F32),
        compiler_params=_cparams("arbitrary", "arbitrary", "arbitrary"),
        name="attn_latent",
    )(q, k, v, cache_k, cache_v, *small)
    return o_prompt, o_latent


HALO8 = 8
HALO16 = 16


def _prev_block(i, rows):
    return jnp.maximum(i * (TILE // rows) - 1, 0)


def _next_block(i, rows):
    return jnp.minimum((i + 1) * (TILE // rows), T_ALL // rows - 1)


def _even_out_kernel(*refs, n_x, n_o):
    x_refs, o_refs = refs[:n_x], refs[n_x:n_x + n_o]
    (bg_ref, cg_ref, hin_ref, cgp_ref, hinp_ref, cgn_ref, hinn_ref, cw_ref, g1_ref, w_ref,
     out_ref) = refs[n_x + n_o:]
    i = pl.program_id(0)
    first, last = _seq_flags(i)
    r = _mod_row(i, TILE)
    z = cg_ref[...] * hin_ref[...]
    zp = jnp.where(first, 0.0, cgp_ref[HALO8 - 1:HALO8, :] * hinp_ref[HALO8 - 1:HALO8, :])
    zn = jnp.where(last, 0.0, cgn_ref[0:1, :] * hinn_ref[0:1, :])
    row = lax.broadcasted_iota(jnp.int32, z.shape, 0)
    z_prev = jnp.where(row == 0, zp, pltpu.roll(z, 1, 0))
    z_next = jnp.where(row == TILE - 1, zn, pltpu.roll(z, TILE - 1, 0))
    cw = cw_ref[...]
    ya = bg_ref[...] * (cw[0:1, :] * z_prev + cw[1:2, :] * z + cw[2:3, :] * z_next)
    o = _tok_load(o_refs, i, TILE)
    y = _dot(ya.astype(BF16), w_ref[0:D_A, :]) + _dot(o.astype(BF16), w_ref[D_A:, :])
    out_ref[...] = _tok_load(x_refs, i, TILE) + g1_ref[pl.ds(r, 1), :] * y


def _even_out(x, u, o, conv_w, mod, w_out_bf16):
    tile_spec = lambda c: pl.BlockSpec((TILE, D_A), lambda i: (i, c))
    prev_spec = lambda c: pl.BlockSpec((HALO8, D_A), lambda i: (_prev_block(i, HALO8), c))
    next_spec = lambda c: pl.BlockSpec((HALO8, D_A), lambda i: (_next_block(i, HALO8), c))
    x_specs, x_ops = _tok_specs(x, TILE, D_MODEL)
    o_specs, o_ops = _tok_specs(o, TILE, D_B)
    return pl.pallas_call(
        functools.partial(_even_out_kernel, n_x=len(x_ops), n_o=len(o_ops)),
        grid=(N_TILES,),
        in_specs=x_specs + o_specs + [
            tile_spec(0), tile_spec(1), tile_spec(2),
            prev_spec(1), prev_spec(2), next_spec(1), next_spec(2),
            pl.BlockSpec((3, D_A), lambda i: (0, 0)),
            pl.BlockSpec((8, D_MODEL), lambda i: (0, 2)),
            pl.BlockSpec((D_MODEL, D_MODEL), lambda i: (0, 0)),
        ],
        out_specs=pl.BlockSpec((TILE, D_MODEL), lambda i: (i, 0)),
        out_shape=jax.ShapeDtypeStruct((T_ALL, D_MODEL), F32),
        compiler_params=_cparams("arbitrary"),
        name="even_mixer_out",
    )(*x_ops, *o_ops, u, u, u, u, u, u, u, conv_w, mod, w_out_bf16)


def _odd_out_kernel(x_ref, a_ref, b_ref, pd_ref, ap_ref, bp_ref, an_ref, bn_ref, pp_ref, pn_ref,
                    cw_ref, cb_ref, lg_ref, lb_ref, wp_ref, ps_ref, g1_ref, w_ref, out_ref,
                    ext_ref, extp_ref, shift_ref):
    i = pl.program_id(0)
    first, last = _seq_flags(i)
    r = _mod_row(i, TILE)
    ext_ref[0:HALO16, :] = jnp.where(first, 0.0, ap_ref[...] * jax.nn.sigmoid(bp_ref[...]))
    ext_ref[HALO16:HALO16 + TILE, :] = a_ref[...] * jax.nn.sigmoid(b_ref[...])
    ext_ref[HALO16 + TILE:, :] = jnp.where(last, 0.0, an_ref[...] * jax.nn.sigmoid(bn_ref[...]))
    base = HALO16 - CONV_C // 2
    parts = []
    for cb in range(D_C // LANES):
        cols = slice(cb * LANES, (cb + 1) * LANES)
        acc = jnp.zeros((TILE, LANES), F32)
        for phase in range(8):
            taps = [j for j in range(CONV_C) if (base + j) % 8 == phase]
            reach = max((base + j) // 8 for j in taps)
            rows = TILE + 8 * reach
            shift_ref[0:rows, :] = ext_ref[pl.ds(phase, rows), cols]
            for j in taps:
                a = (base + j) // 8
                acc = acc + cw_ref[j:j + 1, cols] * shift_ref[8 * a:8 * a + TILE, :]
        parts.append(acc)
    g = jnp.concatenate(parts, axis=-1) + cb_ref[...]
    mu = jnp.mean(g, axis=-1, keepdims=True)
    var = jnp.mean(jnp.square(g - mu), axis=-1, keepdims=True)
    g = _silu(((g - mu) * lax.rsqrt(var + EPS)) * lg_ref[...] + lb_ref[...])
    extp_ref[0:HALO8, :] = jnp.where(first, 0.0, pp_ref[...])
    extp_ref[HALO8:HALO8 + TILE, :] = pd_ref[...]
    extp_ref[HALO8 + TILE:, :] = jnp.where(last, 0.0, pn_ref[...])
    seq_len = jnp.where(i < P_TILES, SEQ, DEC_SEQ)
    pos = _seq_tile(i) * TILE + lax.broadcasted_iota(jnp.int32, (TILE, 1), 0)
    yd = []
    for gi, w in enumerate(POOL_WINDOWS):
        cols = slice(gi * D_DG, (gi + 1) * D_DG)
        s = jnp.zeros((TILE, D_DG), F32)
        for d in range(-(w // 2), w - w // 2):
            s = s + extp_ref[pl.ds(HALO8 + d, TILE), cols]
        lo = jnp.maximum(pos - w // 2, 0)
        hi = jnp.minimum(pos - w // 2 + w, seq_len)
        pooled = s / (hi - lo).astype(F32) - pd_ref[:, cols]
        yd.append(_dot(pooled.astype(BF16), wp_ref[gi]))
    yd = jnp.concatenate(yd, axis=-1) * ps_ref[...]
    y = _dot(g.astype(BF16), w_ref[0:D_C, :]) + _dot(yd.astype(BF16), w_ref[D_C:, :])
    out_ref[...] = x_ref[...] + g1_ref[pl.ds(r, 1), :] * y


def _odd_out(x, u, conv_w, conv_b, ln_g, ln_b, w_pool_bf16, p_scale, mod, w_out_bf16):
    tile_spec = lambda c: pl.BlockSpec((TILE, D_C), lambda i: (i, c))
    prev_spec = lambda rows, c: pl.BlockSpec((rows, D_C), lambda i: (_prev_block(i, rows), c))
    next_spec = lambda rows, c: pl.BlockSpec((rows, D_C), lambda i: (_next_block(i, rows), c))
    vec = lambda: pl.BlockSpec((1, D_C), lambda i: (0, 0))
    return pl.pallas_call(
        _odd_out_kernel,
        grid=(N_TILES,),
        in_specs=[
            pl.BlockSpec((TILE, D_MODEL), lambda i: (i, 0)),
            tile_spec(0), tile_spec(1), tile_spec(2),
            prev_spec(HALO16, 0), prev_spec(HALO16, 1), next_spec(HALO16, 0), next_spec(HALO16, 1),
            prev_spec(HALO8, 2), next_spec(HALO8, 2),
            pl.BlockSpec((CONV_C, D_C), lambda i: (0, 0)),
            vec(), vec(), vec(),
            pl.BlockSpec((len(POOL_WINDOWS), D_DG, D_DG), lambda i: (0, 0, 0)),
            vec(),
            pl.BlockSpec((8, D_MODEL), lambda i: (0, 2)),
            pl.BlockSpec((D_MODEL, D_MODEL), lambda i: (0, 0)),
        ],
        out_specs=pl.BlockSpec((TILE, D_MODEL), lambda i: (i, 0)),
        out_shape=jax.ShapeDtypeStruct((T_ALL, D_MODEL), F32),
        scratch_shapes=[pltpu.VMEM((TILE + 2 * HALO16, D_C), F32),
                        pltpu.VMEM((TILE + 2 * HALO8, D_D), F32),
                        pltpu.VMEM((TILE + 2 * HALO16, LANES), F32)],
        compiler_params=_cparams("arbitrary"),
        name="odd_mixer_out",
    )(x, u, u, u, u, u, u, u, u, u, conv_w, conv_b.reshape(1, D_C), ln_g.reshape(1, D_C),
      ln_b.reshape(1, D_C), w_pool_bf16, p_scale.reshape(1, D_D), mod, w_out_bf16)


GROUP = N_EXPERTS // N_GROUPS
NEG_INF = float("-inf")


def _first_argmax(v, idx, axis):
    m = jnp.max(v, axis=axis, keepdims=True)
    big = jnp.int32(2 ** 30)
    am = jnp.min(jnp.where(v == m, idx, big), axis=axis, keepdims=True)
    return m, am


def _route(scores, biased):
    shape = biased.shape
    member = lax.broadcasted_iota(jnp.int32, shape, 1)
    m1, a1 = _first_argmax(biased, member, 1)
    m2 = jnp.max(jnp.where(member == a1, NEG_INF, biased), axis=1, keepdims=True)
    gscore = m1 + m2
    gidx = lax.broadcasted_iota(jnp.int32, gscore.shape, 0)
    gsel = jnp.zeros(gscore.shape, jnp.bool_)
    for _ in range(TOPK_GROUPS):
        _, am = _first_argmax(gscore, gidx, 0)
        hit = gidx == am
        gsel = jnp.logical_or(gsel, hit)
        gscore = jnp.where(hit, NEG_INF, gscore)
    cand = jnp.where(gsel, biased, NEG_INF)
    eidx = lax.broadcasted_iota(jnp.int32, shape, 0) * GROUP + member
    sel = jnp.zeros(shape, jnp.bool_)
    for _ in range(TOP_K):
        m = jnp.max(jnp.max(cand, axis=1, keepdims=True), axis=0, keepdims=True)
        big = jnp.int32(2 ** 30)
        am = jnp.where(cand == m, eidx, big)
        am = jnp.min(jnp.min(am, axis=1, keepdims=True), axis=0, keepdims=True)
        hit = eidx == am
        sel = jnp.logical_or(sel, hit)
        cand = jnp.where(hit, NEG_INF, cand)
    wsel = jnp.where(sel, scores, 0.0)
    tot = jnp.sum(jnp.sum(wsel, axis=1, keepdims=True), axis=0, keepdims=True)
    return wsel / tot * ROUTED_SCALE, sel


MOE_TC = 4096
N_CHUNKS = T_ALL // MOE_TC
ROW_TILE = 256
N_FLUSH = MOE_TC // TILE
PRE_TM = 1024
ROW_SUB = D_MODEL // LANES


def _moe_pre_kernel(x_ref, g_ref, sh_ref, sc_ref, wr_ref, br_ref, tri_ref, wsg_ref, wsu_ref,
                    wsd_ref, hrow_ref, gate_ref, rank_ref, shared_ref, carry_ref):
    i = pl.program_id(0)
    r = _mod_row(i, PRE_TM)
    h = _modulated_norm(x_ref[...], g_ref[...], sh_ref[pl.ds(r, 1), :], sc_ref[pl.ds(r, 1), :])
    hb = h.astype(BF16)
    for s in range(ROW_SUB):
        hrow_ref[pl.ds(s, PRE_TM, stride=ROW_SUB), :] = h[:, s * LANES:(s + 1) * LANES]
    h_hi, h_lo = hb, (h - hb.astype(F32)).astype(BF16)
    w_hi, w_lo = _split_bf16(wr_ref[...])
    logits = _dot_nt(w_hi, h_hi) + _dot_nt(w_lo, h_hi) + _dot_nt(w_hi, h_lo)
    scores = jax.nn.sigmoid(logits)
    biased = scores + br_ref[:, 0:1]
    shape3 = (N_GROUPS, GROUP, PRE_TM)
    gate_t, sel = _route(scores.reshape(shape3), biased.reshape(shape3))
    gate_t = gate_t.reshape(N_EXPERTS, PRE_TM)
    sel = jnp.where(sel.reshape(N_EXPERTS, PRE_TM), 1.0, 0.0)

    @pl.when(i % (MOE_TC // PRE_TM) == 0)
    def _():
        carry_ref[...] = jnp.zeros_like(carry_ref)

    carry = carry_ref[...]
    local = _dot(sel.astype(BF16), tri_ref[...])
    rank = jnp.where(sel > 0.0, local + jnp.concatenate([carry] * (PRE_TM // LANES), axis=1), -1.0)
    carry_ref[...] = carry + jnp.sum(sel, axis=1, keepdims=True)
    gate_ref[...] = gate_t
    rank_ref[...] = rank
    a = _silu(_dot(hb, wsg_ref[...])) * _dot(hb, wsu_ref[...])
    shared_ref[...] = _dot(a.astype(BF16), wsd_ref[...])


def _moe_pre(x, g, mod, w_router_t, b_router, wsg, wsu, wsd):
    return pl.pallas_call(
        _moe_pre_kernel,
        grid=(T_ALL // PRE_TM,),
        in_specs=[
            pl.BlockSpec((PRE_TM, D_MODEL), lambda i: (i, 0)),
            pl.BlockSpec((1, D_MODEL), lambda i: (0, 0)),
            pl.BlockSpec((8, D_MODEL), lambda i: (0, 3)),
            pl.BlockSpec((8, D_MODEL), lambda i: (0, 4)),
            pl.BlockSpec((N_EXPERTS, D_MODEL), lambda i: (0, 0)),
            pl.BlockSpec((N_EXPERTS, LANES), lambda i: (0, 0)),
            pl.BlockSpec((PRE_TM, PRE_TM), lambda i: (0, 0)),
            pl.BlockSpec((D_MODEL, D_SHARED), lambda i: (0, 0)),
            pl.BlockSpec((D_MODEL, D_SHARED), lambda i: (0, 0)),
            pl.BlockSpec((D_SHARED, D_MODEL), lambda i: (0, 0)),
        ],
        out_specs=[
            pl.BlockSpec((PRE_TM * ROW_SUB, LANES), lambda i: (i, 0)),
            pl.BlockSpec((N_EXPERTS, PRE_TM), lambda i: (0, i)),
            pl.BlockSpec((N_EXPERTS, PRE_TM), lambda i: (0, i)),
            pl.BlockSpec((PRE_TM, D_MODEL), lambda i: (i, 0)),
        ],
        out_shape=[
            jax.ShapeDtypeStruct((T_ALL * ROW_SUB, LANES), F32),
            jax.ShapeDtypeStruct((N_EXPERTS, T_ALL), F32),
            jax.ShapeDtypeStruct((N_EXPERTS, T_ALL), F32),
            jax.ShapeDtypeStruct((T_ALL, D_MODEL), F32),
        ],
        scratch_shapes=[pltpu.VMEM((N_EXPERTS, LANES), F32)],
        compiler_params=_cparams("arbitrary"),
        name="moe_pre",
    )(x, g.reshape(1, D_MODEL), mod, mod, w_router_t,
      jnp.broadcast_to(b_router.reshape(N_EXPERTS, 1), (N_EXPERTS, LANES)),
      jnp.asarray(np.triu(np.ones((PRE_TM, PRE_TM), np.float32), 1), BF16), wsg, wsu, wsd)


LIST_ROWS = MOE_TC // LANES
TILE_ROWS = ROW_TILE // LANES
DUMMY_ROW = MOE_TC * ROW_SUB


def _expert_counts(rank_t):
    sel = (rank_t >= 0.0).astype(jnp.int32).reshape(N_EXPERTS, N_CHUNKS, MOE_TC)
    return jnp.sum(sel, axis=-1).T


def _moe_sort_kernel(rank_ref, gate_ref, cnt_ref, list_ref, w_ref):
    rank = rank_ref[...]
    lane = lax.broadcasted_iota(jnp.int32, rank.shape, 1)
    d = jnp.where(rank >= 0.0, lane - rank.astype(jnp.int32), 0)
    w = gate_ref[...]
    for s in range(MOE_TC.bit_length() - 1):
        k = 1 << s
        d_in = pltpu.roll(d, MOE_TC - k, 1)
        w_in = pltpu.roll(w, MOE_TC - k, 1)
        take = (d_in & k) != 0
        leave = (d & k) != 0
        d = jnp.where(take, d_in, jnp.where(leave, 0, d))
        w = jnp.where(take, w_in, w)
    valid = lane < cnt_ref[0][:, 0:1]
    row = (lane + d) * ROW_SUB
    gsrc = jnp.where(valid, row, 0)
    ssrc = jnp.where(valid, row, DUMMY_ROW)
    w = jnp.where(valid, w, 0.0)
    for b in range(LIST_ROWS):
        cols = slice(b * LANES, (b + 1) * LANES)
        list_ref[pl.ds(b, N_EXPERTS, stride=2 * LIST_ROWS), :] = gsrc[:, cols]
        list_ref[pl.ds(LIST_ROWS + b, N_EXPERTS, stride=2 * LIST_ROWS), :] = ssrc[:, cols]
        w_ref[pl.ds(b, N_EXPERTS, stride=LIST_ROWS), :] = w[:, cols]


def _moe_sort(rank_t, gate_t, counts):
    chunk_spec = pl.BlockSpec((N_EXPERTS, MOE_TC), lambda c: (0, c))
    cnt_b = jnp.broadcast_to(counts[:, :, None], (N_CHUNKS, N_EXPERTS, LANES))
    n_rows = N_CHUNKS * N_EXPERTS * LIST_ROWS
    return pl.pallas_call(
        _moe_sort_kernel,
        grid=(N_CHUNKS,),
        in_specs=[chunk_spec, chunk_spec,
                  pl.BlockSpec((1, N_EXPERTS, LANES), lambda c: (c, 0, 0))],
        out_specs=[pl.BlockSpec((N_EXPERTS * 2 * LIST_ROWS, LANES), lambda c: (c, 0)),
                   pl.BlockSpec((N_EXPERTS * LIST_ROWS, LANES), lambda c: (c, 0))],
        out_shape=[jax.ShapeDtypeStruct((2 * n_rows, LANES), jnp.int32),
                   jax.ShapeDtypeStruct((n_rows, LANES), F32)],
        compiler_params=_cparams("arbitrary"),
        name="moe_sort",
    )(rank_t, gate_t, cnt_b)


SCATTER_BATCH = 16


EXPERTS_PER_STEP = 2
EXPERT_STEPS = N_EXPERTS // EXPERTS_PER_STEP


N_PIECES = 8
PIECE_ROWS = ROW_TILE // N_PIECES


class _ExpertRefs:
    def __init__(self, list0, wt0, list_ref, wt_ref, h_ref, acc_ref, wgb_ref, wub_ref, wdb_ref):
        self.list0, self.wt0 = list0, wt0
        self.list_ref, self.wt_ref, self.h_ref, self.acc_ref = list_ref, wt_ref, h_ref, acc_ref
        self.wgb_ref, self.wub_ref, self.wdb_ref = wgb_ref, wub_ref, wdb_ref


def _gather_rows(ex, t, r0, n, xs_ref):
    base = ex.list0 + t * TILE_ROWS
    for r in range(r0, r0 + n):
        tok = pl.multiple_of(ex.list_ref[base + r // LANES, r % LANES], ROW_SUB)
        xs_ref[r // 8, pl.ds(r % 8, ROW_SUB, stride=8), :] = ex.h_ref[pl.ds(tok, ROW_SUB), :]


def _scatter_rows(ex, t, r0, n, ys_ref):
    base = ex.list0 + LIST_ROWS + t * TILE_ROWS
    for b0 in range(r0, r0 + n, SCATTER_BATCH):
        rows = range(b0, b0 + SCATTER_BATCH)
        dsts = [ex.acc_ref.at[pl.ds(pl.multiple_of(ex.list_ref[base + r // LANES, r % LANES],
                                                   ROW_SUB), ROW_SUB), :] for r in rows]
        news = [dst[...] + ys_ref[r // 8, pl.ds(r % 8, ROW_SUB, stride=8), :]
                for dst, r in zip(dsts, rows)]
        for dst, new in zip(dsts, news):
            dst[...] = new


def _tile_ffn(ex, t, xs_ref, ys_ref, side_work=None):
    kc = D_MODEL // (N_PIECES // 2)
    hg = hu = None
    for p in range(N_PIECES // 2):
        if side_work is not None:
            side_work(p)
        xk = jnp.concatenate(
            [xs_ref[:, s * 8:(s + 1) * 8, :].reshape(ROW_TILE, LANES)
             for s in range(p * kc // LANES, (p + 1) * kc // LANES)], axis=1).astype(BF16)
        dg = _dot(xk, ex.wgb_ref[p * kc:(p + 1) * kc, :])
        du = _dot(xk, ex.wub_ref[p * kc:(p + 1) * kc, :])
        hg = dg if hg is None else hg + dg
        hu = du if hu is None else hu + du
    eye = (lax.broadcasted_iota(jnp.int32, (ROW_TILE, LANES), 0) % LANES
           == lax.broadcasted_iota(jnp.int32, (ROW_TILE, LANES), 1))
    row_blk = lax.broadcasted_iota(jnp.int32, (ROW_TILE, LANES), 0) // LANES
    wrows = functools.reduce(
        lambda a, b: a + b,
        [jnp.where(row_blk == k, ex.wt_ref[pl.ds(ex.wt0 + t * TILE_ROWS + k, 1), :], 0.0)
         for k in range(TILE_ROWS)])
    wcol = jnp.sum(jnp.where(eye, wrows, 0.0), axis=1, keepdims=True)
    a = (_silu(hg) * hu * wcol).astype(BF16)
    for p in range(N_PIECES // 2):
        if side_work is not None:
            side_work(N_PIECES // 2 + p)
        y = _dot(a, ex.wdb_ref[:, p * kc:(p + 1) * kc])
        for q in range(kc // LANES):
            s = p * kc // LANES + q
            ys_ref[:, s * 8:(s + 1) * 8, :] = y[:, q * LANES:(q + 1) * LANES].reshape(
                ROW_TILE // 8, 8, LANES)


def _expert_tile(ex, t, xs_ref, ys_ref):
    _gather_rows(ex, t, 0, ROW_TILE, xs_ref)
    _tile_ffn(ex, t, xs_ref, ys_ref)
    _scatter_rows(ex, t, 0, ROW_TILE, ys_ref)


def _expert_tile_pair(ex, t, xs_refs, ys_refs):
    _gather_rows(ex, t, 0, ROW_TILE, xs_refs[0])
    _tile_ffn(ex, t, xs_refs[0], ys_refs[0],
              lambda p: _gather_rows(ex, t + 1, p * PIECE_ROWS, PIECE_ROWS, xs_refs[1]))
    _tile_ffn(ex, t + 1, xs_refs[1], ys_refs[1],
              lambda p: _scatter_rows(ex, t, p * PIECE_ROWS, PIECE_ROWS, ys_refs[0]))
    _scatter_rows(ex, t + 1, 0, ROW_TILE, ys_refs[1])


def _moe_routed_kernel(cnt_ref, list_ref, wt_ref, h_ref, wg_ref, wu_ref, wd_ref,
                       x_ref, sh_ref, g2_ref, *refs):
    out_refs, scratch = refs[:-8], refs[-8:]
    acc_ref, wgb_ref, wub_ref, wdb_ref = scratch[:4]
    xs_refs, ys_refs = scratch[4:6], scratch[6:8]
    c = pl.program_id(0)
    j = pl.program_id(1)

    @pl.when(j == 0)
    def _():
        acc_ref[...] = jnp.zeros_like(acc_ref)

    for k in range(EXPERTS_PER_STEP):
        expert = jnp.minimum(j, EXPERT_STEPS - 1) * EXPERTS_PER_STEP + k
        count = cnt_ref[c * N_EXPERTS + expert]

        @pl.when(jnp.logical_and(j < EXPERT_STEPS, count > 0))
        def _():
            wgb_ref[...] = wg_ref[0, k].astype(BF16)
            wub_ref[...] = wu_ref[0, k].astype(BF16)
            wdb_ref[...] = wd_ref[0, k].astype(BF16)

            ex = _ExpertRefs(k * 2 * LIST_ROWS, k * LIST_ROWS, list_ref, wt_ref, h_ref, acc_ref,
                             wgb_ref, wub_ref, wdb_ref)
            n_tiles = (count + ROW_TILE - 1) // ROW_TILE

            def tile_pair(m, carry):
                _expert_tile_pair(ex, 2 * m, xs_refs, ys_refs)
                return carry

            lax.fori_loop(0, n_tiles // 2, tile_pair, 0)

            @pl.when(n_tiles % 2 == 1)
            def _():
                _expert_tile(ex, n_tiles - 1, xs_refs[0], ys_refs[0])

    @pl.when(j >= EXPERT_STEPS)
    def _():
        base = (j - EXPERT_STEPS) * (TILE * ROW_SUB)
        moe = jnp.concatenate(
            [acc_ref[pl.ds(base + s, TILE, stride=ROW_SUB), :] for s in range(ROW_SUB)], axis=1)
        new_x = x_ref[...] + g2_ref[pl.ds(c, 1), :] * (moe + sh_ref[...])
        if len(out_refs) == 1:
            out_refs[0][...] = new_x
        else:
            prompt_ref, latent_ref = out_refs

            @pl.when(c < T_P // MOE_TC)
            def _():
                prompt_ref[...] = new_x

            @pl.when(c >= T_P // MOE_TC)
            def _():
                latent_ref[...] = new_x


def _moe_routed(lists, wts, counts, hrows, w_gate, w_up, w_down, li, x, shared, mod,
                split_output):
    group = lambda j: jnp.minimum(j, EXPERT_STEPS - 1)
    list_map = lambda c, j, nt: (c * EXPERT_STEPS + group(j), 0)
    w_map = lambda c, j, nt: (li, group(j), 0, 0)
    w_in_spec = pl.BlockSpec((1, EXPERTS_PER_STEP, D_MODEL, D_EXPERT), w_map)
    w_out_spec = pl.BlockSpec((1, EXPERTS_PER_STEP, D_EXPERT, D_MODEL), w_map)
    out_blk = lambda c, j: c * N_FLUSH + jnp.maximum(j - EXPERT_STEPS, 0)
    tok_spec = pl.BlockSpec((TILE, D_MODEL), lambda c, j, nt: (out_blk(c, j), 0))
    if split_output:
        out_specs = [
            pl.BlockSpec((TILE, D_MODEL), lambda c, j, nt: (jnp.minimum(out_blk(c, j), P_TILES - 1), 0)),
            pl.BlockSpec((TILE, D_MODEL), lambda c, j, nt: (jnp.maximum(out_blk(c, j) - P_TILES, 0), 0)),
        ]
        out_shape = [jax.ShapeDtypeStruct((T_P, D_MODEL), F32),
                     jax.ShapeDtypeStruct((T_S, D_MODEL), F32)]
    else:
        out_specs = tok_spec
        out_shape = jax.ShapeDtypeStruct((T_ALL, D_MODEL), F32)
    stage = pltpu.VMEM((ROW_TILE // 8, 8 * ROW_SUB, LANES), F32)
    grid_spec = pltpu.PrefetchScalarGridSpec(
        num_scalar_prefetch=1,
        grid=(N_CHUNKS, EXPERT_STEPS + N_FLUSH),
        in_specs=[
            pl.BlockSpec((EXPERTS_PER_STEP * 2 * LIST_ROWS, LANES), list_map,
                         memory_space=pltpu.SMEM),
            pl.BlockSpec((EXPERTS_PER_STEP * LIST_ROWS, LANES), list_map),
            pl.BlockSpec((MOE_TC * ROW_SUB, LANES), lambda c, j, nt: (c, 0),
                         pipeline_mode=pl.Buffered(1)),
            w_in_spec, w_in_spec, w_out_spec,
            tok_spec, tok_spec,
            pl.BlockSpec((8, D_MODEL), lambda c, j, nt: (0, 5)),
        ],
        out_specs=out_specs,
        scratch_shapes=[
            pltpu.VMEM(((MOE_TC + 1) * ROW_SUB, LANES), F32),
            pltpu.VMEM((D_MODEL, D_EXPERT), BF16),
            pltpu.VMEM((D_MODEL, D_EXPERT), BF16),
            pltpu.VMEM((D_EXPERT, D_MODEL), BF16),
            stage, stage, stage, stage,
        ],
    )
    return pl.pallas_call(
        _moe_routed_kernel,
        grid_spec=grid_spec,
        out_shape=out_shape,
        compiler_params=_cparams("arbitrary", "arbitrary", vmem_limit=VMEM_LIMIT_ROUTED),
        name="moe_routed",
    )(counts.reshape(-1), lists, wts, hrows, w_gate, w_up, w_down, x, shared, mod)


def kernel(x_prompt, x_sample, cache_k, cache_v, c, c_ctx, w_mod, b_mod, norm1, norm2, w_in_even, conv_a, q_norm, k_norm, lam_q1, lam_k1, lam_q2, lam_k2, subln, w_out_even, w_in_odd, conv_c, conv_c_b, ln_c_g, ln_c_b, w_pool, pool_scale, w_out_odd, w_router, b_router, w_gate, w_up, w_down, ws_gate, ws_up, ws_down):
    x = (x_prompt.reshape(T_P, D_MODEL), x_sample.reshape(T_S, D_MODEL))
    cond = jnp.concatenate([c_ctx[None, :], c, jnp.zeros((8 - 1 - DEC_BATCH, D_MODEL), F32)], axis=0)
    mod_all = _modulation(cond, w_mod, b_mod)

    new_k, new_v = [], []
    for li in range(DEPTH):
        mod = mod_all[li]
        if li % 2 == 0:
            e = li // 2
            lam_init = 0.8 - 0.6 * math.exp(-0.3 * li)
            u = _norm_in(x, norm1[li], mod, w_in_even[e].astype(BF16))
            q, k, v, k_f32 = _qkv_prep(u, q_norm[e], k_norm[e])
            new_k.append(jnp.transpose(k_f32[:, :T_P].reshape(H_B, BATCH, SEQ, LANES), (1, 0, 2, 3)))
            new_v.append(jnp.transpose(
                u[:T_P, 3 * D_A + 2 * D_B:].reshape(BATCH, SEQ, H_B, DV_B), (0, 2, 1, 3)))
            o = _attention(q, k, v, cache_k, cache_v, e,
                           (lam_q1[e], lam_k1[e], lam_q2[e], lam_k2[e]), subln[e], lam_init)
            x = _even_out(x, u, o, conv_a[e], mod, w_out_even[e].astype(BF16))
        else:
            o_ = li // 2
            u = _norm_in(x, norm1[li], mod, w_in_odd[o_].astype(BF16))
            x = _odd_out(x, u, conv_c[o_], conv_c_b[o_], ln_c_g[o_], ln_c_b[o_],
                         w_pool[o_].astype(BF16), pool_scale[o_], mod, w_out_odd[o_].astype(BF16))
        hrows, gate_b, rank_b, shared = _moe_pre(
            x, norm2[li], mod, w_router[li].T, b_router[li], ws_gate[li].astype(BF16),
            ws_up[li].astype(BF16), ws_down[li].astype(BF16))
        counts = _expert_counts(rank_b)
        lists, wts = _moe_sort(rank_b, gate_b, counts)
        x = _moe_routed(lists, wts, counts, hrows, w_gate, w_up, w_down, li, x, shared, mod,
                        split_output=(li == DEPTH - 1))

    y_prompt = x[0].reshape(BATCH, SEQ, D_MODEL)
    y_sample = x[1].reshape(DEC_BATCH, DEC_SEQ, D_MODEL)
    return (y_prompt, y_sample, jnp.stack(new_k, axis=1), jnp.stack(new_v, axis=1))
```

```python
import functools
import math

import numpy as np
import jax
import jax.numpy as jnp
from jax import lax
from jax.experimental import pallas as pl
from jax.experimental.pallas import tpu as pltpu

D_MODEL = 1024
BATCH = 16
SEQ = 256
DEPTH = 2
DEC_BATCH = 2
DEC_SEQ = 4096
PAST_LEN = 512
GRID_W = 64
H_B = 4
DK_B = 64
DV_B = 2 * DK_B
D_A = D_MODEL // 2
D_B = H_B * DV_B
D_C = D_MODEL // 2
D_D = D_MODEL // 2
CONV_C = 31
POOL_WINDOWS = (2, 4, 8, 16)
D_DG = D_D // len(POOL_WINDOWS)
N_EXPERTS = 64
TOP_K = 8
N_GROUPS = 8
TOPK_GROUPS = 4
D_EXPERT = 256
D_SHARED = 256
ROUTED_SCALE = 2.5
ROPE_BASE = 10000.0
EPS = 1e-6

F32 = jnp.float32
BF16 = jnp.bfloat16

T_P = BATCH * SEQ
T_S = DEC_BATCH * DEC_SEQ
T_ALL = T_P + T_S
TILE = 256
N_TILES = T_ALL // TILE
P_TILES = T_P // TILE
S_TILES = DEC_SEQ // TILE
LANES = 128
VMEM_LIMIT = 56 * 1024 * 1024
VMEM_LIMIT_ROUTED = 60 * 1024 * 1024


def _cparams(*sem, vmem_limit=VMEM_LIMIT):
    return pltpu.CompilerParams(dimension_semantics=sem, vmem_limit_bytes=vmem_limit)


def _mod_row(i, tm):
    npt = T_P // tm
    per = DEC_SEQ // tm
    return jnp.where(i < npt, 0, 1 + (i - npt) // per)


def _seq_flags(i):
    j = (i - P_TILES) % S_TILES
    first = jnp.logical_or(i < P_TILES, j == 0)
    last = jnp.logical_or(i < P_TILES, j == S_TILES - 1)
    return first, last


def _seq_tile(i):
    return jnp.where(i < P_TILES, 0, (i - P_TILES) % S_TILES)


def _split_bf16(a):
    hi = a.astype(BF16)
    lo = (a - hi.astype(F32)).astype(BF16)
    return hi, lo


def _dot(a, b):
    return jnp.dot(a, b, preferred_element_type=F32)


def _dot_nt(a, b):
    return lax.dot_general(a, b, (((1,), (1,)), ((), ())), preferred_element_type=F32)


def _dot3(a, b):
    a_hi, a_lo = _split_bf16(a)
    b_hi, b_lo = _split_bf16(b)
    return _dot(a_hi, b_hi) + _dot(a_lo, b_hi) + _dot(a_hi, b_lo)


def _silu(x):
    return x * jax.nn.sigmoid(x)


MOD_TN = 1536


def _mod_kernel(c_ref, w_ref, b_ref, o_ref):
    o_ref[0] = _dot3(_silu(c_ref[...]), w_ref[0]) + b_ref[0]


def _modulation(cond, w_mod, b_mod):
    n = 6 * D_MODEL
    return pl.pallas_call(
        _mod_kernel,
        grid=(DEPTH, n // MOD_TN),
        in_specs=[
            pl.BlockSpec((8, D_MODEL), lambda l, j: (0, 0)),
            pl.BlockSpec((1, D_MODEL, MOD_TN), lambda l, j: (l, 0, j)),
            pl.BlockSpec((1, 1, MOD_TN), lambda l, j: (l, 0, j)),
        ],
        out_specs=pl.BlockSpec((1, 8, MOD_TN), lambda l, j: (l, 0, j)),
        out_shape=jax.ShapeDtypeStruct((DEPTH, 8, n), F32),
        compiler_params=_cparams("arbitrary", "arbitrary"),
        name="modulation",
    )(cond, w_mod, b_mod.reshape(DEPTH, 1, n))


IN_TM = 1024


def _modulated_norm(x, g, shift, scale):
    ms = jnp.mean(x * x, axis=-1, keepdims=True)
    return (x * lax.rsqrt(ms + EPS) * g) * (1.0 + scale) + shift


def _tok_specs(x, tm, width):
    if isinstance(x, tuple):
        n_p = T_P // tm
        return ([pl.BlockSpec((tm, width), lambda i: (jnp.minimum(i, n_p - 1), 0)),
                 pl.BlockSpec((tm, width), lambda i: (jnp.maximum(i - n_p, 0), 0))], list(x))
    return [pl.BlockSpec((tm, width), lambda i: (i, 0))], [x]


def _tok_load(refs, i, tm):
    if len(refs) == 2:
        return jnp.where(i < T_P // tm, refs[0][...], refs[1][...])
    return refs[0][...]


def _norm_in_kernel(*refs, n_x):
    x_refs, (g_ref, sh_ref, sc_ref, w_ref, o_ref) = refs[:n_x], refs[n_x:]
    i = pl.program_id(0)
    r = _mod_row(i, IN_TM)
    h = _modulated_norm(_tok_load(x_refs, i, IN_TM), g_ref[...], sh_ref[pl.ds(r, 1), :],
                        sc_ref[pl.ds(r, 1), :])
    o_ref[...] = _dot(h.astype(BF16), w_ref[...])


def _norm_in(x, g, mod, w_bf16):
    n = w_bf16.shape[1]
    x_specs, x_ops = _tok_specs(x, IN_TM, D_MODEL)
    return pl.pallas_call(
        functools.partial(_norm_in_kernel, n_x=len(x_ops)),
        grid=(T_ALL // IN_TM,),
        in_specs=x_specs + [
            pl.BlockSpec((1, D_MODEL), lambda i: (0, 0)),
            pl.BlockSpec((8, D_MODEL), lambda i: (0, 0)),
            pl.BlockSpec((8, D_MODEL), lambda i: (0, 1)),
            pl.BlockSpec((D_MODEL, n), lambda i: (0, 0)),
        ],
        out_specs=pl.BlockSpec((IN_TM, n), lambda i: (i, 0)),
        out_shape=jax.ShapeDtypeStruct((T_ALL, n), F32),
        compiler_params=_cparams("arbitrary"),
        name="norm_in_proj",
    )(*x_ops, g.reshape(1, D_MODEL), mod, mod, w_bf16)


QKV_TM = 1024


def _rope_tables():
    half = DK_B // 2
    freqs = ROPE_BASE ** (-np.arange(0, half, 2, dtype=np.float64) / half)
    l = np.arange(DEC_SEQ)
    pos_r = (l // GRID_W).astype(np.float64)
    pos_c = (l % GRID_W).astype(np.float64)
    lane = np.arange(LANES)
    jj = lane % DK_B
    m = jj % half
    f = m % (half // 2)
    pos = np.where((jj < half)[None, :], pos_r[:, None], pos_c[:, None])
    ang = pos * freqs[f][None, :]
    sign = np.where(m < half // 2, -1.0, 1.0)[None, :]
    cos = np.concatenate([np.ones((QKV_TM, LANES)), np.cos(ang)], axis=0)
    sin = np.concatenate([np.zeros((QKV_TM, LANES)), sign * np.sin(ang)], axis=0)
    return cos.astype(np.float32), sin.astype(np.float32)


def _segment_mean_matrix():
    lane = np.arange(LANES)
    same = (lane[:, None] // DK_B) == (lane[None, :] // DK_B)
    return (same.astype(np.float32) / DK_B)


def _qk_prep(x, g, cos, sin, seg):
    x2 = x * x
    hi, lo = _split_bf16(x2)
    ms = _dot(hi, seg) + _dot(lo, seg)
    y = x * lax.rsqrt(ms + EPS) * g
    lane = lax.broadcasted_iota(jnp.int32, y.shape, 1)
    lower = (lane % (DK_B // 2)) < (DK_B // 4)
    partner = jnp.where(lower, pltpu.roll(y, LANES - DK_B // 4, 1), pltpu.roll(y, DK_B // 4, 1))
    return y * cos + partner * sin


def _qkv_prep_kernel(q_ref, k_ref, v_ref, qn_ref, kn_ref, cos_ref, sin_ref, seg_ref,
                     qo_ref, ko_ref, vo_ref, kf_ref):
    cos = cos_ref[...]
    sin = sin_ref[...]
    seg = seg_ref[...]
    scale = math.log2(math.e) / math.sqrt(DK_B)
    for h in range(H_B):
        cols = slice(h * LANES, (h + 1) * LANES)
        qo_ref[h] = (_qk_prep(q_ref[:, cols], qn_ref[...], cos, sin, seg) * scale).astype(BF16)
        k = _qk_prep(k_ref[:, cols], kn_ref[...], cos, sin, seg)
        kf_ref[h] = k
        ko_ref[h] = k.astype(BF16)
        vo_ref[h] = v_ref[:, cols].astype(BF16)


def _qkv_prep(u, qn, kn):
    cos, sin = _rope_tables()
    seg = jnp.asarray(_segment_mean_matrix(), BF16)
    qn2 = jnp.concatenate([qn, qn]).reshape(1, LANES)
    kn2 = jnp.concatenate([kn, kn]).reshape(1, LANES)
    col0 = 3 * D_A // D_B
    p_steps = T_P // QKV_TM
    s_steps = DEC_SEQ // QKV_TM

    def tab_map(i):
        return (jnp.where(i < p_steps, 0, 1 + (i - p_steps) % s_steps), 0)

    out_b = jax.ShapeDtypeStruct((H_B, T_ALL, LANES), BF16)
    out_spec = pl.BlockSpec((H_B, QKV_TM, LANES), lambda i: (0, i, 0))
    return pl.pallas_call(
        _qkv_prep_kernel,
        grid=(T_ALL // QKV_TM,),
        in_specs=[
            pl.BlockSpec((QKV_TM, D_B), lambda i: (i, col0)),
            pl.BlockSpec((QKV_TM, D_B), lambda i: (i, col0 + 1)),
            pl.BlockSpec((QKV_TM, D_B), lambda i: (i, col0 + 2)),
            pl.BlockSpec((1, LANES), lambda i: (0, 0)),
            pl.BlockSpec((1, LANES), lambda i: (0, 0)),
            pl.BlockSpec((QKV_TM, LANES), tab_map),
            pl.BlockSpec((QKV_TM, LANES), tab_map),
            pl.BlockSpec((LANES, LANES), lambda i: (0, 0)),
        ],
        out_specs=[out_spec] * 4,
        out_shape=[out_b, out_b, out_b, jax.ShapeDtypeStruct((H_B, T_ALL, LANES), F32)],
        compiler_params=_cparams("arbitrary"),
        name="qkv_prep",
    )(u, u, u, qn2, kn2, jnp.asarray(cos), jnp.asarray(sin), seg)


def _lambda(lq1_ref, lk1_ref, lq2_ref, lk2_ref, lam_init):
    a = jnp.sum(lq1_ref[...] * lk1_ref[...], axis=-1, keepdims=True)
    b = jnp.sum(lq2_ref[...] * lk2_ref[...], axis=-1, keepdims=True)
    return jnp.exp(a) - jnp.exp(b) + lam_init


ATTN_TQ = 512


def _attn_body(q, keys, vals, lam, subg, lam_init):
    lane = lax.broadcasted_iota(jnp.int32, q.shape, 1)
    zero = jnp.zeros_like(q)
    qa = jnp.where(lane < DK_B, q, zero)
    qb = jnp.where(lane < DK_B, zero, q)
    scores = [[_dot_nt(qq, k) for k in keys] for qq in (qa, qb)]
    outs = []
    for ss in scores:
        m = functools.reduce(jnp.maximum, [jnp.max(s, axis=-1, keepdims=True) for s in ss])
        ps = [jnp.exp2(s - m) for s in ss]
        l = functools.reduce(jnp.add, [jnp.sum(p, axis=-1, keepdims=True) for p in ps])
        pv = functools.reduce(jnp.add, [_dot(p.astype(BF16), v) for p, v in zip(ps, vals)])
        outs.append(pv / l)
    o = outs[0] - lam * outs[1]
    ms = jnp.mean(o * o, axis=-1, keepdims=True)
    return (o * lax.rsqrt(ms + EPS) * subg) * (1.0 - lam_init)


def _attn_prompt_kernel(q_ref, k_ref, v_ref, lq1, lk1, lq2, lk2, sg_ref, o_ref, *, lam_init):
    lam = _lambda(lq1, lk1, lq2, lk2, lam_init)
    for h in range(H_B):
        o_ref[:, h * DV_B:(h + 1) * DV_B] = _attn_body(
            q_ref[h], [k_ref[h]], [v_ref[h]], lam, sg_ref[...], lam_init)


def _attn_latent_kernel(q_ref, k_ref, v_ref, ck_ref, cv_ref, lq1, lk1, lq2, lk2, sg_ref, o_ref,
                        *, lam_init):
    lam = _lambda(lq1, lk1, lq2, lk2, lam_init)
    keys = [ck_ref[0, 0, 0].astype(BF16), k_ref[0]]
    vals = [cv_ref[0, 0, 0].astype(BF16), v_ref[0]]
    o_ref[...] = _attn_body(q_ref[0], keys, vals, lam, sg_ref[...], lam_init)


def _attention(q, k, v, cache_k, cache_v, e, lam_params, subg, lam_init):
    small = [p.reshape(1, DK_B) for p in lam_params] + [subg.reshape(1, DV_B)]
    small_specs2 = [pl.BlockSpec((1, DK_B), lambda b: (0, 0))] * 4 + \
                   [pl.BlockSpec((1, DV_B), lambda b: (0, 0))]
    small_specs3 = [pl.BlockSpec((1, DK_B), lambda s, h, j: (0, 0))] * 4 + \
                   [pl.BlockSpec((1, DV_B), lambda s, h, j: (0, 0))]
    o_prompt = pl.pallas_call(
        functools.partial(_attn_prompt_kernel, lam_init=lam_init),
        grid=(BATCH,),
        in_specs=[pl.BlockSpec((H_B, SEQ, LANES), lambda b: (0, b, 0))] * 3 + small_specs2,
        out_specs=pl.BlockSpec((SEQ, D_B), lambda b: (b, 0)),
        out_shape=jax.ShapeDtypeStruct((T_P, D_B), F32),
        compiler_params=_cparams("arbitrary"),
        name="attn_prompt",
    )(q, k, v, *small)

    q_tiles = DEC_SEQ // ATTN_TQ
    kv_spec = pl.BlockSpec((1, DEC_SEQ, LANES), lambda s, h, j: (h, T_P // DEC_SEQ + s, 0))
    c_spec = pl.BlockSpec((1, 1, 1, PAST_LEN, LANES), lambda s, h, j: (s, e, h, 0, 0))
    o_latent = pl.pallas_call(
        functools.partial(_attn_latent_kernel, lam_init=lam_init),
        grid=(DEC_BATCH, H_B, q_tiles),
        in_specs=[pl.BlockSpec((1, ATTN_TQ, LANES),
                               lambda s, h, j: (h, T_P // ATTN_TQ + s * q_tiles + j, 0)),
                  kv_spec, kv_spec, c_spec, c_spec] + small_specs3,
        out_specs=pl.BlockSpec((ATTN_TQ, LANES), lambda s, h, j: (s * q_tiles + j, h)),
        out_shape=jax.ShapeDtypeStruct((T_S, D_B), F32),
        compiler_params=_cparams("arbitrary", "arbitrary", "arbitrary"),
        name="attn_latent",
    )(q, k, v, cache_k, cache_v, *small)
    return o_prompt, o_latent


HALO8 = 8
HALO16 = 16


def _prev_block(i, rows):
    return jnp.maximum(i * (TILE // rows) - 1, 0)


def _next_block(i, rows):
    return jnp.minimum((i + 1) * (TILE // rows), T_ALL // rows - 1)


def _even_out_kernel(*refs, n_x, n_o):
    x_refs, o_refs = refs[:n_x], refs[n_x:n_x + n_o]
    (bg_ref, cg_ref, hin_ref, cgp_ref, hinp_ref, cgn_ref, hinn_ref, cw_ref, g1_ref, w_ref,
     out_ref) = refs[n_x + n_o:]
    i = pl.program_id(0)
    first, last = _seq_flags(i)
    r = _mod_row(i, TILE)
    z = cg_ref[...] * hin_ref[...]
    zp = jnp.where(first, 0.0, cgp_ref[HALO8 - 1:HALO8, :] * hinp_ref[HALO8 - 1:HALO8, :])
    zn = jnp.where(last, 0.0, cgn_ref[0:1, :] * hinn_ref[0:1, :])
    row = lax.broadcasted_iota(jnp.int32, z.shape, 0)
    z_prev = jnp.where(row == 0, zp, pltpu.roll(z, 1, 0))
    z_next = jnp.where(row == TILE - 1, zn, pltpu.roll(z, TILE - 1, 0))
    cw = cw_ref[...]
    ya = bg_ref[...] * (cw[0:1, :] * z_prev + cw[1:2, :] * z + cw[2:3, :] * z_next)
    o = _tok_load(o_refs, i, TILE)
    y = _dot(ya.astype(BF16), w_ref[0:D_A, :]) + _dot(o.astype(BF16), w_ref[D_A:, :])
    out_ref[...] = _tok_load(x_refs, i, TILE) + g1_ref[pl.ds(r, 1), :] * y


def _even_out(x, u, o, conv_w, mod, w_out_bf16):
    tile_spec = lambda c: pl.BlockSpec((TILE, D_A), lambda i: (i, c))
    prev_spec = lambda c: pl.BlockSpec((HALO8, D_A), lambda i: (_prev_block(i, HALO8), c))
    next_spec = lambda c: pl.BlockSpec((HALO8, D_A), lambda i: (_next_block(i, HALO8), c))
    x_specs, x_ops = _tok_specs(x, TILE, D_MODEL)
    o_specs, o_ops = _tok_specs(o, TILE, D_B)
    return pl.pallas_call(
        functools.partial(_even_out_kernel, n_x=len(x_ops), n_o=len(o_ops)),
        grid=(N_TILES,),
        in_specs=x_specs + o_specs + [
            tile_spec(0), tile_spec(1), tile_spec(2),
            prev_spec(1), prev_spec(2), next_spec(1), next_spec(2),
            pl.BlockSpec((3, D_A), lambda i: (0, 0)),
            pl.BlockSpec((8, D_MODEL), lambda i: (0, 2)),
            pl.BlockSpec((D_MODEL, D_MODEL), lambda i: (0, 0)),
        ],
        out_specs=pl.BlockSpec((TILE, D_MODEL), lambda i: (i, 0)),
        out_shape=jax.ShapeDtypeStruct((T_ALL, D_MODEL), F32),
        compiler_params=_cparams("arbitrary"),
        name="even_mixer_out",
    )(*x_ops, *o_ops, u, u, u, u, u, u, u, conv_w, mod, w_out_bf16)


def _odd_out_kernel(x_ref, a_ref, b_ref, pd_ref, ap_ref, bp_ref, an_ref, bn_ref, pp_ref, pn_ref,
                    cw_ref, cb_ref, lg_ref, lb_ref, wp_ref, ps_ref, g1_ref, w_ref, out_ref,
                    ext_ref, extp_ref, shift_ref):
    i = pl.program_id(0)
    first, last = _seq_flags(i)
    r = _mod_row(i, TILE)
    ext_ref[0:HALO16, :] = jnp.where(first, 0.0, ap_ref[...] * jax.nn.sigmoid(bp_ref[...]))
    ext_ref[HALO16:HALO16 + TILE, :] = a_ref[...] * jax.nn.sigmoid(b_ref[...])
    ext_ref[HALO16 + TILE:, :] = jnp.where(last, 0.0, an_ref[...] * jax.nn.sigmoid(bn_ref[...]))
    base = HALO16 - CONV_C // 2
    parts = []
    for cb in range(D_C // LANES):
        cols = slice(cb * LANES, (cb + 1) * LANES)
        acc = jnp.zeros((TILE, LANES), F32)
        for phase in range(8):
            taps = [j for j in range(CONV_C) if (base + j) % 8 == phase]
            reach = max((base + j) // 8 for j in taps)
            rows = TILE + 8 * reach
            shift_ref[0:rows, :] = ext_ref[pl.ds(phase, rows), cols]
            for j in taps:
                a = (base + j) // 8
                acc = acc + cw_ref[j:j + 1, cols] * shift_ref[8 * a:8 * a + TILE, :]
        parts.append(acc)
    g = jnp.concatenate(parts, axis=-1) + cb_ref[...]
    mu = jnp.mean(g, axis=-1, keepdims=True)
    var = jnp.mean(jnp.square(g - mu), axis=-1, keepdims=True)
    g = _silu(((g - mu) * lax.rsqrt(var + EPS)) * lg_ref[...] + lb_ref[...])
    extp_ref[0:HALO8, :] = jnp.where(first, 0.0, pp_ref[...])
    extp_ref[HALO8:HALO8 + TILE, :] = pd_ref[...]
    extp_ref[HALO8 + TILE:, :] = jnp.where(last, 0.0, pn_ref[...])
    seq_len = jnp.where(i < P_TILES, SEQ, DEC_SEQ)
    pos = _seq_tile(i) * TILE + lax.broadcasted_iota(jnp.int32, (TILE, 1), 0)
    yd = []
    for gi, w in enumerate(POOL_WINDOWS):
        cols = slice(gi * D_DG, (gi + 1) * D_DG)
        s = jnp.zeros((TILE, D_DG), F32)
        for d in range(-(w // 2), w - w // 2):
            s = s + extp_ref[pl.ds(HALO8 + d, TILE), cols]
        lo = jnp.maximum(pos - w // 2, 0)
        hi = jnp.minimum(pos - w // 2 + w, seq_len)
        pooled = s / (hi - lo).astype(F32) - pd_ref[:, cols]
        yd.append(_dot(pooled.astype(BF16), wp_ref[gi]))
    yd = jnp.concatenate(yd, axis=-1) * ps_ref[...]
    y = _dot(g.astype(BF16), w_ref[0:D_C, :]) + _dot(yd.astype(BF16), w_ref[D_C:, :])
    out_ref[...] = x_ref[...] + g1_ref[pl.ds(r, 1), :] * y


def _odd_out(x, u, conv_w, conv_b, ln_g, ln_b, w_pool_bf16, p_scale, mod, w_out_bf16):
    tile_spec = lambda c: pl.BlockSpec((TILE, D_C), lambda i: (i, c))
    prev_spec = lambda rows, c: pl.BlockSpec((rows, D_C), lambda i: (_prev_block(i, rows), c))
    next_spec = lambda rows, c: pl.BlockSpec((rows, D_C), lambda i: (_next_block(i, rows), c))
    vec = lambda: pl.BlockSpec((1, D_C), lambda i: (0, 0))
    return pl.pallas_call(
        _odd_out_kernel,
        grid=(N_TILES,),
        in_specs=[
            pl.BlockSpec((TILE, D_MODEL), lambda i: (i, 0)),
            tile_spec(0), tile_spec(1), tile_spec(2),
            prev_spec(HALO16, 0), prev_spec(HALO16, 1), next_spec(HALO16, 0), next_spec(HALO16, 1),
            prev_spec(HALO8, 2), next_spec(HALO8, 2),
            pl.BlockSpec((CONV_C, D_C), lambda i: (0, 0)),
            vec(), vec(), vec(),
            pl.BlockSpec((len(POOL_WINDOWS), D_DG, D_DG), lambda i: (0, 0, 0)),
            vec(),
            pl.BlockSpec((8, D_MODEL), lambda i: (0, 2)),
            pl.BlockSpec((D_MODEL, D_MODEL), lambda i: (0, 0)),
        ],
        out_specs=pl.BlockSpec((TILE, D_MODEL), lambda i: (i, 0)),
        out_shape=jax.ShapeDtypeStruct((T_ALL, D_MODEL), F32),
        scratch_shapes=[pltpu.VMEM((TILE + 2 * HALO16, D_C), F32),
                        pltpu.VMEM((TILE + 2 * HALO8, D_D), F32),
                        pltpu.VMEM((TILE + 2 * HALO16, LANES), F32)],
        compiler_params=_cparams("arbitrary"),
        name="odd_mixer_out",
    )(x, u, u, u, u, u, u, u, u, u, conv_w, conv_b.reshape(1, D_C), ln_g.reshape(1, D_C),
      ln_b.reshape(1, D_C), w_pool_bf16, p_scale.reshape(1, D_D), mod, w_out_bf16)


GROUP = N_EXPERTS // N_GROUPS
NEG_INF = float("-inf")


def _first_argmax(v, idx, axis):
    m = jnp.max(v, axis=axis, keepdims=True)
    big = jnp.int32(2 ** 30)
    am = jnp.min(jnp.where(v == m, idx, big), axis=axis, keepdims=True)
    return m, am


def _route(scores, biased):
    shape = biased.shape
    member = lax.broadcasted_iota(jnp.int32, shape, 1)
    m1, a1 = _first_argmax(biased, member, 1)
    m2 = jnp.max(jnp.where(member == a1, NEG_INF, biased), axis=1, keepdims=True)
    gscore = m1 + m2
    gidx = lax.broadcasted_iota(jnp.int32, gscore.shape, 0)
    gsel = jnp.zeros(gscore.shape, jnp.bool_)
    for _ in range(TOPK_GROUPS):
        _, am = _first_argmax(gscore, gidx, 0)
        hit = gidx == am
        gsel = jnp.logical_or(gsel, hit)
        gscore = jnp.where(hit, NEG_INF, gscore)
    cand = jnp.where(gsel, biased, NEG_INF)
    eidx = lax.broadcasted_iota(jnp.int32, shape, 0) * GROUP + member
    sel = jnp.zeros(shape, jnp.bool_)
    for _ in range(TOP_K):
        m = jnp.max(jnp.max(cand, axis=1, keepdims=True), axis=0, keepdims=True)
        big = jnp.int32(2 ** 30)
        am = jnp.where(cand == m, eidx, big)
        am = jnp.min(jnp.min(am, axis=1, keepdims=True), axis=0, keepdims=True)
        hit = eidx == am
        sel = jnp.logical_or(sel, hit)
        cand = jnp.where(hit, NEG_INF, cand)
    wsel = jnp.where(sel, scores, 0.0)
    tot = jnp.sum(jnp.sum(wsel, axis=1, keepdims=True), axis=0, keepdims=True)
    return wsel / tot * ROUTED_SCALE, sel


MOE_TC = 4096
N_CHUNKS = T_ALL // MOE_TC
ROW_TILE = 256
N_FLUSH = MOE_TC // TILE
PRE_TM = 1024
ROW_SUB = D_MODEL // LANES


def _moe_pre_kernel(x_ref, g_ref, sh_ref, sc_ref, wr_ref, br_ref, tri_ref, wsg_ref, wsu_ref,
                    wsd_ref, hrow_ref, gate_ref, rank_ref, shared_ref, carry_ref):
    i = pl.program_id(0)
    r = _mod_row(i, PRE_TM)
    h = _modulated_norm(x_ref[...], g_ref[...], sh_ref[pl.ds(r, 1), :], sc_ref[pl.ds(r, 1), :])
    hb = h.astype(BF16)
    for s in range(ROW_SUB):
        hrow_ref[pl.ds(s, PRE_TM, stride=ROW_SUB), :] = h[:, s * LANES:(s + 1) * LANES]
    h_hi, h_lo = hb, (h - hb.astype(F32)).astype(BF16)
    w_hi, w_lo = _split_bf16(wr_ref[...])
    logits = _dot_nt(w_hi, h_hi) + _dot_nt(w_lo, h_hi) + _dot_nt(w_hi, h_lo)
    scores = jax.nn.sigmoid(logits)
    biased = scores + br_ref[:, 0:1]
    shape3 = (N_GROUPS, GROUP, PRE_TM)
    gate_t, sel = _route(scores.reshape(shape3), biased.reshape(shape3))
    gate_t = gate_t.reshape(N_EXPERTS, PRE_TM)
    sel = jnp.where(sel.reshape(N_EXPERTS, PRE_TM), 1.0, 0.0)

    @pl.when(i % (MOE_TC // PRE_TM) == 0)
    def _():
        carry_ref[...] = jnp.zeros_like(carry_ref)

    carry = carry_ref[...]
    local = _dot(sel.astype(BF16), tri_ref[...])
    rank = jnp.where(sel > 0.0, local + jnp.concatenate([carry] * (PRE_TM // LANES), axis=1), -1.0)
    carry_ref[...] = carry + jnp.sum(sel, axis=1, keepdims=True)
    gate_ref[...] = gate_t
    rank_ref[...] = rank
    a = _silu(_dot(hb, wsg_ref[...])) * _dot(hb, wsu_ref[...])
    shared_ref[...] = _dot(a.astype(BF16), wsd_ref[...])


def _moe_pre(x, g, mod, w_router_t, b_router, wsg, wsu, wsd):
    return pl.pallas_call(
        _moe_pre_kernel,
        grid=(T_ALL // PRE_TM,),
        in_specs=[
            pl.BlockSpec((PRE_TM, D_MODEL), lambda i: (i, 0)),
            pl.BlockSpec((1, D_MODEL), lambda i: (0, 0)),
            pl.BlockSpec((8, D_MODEL), lambda i: (0, 3)),
            pl.BlockSpec((8, D_MODEL), lambda i: (0, 4)),
            pl.BlockSpec((N_EXPERTS, D_MODEL), lambda i: (0, 0)),
            pl.BlockSpec((N_EXPERTS, LANES), lambda i: (0, 0)),
            pl.BlockSpec((PRE_TM, PRE_TM), lambda i: (0, 0)),
            pl.BlockSpec((D_MODEL, D_SHARED), lambda i: (0, 0)),
            pl.BlockSpec((D_MODEL, D_SHARED), lambda i: (0, 0)),
            pl.BlockSpec((D_SHARED, D_MODEL), lambda i: (0, 0)),
        ],
        out_specs=[
            pl.BlockSpec((PRE_TM * ROW_SUB, LANES), lambda i: (i, 0)),
            pl.BlockSpec((N_EXPERTS, PRE_TM), lambda i: (0, i)),
            pl.BlockSpec((N_EXPERTS, PRE_TM), lambda i: (0, i)),
            pl.BlockSpec((PRE_TM, D_MODEL), lambda i: (i, 0)),
        ],
        out_shape=[
            jax.ShapeDtypeStruct((T_ALL * ROW_SUB, LANES), F32),
            jax.ShapeDtypeStruct((N_EXPERTS, T_ALL), F32),
            jax.ShapeDtypeStruct((N_EXPERTS, T_ALL), F32),
            jax.ShapeDtypeStruct((T_ALL, D_MODEL), F32),
        ],
        scratch_shapes=[pltpu.VMEM((N_EXPERTS, LANES), F32)],
        compiler_params=_cparams("arbitrary"),
        name="moe_pre",
    )(x, g.reshape(1, D_MODEL), mod, mod, w_router_t,
      jnp.broadcast_to(b_router.reshape(N_EXPERTS, 1), (N_EXPERTS, LANES)),
      jnp.asarray(np.triu(np.ones((PRE_TM, PRE_TM), np.float32), 1), BF16), wsg, wsu, wsd)


LIST_ROWS = MOE_TC // LANES
TILE_ROWS = ROW_TILE // LANES
DUMMY_ROW = MOE_TC * ROW_SUB


def _expert_counts(rank_t):
    sel = (rank_t >= 0.0).astype(jnp.int32).reshape(N_EXPERTS, N_CHUNKS, MOE_TC)
    return jnp.sum(sel, axis=-1).T


def _moe_sort_kernel(rank_ref, gate_ref, cnt_ref, list_ref, w_ref):
    rank = rank_ref[...]
    lane = lax.broadcasted_iota(jnp.int32, rank.shape, 1)
    d = jnp.where(rank >= 0.0, lane - rank.astype(jnp.int32), 0)
    w = gate_ref[...]
    for s in range(MOE_TC.bit_length() - 1):
        k = 1 << s
        d_in = pltpu.roll(d, MOE_TC - k, 1)
        w_in = pltpu.roll(w, MOE_TC - k, 1)
        take = (d_in & k) != 0
        leave = (d & k) != 0
        d = jnp.where(take, d_in, jnp.where(leave, 0, d))
        w = jnp.where(take, w_in, w)
    valid = lane < cnt_ref[0][:, 0:1]
    row = (lane + d) * ROW_SUB
    gsrc = jnp.where(valid, row, 0)
    ssrc = jnp.where(valid, row, DUMMY_ROW)
    w = jnp.where(valid, w, 0.0)
    for b in range(LIST_ROWS):
        cols = slice(b * LANES, (b + 1) * LANES)
        list_ref[pl.ds(b, N_EXPERTS, stride=2 * LIST_ROWS), :] = gsrc[:, cols]
        list_ref[pl.ds(LIST_ROWS + b, N_EXPERTS, stride=2 * LIST_ROWS), :] = ssrc[:, cols]
        w_ref[pl.ds(b, N_EXPERTS, stride=LIST_ROWS), :] = w[:, cols]


def _moe_sort(rank_t, gate_t, counts):
    chunk_spec = pl.BlockSpec((N_EXPERTS, MOE_TC), lambda c: (0, c))
    cnt_b = jnp.broadcast_to(counts[:, :, None], (N_CHUNKS, N_EXPERTS, LANES))
    n_rows = N_CHUNKS * N_EXPERTS * LIST_ROWS
    return pl.pallas_call(
        _moe_sort_kernel,
        grid=(N_CHUNKS,),
        in_specs=[chunk_spec, chunk_spec,
                  pl.BlockSpec((1, N_EXPERTS, LANES), lambda c: (c, 0, 0))],
        out_specs=[pl.BlockSpec((N_EXPERTS * 2 * LIST_ROWS, LANES), lambda c: (c, 0)),
                   pl.BlockSpec((N_EXPERTS * LIST_ROWS, LANES), lambda c: (c, 0))],
        out_shape=[jax.ShapeDtypeStruct((2 * n_rows, LANES), jnp.int32),
                   jax.ShapeDtypeStruct((n_rows, LANES), F32)],
        compiler_params=_cparams("arbitrary"),
        name="moe_sort",
    )(rank_t, gate_t, cnt_b)


SCATTER_BATCH = 16


EXPERTS_PER_STEP = 2
EXPERT_STEPS = N_EXPERTS // EXPERTS_PER_STEP


N_PIECES = 8
PIECE_ROWS = ROW_TILE // N_PIECES


class _ExpertRefs:
    def __init__(self, list0, wt0, list_ref, wt_ref, h_ref, acc_ref, wgb_ref, wub_ref, wdb_ref):
        self.list0, self.wt0 = list0, wt0
        self.list_ref, self.wt_ref, self.h_ref, self.acc_ref = list_ref, wt_ref, h_ref, acc_ref
        self.wgb_ref, self.wub_ref, self.wdb_ref = wgb_ref, wub_ref, wdb_ref


def _gather_rows(ex, t, r0, n, xs_ref):
    base = ex.list0 + t * TILE_ROWS
    for r in range(r0, r0 + n):
        tok = pl.multiple_of(ex.list_ref[base + r // LANES, r % LANES], ROW_SUB)
        xs_ref[r // 8, pl.ds(r % 8, ROW_SUB, stride=8), :] = ex.h_ref[pl.ds(tok, ROW_SUB), :]


def _scatter_rows(ex, t, r0, n, ys_ref):
    base = ex.list0 + LIST_ROWS + t * TILE_ROWS
    for b0 in range(r0, r0 + n, SCATTER_BATCH):
        rows = range(b0, b0 + SCATTER_BATCH)
        dsts = [ex.acc_ref.at[pl.ds(pl.multiple_of(ex.list_ref[base + r // LANES, r % LANES],
                                                   ROW_SUB), ROW_SUB), :] for r in rows]
        news = [dst[...] + ys_ref[r // 8, pl.ds(r % 8, ROW_SUB, stride=8), :]
                for dst, r in zip(dsts, rows)]
        for dst, new in zip(dsts, news):
            dst[...] = new


def _tile_ffn(ex, t, xs_ref, ys_ref, side_work=None):
    kc = D_MODEL // (N_PIECES // 2)
    hg = hu = None
    for p in range(N_PIECES // 2):
        if side_work is not None:
            side_work(p)
        xk = jnp.concatenate(
            [xs_ref[:, s * 8:(s + 1) * 8, :].reshape(ROW_TILE, LANES)
             for s in range(p * kc // LANES, (p + 1) * kc // LANES)], axis=1).astype(BF16)
        dg = _dot(xk, ex.wgb_ref[p * kc:(p + 1) * kc, :])
        du = _dot(xk, ex.wub_ref[p * kc:(p + 1) * kc, :])
        hg = dg if hg is None else hg + dg
        hu = du if hu is None else hu + du
    eye = (lax.broadcasted_iota(jnp.int32, (ROW_TILE, LANES), 0) % LANES
           == lax.broadcasted_iota(jnp.int32, (ROW_TILE, LANES), 1))
    row_blk = lax.broadcasted_iota(jnp.int32, (ROW_TILE, LANES), 0) // LANES
    wrows = functools.reduce(
        lambda a, b: a + b,
        [jnp.where(row_blk == k, ex.wt_ref[pl.ds(ex.wt0 + t * TILE_ROWS + k, 1), :], 0.0)
         for k in range(TILE_ROWS)])
    wcol = jnp.sum(jnp.where(eye, wrows, 0.0), axis=1, keepdims=True)
    a = (_silu(hg) * hu * wcol).astype(BF16)
    for p in range(N_PIECES // 2):
        if side_work is not None:
            side_work(N_PIECES // 2 + p)
        y = _dot(a, ex.wdb_ref[:, p * kc:(p + 1) * kc])
        for q in range(kc // LANES):
            s = p * kc // LANES + q
            ys_ref[:, s * 8:(s + 1) * 8, :] = y[:, q * LANES:(q + 1) * LANES].reshape(
                ROW_TILE // 8, 8, LANES)


def _expert_tiles(ex, t, n, xs_refs, ys_refs):
    _gather_rows(ex, t, 0, ROW_TILE, xs_refs[0])
    for i in range(n):
        def side_work(p, i=i):
            if i + 1 < n:
                _gather_rows(ex, t + i + 1, p * PIECE_ROWS, PIECE_ROWS, xs_refs[(i + 1) % 2])
            if i >= 1:
                _scatter_rows(ex, t + i - 1, p * PIECE_ROWS, PIECE_ROWS, ys_refs[(i - 1) % 2])

        _tile_ffn(ex, t + i, xs_refs[i % 2], ys_refs[i % 2], side_work if n > 1 else None)
    _scatter_rows(ex, t + n - 1, 0, ROW_TILE, ys_refs[(n - 1) % 2])


def _moe_routed_kernel(cnt_ref, list_ref, wt_ref, h_ref, wg_ref, wu_ref, wd_ref,
                       x_ref, sh_ref, g2_ref, *refs):
    out_refs, scratch = refs[:-8], refs[-8:]
    acc_ref, wgb_ref, wub_ref, wdb_ref = scratch[:4]
    xs_refs, ys_refs = scratch[4:6], scratch[6:8]
    c = pl.program_id(0)
    j = pl.program_id(1)

    @pl.when(j == 0)
    def _():
        acc_ref[...] = jnp.zeros_like(acc_ref)

    for k in range(EXPERTS_PER_STEP):
        expert = jnp.minimum(j, EXPERT_STEPS - 1) * EXPERTS_PER_STEP + k
        count = cnt_ref[c * N_EXPERTS + expert]

        @pl.when(jnp.logical_and(j < EXPERT_STEPS, count > 0))
        def _():
            wgb_ref[...] = wg_ref[0, k].astype(BF16)
            wub_ref[...] = wu_ref[0, k].astype(BF16)
            wdb_ref[...] = wd_ref[0, k].astype(BF16)

            ex = _ExpertRefs(k * 2 * LIST_ROWS, k * LIST_ROWS, list_ref, wt_ref, h_ref, acc_ref,
                             wgb_ref, wub_ref, wdb_ref)
            n_tiles = (count + ROW_TILE - 1) // ROW_TILE

            last3 = jnp.logical_and(n_tiles % 2 == 1, n_tiles >= 3)
            n_pairs = (n_tiles - jnp.where(last3, 3, n_tiles % 2)) // 2

            def tile_pair(m, carry):
                _expert_tiles(ex, 2 * m, 2, xs_refs, ys_refs)
                return carry

            lax.fori_loop(0, n_pairs, tile_pair, 0)

            @pl.when(last3)
            def _():
                _expert_tiles(ex, n_tiles - 3, 3, xs_refs, ys_refs)

            @pl.when(n_tiles == 1)
            def _():
                _expert_tiles(ex, 0, 1, xs_refs, ys_refs)

    @pl.when(j >= EXPERT_STEPS)
    def _():
        base = (j - EXPERT_STEPS) * (TILE * ROW_SUB)
        moe = jnp.concatenate(
            [acc_ref[pl.ds(base + s, TILE, stride=ROW_SUB), :] for s in range(ROW_SUB)], axis=1)
        new_x = x_ref[...] + g2_ref[pl.ds(c, 1), :] * (moe + sh_ref[...])
        if len(out_refs) == 1:
            out_refs[0][...] = new_x
        else:
            prompt_ref, latent_ref = out_refs

            @pl.when(c < T_P // MOE_TC)
            def _():
                prompt_ref[...] = new_x

            @pl.when(c >= T_P // MOE_TC)
            def _():
                latent_ref[...] = new_x


def _moe_routed(lists, wts, counts, hrows, w_gate, w_up, w_down, li, x, shared, mod,
                split_output):
    group = lambda j: jnp.minimum(j, EXPERT_STEPS - 1)
    list_map = lambda c, j, nt: (c * EXPERT_STEPS + group(j), 0)
    w_map = lambda c, j, nt: (li, group(j), 0, 0)
    w_in_spec = pl.BlockSpec((1, EXPERTS_PER_STEP, D_MODEL, D_EXPERT), w_map)
    w_out_spec = pl.BlockSpec((1, EXPERTS_PER_STEP, D_EXPERT, D_MODEL), w_map)
    out_blk = lambda c, j: c * N_FLUSH + jnp.maximum(j - EXPERT_STEPS, 0)
    tok_spec = pl.BlockSpec((TILE, D_MODEL), lambda c, j, nt: (out_blk(c, j), 0))
    if split_output:
        out_specs = [
            pl.BlockSpec((TILE, D_MODEL), lambda c, j, nt: (jnp.minimum(out_blk(c, j), P_TILES - 1), 0)),
            pl.BlockSpec((TILE, D_MODEL), lambda c, j, nt: (jnp.maximum(out_blk(c, j) - P_TILES, 0), 0)),
        ]
        out_shape = [jax.ShapeDtypeStruct((T_P, D_MODEL), F32),
                     jax.ShapeDtypeStruct((T_S, D_MODEL), F32)]
    else:
        out_specs = tok_spec
        out_shape = jax.ShapeDtypeStruct((T_ALL, D_MODEL), F32)
    stage = pltpu.VMEM((ROW_TILE // 8, 8 * ROW_SUB, LANES), F32)
    grid_spec = pltpu.PrefetchScalarGridSpec(
        num_scalar_prefetch=1,
        grid=(N_CHUNKS, EXPERT_STEPS + N_FLUSH),
        in_specs=[
            pl.BlockSpec((EXPERTS_PER_STEP * 2 * LIST_ROWS, LANES), list_map,
                         memory_space=pltpu.SMEM),
            pl.BlockSpec((EXPERTS_PER_STEP * LIST_ROWS, LANES), list_map),
            pl.BlockSpec((MOE_TC * ROW_SUB, LANES), lambda c, j, nt: (c, 0),
                         pipeline_mode=pl.Buffered(1)),
            w_in_spec, w_in_spec, w_out_spec,
            tok_spec, tok_spec,
            pl.BlockSpec((8, D_MODEL), lambda c, j, nt: (0, 5)),
        ],
        out_specs=out_specs,
        scratch_shapes=[
            pltpu.VMEM(((MOE_TC + 1) * ROW_SUB, LANES), F32),
            pltpu.VMEM((D_MODEL, D_EXPERT), BF16),
            pltpu.VMEM((D_MODEL, D_EXPERT), BF16),
            pltpu.VMEM((D_EXPERT, D_MODEL), BF16),
            stage, stage, stage, stage,
        ],
    )
    return pl.pallas_call(
        _moe_routed_kernel,
        grid_spec=grid_spec,
        out_shape=out_shape,
        compiler_params=_cparams("arbitrary", "arbitrary", vmem_limit=VMEM_LIMIT_ROUTED),
        name="moe_routed",
    )(counts.reshape(-1), lists, wts, hrows, w_gate, w_up, w_down, x, shared, mod)


def kernel(x_prompt, x_sample, cache_k, cache_v, c, c_ctx, w_mod, b_mod, norm1, norm2, w_in_even, conv_a, q_norm, k_norm, lam_q1, lam_k1, lam_q2, lam_k2, subln, w_out_even, w_in_odd, conv_c, conv_c_b, ln_c_g, ln_c_b, w_pool, pool_scale, w_out_odd, w_router, b_router, w_gate, w_up, w_down, ws_gate, ws_up, ws_down):
    x = (x_prompt.reshape(T_P, D_MODEL), x_sample.reshape(T_S, D_MODEL))
    cond = jnp.concatenate([c_ctx[None, :], c, jnp.zeros((8 - 1 - DEC_BATCH, D_MODEL), F32)], axis=0)
    mod_all = _modulation(cond, w_mod, b_mod)

    new_k, new_v = [], []
    for li in range(DEPTH):
        mod = mod_all[li]
        if li % 2 == 0:
            e = li // 2
            lam_init = 0.8 - 0.6 * math.exp(-0.3 * li)
            u = _norm_in(x, norm1[li], mod, w_in_even[e].astype(BF16))
            q, k, v, k_f32 = _qkv_prep(u, q_norm[e], k_norm[e])
            new_k.append(jnp.transpose(k_f32[:, :T_P].reshape(H_B, BATCH, SEQ, LANES), (1, 0, 2, 3)))
            new_v.append(jnp.transpose(
                u[:T_P, 3 * D_A + 2 * D_B:].reshape(BATCH, SEQ, H_B, DV_B), (0, 2, 1, 3)))
            o = _attention(q, k, v, cache_k, cache_v, e,
                           (lam_q1[e], lam_k1[e], lam_q2[e], lam_k2[e]), subln[e], lam_init)
            x = _even_out(x, u, o, conv_a[e], mod, w_out_even[e].astype(BF16))
        else:
            o_ = li // 2
            u = _norm_in(x, norm1[li], mod, w_in_odd[o_].astype(BF16))
            x = _odd_out(x, u, conv_c[o_], conv_c_b[o_], ln_c_g[o_], ln_c_b[o_],
                         w_pool[o_].astype(BF16), pool_scale[o_], mod, w_out_odd[o_].astype(BF16))
        hrows, gate_b, rank_b, shared = _moe_pre(
            x, norm2[li], mod, w_router[li].T, b_router[li], ws_gate[li].astype(BF16),
            ws_up[li].astype(BF16), ws_down[li].astype(BF16))
        counts = _expert_counts(rank_b)
        lists, wts = _moe_sort(rank_b, gate_b, counts)
        x = _moe_routed(lists, wts, counts, hrows, w_gate, w_up, w_down, li, x, shared, mod,
                        split_output=(li == DEPTH - 1))

    y_prompt = x[0].reshape(BATCH, SEQ, D_MODEL)
    y_sample = x[1].reshape(DEC_BATCH, DEC_SEQ, D_MODEL)
    return (y_prompt, y_sample, jnp.stack(new_k, axis=1), jnp.stack(new_v, axis=1))
```

```python
import functools
import math

import numpy as np
import jax
import jax.numpy as jnp
from jax import lax
from jax.experimental import pallas as pl
from jax.experimental.pallas import tpu as pltpu

D_MODEL = 1024
BATCH = 16
SEQ = 256
DEPTH = 2
DEC_BATCH = 2
DEC_SEQ = 4096
PAST_LEN = 512
GRID_W = 64
H_B = 4
DK_B = 64
DV_B = 2 * DK_B
D_A = D_MODEL // 2
D_B = H_B * DV_B
D_C = D_MODEL // 2
D_D = D_MODEL // 2
CONV_C = 31
POOL_WINDOWS = (2, 4, 8, 16)
D_DG = D_D // len(POOL_WINDOWS)
N_EXPERTS = 64
TOP_K = 8
N_GROUPS = 8
TOPK_GROUPS = 4
D_EXPERT = 256
D_SHARED = 256
ROUTED_SCALE = 2.5
ROPE_BASE = 10000.0
EPS = 1e-6

F32 = jnp.float32
BF16 = jnp.bfloat16

T_P = BATCH * SEQ
T_S = DEC_BATCH * DEC_SEQ
T_ALL = T_P + T_S
TILE = 256
N_TILES = T_ALL // TILE
P_TILES = T_P // TILE
S_TILES = DEC_SEQ // TILE
LANES = 128
VMEM_LIMIT = 56 * 1024 * 1024
VMEM_LIMIT_ROUTED = 60 * 1024 * 1024


def _cparams(*sem, vmem_limit=VMEM_LIMIT):
    return pltpu.CompilerParams(dimension_semantics=sem, vmem_limit_bytes=vmem_limit)


def _mod_row(i, tm):
    npt = T_P // tm
    per = DEC_SEQ // tm
    return jnp.where(i < npt, 0, 1 + (i - npt) // per)


def _seq_flags(i):
    j = (i - P_TILES) % S_TILES
    first = jnp.logical_or(i < P_TILES, j == 0)
    last = jnp.logical_or(i < P_TILES, j == S_TILES - 1)
    return first, last


def _seq_tile(i):
    return jnp.where(i < P_TILES, 0, (i - P_TILES) % S_TILES)


def _split_bf16(a):
    hi = a.astype(BF16)
    lo = (a - hi.astype(F32)).astype(BF16)
    return hi, lo


def _dot(a, b):
    return jnp.dot(a, b, preferred_element_type=F32)


def _dot_nt(a, b):
    return lax.dot_general(a, b, (((1,), (1,)), ((), ())), preferred_element_type=F32)


def _dot3(a, b):
    a_hi, a_lo = _split_bf16(a)
    b_hi, b_lo = _split_bf16(b)
    return _dot(a_hi, b_hi) + _dot(a_lo, b_hi) + _dot(a_hi, b_lo)


def _silu(x):
    return x * jax.nn.sigmoid(x)


MOD_TN = 1536


def _mod_kernel(c_ref, w_ref, b_ref, o_ref):
    o_ref[0] = _dot3(_silu(c_ref[...]), w_ref[0]) + b_ref[0]


def _modulation(cond, w_mod, b_mod):
    n = 6 * D_MODEL
    return pl.pallas_call(
        _mod_kernel,
        grid=(DEPTH, n // MOD_TN),
        in_specs=[
            pl.BlockSpec((8, D_MODEL), lambda l, j: (0, 0)),
            pl.BlockSpec((1, D_MODEL, MOD_TN), lambda l, j: (l, 0, j)),
            pl.BlockSpec((1, 1, MOD_TN), lambda l, j: (l, 0, j)),
        ],
        out_specs=pl.BlockSpec((1, 8, MOD_TN), lambda l, j: (l, 0, j)),
        out_shape=jax.ShapeDtypeStruct((DEPTH, 8, n), F32),
        compiler_params=_cparams("arbitrary", "arbitrary"),
        name="modulation",
    )(cond, w_mod, b_mod.reshape(DEPTH, 1, n))


IN_TM = 1024


def _modulated_norm(x, g, shift, scale):
    ms = jnp.mean(x * x, axis=-1, keepdims=True)
    return (x * lax.rsqrt(ms + EPS) * g) * (1.0 + scale) + shift


def _tok_specs(x, tm, width):
    if isinstance(x, tuple):
        n_p = T_P // tm
        return ([pl.BlockSpec((tm, width), lambda i: (jnp.minimum(i, n_p - 1), 0)),
                 pl.BlockSpec((tm, width), lambda i: (jnp.maximum(i - n_p, 0), 0))], list(x))
    return [pl.BlockSpec((tm, width), lambda i: (i, 0))], [x]


def _tok_load(refs, i, tm):
    if len(refs) == 2:
        return jnp.where(i < T_P // tm, refs[0][...], refs[1][...])
    return refs[0][...]


def _norm_in_kernel(*refs, n_x):
    x_refs, (g_ref, sh_ref, sc_ref, w_ref, o_ref) = refs[:n_x], refs[n_x:]
    i = pl.program_id(0)
    r = _mod_row(i, IN_TM)
    h = _modulated_norm(_tok_load(x_refs, i, IN_TM), g_ref[...], sh_ref[pl.ds(r, 1), :],
                        sc_ref[pl.ds(r, 1), :])
    o_ref[...] = _dot(h.astype(BF16), w_ref[...])


def _norm_in(x, g, mod, w_bf16):
    n = w_bf16.shape[1]
    x_specs, x_ops = _tok_specs(x, IN_TM, D_MODEL)
    return pl.pallas_call(
        functools.partial(_norm_in_kernel, n_x=len(x_ops)),
        grid=(T_ALL // IN_TM,),
        in_specs=x_specs + [
            pl.BlockSpec((1, D_MODEL), lambda i: (0, 0)),
            pl.BlockSpec((8, D_MODEL), lambda i: (0, 0)),
            pl.BlockSpec((8, D_MODEL), lambda i: (0, 1)),
            pl.BlockSpec((D_MODEL, n), lambda i: (0, 0)),
        ],
        out_specs=pl.BlockSpec((IN_TM, n), lambda i: (i, 0)),
        out_shape=jax.ShapeDtypeStruct((T_ALL, n), F32),
        compiler_params=_cparams("arbitrary"),
        name="norm_in_proj",
    )(*x_ops, g.reshape(1, D_MODEL), mod, mod, w_bf16)


QKV_TM = 1024


def _rope_tables():
    half = DK_B // 2
    freqs = ROPE_BASE ** (-np.arange(0, half, 2, dtype=np.float64) / half)
    l = np.arange(DEC_SEQ)
    pos_r = (l // GRID_W).astype(np.float64)
    pos_c = (l % GRID_W).astype(np.float64)
    lane = np.arange(LANES)
    jj = lane % DK_B
    m = jj % half
    f = m % (half // 2)
    pos = np.where((jj < half)[None, :], pos_r[:, None], pos_c[:, None])
    ang = pos * freqs[f][None, :]
    sign = np.where(m < half // 2, -1.0, 1.0)[None, :]
    cos = np.concatenate([np.ones((QKV_TM, LANES)), np.cos(ang)], axis=0)
    sin = np.concatenate([np.zeros((QKV_TM, LANES)), sign * np.sin(ang)], axis=0)
    return cos.astype(np.float32), sin.astype(np.float32)


def _segment_mean_matrix():
    lane = np.arange(LANES)
    same = (lane[:, None] // DK_B) == (lane[None, :] // DK_B)
    return (same.astype(np.float32) / DK_B)


def _qk_prep(x, g, cos, sin, seg):
    x2 = x * x
    hi, lo = _split_bf16(x2)
    ms = _dot(hi, seg) + _dot(lo, seg)
    y = x * lax.rsqrt(ms + EPS) * g
    lane = lax.broadcasted_iota(jnp.int32, y.shape, 1)
    lower = (lane % (DK_B // 2)) < (DK_B // 4)
    partner = jnp.where(lower, pltpu.roll(y, LANES - DK_B // 4, 1), pltpu.roll(y, DK_B // 4, 1))
    return y * cos + partner * sin


def _qkv_prep_kernel(q_ref, k_ref, v_ref, qn_ref, kn_ref, cos_ref, sin_ref, seg_ref,
                     qo_ref, ko_ref, vo_ref, kc_ref, vc_ref):
    cos = cos_ref[...]
    sin = sin_ref[...]
    seg = seg_ref[...]
    scale = math.log2(math.e) / math.sqrt(DK_B)
    is_prompt = pl.program_id(0) < T_P // QKV_TM
    for h in range(H_B):
        cols = slice(h * LANES, (h + 1) * LANES)
        qo_ref[h] = (_qk_prep(q_ref[:, cols], qn_ref[...], cos, sin, seg) * scale).astype(BF16)
        k = _qk_prep(k_ref[:, cols], kn_ref[...], cos, sin, seg)
        v = v_ref[:, cols]
        ko_ref[h] = k.astype(BF16)
        vo_ref[h] = v.astype(BF16)

        @pl.when(is_prompt)
        def _():
            for b in range(QKV_TM // SEQ):
                kc_ref[b, h] = k[b * SEQ:(b + 1) * SEQ]
                vc_ref[b, h] = v[b * SEQ:(b + 1) * SEQ]


def _qkv_prep(u, qn, kn):
    cos, sin = _rope_tables()
    seg = jnp.asarray(_segment_mean_matrix(), BF16)
    qn2 = jnp.concatenate([qn, qn]).reshape(1, LANES)
    kn2 = jnp.concatenate([kn, kn]).reshape(1, LANES)
    col0 = 3 * D_A // D_B
    p_steps = T_P // QKV_TM
    s_steps = DEC_SEQ // QKV_TM

    def tab_map(i):
        return (jnp.where(i < p_steps, 0, 1 + (i - p_steps) % s_steps), 0)

    out_b = jax.ShapeDtypeStruct((H_B, T_ALL, LANES), BF16)
    out_spec = pl.BlockSpec((H_B, QKV_TM, LANES), lambda i: (0, i, 0))
    cache = jax.ShapeDtypeStruct((BATCH, H_B, SEQ, LANES), F32)
    cache_spec = pl.BlockSpec((QKV_TM // SEQ, H_B, SEQ, LANES),
                              lambda i: (jnp.minimum(i, p_steps - 1), 0, 0, 0))
    return pl.pallas_call(
        _qkv_prep_kernel,
        grid=(T_ALL // QKV_TM,),
        in_specs=[
            pl.BlockSpec((QKV_TM, D_B), lambda i: (i, col0)),
            pl.BlockSpec((QKV_TM, D_B), lambda i: (i, col0 + 1)),
            pl.BlockSpec((QKV_TM, D_B), lambda i: (i, col0 + 2)),
            pl.BlockSpec((1, LANES), lambda i: (0, 0)),
            pl.BlockSpec((1, LANES), lambda i: (0, 0)),
            pl.BlockSpec((QKV_TM, LANES), tab_map),
            pl.BlockSpec((QKV_TM, LANES), tab_map),
            pl.BlockSpec((LANES, LANES), lambda i: (0, 0)),
        ],
        out_specs=[out_spec] * 3 + [cache_spec] * 2,
        out_shape=[out_b, out_b, out_b, cache, cache],
        compiler_params=_cparams("arbitrary"),
        name="qkv_prep",
    )(u, u, u, qn2, kn2, jnp.asarray(cos), jnp.asarray(sin), seg)


def _lambda(lq1_ref, lk1_ref, lq2_ref, lk2_ref, lam_init):
    a = jnp.sum(lq1_ref[...] * lk1_ref[...], axis=-1, keepdims=True)
    b = jnp.sum(lq2_ref[...] * lk2_ref[...], axis=-1, keepdims=True)
    return jnp.exp(a) - jnp.exp(b) + lam_init


ATTN_TQ = 512


def _attn_body(q, keys, vals, lam, subg, lam_init):
    lane = lax.broadcasted_iota(jnp.int32, q.shape, 1)
    zero = jnp.zeros_like(q)
    qa = jnp.where(lane < DK_B, q, zero)
    qb = jnp.where(lane < DK_B, zero, q)
    scores = [[_dot_nt(qq, k) for k in keys] for qq in (qa, qb)]
    outs = []
    for ss in scores:
        m = functools.reduce(jnp.maximum, [jnp.max(s, axis=-1, keepdims=True) for s in ss])
        ps = [jnp.exp2(s - m) for s in ss]
        l = functools.reduce(jnp.add, [jnp.sum(p, axis=-1, keepdims=True) for p in ps])
        pv = functools.reduce(jnp.add, [_dot(p.astype(BF16), v) for p, v in zip(ps, vals)])
        outs.append(pv / l)
    o = outs[0] - lam * outs[1]
    ms = jnp.mean(o * o, axis=-1, keepdims=True)
    return (o * lax.rsqrt(ms + EPS) * subg) * (1.0 - lam_init)


def _attn_prompt_kernel(q_ref, k_ref, v_ref, lq1, lk1, lq2, lk2, sg_ref, o_ref, *, lam_init):
    lam = _lambda(lq1, lk1, lq2, lk2, lam_init)
    for h in range(H_B):
        o_ref[:, h * DV_B:(h + 1) * DV_B] = _attn_body(
            q_ref[h], [k_ref[h]], [v_ref[h]], lam, sg_ref[...], lam_init)


def _attn_latent_kernel(q_ref, k_ref, v_ref, ck_ref, cv_ref, lq1, lk1, lq2, lk2, sg_ref, o_ref,
                        *, lam_init):
    lam = _lambda(lq1, lk1, lq2, lk2, lam_init)
    keys = [ck_ref[0, 0, 0].astype(BF16), k_ref[0]]
    vals = [cv_ref[0, 0, 0].astype(BF16), v_ref[0]]
    o_ref[...] = _attn_body(q_ref[0], keys, vals, lam, sg_ref[...], lam_init)


def _attention(q, k, v, cache_k, cache_v, e, lam_params, subg, lam_init):
    small = [p.reshape(1, DK_B) for p in lam_params] + [subg.reshape(1, DV_B)]
    small_specs2 = [pl.BlockSpec((1, DK_B), lambda b: (0, 0))] * 4 + \
                   [pl.BlockSpec((1, DV_B), lambda b: (0, 0))]
    small_specs3 = [pl.BlockSpec((1, DK_B), lambda s, h, j: (0, 0))] * 4 + \
                   [pl.BlockSpec((1, DV_B), lambda s, h, j: (0, 0))]
    o_prompt = pl.pallas_call(
        functools.partial(_attn_prompt_kernel, lam_init=lam_init),
        grid=(BATCH,),
        in_specs=[pl.BlockSpec((H_B, SEQ, LANES), lambda b: (0, b, 0))] * 3 + small_specs2,
        out_specs=pl.BlockSpec((SEQ, D_B), lambda b: (b, 0)),
        out_shape=jax.ShapeDtypeStruct((T_P, D_B), F32),
        compiler_params=_cparams("arbitrary"),
        name="attn_prompt",
    )(q, k, v, *small)

    q_tiles = DEC_SEQ // ATTN_TQ
    kv_spec = pl.BlockSpec((1, DEC_SEQ, LANES), lambda s, h, j: (h, T_P // DEC_SEQ + s, 0))
    c_spec = pl.BlockSpec((1, 1, 1, PAST_LEN, LANES), lambda s, h, j: (s, e, h, 0, 0))
    o_latent = pl.pallas_call(
        functools.partial(_attn_latent_kernel, lam_init=lam_init),
        grid=(DEC_BATCH, H_B, q_tiles),
        in_specs=[pl.BlockSpec((1, ATTN_TQ, LANES),
                               lambda s, h, j: (h, T_P // ATTN_TQ + s * q_tiles + j, 0)),
                  kv_spec, kv_spec, c_spec, c_spec] + small_specs3,
        out_specs=pl.BlockSpec((ATTN_TQ, LANES), lambda s, h, j: (s * q_tiles + j, h)),
        out_shape=jax.ShapeDtypeStruct((T_S, D_B), F32),
        compiler_params=_cparams("arbitrary", "arbitrary", "arbitrary"),
        name="attn_latent",
    )(q, k, v, cache_k, cache_v, *small)
    return o_prompt, o_latent


HALO8 = 8
HALO16 = 16


def _prev_block(i, rows):
    return jnp.maximum(i * (TILE // rows) - 1, 0)


def _next_block(i, rows):
    return jnp.minimum((i + 1) * (TILE // rows), T_ALL // rows - 1)


def _even_out_kernel(*refs, n_x, n_o):
    x_refs, o_refs = refs[:n_x], refs[n_x:n_x + n_o]
    (bg_ref, cg_ref, hin_ref, cgp_ref, hinp_ref, cgn_ref, hinn_ref, cw_ref, g1_ref, w_ref,
     out_ref) = refs[n_x + n_o:]
    i = pl.program_id(0)
    first, last = _seq_flags(i)
    r = _mod_row(i, TILE)
    z = cg_ref[...] * hin_ref[...]
    zp = jnp.where(first, 0.0, cgp_ref[HALO8 - 1:HALO8, :] * hinp_ref[HALO8 - 1:HALO8, :])
    zn = jnp.where(last, 0.0, cgn_ref[0:1, :] * hinn_ref[0:1, :])
    row = lax.broadcasted_iota(jnp.int32, z.shape, 0)
    z_prev = jnp.where(row == 0, zp, pltpu.roll(z, 1, 0))
    z_next = jnp.where(row == TILE - 1, zn, pltpu.roll(z, TILE - 1, 0))
    cw = cw_ref[...]
    ya = bg_ref[...] * (cw[0:1, :] * z_prev + cw[1:2, :] * z + cw[2:3, :] * z_next)
    o = _tok_load(o_refs, i, TILE)
    y = _dot(ya.astype(BF16), w_ref[0:D_A, :]) + _dot(o.astype(BF16), w_ref[D_A:, :])
    out_ref[...] = _tok_load(x_refs, i, TILE) + g1_ref[pl.ds(r, 1), :] * y


def _even_out(x, u, o, conv_w, mod, w_out_bf16):
    tile_spec = lambda c: pl.BlockSpec((TILE, D_A), lambda i: (i, c))
    prev_spec = lambda c: pl.BlockSpec((HALO8, D_A), lambda i: (_prev_block(i, HALO8), c))
    next_spec = lambda c: pl.BlockSpec((HALO8, D_A), lambda i: (_next_block(i, HALO8), c))
    x_specs, x_ops = _tok_specs(x, TILE, D_MODEL)
    o_specs, o_ops = _tok_specs(o, TILE, D_B)
    return pl.pallas_call(
        functools.partial(_even_out_kernel, n_x=len(x_ops), n_o=len(o_ops)),
        grid=(N_TILES,),
        in_specs=x_specs + o_specs + [
            tile_spec(0), tile_spec(1), tile_spec(2),
            prev_spec(1), prev_spec(2), next_spec(1), next_spec(2),
            pl.BlockSpec((3, D_A), lambda i: (0, 0)),
            pl.BlockSpec((8, D_MODEL), lambda i: (0, 2)),
            pl.BlockSpec((D_MODEL, D_MODEL), lambda i: (0, 0)),
        ],
        out_specs=pl.BlockSpec((TILE, D_MODEL), lambda i: (i, 0)),
        out_shape=jax.ShapeDtypeStruct((T_ALL, D_MODEL), F32),
        compiler_params=_cparams("arbitrary"),
        name="even_mixer_out",
    )(*x_ops, *o_ops, u, u, u, u, u, u, u, conv_w, mod, w_out_bf16)


def _odd_out_kernel(x_ref, a_ref, b_ref, pd_ref, ap_ref, bp_ref, an_ref, bn_ref, pp_ref, pn_ref,
                    cw_ref, cb_ref, lg_ref, lb_ref, wp_ref, ps_ref, g1_ref, w_ref, out_ref,
                    ext_ref, extp_ref, shift_ref):
    i = pl.program_id(0)
    first, last = _seq_flags(i)
    r = _mod_row(i, TILE)
    ext_ref[0:HALO16, :] = jnp.where(first, 0.0, ap_ref[...] * jax.nn.sigmoid(bp_ref[...]))
    ext_ref[HALO16:HALO16 + TILE, :] = a_ref[...] * jax.nn.sigmoid(b_ref[...])
    ext_ref[HALO16 + TILE:, :] = jnp.where(last, 0.0, an_ref[...] * jax.nn.sigmoid(bn_ref[...]))
    base = HALO16 - CONV_C // 2
    parts = []
    for cb in range(D_C // LANES):
        cols = slice(cb * LANES, (cb + 1) * LANES)
        acc = jnp.zeros((TILE, LANES), F32)
        for phase in range(8):
            taps = [j for j in range(CONV_C) if (base + j) % 8 == phase]
            reach = max((base + j) // 8 for j in taps)
            rows = TILE + 8 * reach
            shift_ref[0:rows, :] = ext_ref[pl.ds(phase, rows), cols]
            for j in taps:
                a = (base + j) // 8
                acc = acc + cw_ref[j:j + 1, cols] * shift_ref[8 * a:8 * a + TILE, :]
        parts.append(acc)
    g = jnp.concatenate(parts, axis=-1) + cb_ref[...]
    mu = jnp.mean(g, axis=-1, keepdims=True)
    var = jnp.mean(jnp.square(g - mu), axis=-1, keepdims=True)
    g = _silu(((g - mu) * lax.rsqrt(var + EPS)) * lg_ref[...] + lb_ref[...])
    extp_ref[0:HALO8, :] = jnp.where(first, 0.0, pp_ref[...])
    extp_ref[HALO8:HALO8 + TILE, :] = pd_ref[...]
    extp_ref[HALO8 + TILE:, :] = jnp.where(last, 0.0, pn_ref[...])
    seq_len = jnp.where(i < P_TILES, SEQ, DEC_SEQ)
    pos = _seq_tile(i) * TILE + lax.broadcasted_iota(jnp.int32, (TILE, 1), 0)
    yd = []
    for gi, w in enumerate(POOL_WINDOWS):
        cols = slice(gi * D_DG, (gi + 1) * D_DG)
        s = jnp.zeros((TILE, D_DG), F32)
        for d in range(-(w // 2), w - w // 2):
            s = s + extp_ref[pl.ds(HALO8 + d, TILE), cols]
        lo = jnp.maximum(pos - w // 2, 0)
        hi = jnp.minimum(pos - w // 2 + w, seq_len)
        pooled = s / (hi - lo).astype(F32) - pd_ref[:, cols]
        yd.append(_dot(pooled.astype(BF16), wp_ref[gi]))
    yd = jnp.concatenate(yd, axis=-1) * ps_ref[...]
    y = _dot(g.astype(BF16), w_ref[0:D_C, :]) + _dot(yd.astype(BF16), w_ref[D_C:, :])
    out_ref[...] = x_ref[...] + g1_ref[pl.ds(r, 1), :] * y


def _odd_out(x, u, conv_w, conv_b, ln_g, ln_b, w_pool_bf16, p_scale, mod, w_out_bf16):
    tile_spec = lambda c: pl.BlockSpec((TILE, D_C), lambda i: (i, c))
    prev_spec = lambda rows, c: pl.BlockSpec((rows, D_C), lambda i: (_prev_block(i, rows), c))
    next_spec = lambda rows, c: pl.BlockSpec((rows, D_C), lambda i: (_next_block(i, rows), c))
    vec = lambda: pl.BlockSpec((1, D_C), lambda i: (0, 0))
    return pl.pallas_call(
        _odd_out_kernel,
        grid=(N_TILES,),
        in_specs=[
            pl.BlockSpec((TILE, D_MODEL), lambda i: (i, 0)),
            tile_spec(0), tile_spec(1), tile_spec(2),
            prev_spec(HALO16, 0), prev_spec(HALO16, 1), next_spec(HALO16, 0), next_spec(HALO16, 1),
            prev_spec(HALO8, 2), next_spec(HALO8, 2),
            pl.BlockSpec((CONV_C, D_C), lambda i: (0, 0)),
            vec(), vec(), vec(),
            pl.BlockSpec((len(POOL_WINDOWS), D_DG, D_DG), lambda i: (0, 0, 0)),
            vec(),
            pl.BlockSpec((8, D_MODEL), lambda i: (0, 2)),
            pl.BlockSpec((D_MODEL, D_MODEL), lambda i: (0, 0)),
        ],
        out_specs=pl.BlockSpec((TILE, D_MODEL), lambda i: (i, 0)),
        out_shape=jax.ShapeDtypeStruct((T_ALL, D_MODEL), F32),
        scratch_shapes=[pltpu.VMEM((TILE + 2 * HALO16, D_C), F32),
                        pltpu.VMEM((TILE + 2 * HALO8, D_D), F32),
                        pltpu.VMEM((TILE + 2 * HALO16, LANES), F32)],
        compiler_params=_cparams("arbitrary"),
        name="odd_mixer_out",
    )(x, u, u, u, u, u, u, u, u, u, conv_w, conv_b.reshape(1, D_C), ln_g.reshape(1, D_C),
      ln_b.reshape(1, D_C), w_pool_bf16, p_scale.reshape(1, D_D), mod, w_out_bf16)


GROUP = N_EXPERTS // N_GROUPS
NEG_INF = float("-inf")


def _first_argmax(v, idx, axis):
    m = jnp.max(v, axis=axis, keepdims=True)
    big = jnp.int32(2 ** 30)
    am = jnp.min(jnp.where(v == m, idx, big), axis=axis, keepdims=True)
    return m, am


def _route(scores, biased):
    shape = biased.shape
    member = lax.broadcasted_iota(jnp.int32, shape, 1)
    m1, a1 = _first_argmax(biased, member, 1)
    m2 = jnp.max(jnp.where(member == a1, NEG_INF, biased), axis=1, keepdims=True)
    gscore = m1 + m2
    gidx = lax.broadcasted_iota(jnp.int32, gscore.shape, 0)
    gsel = jnp.zeros(gscore.shape, jnp.bool_)
    for _ in range(TOPK_GROUPS):
        _, am = _first_argmax(gscore, gidx, 0)
        hit = gidx == am
        gsel = jnp.logical_or(gsel, hit)
        gscore = jnp.where(hit, NEG_INF, gscore)
    cand = jnp.where(gsel, biased, NEG_INF)
    eidx = lax.broadcasted_iota(jnp.int32, shape, 0) * GROUP + member
    sel = jnp.zeros(shape, jnp.bool_)
    for _ in range(TOP_K):
        m = jnp.max(jnp.max(cand, axis=1, keepdims=True), axis=0, keepdims=True)
        big = jnp.int32(2 ** 30)
        am = jnp.where(cand == m, eidx, big)
        am = jnp.min(jnp.min(am, axis=1, keepdims=True), axis=0, keepdims=True)
        hit = eidx == am
        sel = jnp.logical_or(sel, hit)
        cand = jnp.where(hit, NEG_INF, cand)
    wsel = jnp.where(sel, scores, 0.0)
    tot = jnp.sum(jnp.sum(wsel, axis=1, keepdims=True), axis=0, keepdims=True)
    return wsel / tot * ROUTED_SCALE, sel


MOE_TC = 4096
N_CHUNKS = T_ALL // MOE_TC
ROW_TILE = 256
N_FLUSH = MOE_TC // TILE
PRE_TM = 1024
ROW_SUB = D_MODEL // LANES


def _moe_pre_kernel(x_ref, g_ref, sh_ref, sc_ref, wr_ref, br_ref, tri_ref, wsg_ref, wsu_ref,
                    wsd_ref, hrow_ref, gate_ref, rank_ref, shared_ref, cnt_ref, carry_ref):
    i = pl.program_id(0)
    r = _mod_row(i, PRE_TM)
    h = _modulated_norm(x_ref[...], g_ref[...], sh_ref[pl.ds(r, 1), :], sc_ref[pl.ds(r, 1), :])
    hb = h.astype(BF16)
    for s in range(ROW_SUB):
        hrow_ref[pl.ds(s, PRE_TM, stride=ROW_SUB), :] = h[:, s * LANES:(s + 1) * LANES]
    h_hi, h_lo = hb, (h - hb.astype(F32)).astype(BF16)
    w_hi, w_lo = _split_bf16(wr_ref[...])
    logits = _dot_nt(w_hi, h_hi) + _dot_nt(w_lo, h_hi) + _dot_nt(w_hi, h_lo)
    scores = jax.nn.sigmoid(logits)
    biased = scores + br_ref[:, 0:1]
    shape3 = (N_GROUPS, GROUP, PRE_TM)
    gate_t, sel = _route(scores.reshape(shape3), biased.reshape(shape3))
    gate_t = gate_t.reshape(N_EXPERTS, PRE_TM)
    sel = jnp.where(sel.reshape(N_EXPERTS, PRE_TM), 1.0, 0.0)

    @pl.when(i % (MOE_TC // PRE_TM) == 0)
    def _():
        carry_ref[...] = jnp.zeros_like(carry_ref)

    carry = carry_ref[...]
    local = _dot(sel.astype(BF16), tri_ref[...])
    rank = jnp.where(sel > 0.0, local + jnp.concatenate([carry] * (PRE_TM // LANES), axis=1), -1.0)
    carry = carry + jnp.sum(sel, axis=1, keepdims=True)
    carry_ref[...] = carry

    @pl.when(i % (MOE_TC // PRE_TM) == MOE_TC // PRE_TM - 1)
    def _():
        cnt_ref[0] = carry.astype(jnp.int32)
    gate_ref[...] = gate_t
    rank_ref[...] = rank
    a = _silu(_dot(hb, wsg_ref[...])) * _dot(hb, wsu_ref[...])
    shared_ref[...] = _dot(a.astype(BF16), wsd_ref[...])


def _moe_pre(x, g, mod, w_router_t, b_router, wsg, wsu, wsd):
    return pl.pallas_call(
        _moe_pre_kernel,
        grid=(T_ALL // PRE_TM,),
        in_specs=[
            pl.BlockSpec((PRE_TM, D_MODEL), lambda i: (i, 0)),
            pl.BlockSpec((1, D_MODEL), lambda i: (0, 0)),
            pl.BlockSpec((8, D_MODEL), lambda i: (0, 3)),
            pl.BlockSpec((8, D_MODEL), lambda i: (0, 4)),
            pl.BlockSpec((N_EXPERTS, D_MODEL), lambda i: (0, 0)),
            pl.BlockSpec((N_EXPERTS, LANES), lambda i: (0, 0)),
            pl.BlockSpec((PRE_TM, PRE_TM), lambda i: (0, 0)),
            pl.BlockSpec((D_MODEL, D_SHARED), lambda i: (0, 0)),
            pl.BlockSpec((D_MODEL, D_SHARED), lambda i: (0, 0)),
            pl.BlockSpec((D_SHARED, D_MODEL), lambda i: (0, 0)),
        ],
        out_specs=[
            pl.BlockSpec((PRE_TM * ROW_SUB, LANES), lambda i: (i, 0)),
            pl.BlockSpec((N_EXPERTS, PRE_TM), lambda i: (0, i)),
            pl.BlockSpec((N_EXPERTS, PRE_TM), lambda i: (0, i)),
            pl.BlockSpec((PRE_TM, D_MODEL), lambda i: (i, 0)),
            pl.BlockSpec((1, N_EXPERTS, LANES), lambda i: (i // (MOE_TC // PRE_TM), 0, 0)),
        ],
        out_shape=[
            jax.ShapeDtypeStruct((T_ALL * ROW_SUB, LANES), F32),
            jax.ShapeDtypeStruct((N_EXPERTS, T_ALL), F32),
            jax.ShapeDtypeStruct((N_EXPERTS, T_ALL), F32),
            jax.ShapeDtypeStruct((T_ALL, D_MODEL), F32),
            jax.ShapeDtypeStruct((N_CHUNKS, N_EXPERTS, LANES), jnp.int32),
        ],
        scratch_shapes=[pltpu.VMEM((N_EXPERTS, LANES), F32)],
        compiler_params=_cparams("arbitrary"),
        name="moe_pre",
    )(x, g.reshape(1, D_MODEL), mod, mod, w_router_t,
      jnp.broadcast_to(b_router.reshape(N_EXPERTS, 1), (N_EXPERTS, LANES)),
      jnp.asarray(np.triu(np.ones((PRE_TM, PRE_TM), np.float32), 1), BF16), wsg, wsu, wsd)


LIST_ROWS = MOE_TC // LANES
TILE_ROWS = ROW_TILE // LANES
DUMMY_ROW = MOE_TC * ROW_SUB


def _moe_sort_kernel(rank_ref, gate_ref, cnt_ref, list_ref, w_ref):
    rank = rank_ref[...]
    lane = lax.broadcasted_iota(jnp.int32, rank.shape, 1)
    d = jnp.where(rank >= 0.0, lane - rank.astype(jnp.int32), 0)
    w = gate_ref[...]
    for s in range(MOE_TC.bit_length() - 1):
        k = 1 << s
        d_in = pltpu.roll(d, MOE_TC - k, 1)
        w_in = pltpu.roll(w, MOE_TC - k, 1)
        take = (d_in & k) != 0
        leave = (d & k) != 0
        d = jnp.where(take, d_in, jnp.where(leave, 0, d))
        w = jnp.where(take, w_in, w)
    valid = lane < cnt_ref[0][:, 0:1]
    row = (lane + d) * ROW_SUB
    gsrc = jnp.where(valid, row, 0)
    ssrc = jnp.where(valid, row, DUMMY_ROW)
    w = jnp.where(valid, w, 0.0)
    for b in range(LIST_ROWS):
        cols = slice(b * LANES, (b + 1) * LANES)
        list_ref[pl.ds(b, N_EXPERTS, stride=2 * LIST_ROWS), :] = gsrc[:, cols]
        list_ref[pl.ds(LIST_ROWS + b, N_EXPERTS, stride=2 * LIST_ROWS), :] = ssrc[:, cols]
        w_ref[pl.ds(b, N_EXPERTS, stride=LIST_ROWS), :] = w[:, cols]


def _moe_sort(rank_t, gate_t, cnt_b):
    chunk_spec = pl.BlockSpec((N_EXPERTS, MOE_TC), lambda c: (0, c))
    n_rows = N_CHUNKS * N_EXPERTS * LIST_ROWS
    return pl.pallas_call(
        _moe_sort_kernel,
        grid=(N_CHUNKS,),
        in_specs=[chunk_spec, chunk_spec,
                  pl.BlockSpec((1, N_EXPERTS, LANES), lambda c: (c, 0, 0))],
        out_specs=[pl.BlockSpec((N_EXPERTS * 2 * LIST_ROWS, LANES), lambda c: (c, 0)),
                   pl.BlockSpec((N_EXPERTS * LIST_ROWS, LANES), lambda c: (c, 0))],
        out_shape=[jax.ShapeDtypeStruct((2 * n_rows, LANES), jnp.int32),
                   jax.ShapeDtypeStruct((n_rows, LANES), F32)],
        compiler_params=_cparams("arbitrary"),
        name="moe_sort",
    )(rank_t, gate_t, cnt_b)


SCATTER_BATCH = 16


EXPERTS_PER_STEP = 2
EXPERT_STEPS = N_EXPERTS // EXPERTS_PER_STEP


N_PIECES = 8
PIECE_ROWS = ROW_TILE // N_PIECES


class _ExpertRefs:
    def __init__(self, list0, wt0, list_ref, wt_ref, h_ref, acc_ref, wgb_ref, wub_ref, wdb_ref):
        self.list0, self.wt0 = list0, wt0
        self.list_ref, self.wt_ref, self.h_ref, self.acc_ref = list_ref, wt_ref, h_ref, acc_ref
        self.wgb_ref, self.wub_ref, self.wdb_ref = wgb_ref, wub_ref, wdb_ref


def _gather_rows(ex, t, r0, n, xs_ref):
    base = ex.list0 + t * TILE_ROWS
    for r in range(r0, r0 + n):
        tok = pl.multiple_of(ex.list_ref[base + r // LANES, r % LANES], ROW_SUB)
        xs_ref[r // 8, pl.ds(r % 8, ROW_SUB, stride=8), :] = ex.h_ref[pl.ds(tok, ROW_SUB), :]


def _scatter_rows(ex, t, r0, n, ys_ref):
    base = ex.list0 + LIST_ROWS + t * TILE_ROWS
    for b0 in range(r0, r0 + n, SCATTER_BATCH):
        rows = range(b0, b0 + SCATTER_BATCH)
        dsts = [ex.acc_ref.at[pl.ds(pl.multiple_of(ex.list_ref[base + r // LANES, r % LANES],
                                                   ROW_SUB), ROW_SUB), :] for r in rows]
        news = [dst[...] + ys_ref[r // 8, pl.ds(r % 8, ROW_SUB, stride=8), :]
                for dst, r in zip(dsts, rows)]
        for dst, new in zip(dsts, news):
            dst[...] = new


def _tile_ffn(ex, t, xs_ref, ys_ref, side_work=None):
    kc = D_MODEL // (N_PIECES // 2)
    hg = hu = None
    for p in range(N_PIECES // 2):
        if side_work is not None:
            side_work(p)
        xk = jnp.concatenate(
            [xs_ref[:, s * 8:(s + 1) * 8, :].reshape(ROW_TILE, LANES)
             for s in range(p * kc // LANES, (p + 1) * kc // LANES)], axis=1).astype(BF16)
        dg = _dot(xk, ex.wgb_ref[p * kc:(p + 1) * kc, :])
        du = _dot(xk, ex.wub_ref[p * kc:(p + 1) * kc, :])
        hg = dg if hg is None else hg + dg
        hu = du if hu is None else hu + du
    eye = (lax.broadcasted_iota(jnp.int32, (ROW_TILE, LANES), 0) % LANES
           == lax.broadcasted_iota(jnp.int32, (ROW_TILE, LANES), 1))
    row_blk = lax.broadcasted_iota(jnp.int32, (ROW_TILE, LANES), 0) // LANES
    wrows = functools.reduce(
        lambda a, b: a + b,
        [jnp.where(row_blk == k, ex.wt_ref[pl.ds(ex.wt0 + t * TILE_ROWS + k, 1), :], 0.0)
         for k in range(TILE_ROWS)])
    wcol = jnp.sum(jnp.where(eye, wrows, 0.0), axis=1, keepdims=True)
    a = (_silu(hg) * hu * wcol).astype(BF16)
    for p in range(N_PIECES // 2):
        if side_work is not None:
            side_work(N_PIECES // 2 + p)
        y = _dot(a, ex.wdb_ref[:, p * kc:(p + 1) * kc])
        for q in range(kc // LANES):
            s = p * kc // LANES + q
            ys_ref[:, s * 8:(s + 1) * 8, :] = y[:, q * LANES:(q + 1) * LANES].reshape(
                ROW_TILE // 8, 8, LANES)


def _expert_tiles(ex, t, n, xs_refs, ys_refs):
    _gather_rows(ex, t, 0, ROW_TILE, xs_refs[0])
    for i in range(n):
        def side_work(p, i=i):
            if i + 1 < n:
                _gather_rows(ex, t + i + 1, p * PIECE_ROWS, PIECE_ROWS, xs_refs[(i + 1) % 2])
            if i >= 1:
                _scatter_rows(ex, t + i - 1, p * PIECE_ROWS, PIECE_ROWS, ys_refs[(i - 1) % 2])

        _tile_ffn(ex, t + i, xs_refs[i % 2], ys_refs[i % 2], side_work if n > 1 else None)
    _scatter_rows(ex, t + n - 1, 0, ROW_TILE, ys_refs[(n - 1) % 2])


def _moe_routed_kernel(cnt_ref, list_ref, wt_ref, h_ref, wg_ref, wu_ref, wd_ref,
                       x_ref, sh_ref, g2_ref, *refs):
    out_refs, scratch = refs[:-8], refs[-8:]
    acc_ref, wgb_ref, wub_ref, wdb_ref = scratch[:4]
    xs_refs, ys_refs = scratch[4:6], scratch[6:8]
    c = pl.program_id(0)
    j = pl.program_id(1)

    @pl.when(j == 0)
    def _():
        acc_ref[...] = jnp.zeros_like(acc_ref)

    for k in range(EXPERTS_PER_STEP):
        expert = jnp.minimum(j, EXPERT_STEPS - 1) * EXPERTS_PER_STEP + k
        count = cnt_ref[c * N_EXPERTS + expert]

        @pl.when(jnp.logical_and(j < EXPERT_STEPS, count > 0))
        def _():
            wgb_ref[...] = wg_ref[0, k].astype(BF16)
            wub_ref[...] = wu_ref[0, k].astype(BF16)
            wdb_ref[...] = wd_ref[0, k].astype(BF16)

            ex = _ExpertRefs(k * 2 * LIST_ROWS, k * LIST_ROWS, list_ref, wt_ref, h_ref, acc_ref,
                             wgb_ref, wub_ref, wdb_ref)
            n_tiles = (count + ROW_TILE - 1) // ROW_TILE

            last3 = jnp.logical_and(n_tiles % 2 == 1, n_tiles >= 3)
            n_pairs = (n_tiles - jnp.where(last3, 3, n_tiles % 2)) // 2

            def tile_pair(m, carry):
                _expert_tiles(ex, 2 * m, 2, xs_refs, ys_refs)
                return carry

            lax.fori_loop(0, n_pairs, tile_pair, 0)

            @pl.when(last3)
            def _():
                _expert_tiles(ex, n_tiles - 3, 3, xs_refs, ys_refs)

            @pl.when(n_tiles == 1)
            def _():
                _expert_tiles(ex, 0, 1, xs_refs, ys_refs)

    @pl.when(j >= EXPERT_STEPS)
    def _():
        base = (j - EXPERT_STEPS) * (TILE * ROW_SUB)
        moe = jnp.concatenate(
            [acc_ref[pl.ds(base + s, TILE, stride=ROW_SUB), :] for s in range(ROW_SUB)], axis=1)
        new_x = x_ref[...] + g2_ref[pl.ds(c, 1), :] * (moe + sh_ref[...])
        if len(out_refs) == 1:
            out_refs[0][...] = new_x
        else:
            prompt_ref, latent_ref = out_refs

            @pl.when(c < T_P // MOE_TC)
            def _():
                prompt_ref[...] = new_x

            @pl.when(c >= T_P // MOE_TC)
            def _():
                latent_ref[...] = new_x


def _moe_routed(lists, wts, counts, hrows, w_gate, w_up, w_down, li, x, shared, mod,
                split_output):
    group = lambda j: jnp.minimum(j, EXPERT_STEPS - 1)
    list_map = lambda c, j, nt: (c * EXPERT_STEPS + group(j), 0)
    w_map = lambda c, j, nt: (li, group(j), 0, 0)
    w_in_spec = pl.BlockSpec((1, EXPERTS_PER_STEP, D_MODEL, D_EXPERT), w_map)
    w_out_spec = pl.BlockSpec((1, EXPERTS_PER_STEP, D_EXPERT, D_MODEL), w_map)
    out_blk = lambda c, j: c * N_FLUSH + jnp.maximum(j - EXPERT_STEPS, 0)
    tok_spec = pl.BlockSpec((TILE, D_MODEL), lambda c, j, nt: (out_blk(c, j), 0))
    if split_output:
        out_specs = [
            pl.BlockSpec((TILE, D_MODEL), lambda c, j, nt: (jnp.minimum(out_blk(c, j), P_TILES - 1), 0)),
            pl.BlockSpec((TILE, D_MODEL), lambda c, j, nt: (jnp.maximum(out_blk(c, j) - P_TILES, 0), 0)),
        ]
        out_shape = [jax.ShapeDtypeStruct((T_P, D_MODEL), F32),
                     jax.ShapeDtypeStruct((T_S, D_MODEL), F32)]
    else:
        out_specs = tok_spec
        out_shape = jax.ShapeDtypeStruct((T_ALL, D_MODEL), F32)
    stage = pltpu.VMEM((ROW_TILE // 8, 8 * ROW_SUB, LANES), F32)
    grid_spec = pltpu.PrefetchScalarGridSpec(
        num_scalar_prefetch=1,
        grid=(N_CHUNKS, EXPERT_STEPS + N_FLUSH),
        in_specs=[
            pl.BlockSpec((EXPERTS_PER_STEP * 2 * LIST_ROWS, LANES), list_map,
                         memory_space=pltpu.SMEM),
            pl.BlockSpec((EXPERTS_PER_STEP * LIST_ROWS, LANES), list_map),
            pl.BlockSpec((MOE_TC * ROW_SUB, LANES), lambda c, j, nt: (c, 0),
                         pipeline_mode=pl.Buffered(1)),
            w_in_spec, w_in_spec, w_out_spec,
            tok_spec, tok_spec,
            pl.BlockSpec((8, D_MODEL), lambda c, j, nt: (0, 5)),
        ],
        out_specs=out_specs,
        scratch_shapes=[
            pltpu.VMEM(((MOE_TC + 1) * ROW_SUB, LANES), F32),
            pltpu.VMEM((D_MODEL, D_EXPERT), BF16),
            pltpu.VMEM((D_MODEL, D_EXPERT), BF16),
            pltpu.VMEM((D_EXPERT, D_MODEL), BF16),
            stage, stage, stage, stage,
        ],
    )
    return pl.pallas_call(
        _moe_routed_kernel,
        grid_spec=grid_spec,
        out_shape=out_shape,
        compiler_params=_cparams("arbitrary", "arbitrary", vmem_limit=VMEM_LIMIT_ROUTED),
        name="moe_routed",
    )(counts.reshape(-1), lists, wts, hrows, w_gate, w_up, w_down, x, shared, mod)


def kernel(x_prompt, x_sample, cache_k, cache_v, c, c_ctx, w_mod, b_mod, norm1, norm2, w_in_even, conv_a, q_norm, k_norm, lam_q1, lam_k1, lam_q2, lam_k2, subln, w_out_even, w_in_odd, conv_c, conv_c_b, ln_c_g, ln_c_b, w_pool, pool_scale, w_out_odd, w_router, b_router, w_gate, w_up, w_down, ws_gate, ws_up, ws_down):
    x = (x_prompt.reshape(T_P, D_MODEL), x_sample.reshape(T_S, D_MODEL))
    cond = jnp.concatenate([c_ctx[None, :], c, jnp.zeros((8 - 1 - DEC_BATCH, D_MODEL), F32)], axis=0)
    mod_all = _modulation(cond, w_mod, b_mod)

    new_k, new_v = [], []
    for li in range(DEPTH):
        mod = mod_all[li]
        if li % 2 == 0:
            e = li // 2
            lam_init = 0.8 - 0.6 * math.exp(-0.3 * li)
            u = _norm_in(x, norm1[li], mod, w_in_even[e].astype(BF16))
            q, k, v, k_cache, v_cache = _qkv_prep(u, q_norm[e], k_norm[e])
            new_k.append(k_cache)
            new_v.append(v_cache)
            o = _attention(q, k, v, cache_k, cache_v, e,
                           (lam_q1[e], lam_k1[e], lam_q2[e], lam_k2[e]), subln[e], lam_init)
            x = _even_out(x, u, o, conv_a[e], mod, w_out_even[e].astype(BF16))
        else:
            o_ = li // 2
            u = _norm_in(x, norm1[li], mod, w_in_odd[o_].astype(BF16))
            x = _odd_out(x, u, conv_c[o_], conv_c_b[o_], ln_c_g[o_], ln_c_b[o_],
                         w_pool[o_].astype(BF16), pool_scale[o_], mod, w_out_odd[o_].astype(BF16))
        hrows, gate_b, rank_b, shared, cnt_b = _moe_pre(
            x, norm2[li], mod, w_router[li].T, b_router[li], ws_gate[li].astype(BF16),
            ws_up[li].astype(BF16), ws_down[li].astype(BF16))
        lists, wts = _moe_sort(rank_b, gate_b, cnt_b)
        x = _moe_routed(lists, wts, cnt_b[:, :, 0], hrows, w_gate, w_up, w_down, li, x, shared,
                        mod, split_output=(li == DEPTH - 1))

    y_prompt = x[0].reshape(BATCH, SEQ, D_MODEL)
    y_sample = x[1].reshape(DEC_BATCH, DEC_SEQ, D_MODEL)
    return (y_prompt, y_sample, jnp.stack(new_k, axis=1), jnp.stack(new_v, axis=1))
```

```python
import functools
import math

import numpy as np
import jax
import jax.numpy as jnp
from jax import lax
from jax.experimental import pallas as pl
from jax.experimental.pallas import tpu as pltpu

D_MODEL = 1024
BATCH = 16
SEQ = 256
DEPTH = 2
DEC_BATCH = 2
DEC_SEQ = 4096
PAST_LEN = 512
GRID_W = 64
H_B = 4
DK_B = 64
DV_B = 2 * DK_B
D_A = D_MODEL // 2
D_B = H_B * DV_B
D_C = D_MODEL // 2
D_D = D_MODEL // 2
CONV_C = 31
POOL_WINDOWS = (2, 4, 8, 16)
D_DG = D_D // len(POOL_WINDOWS)
N_EXPERTS = 64
TOP_K = 8
N_GROUPS = 8
TOPK_GROUPS = 4
D_EXPERT = 256
D_SHARED = 256
ROUTED_SCALE = 2.5
ROPE_BASE = 10000.0
EPS = 1e-6

F32 = jnp.float32
BF16 = jnp.bfloat16

T_P = BATCH * SEQ
T_S = DEC_BATCH * DEC_SEQ
T_ALL = T_P + T_S
TILE = 256
N_TILES = T_ALL // TILE
P_TILES = T_P // TILE
S_TILES = DEC_SEQ // TILE
LANES = 128
VMEM_LIMIT = 56 * 1024 * 1024
VMEM_LIMIT_ROUTED = 60 * 1024 * 1024


def _cparams(*sem, vmem_limit=VMEM_LIMIT):
    return pltpu.CompilerParams(dimension_semantics=sem, vmem_limit_bytes=vmem_limit)


def _mod_row(i, tm):
    npt = T_P // tm
    per = DEC_SEQ // tm
    return jnp.where(i < npt, 0, 1 + (i - npt) // per)


def _seq_flags(i):
    j = (i - P_TILES) % S_TILES
    first = jnp.logical_or(i < P_TILES, j == 0)
    last = jnp.logical_or(i < P_TILES, j == S_TILES - 1)
    return first, last


def _seq_tile(i):
    return jnp.where(i < P_TILES, 0, (i - P_TILES) % S_TILES)


def _split_bf16(a):
    hi = a.astype(BF16)
    lo = (a - hi.astype(F32)).astype(BF16)
    return hi, lo


def _dot(a, b):
    return jnp.dot(a, b, preferred_element_type=F32)


def _dot_nt(a, b):
    return lax.dot_general(a, b, (((1,), (1,)), ((), ())), preferred_element_type=F32)


def _dot3(a, b):
    a_hi, a_lo = _split_bf16(a)
    b_hi, b_lo = _split_bf16(b)
    return _dot(a_hi, b_hi) + _dot(a_lo, b_hi) + _dot(a_hi, b_lo)


def _silu(x):
    return x * jax.nn.sigmoid(x)


MOD_TN = 1536


def _mod_kernel(c_ref, w_ref, b_ref, o_ref):
    o_ref[0] = _dot3(_silu(c_ref[...]), w_ref[0]) + b_ref[0]


def _modulation(cond, w_mod, b_mod):
    n = 6 * D_MODEL
    return pl.pallas_call(
        _mod_kernel,
        grid=(DEPTH, n // MOD_TN),
        in_specs=[
            pl.BlockSpec((8, D_MODEL), lambda l, j: (0, 0)),
            pl.BlockSpec((1, D_MODEL, MOD_TN), lambda l, j: (l, 0, j)),
            pl.BlockSpec((1, 1, MOD_TN), lambda l, j: (l, 0, j)),
        ],
        out_specs=pl.BlockSpec((1, 8, MOD_TN), lambda l, j: (l, 0, j)),
        out_shape=jax.ShapeDtypeStruct((DEPTH, 8, n), F32),
        compiler_params=_cparams("arbitrary", "arbitrary"),
        name="modulation",
    )(cond, w_mod, b_mod.reshape(DEPTH, 1, n))


IN_TM = 1024


def _modulated_norm(x, g, shift, scale):
    ms = jnp.mean(x * x, axis=-1, keepdims=True)
    return (x * lax.rsqrt(ms + EPS) * g) * (1.0 + scale) + shift


def _tok_specs(x, tm, width):
    if isinstance(x, tuple):
        n_p = T_P // tm
        return ([pl.BlockSpec((tm, width), lambda i: (jnp.minimum(i, n_p - 1), 0)),
                 pl.BlockSpec((tm, width), lambda i: (jnp.maximum(i - n_p, 0), 0))], list(x))
    return [pl.BlockSpec((tm, width), lambda i: (i, 0))], [x]


def _tok_load(refs, i, tm):
    if len(refs) == 2:
        return jnp.where(i < T_P // tm, refs[0][...], refs[1][...])
    return refs[0][...]


def _norm_in_kernel(*refs, n_x):
    x_refs, (g_ref, sh_ref, sc_ref, w_ref, o_ref) = refs[:n_x], refs[n_x:]
    i = pl.program_id(0)
    r = _mod_row(i, IN_TM)
    h = _modulated_norm(_tok_load(x_refs, i, IN_TM), g_ref[...], sh_ref[pl.ds(r, 1), :],
                        sc_ref[pl.ds(r, 1), :])
    o_ref[...] = _dot(h.astype(BF16), w_ref[...])


def _norm_in(x, g, mod, w_bf16):
    n = w_bf16.shape[1]
    x_specs, x_ops = _tok_specs(x, IN_TM, D_MODEL)
    return pl.pallas_call(
        functools.partial(_norm_in_kernel, n_x=len(x_ops)),
        grid=(T_ALL // IN_TM,),
        in_specs=x_specs + [
            pl.BlockSpec((1, D_MODEL), lambda i: (0, 0)),
            pl.BlockSpec((8, D_MODEL), lambda i: (0, 0)),
            pl.BlockSpec((8, D_MODEL), lambda i: (0, 1)),
            pl.BlockSpec((D_MODEL, n), lambda i: (0, 0)),
        ],
        out_specs=pl.BlockSpec((IN_TM, n), lambda i: (i, 0)),
        out_shape=jax.ShapeDtypeStruct((T_ALL, n), F32),
        compiler_params=_cparams("arbitrary"),
        name="norm_in_proj",
    )(*x_ops, g.reshape(1, D_MODEL), mod, mod, w_bf16)


QKV_TM = 1024


def _rope_tables():
    half = DK_B // 2
    freqs = ROPE_BASE ** (-np.arange(0, half, 2, dtype=np.float64) / half)
    l = np.arange(DEC_SEQ)
    pos_r = (l // GRID_W).astype(np.float64)
    pos_c = (l % GRID_W).astype(np.float64)
    lane = np.arange(LANES)
    jj = lane % DK_B
    m = jj % half
    f = m % (half // 2)
    pos = np.where((jj < half)[None, :], pos_r[:, None], pos_c[:, None])
    ang = pos * freqs[f][None, :]
    sign = np.where(m < half // 2, -1.0, 1.0)[None, :]
    cos = np.concatenate([np.ones((QKV_TM, LANES)), np.cos(ang)], axis=0)
    sin = np.concatenate([np.zeros((QKV_TM, LANES)), sign * np.sin(ang)], axis=0)
    return cos.astype(np.float32), sin.astype(np.float32)


def _segment_mean_matrix():
    lane = np.arange(LANES)
    same = (lane[:, None] // DK_B) == (lane[None, :] // DK_B)
    return (same.astype(np.float32) / DK_B)


def _qk_prep(x, g, cos, sin, seg):
    x2 = x * x
    hi, lo = _split_bf16(x2)
    ms = _dot(hi, seg) + _dot(lo, seg)
    y = x * lax.rsqrt(ms + EPS) * g
    lane = lax.broadcasted_iota(jnp.int32, y.shape, 1)
    lower = (lane % (DK_B // 2)) < (DK_B // 4)
    partner = jnp.where(lower, pltpu.roll(y, LANES - DK_B // 4, 1), pltpu.roll(y, DK_B // 4, 1))
    return y * cos + partner * sin


def _qkv_prep_kernel(q_ref, k_ref, v_ref, qn_ref, kn_ref, cos_ref, sin_ref, seg_ref,
                     qo_ref, ko_ref, vo_ref, kc_ref, vc_ref):
    cos = cos_ref[...]
    sin = sin_ref[...]
    seg = seg_ref[...]
    scale = math.log2(math.e) / math.sqrt(DK_B)
    is_prompt = pl.program_id(0) < T_P // QKV_TM
    for h in range(H_B):
        cols = slice(h * LANES, (h + 1) * LANES)
        qo_ref[h] = (_qk_prep(q_ref[:, cols], qn_ref[...], cos, sin, seg) * scale).astype(BF16)
        k = _qk_prep(k_ref[:, cols], kn_ref[...], cos, sin, seg)
        v = v_ref[:, cols]
        ko_ref[h] = k.astype(BF16)
        vo_ref[h] = v.astype(BF16)

        @pl.when(is_prompt)
        def _():
            for b in range(QKV_TM // SEQ):
                kc_ref[b, h] = k[b * SEQ:(b + 1) * SEQ]
                vc_ref[b, h] = v[b * SEQ:(b + 1) * SEQ]


def _qkv_prep(u, qn, kn):
    cos, sin = _rope_tables()
    seg = jnp.asarray(_segment_mean_matrix(), BF16)
    qn2 = jnp.concatenate([qn, qn]).reshape(1, LANES)
    kn2 = jnp.concatenate([kn, kn]).reshape(1, LANES)
    col0 = 3 * D_A // D_B
    p_steps = T_P // QKV_TM
    s_steps = DEC_SEQ // QKV_TM

    def tab_map(i):
        return (jnp.where(i < p_steps, 0, 1 + (i - p_steps) % s_steps), 0)

    out_b = jax.ShapeDtypeStruct((H_B, T_ALL, LANES), BF16)
    out_spec = pl.BlockSpec((H_B, QKV_TM, LANES), lambda i: (0, i, 0))
    cache = jax.ShapeDtypeStruct((BATCH, H_B, SEQ, LANES), F32)
    cache_spec = pl.BlockSpec((QKV_TM // SEQ, H_B, SEQ, LANES),
                              lambda i: (jnp.minimum(i, p_steps - 1), 0, 0, 0))
    return pl.pallas_call(
        _qkv_prep_kernel,
        grid=(T_ALL // QKV_TM,),
        in_specs=[
            pl.BlockSpec((QKV_TM, D_B), lambda i: (i, col0)),
            pl.BlockSpec((QKV_TM, D_B), lambda i: (i, col0 + 1)),
            pl.BlockSpec((QKV_TM, D_B), lambda i: (i, col0 + 2)),
            pl.BlockSpec((1, LANES), lambda i: (0, 0)),
            pl.BlockSpec((1, LANES), lambda i: (0, 0)),
            pl.BlockSpec((QKV_TM, LANES), tab_map),
            pl.BlockSpec((QKV_TM, LANES), tab_map),
            pl.BlockSpec((LANES, LANES), lambda i: (0, 0)),
        ],
        out_specs=[out_spec] * 3 + [cache_spec] * 2,
        out_shape=[out_b, out_b, out_b, cache, cache],
        compiler_params=_cparams("arbitrary"),
        name="qkv_prep",
    )(u, u, u, qn2, kn2, jnp.asarray(cos), jnp.asarray(sin), seg)


def _lambda(lq1_ref, lk1_ref, lq2_ref, lk2_ref, lam_init):
    a = jnp.sum(lq1_ref[...] * lk1_ref[...], axis=-1, keepdims=True)
    b = jnp.sum(lq2_ref[...] * lk2_ref[...], axis=-1, keepdims=True)
    return jnp.exp(a) - jnp.exp(b) + lam_init


ATTN_TQ = 1024
ATTN_PIECE = 128


def _attn_body(q, keys, vals, lam, subg, lam_init):
    lane = lax.broadcasted_iota(jnp.int32, q.shape, 1)
    zero = jnp.zeros_like(q)
    qa = jnp.where(lane < DK_B, q, zero)
    qb = jnp.where(lane < DK_B, zero, q)
    scores = [[_dot_nt(qq, k) for k in keys] for qq in (qa, qb)]
    outs = []
    for ss in scores:
        m = functools.reduce(jnp.maximum, [jnp.max(s, axis=-1, keepdims=True) for s in ss])
        ps = [jnp.exp2(s - m) for s in ss]
        l = functools.reduce(jnp.add, [jnp.sum(p, axis=-1, keepdims=True) for p in ps])
        pv = functools.reduce(jnp.add, [_dot(p.astype(BF16), v) for p, v in zip(ps, vals)])
        outs.append(pv / l)
    o = outs[0] - lam * outs[1]
    ms = jnp.mean(o * o, axis=-1, keepdims=True)
    return (o * lax.rsqrt(ms + EPS) * subg) * (1.0 - lam_init)


def _attn_prompt_kernel(q_ref, k_ref, v_ref, lq1, lk1, lq2, lk2, sg_ref, o_ref, *, lam_init):
    lam = _lambda(lq1, lk1, lq2, lk2, lam_init)
    for h in range(H_B):
        o_ref[:, h * DV_B:(h + 1) * DV_B] = _attn_body(
            q_ref[h], [k_ref[h]], [v_ref[h]], lam, sg_ref[...], lam_init)


def _attn_latent_kernel(q_ref, k_ref, v_ref, ck_ref, cv_ref, lq1, lk1, lq2, lk2, sg_ref, o_ref,
                        *, lam_init):
    lam = _lambda(lq1, lk1, lq2, lk2, lam_init)
    keys = [ck_ref[0, 0, 0].astype(BF16), k_ref[0]]
    vals = [cv_ref[0, 0, 0].astype(BF16), v_ref[0]]
    for r0 in range(0, ATTN_TQ, ATTN_PIECE):
        o_ref[r0:r0 + ATTN_PIECE, :] = _attn_body(q_ref[0, r0:r0 + ATTN_PIECE, :], keys, vals,
                                                  lam, sg_ref[...], lam_init)


def _attention(q, k, v, cache_k, cache_v, e, lam_params, subg, lam_init):
    small = [p.reshape(1, DK_B) for p in lam_params] + [subg.reshape(1, DV_B)]
    small_specs2 = [pl.BlockSpec((1, DK_B), lambda b: (0, 0))] * 4 + \
                   [pl.BlockSpec((1, DV_B), lambda b: (0, 0))]
    small_specs3 = [pl.BlockSpec((1, DK_B), lambda s, h, j: (0, 0))] * 4 + \
                   [pl.BlockSpec((1, DV_B), lambda s, h, j: (0, 0))]
    o_prompt = pl.pallas_call(
        functools.partial(_attn_prompt_kernel, lam_init=lam_init),
        grid=(BATCH,),
        in_specs=[pl.BlockSpec((H_B, SEQ, LANES), lambda b: (0, b, 0))] * 3 + small_specs2,
        out_specs=pl.BlockSpec((SEQ, D_B), lambda b: (b, 0)),
        out_shape=jax.ShapeDtypeStruct((T_P, D_B), F32),
        compiler_params=_cparams("arbitrary"),
        name="attn_prompt",
    )(q, k, v, *small)

    q_tiles = DEC_SEQ // ATTN_TQ
    kv_spec = pl.BlockSpec((1, DEC_SEQ, LANES), lambda s, h, j: (h, T_P // DEC_SEQ + s, 0))
    c_spec = pl.BlockSpec((1, 1, 1, PAST_LEN, LANES), lambda s, h, j: (s, e, h, 0, 0))
    o_latent = pl.pallas_call(
        functools.partial(_attn_latent_kernel, lam_init=lam_init),
        grid=(DEC_BATCH, H_B, q_tiles),
        in_specs=[pl.BlockSpec((1, ATTN_TQ, LANES),
                               lambda s, h, j: (h, T_P // ATTN_TQ + s * q_tiles + j, 0)),
                  kv_spec, kv_spec, c_spec, c_spec] + small_specs3,
        out_specs=pl.BlockSpec((ATTN_TQ, LANES), lambda s, h, j: (s * q_tiles + j, h)),
        out_shape=jax.ShapeDtypeStruct((T_S, D_B), F32),
        compiler_params=_cparams("arbitrary", "arbitrary", "arbitrary"),
        name="attn_latent",
    )(q, k, v, cache_k, cache_v, *small)
    return o_prompt, o_latent


HALO8 = 8
HALO16 = 16


def _prev_block(i, rows):
    return jnp.maximum(i * (TILE // rows) - 1, 0)


def _next_block(i, rows):
    return jnp.minimum((i + 1) * (TILE // rows), T_ALL // rows - 1)


def _even_out_kernel(*refs, n_x, n_o):
    x_refs, o_refs = refs[:n_x], refs[n_x:n_x + n_o]
    (bg_ref, cg_ref, hin_ref, cgp_ref, hinp_ref, cgn_ref, hinn_ref, cw_ref, g1_ref, w_ref,
     out_ref) = refs[n_x + n_o:]
    i = pl.program_id(0)
    first, last = _seq_flags(i)
    r = _mod_row(i, TILE)
    z = cg_ref[...] * hin_ref[...]
    zp = jnp.where(first, 0.0, cgp_ref[HALO8 - 1:HALO8, :] * hinp_ref[HALO8 - 1:HALO8, :])
    zn = jnp.where(last, 0.0, cgn_ref[0:1, :] * hinn_ref[0:1, :])
    row = lax.broadcasted_iota(jnp.int32, z.shape, 0)
    z_prev = jnp.where(row == 0, zp, pltpu.roll(z, 1, 0))
    z_next = jnp.where(row == TILE - 1, zn, pltpu.roll(z, TILE - 1, 0))
    cw = cw_ref[...]
    ya = bg_ref[...] * (cw[0:1, :] * z_prev + cw[1:2, :] * z + cw[2:3, :] * z_next)
    o = _tok_load(o_refs, i, TILE)
    y = _dot(ya.astype(BF16), w_ref[0:D_A, :]) + _dot(o.astype(BF16), w_ref[D_A:, :])
    out_ref[...] = _tok_load(x_refs, i, TILE) + g1_ref[pl.ds(r, 1), :] * y


def _even_out(x, u, o, conv_w, mod, w_out_bf16):
    tile_spec = lambda c: pl.BlockSpec((TILE, D_A), lambda i: (i, c))
    prev_spec = lambda c: pl.BlockSpec((HALO8, D_A), lambda i: (_prev_block(i, HALO8), c))
    next_spec = lambda c: pl.BlockSpec((HALO8, D_A), lambda i: (_next_block(i, HALO8), c))
    x_specs, x_ops = _tok_specs(x, TILE, D_MODEL)
    o_specs, o_ops = _tok_specs(o, TILE, D_B)
    return pl.pallas_call(
        functools.partial(_even_out_kernel, n_x=len(x_ops), n_o=len(o_ops)),
        grid=(N_TILES,),
        in_specs=x_specs + o_specs + [
            tile_spec(0), tile_spec(1), tile_spec(2),
            prev_spec(1), prev_spec(2), next_spec(1), next_spec(2),
            pl.BlockSpec((3, D_A), lambda i: (0, 0)),
            pl.BlockSpec((8, D_MODEL), lambda i: (0, 2)),
            pl.BlockSpec((D_MODEL, D_MODEL), lambda i: (0, 0)),
        ],
        out_specs=pl.BlockSpec((TILE, D_MODEL), lambda i: (i, 0)),
        out_shape=jax.ShapeDtypeStruct((T_ALL, D_MODEL), F32),
        compiler_params=_cparams("arbitrary"),
        name="even_mixer_out",
    )(*x_ops, *o_ops, u, u, u, u, u, u, u, conv_w, mod, w_out_bf16)


def _odd_out_kernel(x_ref, a_ref, b_ref, pd_ref, ap_ref, bp_ref, an_ref, bn_ref, pp_ref, pn_ref,
                    cw_ref, cb_ref, lg_ref, lb_ref, wp_ref, ps_ref, g1_ref, w_ref, out_ref,
                    ext_ref, extp_ref, shift_ref):
    i = pl.program_id(0)
    first, last = _seq_flags(i)
    r = _mod_row(i, TILE)
    ext_ref[0:HALO16, :] = jnp.where(first, 0.0, ap_ref[...] * jax.nn.sigmoid(bp_ref[...]))
    ext_ref[HALO16:HALO16 + TILE, :] = a_ref[...] * jax.nn.sigmoid(b_ref[...])
    ext_ref[HALO16 + TILE:, :] = jnp.where(last, 0.0, an_ref[...] * jax.nn.sigmoid(bn_ref[...]))
    base = HALO16 - CONV_C // 2
    parts = []
    for cb in range(D_C // LANES):
        cols = slice(cb * LANES, (cb + 1) * LANES)
        acc = jnp.zeros((TILE, LANES), F32)
        for phase in range(8):
            taps = [j for j in range(CONV_C) if (base + j) % 8 == phase]
            reach = max((base + j) // 8 for j in taps)
            rows = TILE + 8 * reach
            shift_ref[0:rows, :] = ext_ref[pl.ds(phase, rows), cols]
            for j in taps:
                a = (base + j) // 8
                acc = acc + cw_ref[j:j + 1, cols] * shift_ref[8 * a:8 * a + TILE, :]
        parts.append(acc)
    g = jnp.concatenate(parts, axis=-1) + cb_ref[...]
    mu = jnp.mean(g, axis=-1, keepdims=True)
    var = jnp.mean(jnp.square(g - mu), axis=-1, keepdims=True)
    g = _silu(((g - mu) * lax.rsqrt(var + EPS)) * lg_ref[...] + lb_ref[...])
    extp_ref[0:HALO8, :] = jnp.where(first, 0.0, pp_ref[...])
    extp_ref[HALO8:HALO8 + TILE, :] = pd_ref[...]
    extp_ref[HALO8 + TILE:, :] = jnp.where(last, 0.0, pn_ref[...])
    seq_len = jnp.where(i < P_TILES, SEQ, DEC_SEQ)
    pos = _seq_tile(i) * TILE + lax.broadcasted_iota(jnp.int32, (TILE, 1), 0)
    yd = []
    for gi, w in enumerate(POOL_WINDOWS):
        cols = slice(gi * D_DG, (gi + 1) * D_DG)
        s = jnp.zeros((TILE, D_DG), F32)
        for d in range(-(w // 2), w - w // 2):
            s = s + extp_ref[pl.ds(HALO8 + d, TILE), cols]
        lo = jnp.maximum(pos - w // 2, 0)
        hi = jnp.minimum(pos - w // 2 + w, seq_len)
        pooled = s / (hi - lo).astype(F32) - pd_ref[:, cols]
        yd.append(_dot(pooled.astype(BF16), wp_ref[gi]))
    yd = jnp.concatenate(yd, axis=-1) * ps_ref[...]
    y = _dot(g.astype(BF16), w_ref[0:D_C, :]) + _dot(yd.astype(BF16), w_ref[D_C:, :])
    out_ref[...] = x_ref[...] + g1_ref[pl.ds(r, 1), :] * y


def _odd_out(x, u, conv_w, conv_b, ln_g, ln_b, w_pool_bf16, p_scale, mod, w_out_bf16):
    tile_spec = lambda c: pl.BlockSpec((TILE, D_C), lambda i: (i, c))
    prev_spec = lambda rows, c: pl.BlockSpec((rows, D_C), lambda i: (_prev_block(i, rows), c))
    next_spec = lambda rows, c: pl.BlockSpec((rows, D_C), lambda i: (_next_block(i, rows), c))
    vec = lambda: pl.BlockSpec((1, D_C), lambda i: (0, 0))
    return pl.pallas_call(
        _odd_out_kernel,
        grid=(N_TILES,),
        in_specs=[
            pl.BlockSpec((TILE, D_MODEL), lambda i: (i, 0)),
            tile_spec(0), tile_spec(1), tile_spec(2),
            prev_spec(HALO16, 0), prev_spec(HALO16, 1), next_spec(HALO16, 0), next_spec(HALO16, 1),
            prev_spec(HALO8, 2), next_spec(HALO8, 2),
            pl.BlockSpec((CONV_C, D_C), lambda i: (0, 0)),
            vec(), vec(), vec(),
            pl.BlockSpec((len(POOL_WINDOWS), D_DG, D_DG), lambda i: (0, 0, 0)),
            vec(),
            pl.BlockSpec((8, D_MODEL), lambda i: (0, 2)),
            pl.BlockSpec((D_MODEL, D_MODEL), lambda i: (0, 0)),
        ],
        out_specs=pl.BlockSpec((TILE, D_MODEL), lambda i: (i, 0)),
        out_shape=jax.ShapeDtypeStruct((T_ALL, D_MODEL), F32),
        scratch_shapes=[pltpu.VMEM((TILE + 2 * HALO16, D_C), F32),
                        pltpu.VMEM((TILE + 2 * HALO8, D_D), F32),
                        pltpu.VMEM((TILE + 2 * HALO16, LANES), F32)],
        compiler_params=_cparams("arbitrary"),
        name="odd_mixer_out",
    )(x, u, u, u, u, u, u, u, u, u, conv_w, conv_b.reshape(1, D_C), ln_g.reshape(1, D_C),
      ln_b.reshape(1, D_C), w_pool_bf16, p_scale.reshape(1, D_D), mod, w_out_bf16)


GROUP = N_EXPERTS // N_GROUPS
NEG_INF = float("-inf")


def _first_argmax(v, idx, axis):
    m = jnp.max(v, axis=axis, keepdims=True)
    big = jnp.int32(2 ** 30)
    am = jnp.min(jnp.where(v == m, idx, big), axis=axis, keepdims=True)
    return m, am


def _route(scores, biased):
    shape = biased.shape
    member = lax.broadcasted_iota(jnp.int32, shape, 1)
    m1, a1 = _first_argmax(biased, member, 1)
    m2 = jnp.max(jnp.where(member == a1, NEG_INF, biased), axis=1, keepdims=True)
    gscore = m1 + m2
    gidx = lax.broadcasted_iota(jnp.int32, gscore.shape, 0)
    gsel = jnp.zeros(gscore.shape, jnp.bool_)
    for _ in range(TOPK_GROUPS):
        _, am = _first_argmax(gscore, gidx, 0)
        hit = gidx == am
        gsel = jnp.logical_or(gsel, hit)
        gscore = jnp.where(hit, NEG_INF, gscore)
    cand = jnp.where(gsel, biased, NEG_INF)
    eidx = lax.broadcasted_iota(jnp.int32, shape, 0) * GROUP + member
    sel = jnp.zeros(shape, jnp.bool_)
    for _ in range(TOP_K):
        m = jnp.max(jnp.max(cand, axis=1, keepdims=True), axis=0, keepdims=True)
        big = jnp.int32(2 ** 30)
        am = jnp.where(cand == m, eidx, big)
        am = jnp.min(jnp.min(am, axis=1, keepdims=True), axis=0, keepdims=True)
        hit = eidx == am
        sel = jnp.logical_or(sel, hit)
        cand = jnp.where(hit, NEG_INF, cand)
    wsel = jnp.where(sel, scores, 0.0)
    tot = jnp.sum(jnp.sum(wsel, axis=1, keepdims=True), axis=0, keepdims=True)
    return wsel / tot * ROUTED_SCALE, sel


MOE_TC = 4096
N_CHUNKS = T_ALL // MOE_TC
ROW_TILE = 256
N_FLUSH = MOE_TC // TILE
PRE_TM = 1024
ROW_SUB = D_MODEL // LANES


def _moe_pre_kernel(x_ref, g_ref, sh_ref, sc_ref, wr_ref, br_ref, tri_ref, wsg_ref, wsu_ref,
                    wsd_ref, hrow_ref, gate_ref, rank_ref, shared_ref, cnt_ref, carry_ref):
    i = pl.program_id(0)
    r = _mod_row(i, PRE_TM)
    h = _modulated_norm(x_ref[...], g_ref[...], sh_ref[pl.ds(r, 1), :], sc_ref[pl.ds(r, 1), :])
    hb = h.astype(BF16)
    for s in range(ROW_SUB):
        hrow_ref[pl.ds(s, PRE_TM, stride=ROW_SUB), :] = h[:, s * LANES:(s + 1) * LANES]
    h_hi, h_lo = hb, (h - hb.astype(F32)).astype(BF16)
    w_hi, w_lo = _split_bf16(wr_ref[...])
    logits = _dot_nt(w_hi, h_hi) + _dot_nt(w_lo, h_hi) + _dot_nt(w_hi, h_lo)
    scores = jax.nn.sigmoid(logits)
    biased = scores + br_ref[:, 0:1]
    shape3 = (N_GROUPS, GROUP, PRE_TM)
    gate_t, sel = _route(scores.reshape(shape3), biased.reshape(shape3))
    gate_t = gate_t.reshape(N_EXPERTS, PRE_TM)
    sel = jnp.where(sel.reshape(N_EXPERTS, PRE_TM), 1.0, 0.0)

    @pl.when(i % (MOE_TC // PRE_TM) == 0)
    def _():
        carry_ref[...] = jnp.zeros_like(carry_ref)

    carry = carry_ref[...]
    local = _dot(sel.astype(BF16), tri_ref[...])
    rank = jnp.where(sel > 0.0, local + jnp.concatenate([carry] * (PRE_TM // LANES), axis=1), -1.0)
    carry = carry + jnp.sum(sel, axis=1, keepdims=True)
    carry_ref[...] = carry

    @pl.when(i % (MOE_TC // PRE_TM) == MOE_TC // PRE_TM - 1)
    def _():
        cnt_ref[0] = carry.astype(jnp.int32)
    gate_ref[...] = gate_t
    rank_ref[...] = rank
    a = _silu(_dot(hb, wsg_ref[...])) * _dot(hb, wsu_ref[...])
    shared_ref[...] = _dot(a.astype(BF16), wsd_ref[...])


def _moe_pre(x, g, mod, w_router_t, b_router, wsg, wsu, wsd):
    return pl.pallas_call(
        _moe_pre_kernel,
        grid=(T_ALL // PRE_TM,),
        in_specs=[
            pl.BlockSpec((PRE_TM, D_MODEL), lambda i: (i, 0)),
            pl.BlockSpec((1, D_MODEL), lambda i: (0, 0)),
            pl.BlockSpec((8, D_MODEL), lambda i: (0, 3)),
            pl.BlockSpec((8, D_MODEL), lambda i: (0, 4)),
            pl.BlockSpec((N_EXPERTS, D_MODEL), lambda i: (0, 0)),
            pl.BlockSpec((N_EXPERTS, LANES), lambda i: (0, 0)),
            pl.BlockSpec((PRE_TM, PRE_TM), lambda i: (0, 0)),
            pl.BlockSpec((D_MODEL, D_SHARED), lambda i: (0, 0)),
            pl.BlockSpec((D_MODEL, D_SHARED), lambda i: (0, 0)),
            pl.BlockSpec((D_SHARED, D_MODEL), lambda i: (0, 0)),
        ],
        out_specs=[
            pl.BlockSpec((PRE_TM * ROW_SUB, LANES), lambda i: (i, 0)),
            pl.BlockSpec((N_EXPERTS, PRE_TM), lambda i: (0, i)),
            pl.BlockSpec((N_EXPERTS, PRE_TM), lambda i: (0, i)),
            pl.BlockSpec((PRE_TM, D_MODEL), lambda i: (i, 0)),
            pl.BlockSpec((1, N_EXPERTS, LANES), lambda i: (i // (MOE_TC // PRE_TM), 0, 0)),
        ],
        out_shape=[
            jax.ShapeDtypeStruct((T_ALL * ROW_SUB, LANES), F32),
            jax.ShapeDtypeStruct((N_EXPERTS, T_ALL), F32),
            jax.ShapeDtypeStruct((N_EXPERTS, T_ALL), F32),
            jax.ShapeDtypeStruct((T_ALL, D_MODEL), F32),
            jax.ShapeDtypeStruct((N_CHUNKS, N_EXPERTS, LANES), jnp.int32),
        ],
        scratch_shapes=[pltpu.VMEM((N_EXPERTS, LANES), F32)],
        compiler_params=_cparams("arbitrary"),
        name="moe_pre",
    )(x, g.reshape(1, D_MODEL), mod, mod, w_router_t,
      jnp.broadcast_to(b_router.reshape(N_EXPERTS, 1), (N_EXPERTS, LANES)),
      jnp.asarray(np.triu(np.ones((PRE_TM, PRE_TM), np.float32), 1), BF16), wsg, wsu, wsd)


LIST_ROWS = MOE_TC // LANES
TILE_ROWS = ROW_TILE // LANES
DUMMY_ROW = MOE_TC * ROW_SUB


def _moe_sort_kernel(rank_ref, gate_ref, cnt_ref, list_ref, w_ref):
    rank = rank_ref[...]
    lane = lax.broadcasted_iota(jnp.int32, rank.shape, 1)
    d = jnp.where(rank >= 0.0, lane - rank.astype(jnp.int32), 0)
    w = gate_ref[...]
    for s in range(MOE_TC.bit_length() - 1):
        k = 1 << s
        d_in = pltpu.roll(d, MOE_TC - k, 1)
        w_in = pltpu.roll(w, MOE_TC - k, 1)
        take = (d_in & k) != 0
        leave = (d & k) != 0
        d = jnp.where(take, d_in, jnp.where(leave, 0, d))
        w = jnp.where(take, w_in, w)
    valid = lane < cnt_ref[0][:, 0:1]
    row = (lane + d) * ROW_SUB
    gsrc = jnp.where(valid, row, 0)
    ssrc = jnp.where(valid, row, DUMMY_ROW)
    w = jnp.where(valid, w, 0.0)
    for b in range(LIST_ROWS):
        cols = slice(b * LANES, (b + 1) * LANES)
        list_ref[pl.ds(b, N_EXPERTS, stride=2 * LIST_ROWS), :] = gsrc[:, cols]
        list_ref[pl.ds(LIST_ROWS + b, N_EXPERTS, stride=2 * LIST_ROWS), :] = ssrc[:, cols]
        w_ref[pl.ds(b, N_EXPERTS, stride=LIST_ROWS), :] = w[:, cols]


def _moe_sort(rank_t, gate_t, cnt_b):
    chunk_spec = pl.BlockSpec((N_EXPERTS, MOE_TC), lambda c: (0, c))
    n_rows = N_CHUNKS * N_EXPERTS * LIST_ROWS
    return pl.pallas_call(
        _moe_sort_kernel,
        grid=(N_CHUNKS,),
        in_specs=[chunk_spec, chunk_spec,
                  pl.BlockSpec((1, N_EXPERTS, LANES), lambda c: (c, 0, 0))],
        out_specs=[pl.BlockSpec((N_EXPERTS * 2 * LIST_ROWS, LANES), lambda c: (c, 0)),
                   pl.BlockSpec((N_EXPERTS * LIST_ROWS, LANES), lambda c: (c, 0))],
        out_shape=[jax.ShapeDtypeStruct((2 * n_rows, LANES), jnp.int32),
                   jax.ShapeDtypeStruct((n_rows, LANES), F32)],
        compiler_params=_cparams("arbitrary"),
        name="moe_sort",
    )(rank_t, gate_t, cnt_b)


SCATTER_BATCH = 16


EXPERTS_PER_STEP = 2
EXPERT_STEPS = N_EXPERTS // EXPERTS_PER_STEP


N_PIECES = 8
PIECE_ROWS = ROW_TILE // N_PIECES


class _ExpertRefs:
    def __init__(self, list0, wt0, list_ref, wt_ref, h_ref, acc_ref, wgb_ref, wub_ref, wdb_ref):
        self.list0, self.wt0 = list0, wt0
        self.list_ref, self.wt_ref, self.h_ref, self.acc_ref = list_ref, wt_ref, h_ref, acc_ref
        self.wgb_ref, self.wub_ref, self.wdb_ref = wgb_ref, wub_ref, wdb_ref


def _gather_rows(ex, t, r0, n, xs_ref):
    base = ex.list0 + t * TILE_ROWS
    for r in range(r0, r0 + n):
        tok = pl.multiple_of(ex.list_ref[base + r // LANES, r % LANES], ROW_SUB)
        xs_ref[r // 8, pl.ds(r % 8, ROW_SUB, stride=8), :] = ex.h_ref[pl.ds(tok, ROW_SUB), :]


def _scatter_rows(ex, t, r0, n, ys_ref):
    base = ex.list0 + LIST_ROWS + t * TILE_ROWS
    for b0 in range(r0, r0 + n, SCATTER_BATCH):
        rows = range(b0, b0 + SCATTER_BATCH)
        dsts = [ex.acc_ref.at[pl.ds(pl.multiple_of(ex.list_ref[base + r // LANES, r % LANES],
                                                   ROW_SUB), ROW_SUB), :] for r in rows]
        news = [dst[...] + ys_ref[r // 8, pl.ds(r % 8, ROW_SUB, stride=8), :]
                for dst, r in zip(dsts, rows)]
        for dst, new in zip(dsts, news):
            dst[...] = new


def _tile_ffn(ex, t, xs_ref, ys_ref, side_work=None):
    kc = D_MODEL // (N_PIECES // 2)
    hg = hu = None
    for p in range(N_PIECES // 2):
        if side_work is not None:
            side_work(p)
        xk = jnp.concatenate(
            [xs_ref[:, s * 8:(s + 1) * 8, :].reshape(ROW_TILE, LANES)
             for s in range(p * kc // LANES, (p + 1) * kc // LANES)], axis=1).astype(BF16)
        dg = _dot(xk, ex.wgb_ref[p * kc:(p + 1) * kc, :])
        du = _dot(xk, ex.wub_ref[p * kc:(p + 1) * kc, :])
        hg = dg if hg is None else hg + dg
        hu = du if hu is None else hu + du
    eye = (lax.broadcasted_iota(jnp.int32, (ROW_TILE, LANES), 0) % LANES
           == lax.broadcasted_iota(jnp.int32, (ROW_TILE, LANES), 1))
    row_blk = lax.broadcasted_iota(jnp.int32, (ROW_TILE, LANES), 0) // LANES
    wrows = functools.reduce(
        lambda a, b: a + b,
        [jnp.where(row_blk == k, ex.wt_ref[pl.ds(ex.wt0 + t * TILE_ROWS + k, 1), :], 0.0)
         for k in range(TILE_ROWS)])
    wcol = jnp.sum(jnp.where(eye, wrows, 0.0), axis=1, keepdims=True)
    a = (_silu(hg) * hu * wcol).astype(BF16)
    for p in range(N_PIECES // 2):
        if side_work is not None:
            side_work(N_PIECES // 2 + p)
        y = _dot(a, ex.wdb_ref[:, p * kc:(p + 1) * kc])
        for q in range(kc // LANES):
            s = p * kc // LANES + q
            ys_ref[:, s * 8:(s + 1) * 8, :] = y[:, q * LANES:(q + 1) * LANES].reshape(
                ROW_TILE // 8, 8, LANES)


def _expert_tiles(ex, t, n, xs_refs, ys_refs):
    _gather_rows(ex, t, 0, ROW_TILE, xs_refs[0])
    for i in range(n):
        def side_work(p, i=i):
            if i + 1 < n:
                _gather_rows(ex, t + i + 1, p * PIECE_ROWS, PIECE_ROWS, xs_refs[(i + 1) % 2])
            if i >= 1:
                _scatter_rows(ex, t + i - 1, p * PIECE_ROWS, PIECE_ROWS, ys_refs[(i - 1) % 2])

        _tile_ffn(ex, t + i, xs_refs[i % 2], ys_refs[i % 2], side_work if n > 1 else None)
    _scatter_rows(ex, t + n - 1, 0, ROW_TILE, ys_refs[(n - 1) % 2])


def _moe_routed_kernel(cnt_ref, list_ref, wt_ref, h_ref, wg_ref, wu_ref, wd_ref,
                       x_ref, sh_ref, g2_ref, *refs):
    out_refs, scratch = refs[:-8], refs[-8:]
    acc_ref, wgb_ref, wub_ref, wdb_ref = scratch[:4]
    xs_refs, ys_refs = scratch[4:6], scratch[6:8]
    c = pl.program_id(0)
    j = pl.program_id(1)

    @pl.when(j == 0)
    def _():
        acc_ref[...] = jnp.zeros_like(acc_ref)

    for k in range(EXPERTS_PER_STEP):
        expert = jnp.minimum(j, EXPERT_STEPS - 1) * EXPERTS_PER_STEP + k
        count = cnt_ref[c * N_EXPERTS + expert]

        @pl.when(jnp.logical_and(j < EXPERT_STEPS, count > 0))
        def _():
            wgb_ref[...] = wg_ref[0, k].astype(BF16)
            wub_ref[...] = wu_ref[0, k].astype(BF16)
            wdb_ref[...] = wd_ref[0, k].astype(BF16)

            ex = _ExpertRefs(k * 2 * LIST_ROWS, k * LIST_ROWS, list_ref, wt_ref, h_ref, acc_ref,
                             wgb_ref, wub_ref, wdb_ref)
            n_tiles = (count + ROW_TILE - 1) // ROW_TILE

            last3 = jnp.logical_and(n_tiles % 2 == 1, n_tiles >= 3)
            n_pairs = (n_tiles - jnp.where(last3, 3, n_tiles % 2)) // 2

            def tile_pair(m, carry):
                _expert_tiles(ex, 2 * m, 2, xs_refs, ys_refs)
                return carry

            lax.fori_loop(0, n_pairs, tile_pair, 0)

            @pl.when(last3)
            def _():
                _expert_tiles(ex, n_tiles - 3, 3, xs_refs, ys_refs)

            @pl.when(n_tiles == 1)
            def _():
                _expert_tiles(ex, 0, 1, xs_refs, ys_refs)

    @pl.when(j >= EXPERT_STEPS)
    def _():
        base = (j - EXPERT_STEPS) * (TILE * ROW_SUB)
        moe = jnp.concatenate(
            [acc_ref[pl.ds(base + s, TILE, stride=ROW_SUB), :] for s in range(ROW_SUB)], axis=1)
        new_x = x_ref[...] + g2_ref[pl.ds(c, 1), :] * (moe + sh_ref[...])
        if len(out_refs) == 1:
            out_refs[0][...] = new_x
        else:
            prompt_ref, latent_ref = out_refs

            @pl.when(c < T_P // MOE_TC)
            def _():
                prompt_ref[...] = new_x

            @pl.when(c >= T_P // MOE_TC)
            def _():
                latent_ref[...] = new_x


def _moe_routed(lists, wts, counts, hrows, w_gate, w_up, w_down, li, x, shared, mod,
                split_output):
    group = lambda j: jnp.minimum(j, EXPERT_STEPS - 1)
    list_map = lambda c, j, nt: (c * EXPERT_STEPS + group(j), 0)
    w_map = lambda c, j, nt: (li, group(j), 0, 0)
    w_in_spec = pl.BlockSpec((1, EXPERTS_PER_STEP, D_MODEL, D_EXPERT), w_map)
    w_out_spec = pl.BlockSpec((1, EXPERTS_PER_STEP, D_EXPERT, D_MODEL), w_map)
    out_blk = lambda c, j: c * N_FLUSH + jnp.maximum(j - EXPERT_STEPS, 0)
    tok_spec = pl.BlockSpec((TILE, D_MODEL), lambda c, j, nt: (out_blk(c, j), 0))
    if split_output:
        out_specs = [
            pl.BlockSpec((TILE, D_MODEL), lambda c, j, nt: (jnp.minimum(out_blk(c, j), P_TILES - 1), 0)),
            pl.BlockSpec((TILE, D_MODEL), lambda c, j, nt: (jnp.maximum(out_blk(c, j) - P_TILES, 0), 0)),
        ]
        out_shape = [jax.ShapeDtypeStruct((T_P, D_MODEL), F32),
                     jax.ShapeDtypeStruct((T_S, D_MODEL), F32)]
    else:
        out_specs = tok_spec
        out_shape = jax.ShapeDtypeStruct((T_ALL, D_MODEL), F32)
    stage = pltpu.VMEM((ROW_TILE // 8, 8 * ROW_SUB, LANES), F32)
    grid_spec = pltpu.PrefetchScalarGridSpec(
        num_scalar_prefetch=1,
        grid=(N_CHUNKS, EXPERT_STEPS + N_FLUSH),
        in_specs=[
            pl.BlockSpec((EXPERTS_PER_STEP * 2 * LIST_ROWS, LANES), list_map,
                         memory_space=pltpu.SMEM),
            pl.BlockSpec((EXPERTS_PER_STEP * LIST_ROWS, LANES), list_map),
            pl.BlockSpec((MOE_TC * ROW_SUB, LANES), lambda c, j, nt: (c, 0),
                         pipeline_mode=pl.Buffered(1)),
            w_in_spec, w_in_spec, w_out_spec,
            tok_spec, tok_spec,
            pl.BlockSpec((8, D_MODEL), lambda c, j, nt: (0, 5)),
        ],
        out_specs=out_specs,
        scratch_shapes=[
            pltpu.VMEM(((MOE_TC + 1) * ROW_SUB, LANES), F32),
            pltpu.VMEM((D_MODEL, D_EXPERT), BF16),
            pltpu.VMEM((D_MODEL, D_EXPERT), BF16),
            pltpu.VMEM((D_EXPERT, D_MODEL), BF16),
            stage, stage, stage, stage,
        ],
    )
    return pl.pallas_call(
        _moe_routed_kernel,
        grid_spec=grid_spec,
        out_shape=out_shape,
        compiler_params=_cparams("arbitrary", "arbitrary", vmem_limit=VMEM_LIMIT_ROUTED),
        name="moe_routed",
    )(counts.reshape(-1), lists, wts, hrows, w_gate, w_up, w_down, x, shared, mod)


def kernel(x_prompt, x_sample, cache_k, cache_v, c, c_ctx, w_mod, b_mod, norm1, norm2, w_in_even, conv_a, q_norm, k_norm, lam_q1, lam_k1, lam_q2, lam_k2, subln, w_out_even, w_in_odd, conv_c, conv_c_b, ln_c_g, ln_c_b, w_pool, pool_scale, w_out_odd, w_router, b_router, w_gate, w_up, w_down, ws_gate, ws_up, ws_down):
    x = (x_prompt.reshape(T_P, D_MODEL), x_sample.reshape(T_S, D_MODEL))
    cond = jnp.concatenate([c_ctx[None, :], c, jnp.zeros((8 - 1 - DEC_BATCH, D_MODEL), F32)], axis=0)
    mod_all = _modulation(cond, w_mod, b_mod)

    new_k, new_v = [], []
    for li in range(DEPTH):
        mod = mod_all[li]
        if li % 2 == 0:
            e = li // 2
            lam_init = 0.8 - 0.6 * math.exp(-0.3 * li)
            u = _norm_in(x, norm1[li], mod, w_in_even[e].astype(BF16))
            q, k, v, k_cache, v_cache = _qkv_prep(u, q_norm[e], k_norm[e])
            new_k.append(k_cache)
            new_v.append(v_cache)
            o = _attention(q, k, v, cache_k, cache_v, e,
                           (lam_q1[e], lam_k1[e], lam_q2[e], lam_k2[e]), subln[e], lam_init)
            x = _even_out(x, u, o, conv_a[e], mod, w_out_even[e].astype(BF16))
        else:
            o_ = li // 2
            u = _norm_in(x, norm1[li], mod, w_in_odd[o_].astype(BF16))
            x = _odd_out(x, u, conv_c[o_], conv_c_b[o_], ln_c_g[o_], ln_c_b[o_],
                         w_pool[o_].astype(BF16), pool_scale[o_], mod, w_out_odd[o_].astype(BF16))
        hrows, gate_b, rank_b, shared, cnt_b = _moe_pre(
            x, norm2[li], mod, w_router[li].T, b_router[li], ws_gate[li].astype(BF16),
            ws_up[li].astype(BF16), ws_down[li].astype(BF16))
        lists, wts = _moe_sort(rank_b, gate_b, cnt_b)
        x = _moe_routed(lists, wts, cnt_b[:, :, 0], hrows, w_gate, w_up, w_down, li, x, shared,
                        mod, split_output=(li == DEPTH - 1))

    y_prompt = x[0].reshape(BATCH, SEQ, D_MODEL)
    y_sample = x[1].reshape(DEC_BATCH, DEC_SEQ, D_MODEL)
    return (y_prompt, y_sample, jnp.stack(new_k, axis=1), jnp.stack(new_v, axis=1))
```

```python
import functools
import math

import numpy as np
import jax
import jax.numpy as jnp
from jax import lax
from jax.experimental import pallas as pl
from jax.experimental.pallas import tpu as pltpu

D_MODEL = 1024
BATCH = 16
SEQ = 256
DEPTH = 2
DEC_BATCH = 2
DEC_SEQ = 4096
PAST_LEN = 512
GRID_W = 64
H_B = 4
DK_B = 64
DV_B = 2 * DK_B
D_A = D_MODEL // 2
D_B = H_B * DV_B
D_C = D_MODEL // 2
D_D = D_MODEL // 2
CONV_C = 31
POOL_WINDOWS = (2, 4, 8, 16)
D_DG = D_D // len(POOL_WINDOWS)
N_EXPERTS = 64
TOP_K = 8
N_GROUPS = 8
TOPK_GROUPS = 4
D_EXPERT = 256
D_SHARED = 256
ROUTED_SCALE = 2.5
ROPE_BASE = 10000.0
EPS = 1e-6

F32 = jnp.float32
BF16 = jnp.bfloat16

T_P = BATCH * SEQ
T_S = DEC_BATCH * DEC_SEQ
T_ALL = T_P + T_S
TILE = 256
N_TILES = T_ALL // TILE
P_TILES = T_P // TILE
S_TILES = DEC_SEQ // TILE
LANES = 128
VMEM_LIMIT = 56 * 1024 * 1024
VMEM_LIMIT_ROUTED = 60 * 1024 * 1024


def _cparams(*sem, vmem_limit=VMEM_LIMIT):
    return pltpu.CompilerParams(dimension_semantics=sem, vmem_limit_bytes=vmem_limit)


def _mod_row(i, tm):
    npt = T_P // tm
    per = DEC_SEQ // tm
    return jnp.where(i < npt, 0, 1 + (i - npt) // per)


def _seq_flags(i):
    j = (i - P_TILES) % S_TILES
    first = jnp.logical_or(i < P_TILES, j == 0)
    last = jnp.logical_or(i < P_TILES, j == S_TILES - 1)
    return first, last


def _seq_tile(i):
    return jnp.where(i < P_TILES, 0, (i - P_TILES) % S_TILES)


def _split_bf16(a):
    hi = a.astype(BF16)
    lo = (a - hi.astype(F32)).astype(BF16)
    return hi, lo


def _dot(a, b):
    return jnp.dot(a, b, preferred_element_type=F32)


def _dot_nt(a, b):
    return lax.dot_general(a, b, (((1,), (1,)), ((), ())), preferred_element_type=F32)


def _dot3(a, b):
    a_hi, a_lo = _split_bf16(a)
    b_hi, b_lo = _split_bf16(b)
    return _dot(a_hi, b_hi) + _dot(a_lo, b_hi) + _dot(a_hi, b_lo)


def _silu(x):
    return x * jax.nn.sigmoid(x)


MOD_TN = 1536


def _mod_kernel(c_ref, w_ref, b_ref, o_ref):
    o_ref[0] = _dot3(_silu(c_ref[...]), w_ref[0]) + b_ref[0]


def _modulation(cond, w_mod, b_mod):
    n = 6 * D_MODEL
    return pl.pallas_call(
        _mod_kernel,
        grid=(DEPTH, n // MOD_TN),
        in_specs=[
            pl.BlockSpec((8, D_MODEL), lambda l, j: (0, 0)),
            pl.BlockSpec((1, D_MODEL, MOD_TN), lambda l, j: (l, 0, j)),
            pl.BlockSpec((1, 1, MOD_TN), lambda l, j: (l, 0, j)),
        ],
        out_specs=pl.BlockSpec((1, 8, MOD_TN), lambda l, j: (l, 0, j)),
        out_shape=jax.ShapeDtypeStruct((DEPTH, 8, n), F32),
        compiler_params=_cparams("arbitrary", "arbitrary"),
        name="modulation",
    )(cond, w_mod, b_mod.reshape(DEPTH, 1, n))


IN_TM = 1024


def _modulated_norm(x, g, shift, scale):
    ms = jnp.mean(x * x, axis=-1, keepdims=True)
    return (x * lax.rsqrt(ms + EPS) * g) * (1.0 + scale) + shift


def _tok_specs(x, tm, width):
    if isinstance(x, tuple):
        n_p = T_P // tm
        return ([pl.BlockSpec((tm, width), lambda i: (jnp.minimum(i, n_p - 1), 0)),
                 pl.BlockSpec((tm, width), lambda i: (jnp.maximum(i - n_p, 0), 0))], list(x))
    return [pl.BlockSpec((tm, width), lambda i: (i, 0))], [x]


def _tok_load(refs, i, tm):
    if len(refs) == 2:
        return jnp.where(i < T_P // tm, refs[0][...], refs[1][...])
    return refs[0][...]


def _norm_in_kernel(*refs, n_x):
    x_refs, (g_ref, sh_ref, sc_ref, w_ref, o_ref) = refs[:n_x], refs[n_x:]
    i = pl.program_id(0)
    r = _mod_row(i, IN_TM)
    h = _modulated_norm(_tok_load(x_refs, i, IN_TM), g_ref[...], sh_ref[pl.ds(r, 1), :],
                        sc_ref[pl.ds(r, 1), :])
    o_ref[...] = _dot(h.astype(BF16), w_ref[...])


def _norm_in(x, g, mod, w_bf16):
    n = w_bf16.shape[1]
    x_specs, x_ops = _tok_specs(x, IN_TM, D_MODEL)
    return pl.pallas_call(
        functools.partial(_norm_in_kernel, n_x=len(x_ops)),
        grid=(T_ALL // IN_TM,),
        in_specs=x_specs + [
            pl.BlockSpec((1, D_MODEL), lambda i: (0, 0)),
            pl.BlockSpec((8, D_MODEL), lambda i: (0, 0)),
            pl.BlockSpec((8, D_MODEL), lambda i: (0, 1)),
            pl.BlockSpec((D_MODEL, n), lambda i: (0, 0)),
        ],
        out_specs=pl.BlockSpec((IN_TM, n), lambda i: (i, 0)),
        out_shape=jax.ShapeDtypeStruct((T_ALL, n), F32),
        compiler_params=_cparams("arbitrary"),
        name="norm_in_proj",
    )(*x_ops, g.reshape(1, D_MODEL), mod, mod, w_bf16)


QKV_TM = 1024


def _rope_tables():
    half = DK_B // 2
    freqs = ROPE_BASE ** (-np.arange(0, half, 2, dtype=np.float64) / half)
    l = np.arange(DEC_SEQ)
    pos_r = (l // GRID_W).astype(np.float64)
    pos_c = (l % GRID_W).astype(np.float64)
    lane = np.arange(LANES)
    jj = lane % DK_B
    m = jj % half
    f = m % (half // 2)
    pos = np.where((jj < half)[None, :], pos_r[:, None], pos_c[:, None])
    ang = pos * freqs[f][None, :]
    sign = np.where(m < half // 2, -1.0, 1.0)[None, :]
    cos = np.concatenate([np.ones((QKV_TM, LANES)), np.cos(ang)], axis=0)
    sin = np.concatenate([np.zeros((QKV_TM, LANES)), sign * np.sin(ang)], axis=0)
    return cos.astype(np.float32), sin.astype(np.float32)


def _segment_mean_matrix():
    lane = np.arange(LANES)
    same = (lane[:, None] // DK_B) == (lane[None, :] // DK_B)
    return (same.astype(np.float32) / DK_B)


def _qk_prep(x, g, cos, sin, seg):
    x2 = x * x
    hi, lo = _split_bf16(x2)
    ms = _dot(hi, seg) + _dot(lo, seg)
    y = x * lax.rsqrt(ms + EPS) * g
    lane = lax.broadcasted_iota(jnp.int32, y.shape, 1)
    lower = (lane % (DK_B // 2)) < (DK_B // 4)
    partner = jnp.where(lower, pltpu.roll(y, LANES - DK_B // 4, 1), pltpu.roll(y, DK_B // 4, 1))
    return y * cos + partner * sin


def _qkv_prep_kernel(q_ref, k_ref, v_ref, qn_ref, kn_ref, cos_ref, sin_ref, seg_ref,
                     qo_ref, ko_ref, vo_ref, kc_ref, vc_ref):
    cos = cos_ref[...]
    sin = sin_ref[...]
    seg = seg_ref[...]
    scale = math.log2(math.e) / math.sqrt(DK_B)
    is_prompt = pl.program_id(0) < T_P // QKV_TM
    for h in range(H_B):
        cols = slice(h * LANES, (h + 1) * LANES)
        qo_ref[h] = (_qk_prep(q_ref[:, cols], qn_ref[...], cos, sin, seg) * scale).astype(BF16)
        k = _qk_prep(k_ref[:, cols], kn_ref[...], cos, sin, seg)
        v = v_ref[:, cols]
        ko_ref[h] = k.astype(BF16)
        vo_ref[h] = v.astype(BF16)

        @pl.when(is_prompt)
        def _():
            for b in range(QKV_TM // SEQ):
                kc_ref[b, h] = k[b * SEQ:(b + 1) * SEQ]
                vc_ref[b, h] = v[b * SEQ:(b + 1) * SEQ]


def _qkv_prep(u, qn, kn):
    cos, sin = _rope_tables()
    seg = jnp.asarray(_segment_mean_matrix(), BF16)
    qn2 = jnp.concatenate([qn, qn]).reshape(1, LANES)
    kn2 = jnp.concatenate([kn, kn]).reshape(1, LANES)
    col0 = 3 * D_A // D_B
    p_steps = T_P // QKV_TM
    s_steps = DEC_SEQ // QKV_TM

    def tab_map(i):
        return (jnp.where(i < p_steps, 0, 1 + (i - p_steps) % s_steps), 0)

    out_b = jax.ShapeDtypeStruct((H_B, T_ALL, LANES), BF16)
    out_spec = pl.BlockSpec((H_B, QKV_TM, LANES), lambda i: (0, i, 0))
    cache = jax.ShapeDtypeStruct((BATCH, H_B, SEQ, LANES), F32)
    cache_spec = pl.BlockSpec((QKV_TM // SEQ, H_B, SEQ, LANES),
                              lambda i: (jnp.minimum(i, p_steps - 1), 0, 0, 0))
    return pl.pallas_call(
        _qkv_prep_kernel,
        grid=(T_ALL // QKV_TM,),
        in_specs=[
            pl.BlockSpec((QKV_TM, D_B), lambda i: (i, col0)),
            pl.BlockSpec((QKV_TM, D_B), lambda i: (i, col0 + 1)),
            pl.BlockSpec((QKV_TM, D_B), lambda i: (i, col0 + 2)),
            pl.BlockSpec((1, LANES), lambda i: (0, 0)),
            pl.BlockSpec((1, LANES), lambda i: (0, 0)),
            pl.BlockSpec((QKV_TM, LANES), tab_map),
            pl.BlockSpec((QKV_TM, LANES), tab_map),
            pl.BlockSpec((LANES, LANES), lambda i: (0, 0)),
        ],
        out_specs=[out_spec] * 3 + [cache_spec] * 2,
        out_shape=[out_b, out_b, out_b, cache, cache],
        compiler_params=_cparams("arbitrary"),
        name="qkv_prep",
    )(u, u, u, qn2, kn2, jnp.asarray(cos), jnp.asarray(sin), seg)


def _lambda(lq1_ref, lk1_ref, lq2_ref, lk2_ref, lam_init):
    a = jnp.sum(lq1_ref[...] * lk1_ref[...], axis=-1, keepdims=True)
    b = jnp.sum(lq2_ref[...] * lk2_ref[...], axis=-1, keepdims=True)
    return jnp.exp(a) - jnp.exp(b) + lam_init


ATTN_TQ = 512


def _attn_body(q, keys, vals, lam, subg, lam_init):
    lane = lax.broadcasted_iota(jnp.int32, q.shape, 1)
    zero = jnp.zeros_like(q)
    qa = jnp.where(lane < DK_B, q, zero)
    qb = jnp.where(lane < DK_B, zero, q)
    tq = q.shape[0]
    stacked = [_dot_nt(jnp.concatenate([qa, qb], axis=0), k) for k in keys]
    scores = [[s[:tq] for s in stacked], [s[tq:] for s in stacked]]
    outs = []
    for ss in scores:
        m = functools.reduce(jnp.maximum, [jnp.max(s, axis=-1, keepdims=True) for s in ss])
        ps = [jnp.exp2(s - m) for s in ss]
        l = functools.reduce(jnp.add, [jnp.sum(p, axis=-1, keepdims=True) for p in ps])
        pv = functools.reduce(jnp.add, [_dot(p.astype(BF16), v) for p, v in zip(ps, vals)])
        outs.append(pv / l)
    o = outs[0] - lam * outs[1]
    ms = jnp.mean(o * o, axis=-1, keepdims=True)
    return (o * lax.rsqrt(ms + EPS) * subg) * (1.0 - lam_init)


def _attn_prompt_kernel(q_ref, k_ref, v_ref, lq1, lk1, lq2, lk2, sg_ref, o_ref, *, lam_init):
    lam = _lambda(lq1, lk1, lq2, lk2, lam_init)
    for h in range(H_B):
        o_ref[:, h * DV_B:(h + 1) * DV_B] = _attn_body(
            q_ref[h], [k_ref[h]], [v_ref[h]], lam, sg_ref[...], lam_init)


def _attn_latent_kernel(q_ref, k_ref, v_ref, ck_ref, cv_ref, lq1, lk1, lq2, lk2, sg_ref, o_ref,
                        *, lam_init):
    lam = _lambda(lq1, lk1, lq2, lk2, lam_init)
    keys = [ck_ref[0, 0, 0].astype(BF16), k_ref[0]]
    vals = [cv_ref[0, 0, 0].astype(BF16), v_ref[0]]
    o_ref[...] = _attn_body(q_ref[0], keys, vals, lam, sg_ref[...], lam_init)


def _attention(q, k, v, cache_k, cache_v, e, lam_params, subg, lam_init):
    small = [p.reshape(1, DK_B) for p in lam_params] + [subg.reshape(1, DV_B)]
    small_specs2 = [pl.BlockSpec((1, DK_B), lambda b: (0, 0))] * 4 + \
                   [pl.BlockSpec((1, DV_B), lambda b: (0, 0))]
    small_specs3 = [pl.BlockSpec((1, DK_B), lambda s, h, j: (0, 0))] * 4 + \
                   [pl.BlockSpec((1, DV_B), lambda s, h, j: (0, 0))]
    o_prompt = pl.pallas_call(
        functools.partial(_attn_prompt_kernel, lam_init=lam_init),
        grid=(BATCH,),
        in_specs=[pl.BlockSpec((H_B, SEQ, LANES), lambda b: (0, b, 0))] * 3 + small_specs2,
        out_specs=pl.BlockSpec((SEQ, D_B), lambda b: (b, 0)),
        out_shape=jax.ShapeDtypeStruct((T_P, D_B), F32),
        compiler_params=_cparams("arbitrary"),
        name="attn_prompt",
    )(q, k, v, *small)

    q_tiles = DEC_SEQ // ATTN_TQ
    kv_spec = pl.BlockSpec((1, DEC_SEQ, LANES), lambda s, h, j: (h, T_P // DEC_SEQ + s, 0))
    c_spec = pl.BlockSpec((1, 1, 1, PAST_LEN, LANES), lambda s, h, j: (s, e, h, 0, 0))
    o_latent = pl.pallas_call(
        functools.partial(_attn_latent_kernel, lam_init=lam_init),
        grid=(DEC_BATCH, H_B, q_tiles),
        in_specs=[pl.BlockSpec((1, ATTN_TQ, LANES),
                               lambda s, h, j: (h, T_P // ATTN_TQ + s * q_tiles + j, 0)),
                  kv_spec, kv_spec, c_spec, c_spec] + small_specs3,
        out_specs=pl.BlockSpec((ATTN_TQ, LANES), lambda s, h, j: (s * q_tiles + j, h)),
        out_shape=jax.ShapeDtypeStruct((T_S, D_B), F32),
        compiler_params=_cparams("arbitrary", "arbitrary", "arbitrary"),
        name="attn_latent",
    )(q, k, v, cache_k, cache_v, *small)
    return o_prompt, o_latent


HALO8 = 8
HALO16 = 16


def _prev_block(i, rows):
    return jnp.maximum(i * (TILE // rows) - 1, 0)


def _next_block(i, rows):
    return jnp.minimum((i + 1) * (TILE // rows), T_ALL // rows - 1)


def _even_out_kernel(*refs, n_x, n_o):
    x_refs, o_refs = refs[:n_x], refs[n_x:n_x + n_o]
    (bg_ref, cg_ref, hin_ref, cgp_ref, hinp_ref, cgn_ref, hinn_ref, cw_ref, g1_ref, w_ref,
     out_ref) = refs[n_x + n_o:]
    i = pl.program_id(0)
    first, last = _seq_flags(i)
    r = _mod_row(i, TILE)
    z = cg_ref[...] * hin_ref[...]
    zp = jnp.where(first, 0.0, cgp_ref[HALO8 - 1:HALO8, :] * hinp_ref[HALO8 - 1:HALO8, :])
    zn = jnp.where(last, 0.0, cgn_ref[0:1, :] * hinn_ref[0:1, :])
    row = lax.broadcasted_iota(jnp.int32, z.shape, 0)
    z_prev = jnp.where(row == 0, zp, pltpu.roll(z, 1, 0))
    z_next = jnp.where(row == TILE - 1, zn, pltpu.roll(z, TILE - 1, 0))
    cw = cw_ref[...]
    ya = bg_ref[...] * (cw[0:1, :] * z_prev + cw[1:2, :] * z + cw[2:3, :] * z_next)
    o = _tok_load(o_refs, i, TILE)
    y = _dot(ya.astype(BF16), w_ref[0:D_A, :]) + _dot(o.astype(BF16), w_ref[D_A:, :])
    out_ref[...] = _tok_load(x_refs, i, TILE) + g1_ref[pl.ds(r, 1), :] * y


def _even_out(x, u, o, conv_w, mod, w_out_bf16):
    tile_spec = lambda c: pl.BlockSpec((TILE, D_A), lambda i: (i, c))
    prev_spec = lambda c: pl.BlockSpec((HALO8, D_A), lambda i: (_prev_block(i, HALO8), c))
    next_spec = lambda c: pl.BlockSpec((HALO8, D_A), lambda i: (_next_block(i, HALO8), c))
    x_specs, x_ops = _tok_specs(x, TILE, D_MODEL)
    o_specs, o_ops = _tok_specs(o, TILE, D_B)
    return pl.pallas_call(
        functools.partial(_even_out_kernel, n_x=len(x_ops), n_o=len(o_ops)),
        grid=(N_TILES,),
        in_specs=x_specs + o_specs + [
            tile_spec(0), tile_spec(1), tile_spec(2),
            prev_spec(1), prev_spec(2), next_spec(1), next_spec(2),
            pl.BlockSpec((3, D_A), lambda i: (0, 0)),
            pl.BlockSpec((8, D_MODEL), lambda i: (0, 2)),
            pl.BlockSpec((D_MODEL, D_MODEL), lambda i: (0, 0)),
        ],
        out_specs=pl.BlockSpec((TILE, D_MODEL), lambda i: (i, 0)),
        out_shape=jax.ShapeDtypeStruct((T_ALL, D_MODEL), F32),
        compiler_params=_cparams("arbitrary"),
        name="even_mixer_out",
    )(*x_ops, *o_ops, u, u, u, u, u, u, u, conv_w, mod, w_out_bf16)


def _odd_out_kernel(x_ref, a_ref, b_ref, pd_ref, ap_ref, bp_ref, an_ref, bn_ref, pp_ref, pn_ref,
                    cw_ref, cb_ref, lg_ref, lb_ref, wp_ref, ps_ref, g1_ref, w_ref, out_ref,
                    ext_ref, extp_ref, shift_ref):
    i = pl.program_id(0)
    first, last = _seq_flags(i)
    r = _mod_row(i, TILE)
    ext_ref[0:HALO16, :] = jnp.where(first, 0.0, ap_ref[...] * jax.nn.sigmoid(bp_ref[...]))
    ext_ref[HALO16:HALO16 + TILE, :] = a_ref[...] * jax.nn.sigmoid(b_ref[...])
    ext_ref[HALO16 + TILE:, :] = jnp.where(last, 0.0, an_ref[...] * jax.nn.sigmoid(bn_ref[...]))
    base = HALO16 - CONV_C // 2
    parts = []
    for cb in range(D_C // LANES):
        cols = slice(cb * LANES, (cb + 1) * LANES)
        acc = jnp.zeros((TILE, LANES), F32)
        for phase in range(8):
            taps = [j for j in range(CONV_C) if (base + j) % 8 == phase]
            reach = max((base + j) // 8 for j in taps)
            rows = TILE + 8 * reach
            shift_ref[0:rows, :] = ext_ref[pl.ds(phase, rows), cols]
            for j in taps:
                a = (base + j) // 8
                acc = acc + cw_ref[j:j + 1, cols] * shift_ref[8 * a:8 * a + TILE, :]
        parts.append(acc)
    g = jnp.concatenate(parts, axis=-1) + cb_ref[...]
    mu = jnp.mean(g, axis=-1, keepdims=True)
    var = jnp.mean(jnp.square(g - mu), axis=-1, keepdims=True)
    g = _silu(((g - mu) * lax.rsqrt(var + EPS)) * lg_ref[...] + lb_ref[...])
    extp_ref[0:HALO8, :] = jnp.where(first, 0.0, pp_ref[...])
    extp_ref[HALO8:HALO8 + TILE, :] = pd_ref[...]
    extp_ref[HALO8 + TILE:, :] = jnp.where(last, 0.0, pn_ref[...])
    seq_len = jnp.where(i < P_TILES, SEQ, DEC_SEQ)
    pos = _seq_tile(i) * TILE + lax.broadcasted_iota(jnp.int32, (TILE, 1), 0)
    yd = []
    for gi, w in enumerate(POOL_WINDOWS):
        cols = slice(gi * D_DG, (gi + 1) * D_DG)
        s = jnp.zeros((TILE, D_DG), F32)
        for d in range(-(w // 2), w - w // 2):
            s = s + extp_ref[pl.ds(HALO8 + d, TILE), cols]
        lo = jnp.maximum(pos - w // 2, 0)
        hi = jnp.minimum(pos - w // 2 + w, seq_len)
        pooled = s / (hi - lo).astype(F32) - pd_ref[:, cols]
        yd.append(_dot(pooled.astype(BF16), wp_ref[gi]))
    yd = jnp.concatenate(yd, axis=-1) * ps_ref[...]
    y = _dot(g.astype(BF16), w_ref[0:D_C, :]) + _dot(yd.astype(BF16), w_ref[D_C:, :])
    out_ref[...] = x_ref[...] + g1_ref[pl.ds(r, 1), :] * y


def _odd_out(x, u, conv_w, conv_b, ln_g, ln_b, w_pool_bf16, p_scale, mod, w_out_bf16):
    tile_spec = lambda c: pl.BlockSpec((TILE, D_C), lambda i: (i, c))
    prev_spec = lambda rows, c: pl.BlockSpec((rows, D_C), lambda i: (_prev_block(i, rows), c))
    next_spec = lambda rows, c: pl.BlockSpec((rows, D_C), lambda i: (_next_block(i, rows), c))
    vec = lambda: pl.BlockSpec((1, D_C), lambda i: (0, 0))
    return pl.pallas_call(
        _odd_out_kernel,
        grid=(N_TILES,),
        in_specs=[
            pl.BlockSpec((TILE, D_MODEL), lambda i: (i, 0)),
            tile_spec(0), tile_spec(1), tile_spec(2),
            prev_spec(HALO16, 0), prev_spec(HALO16, 1), next_spec(HALO16, 0), next_spec(HALO16, 1),
            prev_spec(HALO8, 2), next_spec(HALO8, 2),
            pl.BlockSpec((CONV_C, D_C), lambda i: (0, 0)),
            vec(), vec(), vec(),
            pl.BlockSpec((len(POOL_WINDOWS), D_DG, D_DG), lambda i: (0, 0, 0)),
            vec(),
            pl.BlockSpec((8, D_MODEL), lambda i: (0, 2)),
            pl.BlockSpec((D_MODEL, D_MODEL), lambda i: (0, 0)),
        ],
        out_specs=pl.BlockSpec((TILE, D_MODEL), lambda i: (i, 0)),
        out_shape=jax.ShapeDtypeStruct((T_ALL, D_MODEL), F32),
        scratch_shapes=[pltpu.VMEM((TILE + 2 * HALO16, D_C), F32),
                        pltpu.VMEM((TILE + 2 * HALO8, D_D), F32),
                        pltpu.VMEM((TILE + 2 * HALO16, LANES), F32)],
        compiler_params=_cparams("arbitrary"),
        name="odd_mixer_out",
    )(x, u, u, u, u, u, u, u, u, u, conv_w, conv_b.reshape(1, D_C), ln_g.reshape(1, D_C),
      ln_b.reshape(1, D_C), w_pool_bf16, p_scale.reshape(1, D_D), mod, w_out_bf16)


GROUP = N_EXPERTS // N_GROUPS
NEG_INF = float("-inf")


def _first_argmax(v, idx, axis):
    m = jnp.max(v, axis=axis, keepdims=True)
    big = jnp.int32(2 ** 30)
    am = jnp.min(jnp.where(v == m, idx, big), axis=axis, keepdims=True)
    return m, am


def _route(scores, biased):
    shape = biased.shape
    member = lax.broadcasted_iota(jnp.int32, shape, 1)
    m1, a1 = _first_argmax(biased, member, 1)
    m2 = jnp.max(jnp.where(member == a1, NEG_INF, biased), axis=1, keepdims=True)
    gscore = m1 + m2
    gidx = lax.broadcasted_iota(jnp.int32, gscore.shape, 0)
    gsel = jnp.zeros(gscore.shape, jnp.bool_)
    for _ in range(TOPK_GROUPS):
        _, am = _first_argmax(gscore, gidx, 0)
        hit = gidx == am
        gsel = jnp.logical_or(gsel, hit)
        gscore = jnp.where(hit, NEG_INF, gscore)
    cand = jnp.where(gsel, biased, NEG_INF)
    eidx = lax.broadcasted_iota(jnp.int32, shape, 0) * GROUP + member
    sel = jnp.zeros(shape, jnp.bool_)
    for _ in range(TOP_K):
        m = jnp.max(jnp.max(cand, axis=1, keepdims=True), axis=0, keepdims=True)
        big = jnp.int32(2 ** 30)
        am = jnp.where(cand == m, eidx, big)
        am = jnp.min(jnp.min(am, axis=1, keepdims=True), axis=0, keepdims=True)
        hit = eidx == am
        sel = jnp.logical_or(sel, hit)
        cand = jnp.where(hit, NEG_INF, cand)
    wsel = jnp.where(sel, scores, 0.0)
    tot = jnp.sum(jnp.sum(wsel, axis=1, keepdims=True), axis=0, keepdims=True)
    return wsel / tot * ROUTED_SCALE, sel


MOE_TC = 4096
N_CHUNKS = T_ALL // MOE_TC
ROW_TILE = 256
N_FLUSH = MOE_TC // TILE
PRE_TM = 1024
ROW_SUB = D_MODEL // LANES


def _moe_pre_kernel(x_ref, g_ref, sh_ref, sc_ref, wr_ref, br_ref, tri_ref, wsg_ref, wsu_ref,
                    wsd_ref, hrow_ref, gate_ref, rank_ref, shared_ref, cnt_ref, carry_ref):
    i = pl.program_id(0)
    r = _mod_row(i, PRE_TM)
    h = _modulated_norm(x_ref[...], g_ref[...], sh_ref[pl.ds(r, 1), :], sc_ref[pl.ds(r, 1), :])
    hb = h.astype(BF16)
    for s in range(ROW_SUB):
        hrow_ref[pl.ds(s, PRE_TM, stride=ROW_SUB), :] = h[:, s * LANES:(s + 1) * LANES]
    h_hi, h_lo = hb, (h - hb.astype(F32)).astype(BF16)
    w_hi, w_lo = _split_bf16(wr_ref[...])
    logits = _dot_nt(w_hi, h_hi) + _dot_nt(w_lo, h_hi) + _dot_nt(w_hi, h_lo)
    scores = jax.nn.sigmoid(logits)
    biased = scores + br_ref[:, 0:1]
    shape3 = (N_GROUPS, GROUP, PRE_TM)
    gate_t, sel = _route(scores.reshape(shape3), biased.reshape(shape3))
    gate_t = gate_t.reshape(N_EXPERTS, PRE_TM)
    sel = jnp.where(sel.reshape(N_EXPERTS, PRE_TM), 1.0, 0.0)

    @pl.when(i % (MOE_TC // PRE_TM) == 0)
    def _():
        carry_ref[...] = jnp.zeros_like(carry_ref)

    carry = carry_ref[...]
    local = _dot(sel.astype(BF16), tri_ref[...])
    rank = jnp.where(sel > 0.0, local + jnp.concatenate([carry] * (PRE_TM // LANES), axis=1), -1.0)
    carry = carry + jnp.sum(sel, axis=1, keepdims=True)
    carry_ref[...] = carry

    @pl.when(i % (MOE_TC // PRE_TM) == MOE_TC // PRE_TM - 1)
    def _():
        cnt_ref[0] = carry.astype(jnp.int32)
    gate_ref[...] = gate_t
    rank_ref[...] = rank
    a = _silu(_dot(hb, wsg_ref[...])) * _dot(hb, wsu_ref[...])
    shared_ref[...] = _dot(a.astype(BF16), wsd_ref[...])


def _moe_pre(x, g, mod, w_router_t, b_router, wsg, wsu, wsd):
    return pl.pallas_call(
        _moe_pre_kernel,
        grid=(T_ALL // PRE_TM,),
        in_specs=[
            pl.BlockSpec((PRE_TM, D_MODEL), lambda i: (i, 0)),
            pl.BlockSpec((1, D_MODEL), lambda i: (0, 0)),
            pl.BlockSpec((8, D_MODEL), lambda i: (0, 3)),
            pl.BlockSpec((8, D_MODEL), lambda i: (0, 4)),
            pl.BlockSpec((N_EXPERTS, D_MODEL), lambda i: (0, 0)),
            pl.BlockSpec((N_EXPERTS, LANES), lambda i: (0, 0)),
            pl.BlockSpec((PRE_TM, PRE_TM), lambda i: (0, 0)),
            pl.BlockSpec((D_MODEL, D_SHARED), lambda i: (0, 0)),
            pl.BlockSpec((D_MODEL, D_SHARED), lambda i: (0, 0)),
            pl.BlockSpec((D_SHARED, D_MODEL), lambda i: (0, 0)),
        ],
        out_specs=[
            pl.BlockSpec((PRE_TM * ROW_SUB, LANES), lambda i: (i, 0)),
            pl.BlockSpec((N_EXPERTS, PRE_TM), lambda i: (0, i)),
            pl.BlockSpec((N_EXPERTS, PRE_TM), lambda i: (0, i)),
            pl.BlockSpec((PRE_TM, D_MODEL), lambda i: (i, 0)),
            pl.BlockSpec((1, N_EXPERTS, LANES), lambda i: (i // (MOE_TC // PRE_TM), 0, 0)),
        ],
        out_shape=[
            jax.ShapeDtypeStruct((T_ALL * ROW_SUB, LANES), F32),
            jax.ShapeDtypeStruct((N_EXPERTS, T_ALL), F32),
            jax.ShapeDtypeStruct((N_EXPERTS, T_ALL), F32),
            jax.ShapeDtypeStruct((T_ALL, D_MODEL), F32),
            jax.ShapeDtypeStruct((N_CHUNKS, N_EXPERTS, LANES), jnp.int32),
        ],
        scratch_shapes=[pltpu.VMEM((N_EXPERTS, LANES), F32)],
        compiler_params=_cparams("arbitrary"),
        name="moe_pre",
    )(x, g.reshape(1, D_MODEL), mod, mod, w_router_t,
      jnp.broadcast_to(b_router.reshape(N_EXPERTS, 1), (N_EXPERTS, LANES)),
      jnp.asarray(np.triu(np.ones((PRE_TM, PRE_TM), np.float32), 1), BF16), wsg, wsu, wsd)


LIST_ROWS = MOE_TC // LANES
TILE_ROWS = ROW_TILE // LANES
DUMMY_ROW = MOE_TC * ROW_SUB


def _moe_sort_kernel(rank_ref, gate_ref, cnt_ref, list_ref, w_ref):
    rank = rank_ref[...]
    lane = lax.broadcasted_iota(jnp.int32, rank.shape, 1)
    d = jnp.where(rank >= 0.0, lane - rank.astype(jnp.int32), 0)
    w = gate_ref[...]
    for s in range(MOE_TC.bit_length() - 1):
        k = 1 << s
        d_in = pltpu.roll(d, MOE_TC - k, 1)
        w_in = pltpu.roll(w, MOE_TC - k, 1)
        take = (d_in & k) != 0
        leave = (d & k) != 0
        d = jnp.where(take, d_in, jnp.where(leave, 0, d))
        w = jnp.where(take, w_in, w)
    valid = lane < cnt_ref[0][:, 0:1]
    row = (lane + d) * ROW_SUB
    gsrc = jnp.where(valid, row, 0)
    ssrc = jnp.where(valid, row, DUMMY_ROW)
    w = jnp.where(valid, w, 0.0)
    for b in range(LIST_ROWS):
        cols = slice(b * LANES, (b + 1) * LANES)
        list_ref[pl.ds(b, N_EXPERTS, stride=2 * LIST_ROWS), :] = gsrc[:, cols]
        list_ref[pl.ds(LIST_ROWS + b, N_EXPERTS, stride=2 * LIST_ROWS), :] = ssrc[:, cols]
        w_ref[pl.ds(b, N_EXPERTS, stride=LIST_ROWS), :] = w[:, cols]


def _moe_sort(rank_t, gate_t, cnt_b):
    chunk_spec = pl.BlockSpec((N_EXPERTS, MOE_TC), lambda c: (0, c))
    n_rows = N_CHUNKS * N_EXPERTS * LIST_ROWS
    return pl.pallas_call(
        _moe_sort_kernel,
        grid=(N_CHUNKS,),
        in_specs=[chunk_spec, chunk_spec,
                  pl.BlockSpec((1, N_EXPERTS, LANES), lambda c: (c, 0, 0))],
        out_specs=[pl.BlockSpec((N_EXPERTS * 2 * LIST_ROWS, LANES), lambda c: (c, 0)),
                   pl.BlockSpec((N_EXPERTS * LIST_ROWS, LANES), lambda c: (c, 0))],
        out_shape=[jax.ShapeDtypeStruct((2 * n_rows, LANES), jnp.int32),
                   jax.ShapeDtypeStruct((n_rows, LANES), F32)],
        compiler_params=_cparams("arbitrary"),
        name="moe_sort",
    )(rank_t, gate_t, cnt_b)


SCATTER_BATCH = 16


EXPERTS_PER_STEP = 2
EXPERT_STEPS = N_EXPERTS // EXPERTS_PER_STEP


N_PIECES = 8
PIECE_ROWS = ROW_TILE // N_PIECES


class _ExpertRefs:
    def __init__(self, list0, wt0, list_ref, wt_ref, h_ref, acc_ref, wgb_ref, wub_ref, wdb_ref):
        self.list0, self.wt0 = list0, wt0
        self.list_ref, self.wt_ref, self.h_ref, self.acc_ref = list_ref, wt_ref, h_ref, acc_ref
        self.wgb_ref, self.wub_ref, self.wdb_ref = wgb_ref, wub_ref, wdb_ref


def _gather_rows(ex, t, r0, n, xs_ref):
    base = ex.list0 + t * TILE_ROWS
    for r in range(r0, r0 + n):
        tok = pl.multiple_of(ex.list_ref[base + r // LANES, r % LANES], ROW_SUB)
        xs_ref[r // 8, pl.ds(r % 8, ROW_SUB, stride=8), :] = ex.h_ref[pl.ds(tok, ROW_SUB), :]


def _scatter_rows(ex, t, r0, n, ys_ref):
    base = ex.list0 + LIST_ROWS + t * TILE_ROWS
    for b0 in range(r0, r0 + n, SCATTER_BATCH):
        rows = range(b0, b0 + SCATTER_BATCH)
        dsts = [ex.acc_ref.at[pl.ds(pl.multiple_of(ex.list_ref[base + r // LANES, r % LANES],
                                                   ROW_SUB), ROW_SUB), :] for r in rows]
        news = [dst[...] + ys_ref[r // 8, pl.ds(r % 8, ROW_SUB, stride=8), :]
                for dst, r in zip(dsts, rows)]
        for dst, new in zip(dsts, news):
            dst[...] = new


def _tile_ffn(ex, t, xs_ref, ys_ref, side_work=None):
    kc = D_MODEL // (N_PIECES // 2)
    hg = hu = None
    for p in range(N_PIECES // 2):
        if side_work is not None:
            side_work(p)
        xk = jnp.concatenate(
            [xs_ref[:, s * 8:(s + 1) * 8, :].reshape(ROW_TILE, LANES)
             for s in range(p * kc // LANES, (p + 1) * kc // LANES)], axis=1).astype(BF16)
        dg = _dot(xk, ex.wgb_ref[p * kc:(p + 1) * kc, :])
        du = _dot(xk, ex.wub_ref[p * kc:(p + 1) * kc, :])
        hg = dg if hg is None else hg + dg
        hu = du if hu is None else hu + du
    eye = (lax.broadcasted_iota(jnp.int32, (ROW_TILE, LANES), 0) % LANES
           == lax.broadcasted_iota(jnp.int32, (ROW_TILE, LANES), 1))
    row_blk = lax.broadcasted_iota(jnp.int32, (ROW_TILE, LANES), 0) // LANES
    wrows = functools.reduce(
        lambda a, b: a + b,
        [jnp.where(row_blk == k, ex.wt_ref[pl.ds(ex.wt0 + t * TILE_ROWS + k, 1), :], 0.0)
         for k in range(TILE_ROWS)])
    wcol = jnp.sum(jnp.where(eye, wrows, 0.0), axis=1, keepdims=True)
    a = (_silu(hg) * hu * wcol).astype(BF16)
    for p in range(N_PIECES // 2):
        if side_work is not None:
            side_work(N_PIECES // 2 + p)
        y = _dot(a, ex.wdb_ref[:, p * kc:(p + 1) * kc])
        for q in range(kc // LANES):
            s = p * kc // LANES + q
            ys_ref[:, s * 8:(s + 1) * 8, :] = y[:, q * LANES:(q + 1) * LANES].reshape(
                ROW_TILE // 8, 8, LANES)


def _expert_tiles(ex, t, n, xs_refs, ys_refs):
    _gather_rows(ex, t, 0, ROW_TILE, xs_refs[0])
    for i in range(n):
        def side_work(p, i=i):
            if i + 1 < n:
                _gather_rows(ex, t + i + 1, p * PIECE_ROWS, PIECE_ROWS, xs_refs[(i + 1) % 2])
            if i >= 1:
                _scatter_rows(ex, t + i - 1, p * PIECE_ROWS, PIECE_ROWS, ys_refs[(i - 1) % 2])

        _tile_ffn(ex, t + i, xs_refs[i % 2], ys_refs[i % 2], side_work if n > 1 else None)
    _scatter_rows(ex, t + n - 1, 0, ROW_TILE, ys_refs[(n - 1) % 2])


def _moe_routed_kernel(cnt_ref, list_ref, wt_ref, h_ref, wg_ref, wu_ref, wd_ref,
                       x_ref, sh_ref, g2_ref, *refs):
    out_refs, scratch = refs[:-8], refs[-8:]
    acc_ref, wgb_ref, wub_ref, wdb_ref = scratch[:4]
    xs_refs, ys_refs = scratch[4:6], scratch[6:8]
    c = pl.program_id(0)
    j = pl.program_id(1)

    @pl.when(j == 0)
    def _():
        acc_ref[...] = jnp.zeros_like(acc_ref)

    for k in range(EXPERTS_PER_STEP):
        expert = jnp.minimum(j, EXPERT_STEPS - 1) * EXPERTS_PER_STEP + k
        count = cnt_ref[c * N_EXPERTS + expert]

        @pl.when(jnp.logical_and(j < EXPERT_STEPS, count > 0))
        def _():
            wgb_ref[...] = wg_ref[0, k].astype(BF16)
            wub_ref[...] = wu_ref[0, k].astype(BF16)
            wdb_ref[...] = wd_ref[0, k].astype(BF16)

            ex = _ExpertRefs(k * 2 * LIST_ROWS, k * LIST_ROWS, list_ref, wt_ref, h_ref, acc_ref,
                             wgb_ref, wub_ref, wdb_ref)
            n_tiles = (count + ROW_TILE - 1) // ROW_TILE

            last3 = jnp.logical_and(n_tiles % 2 == 1, n_tiles >= 3)
            n_pairs = (n_tiles - jnp.where(last3, 3, n_tiles % 2)) // 2

            def tile_pair(m, carry):
                _expert_tiles(ex, 2 * m, 2, xs_refs, ys_refs)
                return carry

            lax.fori_loop(0, n_pairs, tile_pair, 0)

            @pl.when(last3)
            def _():
                _expert_tiles(ex, n_tiles - 3, 3, xs_refs, ys_refs)

            @pl.when(n_tiles == 1)
            def _():
                _expert_tiles(ex, 0, 1, xs_refs, ys_refs)

    @pl.when(j >= EXPERT_STEPS)
    def _():
        base = (j - EXPERT_STEPS) * (TILE * ROW_SUB)
        moe = jnp.concatenate(
            [acc_ref[pl.ds(base + s, TILE, stride=ROW_SUB), :] for s in range(ROW_SUB)], axis=1)
        new_x = x_ref[...] + g2_ref[pl.ds(c, 1), :] * (moe + sh_ref[...])
        if len(out_refs) == 1:
            out_refs[0][...] = new_x
        else:
            prompt_ref, latent_ref = out_refs

            @pl.when(c < T_P // MOE_TC)
            def _():
                prompt_ref[...] = new_x

            @pl.when(c >= T_P // MOE_TC)
            def _():
                latent_ref[...] = new_x


def _moe_routed(lists, wts, counts, hrows, w_gate, w_up, w_down, li, x, shared, mod,
                split_output):
    group = lambda j: jnp.minimum(j, EXPERT_STEPS - 1)
    list_map = lambda c, j, nt: (c * EXPERT_STEPS + group(j), 0)
    w_map = lambda c, j, nt: (li, group(j), 0, 0)
    w_in_spec = pl.BlockSpec((1, EXPERTS_PER_STEP, D_MODEL, D_EXPERT), w_map)
    w_out_spec = pl.BlockSpec((1, EXPERTS_PER_STEP, D_EXPERT, D_MODEL), w_map)
    out_blk = lambda c, j: c * N_FLUSH + jnp.maximum(j - EXPERT_STEPS, 0)
    tok_spec = pl.BlockSpec((TILE, D_MODEL), lambda c, j, nt: (out_blk(c, j), 0))
    if split_output:
        out_specs = [
            pl.BlockSpec((TILE, D_MODEL), lambda c, j, nt: (jnp.minimum(out_blk(c, j), P_TILES - 1), 0)),
            pl.BlockSpec((TILE, D_MODEL), lambda c, j, nt: (jnp.maximum(out_blk(c, j) - P_TILES, 0), 0)),
        ]
        out_shape = [jax.ShapeDtypeStruct((T_P, D_MODEL), F32),
                     jax.ShapeDtypeStruct((T_S, D_MODEL), F32)]
    else:
        out_specs = tok_spec
        out_shape = jax.ShapeDtypeStruct((T_ALL, D_MODEL), F32)
    stage = pltpu.VMEM((ROW_TILE // 8, 8 * ROW_SUB, LANES), F32)
    grid_spec = pltpu.PrefetchScalarGridSpec(
        num_scalar_prefetch=1,
        grid=(N_CHUNKS, EXPERT_STEPS + N_FLUSH),
        in_specs=[
            pl.BlockSpec((EXPERTS_PER_STEP * 2 * LIST_ROWS, LANES), list_map,
                         memory_space=pltpu.SMEM),
            pl.BlockSpec((EXPERTS_PER_STEP * LIST_ROWS, LANES), list_map),
            pl.BlockSpec((MOE_TC * ROW_SUB, LANES), lambda c, j, nt: (c, 0),
                         pipeline_mode=pl.Buffered(1)),
            w_in_spec, w_in_spec, w_out_spec,
            tok_spec, tok_spec,
            pl.BlockSpec((8, D_MODEL), lambda c, j, nt: (0, 5)),
        ],
        out_specs=out_specs,
        scratch_shapes=[
            pltpu.VMEM(((MOE_TC + 1) * ROW_SUB, LANES), F32),
            pltpu.VMEM((D_MODEL, D_EXPERT), BF16),
            pltpu.VMEM((D_MODEL, D_EXPERT), BF16),
            pltpu.VMEM((D_EXPERT, D_MODEL), BF16),
            stage, stage, stage, stage,
        ],
    )
    return pl.pallas_call(
        _moe_routed_kernel,
        grid_spec=grid_spec,
        out_shape=out_shape,
        compiler_params=_cparams("arbitrary", "arbitrary", vmem_limit=VMEM_LIMIT_ROUTED),
        name="moe_routed",
    )(counts.reshape(-1), lists, wts, hrows, w_gate, w_up, w_down, x, shared, mod)


def kernel(x_prompt, x_sample, cache_k, cache_v, c, c_ctx, w_mod, b_mod, norm1, norm2, w_in_even, conv_a, q_norm, k_norm, lam_q1, lam_k1, lam_q2, lam_k2, subln, w_out_even, w_in_odd, conv_c, conv_c_b, ln_c_g, ln_c_b, w_pool, pool_scale, w_out_odd, w_router, b_router, w_gate, w_up, w_down, ws_gate, ws_up, ws_down):
    x = (x_prompt.reshape(T_P, D_MODEL), x_sample.reshape(T_S, D_MODEL))
    cond = jnp.concatenate([c_ctx[None, :], c, jnp.zeros((8 - 1 - DEC_BATCH, D_MODEL), F32)], axis=0)
    mod_all = _modulation(cond, w_mod, b_mod)

    new_k, new_v = [], []
    for li in range(DEPTH):
        mod = mod_all[li]
        if li % 2 == 0:
            e = li // 2
            lam_init = 0.8 - 0.6 * math.exp(-0.3 * li)
            u = _norm_in(x, norm1[li], mod, w_in_even[e].astype(BF16))
            q, k, v, k_cache, v_cache = _qkv_prep(u, q_norm[e], k_norm[e])
            new_k.append(k_cache)
            new_v.append(v_cache)
            o = _attention(q, k, v, cache_k, cache_v, e,
                           (lam_q1[e], lam_k1[e], lam_q2[e], lam_k2[e]), subln[e], lam_init)
            x = _even_out(x, u, o, conv_a[e], mod, w_out_even[e].astype(BF16))
        else:
            o_ = li // 2
            u = _norm_in(x, norm1[li], mod, w_in_odd[o_].astype(BF16))
            x = _odd_out(x, u, conv_c[o_], conv_c_b[o_], ln_c_g[o_], ln_c_b[o_],
                         w_pool[o_].astype(BF16), pool_scale[o_], mod, w_out_odd[o_].astype(BF16))
        hrows, gate_b, rank_b, shared, cnt_b = _moe_pre(
            x, norm2[li], mod, w_router[li].T, b_router[li], ws_gate[li].astype(BF16),
            ws_up[li].astype(BF16), ws_down[li].astype(BF16))
        lists, wts = _moe_sort(rank_b, gate_b, cnt_b)
        x = _moe_routed(lists, wts, cnt_b[:, :, 0], hrows, w_gate, w_up, w_down, li, x, shared,
                        mod, split_output=(li == DEPTH - 1))

    y_prompt = x[0].reshape(BATCH, SEQ, D_MODEL)
    y_sample = x[1].reshape(DEC_BATCH, DEC_SEQ, D_MODEL)
    return (y_prompt, y_sample, jnp.stack(new_k, axis=1), jnp.stack(new_v, axis=1))
```

```python
import functools
import math

import numpy as np
import jax
import jax.numpy as jnp
from jax import lax
from jax.experimental import pallas as pl
from jax.experimental.pallas import tpu as pltpu

D_MODEL = 1024
BATCH = 16
SEQ = 256
DEPTH = 2
DEC_BATCH = 2
DEC_SEQ = 4096
PAST_LEN = 512
GRID_W = 64
H_B = 4
DK_B = 64
DV_B = 2 * DK_B
D_A = D_MODEL // 2
D_B = H_B * DV_B
D_C = D_MODEL // 2
D_D = D_MODEL // 2
CONV_C = 31
POOL_WINDOWS = (2, 4, 8, 16)
D_DG = D_D // len(POOL_WINDOWS)
N_EXPERTS = 64
TOP_K = 8
N_GROUPS = 8
TOPK_GROUPS = 4
D_EXPERT = 256
D_SHARED = 256
ROUTED_SCALE = 2.5
ROPE_BASE = 10000.0
EPS = 1e-6

F32 = jnp.float32
BF16 = jnp.bfloat16

T_P = BATCH * SEQ
T_S = DEC_BATCH * DEC_SEQ
T_ALL = T_P + T_S
TILE = 256
N_TILES = T_ALL // TILE
P_TILES = T_P // TILE
S_TILES = DEC_SEQ // TILE
LANES = 128
VMEM_LIMIT = 56 * 1024 * 1024
VMEM_LIMIT_ROUTED = 60 * 1024 * 1024


def _cparams(*sem, vmem_limit=VMEM_LIMIT):
    return pltpu.CompilerParams(dimension_semantics=sem, vmem_limit_bytes=vmem_limit)


def _mod_row(i, tm):
    npt = T_P // tm
    per = DEC_SEQ // tm
    return jnp.where(i < npt, 0, 1 + (i - npt) // per)


def _seq_flags(i):
    j = (i - P_TILES) % S_TILES
    first = jnp.logical_or(i < P_TILES, j == 0)
    last = jnp.logical_or(i < P_TILES, j == S_TILES - 1)
    return first, last


def _seq_tile(i):
    return jnp.where(i < P_TILES, 0, (i - P_TILES) % S_TILES)


def _split_bf16(a):
    hi = a.astype(BF16)
    lo = (a - hi.astype(F32)).astype(BF16)
    return hi, lo


def _dot(a, b):
    return jnp.dot(a, b, preferred_element_type=F32)


def _dot_nt(a, b):
    return lax.dot_general(a, b, (((1,), (1,)), ((), ())), preferred_element_type=F32)


def _dot3(a, b):
    a_hi, a_lo = _split_bf16(a)
    b_hi, b_lo = _split_bf16(b)
    return _dot(a_hi, b_hi) + _dot(a_lo, b_hi) + _dot(a_hi, b_lo)


def _silu(x):
    return x * jax.nn.sigmoid(x)


MOD_TN = 1536


def _mod_kernel(c_ref, w_ref, b_ref, o_ref):
    o_ref[0] = _dot3(_silu(c_ref[...]), w_ref[0]) + b_ref[0]


def _modulation(cond, w_mod, b_mod):
    n = 6 * D_MODEL
    return pl.pallas_call(
        _mod_kernel,
        grid=(DEPTH, n // MOD_TN),
        in_specs=[
            pl.BlockSpec((8, D_MODEL), lambda l, j: (0, 0)),
            pl.BlockSpec((1, D_MODEL, MOD_TN), lambda l, j: (l, 0, j)),
            pl.BlockSpec((1, 1, MOD_TN), lambda l, j: (l, 0, j)),
        ],
        out_specs=pl.BlockSpec((1, 8, MOD_TN), lambda l, j: (l, 0, j)),
        out_shape=jax.ShapeDtypeStruct((DEPTH, 8, n), F32),
        compiler_params=_cparams("arbitrary", "arbitrary"),
        name="modulation",
    )(cond, w_mod, b_mod.reshape(DEPTH, 1, n))


IN_TM = 1024


def _modulated_norm(x, g, shift, scale):
    ms = jnp.mean(x * x, axis=-1, keepdims=True)
    return (x * lax.rsqrt(ms + EPS) * g) * (1.0 + scale) + shift


def _tok_specs(x, tm, width):
    if isinstance(x, tuple):
        n_p = T_P // tm
        return ([pl.BlockSpec((tm, width), lambda i: (jnp.minimum(i, n_p - 1), 0)),
                 pl.BlockSpec((tm, width), lambda i: (jnp.maximum(i - n_p, 0), 0))], list(x))
    return [pl.BlockSpec((tm, width), lambda i: (i, 0))], [x]


def _tok_load(refs, i, tm):
    if len(refs) == 2:
        return jnp.where(i < T_P // tm, refs[0][...], refs[1][...])
    return refs[0][...]


def _norm_in_kernel(*refs, n_x):
    x_refs, (g_ref, sh_ref, sc_ref, w_ref, o_ref) = refs[:n_x], refs[n_x:]
    i = pl.program_id(0)
    r = _mod_row(i, IN_TM)
    h = _modulated_norm(_tok_load(x_refs, i, IN_TM), g_ref[...], sh_ref[pl.ds(r, 1), :],
                        sc_ref[pl.ds(r, 1), :])
    o_ref[...] = _dot(h.astype(BF16), w_ref[...])


def _norm_in(x, g, mod, w_bf16):
    n = w_bf16.shape[1]
    x_specs, x_ops = _tok_specs(x, IN_TM, D_MODEL)
    return pl.pallas_call(
        functools.partial(_norm_in_kernel, n_x=len(x_ops)),
        grid=(T_ALL // IN_TM,),
        in_specs=x_specs + [
            pl.BlockSpec((1, D_MODEL), lambda i: (0, 0)),
            pl.BlockSpec((8, D_MODEL), lambda i: (0, 0)),
            pl.BlockSpec((8, D_MODEL), lambda i: (0, 1)),
            pl.BlockSpec((D_MODEL, n), lambda i: (0, 0)),
        ],
        out_specs=pl.BlockSpec((IN_TM, n), lambda i: (i, 0)),
        out_shape=jax.ShapeDtypeStruct((T_ALL, n), F32),
        compiler_params=_cparams("arbitrary"),
        name="norm_in_proj",
    )(*x_ops, g.reshape(1, D_MODEL), mod, mod, w_bf16)


QKV_TM = 512


def _rope_tables():
    half = DK_B // 2
    freqs = ROPE_BASE ** (-np.arange(0, half, 2, dtype=np.float64) / half)
    l = np.arange(DEC_SEQ)
    pos_r = (l // GRID_W).astype(np.float64)
    pos_c = (l % GRID_W).astype(np.float64)
    lane = np.arange(LANES)
    jj = lane % DK_B
    m = jj % half
    f = m % (half // 2)
    pos = np.where((jj < half)[None, :], pos_r[:, None], pos_c[:, None])
    ang = pos * freqs[f][None, :]
    sign = np.where(m < half // 2, -1.0, 1.0)[None, :]
    cos = np.concatenate([np.ones((QKV_TM, LANES)), np.cos(ang)], axis=0)
    sin = np.concatenate([np.zeros((QKV_TM, LANES)), sign * np.sin(ang)], axis=0)
    return cos.astype(np.float32), sin.astype(np.float32)


def _segment_mean_matrix():
    lane = np.arange(LANES)
    same = (lane[:, None] // DK_B) == (lane[None, :] // DK_B)
    return (same.astype(np.float32) / DK_B)


def _qk_prep(x, g, cos, sin, seg):
    x2 = x * x
    hi, lo = _split_bf16(x2)
    ms = _dot(hi, seg) + _dot(lo, seg)
    y = x * lax.rsqrt(ms + EPS) * g
    lane = lax.broadcasted_iota(jnp.int32, y.shape, 1)
    lower = (lane % (DK_B // 2)) < (DK_B // 4)
    partner = jnp.where(lower, pltpu.roll(y, LANES - DK_B // 4, 1), pltpu.roll(y, DK_B // 4, 1))
    return y * cos + partner * sin


def _even_in_kernel(*refs, n_x):
    x_refs = refs[:n_x]
    (g_ref, sh_ref, sc_ref, w_ref, qn_ref, kn_ref, cos_ref, sin_ref, seg_ref,
     ua_ref, qo_ref, ko_ref, vo_ref, kc_ref, vc_ref) = refs[n_x:]
    i = pl.program_id(0)
    r = _mod_row(i, QKV_TM)
    hn = _modulated_norm(_tok_load(x_refs, i, QKV_TM), g_ref[...], sh_ref[pl.ds(r, 1), :],
                         sc_ref[pl.ds(r, 1), :]).astype(BF16)
    ua_ref[...] = _dot(hn, w_ref[:, :3 * D_A])
    qkv = _dot(hn, w_ref[:, 3 * D_A:])
    cos = cos_ref[...]
    sin = sin_ref[...]
    seg = seg_ref[...]
    scale = math.log2(math.e) / math.sqrt(DK_B)
    is_prompt = i < T_P // QKV_TM
    for h in range(H_B):
        col = lambda part: slice(part * D_B + h * LANES, part * D_B + (h + 1) * LANES)
        qo_ref[h] = (_qk_prep(qkv[:, col(0)], qn_ref[...], cos, sin, seg) * scale).astype(BF16)
        k = _qk_prep(qkv[:, col(1)], kn_ref[...], cos, sin, seg)
        v = qkv[:, col(2)]
        ko_ref[h] = k.astype(BF16)
        vo_ref[h] = v.astype(BF16)

        @pl.when(is_prompt)
        def _():
            for b in range(QKV_TM // SEQ):
                kc_ref[b, h] = k[b * SEQ:(b + 1) * SEQ]
                vc_ref[b, h] = v[b * SEQ:(b + 1) * SEQ]


def _even_in(x, g, mod, w_bf16, qn, kn):
    cos, sin = _rope_tables()
    seg = jnp.asarray(_segment_mean_matrix(), BF16)
    qn2 = jnp.concatenate([qn, qn]).reshape(1, LANES)
    kn2 = jnp.concatenate([kn, kn]).reshape(1, LANES)
    x_specs, x_ops = _tok_specs(x, QKV_TM, D_MODEL)
    p_steps = T_P // QKV_TM
    s_steps = DEC_SEQ // QKV_TM

    def tab_map(i):
        return (jnp.where(i < p_steps, 0, 1 + (i - p_steps) % s_steps), 0)

    out_b = jax.ShapeDtypeStruct((H_B, T_ALL, LANES), BF16)
    out_spec = pl.BlockSpec((H_B, QKV_TM, LANES), lambda i: (0, i, 0))
    cache = jax.ShapeDtypeStruct((BATCH, H_B, SEQ, LANES), F32)
    cache_spec = pl.BlockSpec((QKV_TM // SEQ, H_B, SEQ, LANES),
                              lambda i: (jnp.minimum(i, p_steps - 1), 0, 0, 0))
    n = w_bf16.shape[1]
    return pl.pallas_call(
        functools.partial(_even_in_kernel, n_x=len(x_ops)),
        grid=(T_ALL // QKV_TM,),
        in_specs=x_specs + [
            pl.BlockSpec((1, D_MODEL), lambda i: (0, 0)),
            pl.BlockSpec((8, D_MODEL), lambda i: (0, 0)),
            pl.BlockSpec((8, D_MODEL), lambda i: (0, 1)),
            pl.BlockSpec((D_MODEL, n), lambda i: (0, 0)),
            pl.BlockSpec((1, LANES), lambda i: (0, 0)),
            pl.BlockSpec((1, LANES), lambda i: (0, 0)),
            pl.BlockSpec((QKV_TM, LANES), tab_map),
            pl.BlockSpec((QKV_TM, LANES), tab_map),
            pl.BlockSpec((LANES, LANES), lambda i: (0, 0)),
        ],
        out_specs=[pl.BlockSpec((QKV_TM, 3 * D_A), lambda i: (i, 0))] + [out_spec] * 3
        + [cache_spec] * 2,
        out_shape=[jax.ShapeDtypeStruct((T_ALL, 3 * D_A), F32), out_b, out_b, out_b, cache, cache],
        compiler_params=_cparams("arbitrary"),
        name="even_in_proj",
    )(*x_ops, g.reshape(1, D_MODEL), mod, mod, w_bf16, qn2, kn2, jnp.asarray(cos),
      jnp.asarray(sin), seg)


def _lambda(lq1_ref, lk1_ref, lq2_ref, lk2_ref, lam_init):
    a = jnp.sum(lq1_ref[...] * lk1_ref[...], axis=-1, keepdims=True)
    b = jnp.sum(lq2_ref[...] * lk2_ref[...], axis=-1, keepdims=True)
    return jnp.exp(a) - jnp.exp(b) + lam_init


ATTN_TQ = 512


def _attn_body(q, keys, vals, lam, subg, lam_init):
    lane = lax.broadcasted_iota(jnp.int32, q.shape, 1)
    zero = jnp.zeros_like(q)
    qa = jnp.where(lane < DK_B, q, zero)
    qb = jnp.where(lane < DK_B, zero, q)
    scores = [[_dot_nt(qq, k) for k in keys] for qq in (qa, qb)]
    outs = []
    for ss in scores:
        m = functools.reduce(jnp.maximum, [jnp.max(s, axis=-1, keepdims=True) for s in ss])
        ps = [jnp.exp2(s - m) for s in ss]
        l = functools.reduce(jnp.add, [jnp.sum(p, axis=-1, keepdims=True) for p in ps])
        pv = functools.reduce(jnp.add, [_dot(p.astype(BF16), v) for p, v in zip(ps, vals)])
        outs.append(pv / l)
    o = outs[0] - lam * outs[1]
    ms = jnp.mean(o * o, axis=-1, keepdims=True)
    return (o * lax.rsqrt(ms + EPS) * subg) * (1.0 - lam_init)


def _attn_prompt_kernel(q_ref, k_ref, v_ref, lq1, lk1, lq2, lk2, sg_ref, o_ref, *, lam_init):
    lam = _lambda(lq1, lk1, lq2, lk2, lam_init)
    for h in range(H_B):
        o_ref[:, h * DV_B:(h + 1) * DV_B] = _attn_body(
            q_ref[h], [k_ref[h]], [v_ref[h]], lam, sg_ref[...], lam_init)


def _attn_latent_kernel(q_ref, k_ref, v_ref, ck_ref, cv_ref, lq1, lk1, lq2, lk2, sg_ref, o_ref,
                        *, lam_init):
    lam = _lambda(lq1, lk1, lq2, lk2, lam_init)
    keys = [ck_ref[0, 0, 0].astype(BF16), k_ref[0]]
    vals = [cv_ref[0, 0, 0].astype(BF16), v_ref[0]]
    o_ref[...] = _attn_body(q_ref[0], keys, vals, lam, sg_ref[...], lam_init)


def _attention(q, k, v, cache_k, cache_v, e, lam_params, subg, lam_init):
    small = [p.reshape(1, DK_B) for p in lam_params] + [subg.reshape(1, DV_B)]
    small_specs2 = [pl.BlockSpec((1, DK_B), lambda b: (0, 0))] * 4 + \
                   [pl.BlockSpec((1, DV_B), lambda b: (0, 0))]
    small_specs3 = [pl.BlockSpec((1, DK_B), lambda s, h, j: (0, 0))] * 4 + \
                   [pl.BlockSpec((1, DV_B), lambda s, h, j: (0, 0))]
    o_prompt = pl.pallas_call(
        functools.partial(_attn_prompt_kernel, lam_init=lam_init),
        grid=(BATCH,),
        in_specs=[pl.BlockSpec((H_B, SEQ, LANES), lambda b: (0, b, 0))] * 3 + small_specs2,
        out_specs=pl.BlockSpec((SEQ, D_B), lambda b: (b, 0)),
        out_shape=jax.ShapeDtypeStruct((T_P, D_B), F32),
        compiler_params=_cparams("arbitrary"),
        name="attn_prompt",
    )(q, k, v, *small)

    q_tiles = DEC_SEQ // ATTN_TQ
    kv_spec = pl.BlockSpec((1, DEC_SEQ, LANES), lambda s, h, j: (h, T_P // DEC_SEQ + s, 0))
    c_spec = pl.BlockSpec((1, 1, 1, PAST_LEN, LANES), lambda s, h, j: (s, e, h, 0, 0))
    o_latent = pl.pallas_call(
        functools.partial(_attn_latent_kernel, lam_init=lam_init),
        grid=(DEC_BATCH, H_B, q_tiles),
        in_specs=[pl.BlockSpec((1, ATTN_TQ, LANES),
                               lambda s, h, j: (h, T_P // ATTN_TQ + s * q_tiles + j, 0)),
                  kv_spec, kv_spec, c_spec, c_spec] + small_specs3,
        out_specs=pl.BlockSpec((ATTN_TQ, LANES), lambda s, h, j: (s * q_tiles + j, h)),
        out_shape=jax.ShapeDtypeStruct((T_S, D_B), F32),
        compiler_params=_cparams("arbitrary", "arbitrary", "arbitrary"),
        name="attn_latent",
    )(q, k, v, cache_k, cache_v, *small)
    return o_prompt, o_latent


HALO8 = 8
HALO16 = 16


def _prev_block(i, rows):
    return jnp.maximum(i * (TILE // rows) - 1, 0)


def _next_block(i, rows):
    return jnp.minimum((i + 1) * (TILE // rows), T_ALL // rows - 1)


def _even_out_kernel(*refs, n_x, n_o):
    x_refs, o_refs = refs[:n_x], refs[n_x:n_x + n_o]
    (bg_ref, cg_ref, hin_ref, cgp_ref, hinp_ref, cgn_ref, hinn_ref, cw_ref, g1_ref, w_ref,
     out_ref) = refs[n_x + n_o:]
    i = pl.program_id(0)
    first, last = _seq_flags(i)
    r = _mod_row(i, TILE)
    z = cg_ref[...] * hin_ref[...]
    zp = jnp.where(first, 0.0, cgp_ref[HALO8 - 1:HALO8, :] * hinp_ref[HALO8 - 1:HALO8, :])
    zn = jnp.where(last, 0.0, cgn_ref[0:1, :] * hinn_ref[0:1, :])
    row = lax.broadcasted_iota(jnp.int32, z.shape, 0)
    z_prev = jnp.where(row == 0, zp, pltpu.roll(z, 1, 0))
    z_next = jnp.where(row == TILE - 1, zn, pltpu.roll(z, TILE - 1, 0))
    cw = cw_ref[...]
    ya = bg_ref[...] * (cw[0:1, :] * z_prev + cw[1:2, :] * z + cw[2:3, :] * z_next)
    o = _tok_load(o_refs, i, TILE)
    y = _dot(ya.astype(BF16), w_ref[0:D_A, :]) + _dot(o.astype(BF16), w_ref[D_A:, :])
    out_ref[...] = _tok_load(x_refs, i, TILE) + g1_ref[pl.ds(r, 1), :] * y


def _even_out(x, u, o, conv_w, mod, w_out_bf16):
    tile_spec = lambda c: pl.BlockSpec((TILE, D_A), lambda i: (i, c))
    prev_spec = lambda c: pl.BlockSpec((HALO8, D_A), lambda i: (_prev_block(i, HALO8), c))
    next_spec = lambda c: pl.BlockSpec((HALO8, D_A), lambda i: (_next_block(i, HALO8), c))
    x_specs, x_ops = _tok_specs(x, TILE, D_MODEL)
    o_specs, o_ops = _tok_specs(o, TILE, D_B)
    return pl.pallas_call(
        functools.partial(_even_out_kernel, n_x=len(x_ops), n_o=len(o_ops)),
        grid=(N_TILES,),
        in_specs=x_specs + o_specs + [
            tile_spec(0), tile_spec(1), tile_spec(2),
            prev_spec(1), prev_spec(2), next_spec(1), next_spec(2),
            pl.BlockSpec((3, D_A), lambda i: (0, 0)),
            pl.BlockSpec((8, D_MODEL), lambda i: (0, 2)),
            pl.BlockSpec((D_MODEL, D_MODEL), lambda i: (0, 0)),
        ],
        out_specs=pl.BlockSpec((TILE, D_MODEL), lambda i: (i, 0)),
        out_shape=jax.ShapeDtypeStruct((T_ALL, D_MODEL), F32),
        compiler_params=_cparams("arbitrary"),
        name="even_mixer_out",
    )(*x_ops, *o_ops, u, u, u, u, u, u, u, conv_w, mod, w_out_bf16)


def _odd_out_kernel(x_ref, a_ref, b_ref, pd_ref, ap_ref, bp_ref, an_ref, bn_ref, pp_ref, pn_ref,
                    cw_ref, cb_ref, lg_ref, lb_ref, wp_ref, ps_ref, g1_ref, w_ref, out_ref,
                    ext_ref, extp_ref, shift_ref):
    i = pl.program_id(0)
    first, last = _seq_flags(i)
    r = _mod_row(i, TILE)
    ext_ref[0:HALO16, :] = jnp.where(first, 0.0, ap_ref[...] * jax.nn.sigmoid(bp_ref[...]))
    ext_ref[HALO16:HALO16 + TILE, :] = a_ref[...] * jax.nn.sigmoid(b_ref[...])
    ext_ref[HALO16 + TILE:, :] = jnp.where(last, 0.0, an_ref[...] * jax.nn.sigmoid(bn_ref[...]))
    base = HALO16 - CONV_C // 2
    parts = []
    for cb in range(D_C // LANES):
        cols = slice(cb * LANES, (cb + 1) * LANES)
        acc = jnp.zeros((TILE, LANES), F32)
        for phase in range(8):
            taps = [j for j in range(CONV_C) if (base + j) % 8 == phase]
            reach = max((base + j) // 8 for j in taps)
            rows = TILE + 8 * reach
            shift_ref[0:rows, :] = ext_ref[pl.ds(phase, rows), cols]
            for j in taps:
                a = (base + j) // 8
                acc = acc + cw_ref[j:j + 1, cols] * shift_ref[8 * a:8 * a + TILE, :]
        parts.append(acc)
    g = jnp.concatenate(parts, axis=-1) + cb_ref[...]
    mu = jnp.mean(g, axis=-1, keepdims=True)
    var = jnp.mean(jnp.square(g - mu), axis=-1, keepdims=True)
    g = _silu(((g - mu) * lax.rsqrt(var + EPS)) * lg_ref[...] + lb_ref[...])
    extp_ref[0:HALO8, :] = jnp.where(first, 0.0, pp_ref[...])
    extp_ref[HALO8:HALO8 + TILE, :] = pd_ref[...]
    extp_ref[HALO8 + TILE:, :] = jnp.where(last, 0.0, pn_ref[...])
    seq_len = jnp.where(i < P_TILES, SEQ, DEC_SEQ)
    pos = _seq_tile(i) * TILE + lax.broadcasted_iota(jnp.int32, (TILE, 1), 0)
    yd = []
    for gi, w in enumerate(POOL_WINDOWS):
        cols = slice(gi * D_DG, (gi + 1) * D_DG)
        s = jnp.zeros((TILE, D_DG), F32)
        for d in range(-(w // 2), w - w // 2):
            s = s + extp_ref[pl.ds(HALO8 + d, TILE), cols]
        lo = jnp.maximum(pos - w // 2, 0)
        hi = jnp.minimum(pos - w // 2 + w, seq_len)
        pooled = s / (hi - lo).astype(F32) - pd_ref[:, cols]
        yd.append(_dot(pooled.astype(BF16), wp_ref[gi]))
    yd = jnp.concatenate(yd, axis=-1) * ps_ref[...]
    y = _dot(g.astype(BF16), w_ref[0:D_C, :]) + _dot(yd.astype(BF16), w_ref[D_C:, :])
    out_ref[...] = x_ref[...] + g1_ref[pl.ds(r, 1), :] * y


def _odd_out(x, u, conv_w, conv_b, ln_g, ln_b, w_pool_bf16, p_scale, mod, w_out_bf16):
    tile_spec = lambda c: pl.BlockSpec((TILE, D_C), lambda i: (i, c))
    prev_spec = lambda rows, c: pl.BlockSpec((rows, D_C), lambda i: (_prev_block(i, rows), c))
    next_spec = lambda rows, c: pl.BlockSpec((rows, D_C), lambda i: (_next_block(i, rows), c))
    vec = lambda: pl.BlockSpec((1, D_C), lambda i: (0, 0))
    return pl.pallas_call(
        _odd_out_kernel,
        grid=(N_TILES,),
        in_specs=[
            pl.BlockSpec((TILE, D_MODEL), lambda i: (i, 0)),
            tile_spec(0), tile_spec(1), tile_spec(2),
            prev_spec(HALO16, 0), prev_spec(HALO16, 1), next_spec(HALO16, 0), next_spec(HALO16, 1),
            prev_spec(HALO8, 2), next_spec(HALO8, 2),
            pl.BlockSpec((CONV_C, D_C), lambda i: (0, 0)),
            vec(), vec(), vec(),
            pl.BlockSpec((len(POOL_WINDOWS), D_DG, D_DG), lambda i: (0, 0, 0)),
            vec(),
            pl.BlockSpec((8, D_MODEL), lambda i: (0, 2)),
            pl.BlockSpec((D_MODEL, D_MODEL), lambda i: (0, 0)),
        ],
        out_specs=pl.BlockSpec((TILE, D_MODEL), lambda i: (i, 0)),
        out_shape=jax.ShapeDtypeStruct((T_ALL, D_MODEL), F32),
        scratch_shapes=[pltpu.VMEM((TILE + 2 * HALO16, D_C), F32),
                        pltpu.VMEM((TILE + 2 * HALO8, D_D), F32),
                        pltpu.VMEM((TILE + 2 * HALO16, LANES), F32)],
        compiler_params=_cparams("arbitrary"),
        name="odd_mixer_out",
    )(x, u, u, u, u, u, u, u, u, u, conv_w, conv_b.reshape(1, D_C), ln_g.reshape(1, D_C),
      ln_b.reshape(1, D_C), w_pool_bf16, p_scale.reshape(1, D_D), mod, w_out_bf16)


GROUP = N_EXPERTS // N_GROUPS
NEG_INF = float("-inf")


def _first_argmax(v, idx, axis):
    m = jnp.max(v, axis=axis, keepdims=True)
    big = jnp.int32(2 ** 30)
    am = jnp.min(jnp.where(v == m, idx, big), axis=axis, keepdims=True)
    return m, am


def _route(scores, biased):
    shape = biased.shape
    member = lax.broadcasted_iota(jnp.int32, shape, 1)
    m1, a1 = _first_argmax(biased, member, 1)
    m2 = jnp.max(jnp.where(member == a1, NEG_INF, biased), axis=1, keepdims=True)
    gscore = m1 + m2
    gidx = lax.broadcasted_iota(jnp.int32, gscore.shape, 0)
    gsel = jnp.zeros(gscore.shape, jnp.bool_)
    for _ in range(TOPK_GROUPS):
        _, am = _first_argmax(gscore, gidx, 0)
        hit = gidx == am
        gsel = jnp.logical_or(gsel, hit)
        gscore = jnp.where(hit, NEG_INF, gscore)
    cand = jnp.where(gsel, biased, NEG_INF)
    eidx = lax.broadcasted_iota(jnp.int32, shape, 0) * GROUP + member
    sel = jnp.zeros(shape, jnp.bool_)
    for _ in range(TOP_K):
        m = jnp.max(jnp.max(cand, axis=1, keepdims=True), axis=0, keepdims=True)
        big = jnp.int32(2 ** 30)
        am = jnp.where(cand == m, eidx, big)
        am = jnp.min(jnp.min(am, axis=1, keepdims=True), axis=0, keepdims=True)
        hit = eidx == am
        sel = jnp.logical_or(sel, hit)
        cand = jnp.where(hit, NEG_INF, cand)
    wsel = jnp.where(sel, scores, 0.0)
    tot = jnp.sum(jnp.sum(wsel, axis=1, keepdims=True), axis=0, keepdims=True)
    return wsel / tot * ROUTED_SCALE, sel


MOE_TC = 4096
N_CHUNKS = T_ALL // MOE_TC
ROW_TILE = 256
N_FLUSH = MOE_TC // TILE
PRE_TM = 1024
ROW_SUB = D_MODEL // LANES


def _moe_pre_kernel(x_ref, g_ref, sh_ref, sc_ref, wr_ref, br_ref, tri_ref, wsg_ref, wsu_ref,
                    wsd_ref, hrow_ref, gate_ref, rank_ref, shared_ref, cnt_ref, carry_ref):
    i = pl.program_id(0)
    r = _mod_row(i, PRE_TM)
    h = _modulated_norm(x_ref[...], g_ref[...], sh_ref[pl.ds(r, 1), :], sc_ref[pl.ds(r, 1), :])
    hb = h.astype(BF16)
    for s in range(ROW_SUB):
        hrow_ref[pl.ds(s, PRE_TM, stride=ROW_SUB), :] = h[:, s * LANES:(s + 1) * LANES]
    h_hi, h_lo = hb, (h - hb.astype(F32)).astype(BF16)
    w_hi, w_lo = _split_bf16(wr_ref[...])
    logits = _dot_nt(w_hi, h_hi) + _dot_nt(w_lo, h_hi) + _dot_nt(w_hi, h_lo)
    scores = jax.nn.sigmoid(logits)
    biased = scores + br_ref[:, 0:1]
    shape3 = (N_GROUPS, GROUP, PRE_TM)
    gate_t, sel = _route(scores.reshape(shape3), biased.reshape(shape3))
    gate_t = gate_t.reshape(N_EXPERTS, PRE_TM)
    sel = jnp.where(sel.reshape(N_EXPERTS, PRE_TM), 1.0, 0.0)

    @pl.when(i % (MOE_TC // PRE_TM) == 0)
    def _():
        carry_ref[...] = jnp.zeros_like(carry_ref)

    carry = carry_ref[...]
    local = _dot(sel.astype(BF16), tri_ref[...])
    rank = jnp.where(sel > 0.0, local + jnp.concatenate([carry] * (PRE_TM // LANES), axis=1), -1.0)
    carry = carry + jnp.sum(sel, axis=1, keepdims=True)
    carry_ref[...] = carry

    @pl.when(i % (MOE_TC // PRE_TM) == MOE_TC // PRE_TM - 1)
    def _():
        cnt_ref[0] = carry.astype(jnp.int32)
    gate_ref[...] = gate_t
    rank_ref[...] = rank
    a = _silu(_dot(hb, wsg_ref[...])) * _dot(hb, wsu_ref[...])
    shared_ref[...] = _dot(a.astype(BF16), wsd_ref[...])


def _moe_pre(x, g, mod, w_router_t, b_router, wsg, wsu, wsd):
    return pl.pallas_call(
        _moe_pre_kernel,
        grid=(T_ALL // PRE_TM,),
        in_specs=[
            pl.BlockSpec((PRE_TM, D_MODEL), lambda i: (i, 0)),
            pl.BlockSpec((1, D_MODEL), lambda i: (0, 0)),
            pl.BlockSpec((8, D_MODEL), lambda i: (0, 3)),
            pl.BlockSpec((8, D_MODEL), lambda i: (0, 4)),
            pl.BlockSpec((N_EXPERTS, D_MODEL), lambda i: (0, 0)),
            pl.BlockSpec((N_EXPERTS, LANES), lambda i: (0, 0)),
            pl.BlockSpec((PRE_TM, PRE_TM), lambda i: (0, 0)),
            pl.BlockSpec((D_MODEL, D_SHARED), lambda i: (0, 0)),
            pl.BlockSpec((D_MODEL, D_SHARED), lambda i: (0, 0)),
            pl.BlockSpec((D_SHARED, D_MODEL), lambda i: (0, 0)),
        ],
        out_specs=[
            pl.BlockSpec((PRE_TM * ROW_SUB, LANES), lambda i: (i, 0)),
            pl.BlockSpec((N_EXPERTS, PRE_TM), lambda i: (0, i)),
            pl.BlockSpec((N_EXPERTS, PRE_TM), lambda i: (0, i)),
            pl.BlockSpec((PRE_TM, D_MODEL), lambda i: (i, 0)),
            pl.BlockSpec((1, N_EXPERTS, LANES), lambda i: (i // (MOE_TC // PRE_TM), 0, 0)),
        ],
        out_shape=[
            jax.ShapeDtypeStruct((T_ALL * ROW_SUB, LANES), F32),
            jax.ShapeDtypeStruct((N_EXPERTS, T_ALL), F32),
            jax.ShapeDtypeStruct((N_EXPERTS, T_ALL), F32),
            jax.ShapeDtypeStruct((T_ALL, D_MODEL), F32),
            jax.ShapeDtypeStruct((N_CHUNKS, N_EXPERTS, LANES), jnp.int32),
        ],
        scratch_shapes=[pltpu.VMEM((N_EXPERTS, LANES), F32)],
        compiler_params=_cparams("arbitrary"),
        name="moe_pre",
    )(x, g.reshape(1, D_MODEL), mod, mod, w_router_t,
      jnp.broadcast_to(b_router.reshape(N_EXPERTS, 1), (N_EXPERTS, LANES)),
      jnp.asarray(np.triu(np.ones((PRE_TM, PRE_TM), np.float32), 1), BF16), wsg, wsu, wsd)


LIST_ROWS = MOE_TC // LANES
TILE_ROWS = ROW_TILE // LANES
DUMMY_ROW = MOE_TC * ROW_SUB


def _moe_sort_kernel(rank_ref, gate_ref, cnt_ref, list_ref, w_ref):
    rank = rank_ref[...]
    lane = lax.broadcasted_iota(jnp.int32, rank.shape, 1)
    d = jnp.where(rank >= 0.0, lane - rank.astype(jnp.int32), 0)
    w = gate_ref[...]
    for s in range(MOE_TC.bit_length() - 1):
        k = 1 << s
        d_in = pltpu.roll(d, MOE_TC - k, 1)
        w_in = pltpu.roll(w, MOE_TC - k, 1)
        take = (d_in & k) != 0
        leave = (d & k) != 0
        d = jnp.where(take, d_in, jnp.where(leave, 0, d))
        w = jnp.where(take, w_in, w)
    valid = lane < cnt_ref[0][:, 0:1]
    row = (lane + d) * ROW_SUB
    gsrc = jnp.where(valid, row, 0)
    ssrc = jnp.where(valid, row, DUMMY_ROW)
    w = jnp.where(valid, w, 0.0)
    for b in range(LIST_ROWS):
        cols = slice(b * LANES, (b + 1) * LANES)
        list_ref[pl.ds(b, N_EXPERTS, stride=2 * LIST_ROWS), :] = gsrc[:, cols]
        list_ref[pl.ds(LIST_ROWS + b, N_EXPERTS, stride=2 * LIST_ROWS), :] = ssrc[:, cols]
        w_ref[pl.ds(b, N_EXPERTS, stride=LIST_ROWS), :] = w[:, cols]


def _moe_sort(rank_t, gate_t, cnt_b):
    chunk_spec = pl.BlockSpec((N_EXPERTS, MOE_TC), lambda c: (0, c))
    n_rows = N_CHUNKS * N_EXPERTS * LIST_ROWS
    return pl.pallas_call(
        _moe_sort_kernel,
        grid=(N_CHUNKS,),
        in_specs=[chunk_spec, chunk_spec,
                  pl.BlockSpec((1, N_EXPERTS, LANES), lambda c: (c, 0, 0))],
        out_specs=[pl.BlockSpec((N_EXPERTS * 2 * LIST_ROWS, LANES), lambda c: (c, 0)),
                   pl.BlockSpec((N_EXPERTS * LIST_ROWS, LANES), lambda c: (c, 0))],
        out_shape=[jax.ShapeDtypeStruct((2 * n_rows, LANES), jnp.int32),
                   jax.ShapeDtypeStruct((n_rows, LANES), F32)],
        compiler_params=_cparams("arbitrary"),
        name="moe_sort",
    )(rank_t, gate_t, cnt_b)


SCATTER_BATCH = 16


EXPERTS_PER_STEP = 2
EXPERT_STEPS = N_EXPERTS // EXPERTS_PER_STEP


N_PIECES = 8
PIECE_ROWS = ROW_TILE // N_PIECES


class _ExpertRefs:
    def __init__(self, list0, wt0, list_ref, wt_ref, h_ref, acc_ref, wgb_ref, wub_ref, wdb_ref):
        self.list0, self.wt0 = list0, wt0
        self.list_ref, self.wt_ref, self.h_ref, self.acc_ref = list_ref, wt_ref, h_ref, acc_ref
        self.wgb_ref, self.wub_ref, self.wdb_ref = wgb_ref, wub_ref, wdb_ref


def _gather_rows(ex, t, r0, n, xs_ref):
    base = ex.list0 + t * TILE_ROWS
    for r in range(r0, r0 + n):
        tok = pl.multiple_of(ex.list_ref[base + r // LANES, r % LANES], ROW_SUB)
        xs_ref[r // 8, pl.ds(r % 8, ROW_SUB, stride=8), :] = ex.h_ref[pl.ds(tok, ROW_SUB), :]


def _scatter_rows(ex, t, r0, n, ys_ref):
    base = ex.list0 + LIST_ROWS + t * TILE_ROWS
    for b0 in range(r0, r0 + n, SCATTER_BATCH):
        rows = range(b0, b0 + SCATTER_BATCH)
        dsts = [ex.acc_ref.at[pl.ds(pl.multiple_of(ex.list_ref[base + r // LANES, r % LANES],
                                                   ROW_SUB), ROW_SUB), :] for r in rows]
        news = [dst[...] + ys_ref[r // 8, pl.ds(r % 8, ROW_SUB, stride=8), :]
                for dst, r in zip(dsts, rows)]
        for dst, new in zip(dsts, news):
            dst[...] = new


def _tile_ffn(ex, t, xs_ref, ys_ref, side_work=None):
    kc = D_MODEL // (N_PIECES // 2)
    hg = hu = None
    for p in range(N_PIECES // 2):
        if side_work is not None:
            side_work(p)
        xk = jnp.concatenate(
            [xs_ref[:, s * 8:(s + 1) * 8, :].reshape(ROW_TILE, LANES)
             for s in range(p * kc // LANES, (p + 1) * kc // LANES)], axis=1).astype(BF16)
        dg = _dot(xk, ex.wgb_ref[p * kc:(p + 1) * kc, :])
        du = _dot(xk, ex.wub_ref[p * kc:(p + 1) * kc, :])
        hg = dg if hg is None else hg + dg
        hu = du if hu is None else hu + du
    eye = (lax.broadcasted_iota(jnp.int32, (ROW_TILE, LANES), 0) % LANES
           == lax.broadcasted_iota(jnp.int32, (ROW_TILE, LANES), 1))
    row_blk = lax.broadcasted_iota(jnp.int32, (ROW_TILE, LANES), 0) // LANES
    wrows = functools.reduce(
        lambda a, b: a + b,
        [jnp.where(row_blk == k, ex.wt_ref[pl.ds(ex.wt0 + t * TILE_ROWS + k, 1), :], 0.0)
         for k in range(TILE_ROWS)])
    wcol = jnp.sum(jnp.where(eye, wrows, 0.0), axis=1, keepdims=True)
    a = (_silu(hg) * hu * wcol).astype(BF16)
    for p in range(N_PIECES // 2):
        if side_work is not None:
            side_work(N_PIECES // 2 + p)
        y = _dot(a, ex.wdb_ref[:, p * kc:(p + 1) * kc])
        for q in range(kc // LANES):
            s = p * kc // LANES + q
            ys_ref[:, s * 8:(s + 1) * 8, :] = y[:, q * LANES:(q + 1) * LANES].reshape(
                ROW_TILE // 8, 8, LANES)


def _expert_tiles(ex, t, n, xs_refs, ys_refs):
    _gather_rows(ex, t, 0, ROW_TILE, xs_refs[0])
    for i in range(n):
        def side_work(p, i=i):
            if i + 1 < n:
                _gather_rows(ex, t + i + 1, p * PIECE_ROWS, PIECE_ROWS, xs_refs[(i + 1) % 2])
            if i >= 1:
                _scatter_rows(ex, t + i - 1, p * PIECE_ROWS, PIECE_ROWS, ys_refs[(i - 1) % 2])

        _tile_ffn(ex, t + i, xs_refs[i % 2], ys_refs[i % 2], side_work if n > 1 else None)
    _scatter_rows(ex, t + n - 1, 0, ROW_TILE, ys_refs[(n - 1) % 2])


def _moe_routed_kernel(cnt_ref, list_ref, wt_ref, h_ref, wg_ref, wu_ref, wd_ref,
                       x_ref, sh_ref, g2_ref, *refs):
    out_refs, scratch = refs[:-8], refs[-8:]
    acc_ref, wgb_ref, wub_ref, wdb_ref = scratch[:4]
    xs_refs, ys_refs = scratch[4:6], scratch[6:8]
    c = pl.program_id(0)
    j = pl.program_id(1)

    @pl.when(j == 0)
    def _():
        acc_ref[...] = jnp.zeros_like(acc_ref)

    for k in range(EXPERTS_PER_STEP):
        expert = jnp.minimum(j, EXPERT_STEPS - 1) * EXPERTS_PER_STEP + k
        count = cnt_ref[c * N_EXPERTS + expert]

        @pl.when(jnp.logical_and(j < EXPERT_STEPS, count > 0))
        def _():
            wgb_ref[...] = wg_ref[0, k].astype(BF16)
            wub_ref[...] = wu_ref[0, k].astype(BF16)
            wdb_ref[...] = wd_ref[0, k].astype(BF16)

            ex = _ExpertRefs(k * 2 * LIST_ROWS, k * LIST_ROWS, list_ref, wt_ref, h_ref, acc_ref,
                             wgb_ref, wub_ref, wdb_ref)
            n_tiles = (count + ROW_TILE - 1) // ROW_TILE

            last3 = jnp.logical_and(n_tiles % 2 == 1, n_tiles >= 3)
            n_pairs = (n_tiles - jnp.where(last3, 3, n_tiles % 2)) // 2

            def tile_pair(m, carry):
                _expert_tiles(ex, 2 * m, 2, xs_refs, ys_refs)
                return carry

            lax.fori_loop(0, n_pairs, tile_pair, 0)

            @pl.when(last3)
            def _():
                _expert_tiles(ex, n_tiles - 3, 3, xs_refs, ys_refs)

            @pl.when(n_tiles == 1)
            def _():
                _expert_tiles(ex, 0, 1, xs_refs, ys_refs)

    @pl.when(j >= EXPERT_STEPS)
    def _():
        base = (j - EXPERT_STEPS) * (TILE * ROW_SUB)
        moe = jnp.concatenate(
            [acc_ref[pl.ds(base + s, TILE, stride=ROW_SUB), :] for s in range(ROW_SUB)], axis=1)
        new_x = x_ref[...] + g2_ref[pl.ds(c, 1), :] * (moe + sh_ref[...])
        if len(out_refs) == 1:
            out_refs[0][...] = new_x
        else:
            prompt_ref, latent_ref = out_refs

            @pl.when(c < T_P // MOE_TC)
            def _():
                prompt_ref[...] = new_x

            @pl.when(c >= T_P // MOE_TC)
            def _():
                latent_ref[...] = new_x


def _moe_routed(lists, wts, counts, hrows, w_gate, w_up, w_down, li, x, shared, mod,
                split_output):
    group = lambda j: jnp.minimum(j, EXPERT_STEPS - 1)
    list_map = lambda c, j, nt: (c * EXPERT_STEPS + group(j), 0)
    w_map = lambda c, j, nt: (li, group(j), 0, 0)
    w_in_spec = pl.BlockSpec((1, EXPERTS_PER_STEP, D_MODEL, D_EXPERT), w_map)
    w_out_spec = pl.BlockSpec((1, EXPERTS_PER_STEP, D_EXPERT, D_MODEL), w_map)
    out_blk = lambda c, j: c * N_FLUSH + jnp.maximum(j - EXPERT_STEPS, 0)
    tok_spec = pl.BlockSpec((TILE, D_MODEL), lambda c, j, nt: (out_blk(c, j), 0))
    if split_output:
        out_specs = [
            pl.BlockSpec((TILE, D_MODEL), lambda c, j, nt: (jnp.minimum(out_blk(c, j), P_TILES - 1), 0)),
            pl.BlockSpec((TILE, D_MODEL), lambda c, j, nt: (jnp.maximum(out_blk(c, j) - P_TILES, 0), 0)),
        ]
        out_shape = [jax.ShapeDtypeStruct((T_P, D_MODEL), F32),
                     jax.ShapeDtypeStruct((T_S, D_MODEL), F32)]
    else:
        out_specs = tok_spec
        out_shape = jax.ShapeDtypeStruct((T_ALL, D_MODEL), F32)
    stage = pltpu.VMEM((ROW_TILE // 8, 8 * ROW_SUB, LANES), F32)
    grid_spec = pltpu.PrefetchScalarGridSpec(
        num_scalar_prefetch=1,
        grid=(N_CHUNKS, EXPERT_STEPS + N_FLUSH),
        in_specs=[
            pl.BlockSpec((EXPERTS_PER_STEP * 2 * LIST_ROWS, LANES), list_map,
                         memory_space=pltpu.SMEM),
            pl.BlockSpec((EXPERTS_PER_STEP * LIST_ROWS, LANES), list_map),
            pl.BlockSpec((MOE_TC * ROW_SUB, LANES), lambda c, j, nt: (c, 0),
                         pipeline_mode=pl.Buffered(1)),
            w_in_spec, w_in_spec, w_out_spec,
            tok_spec, tok_spec,
            pl.BlockSpec((8, D_MODEL), lambda c, j, nt: (0, 5)),
        ],
        out_specs=out_specs,
        scratch_shapes=[
            pltpu.VMEM(((MOE_TC + 1) * ROW_SUB, LANES), F32),
            pltpu.VMEM((D_MODEL, D_EXPERT), BF16),
            pltpu.VMEM((D_MODEL, D_EXPERT), BF16),
            pltpu.VMEM((D_EXPERT, D_MODEL), BF16),
            stage, stage, stage, stage,
        ],
    )
    return pl.pallas_call(
        _moe_routed_kernel,
        grid_spec=grid_spec,
        out_shape=out_shape,
        compiler_params=_cparams("arbitrary", "arbitrary", vmem_limit=VMEM_LIMIT_ROUTED),
        name="moe_routed",
    )(counts.reshape(-1), lists, wts, hrows, w_gate, w_up, w_down, x, shared, mod)


def kernel(x_prompt, x_sample, cache_k, cache_v, c, c_ctx, w_mod, b_mod, norm1, norm2, w_in_even, conv_a, q_norm, k_norm, lam_q1, lam_k1, lam_q2, lam_k2, subln, w_out_even, w_in_odd, conv_c, conv_c_b, ln_c_g, ln_c_b, w_pool, pool_scale, w_out_odd, w_router, b_router, w_gate, w_up, w_down, ws_gate, ws_up, ws_down):
    x = (x_prompt.reshape(T_P, D_MODEL), x_sample.reshape(T_S, D_MODEL))
    cond = jnp.concatenate([c_ctx[None, :], c, jnp.zeros((8 - 1 - DEC_BATCH, D_MODEL), F32)], axis=0)
    mod_all = _modulation(cond, w_mod, b_mod)

    new_k, new_v = [], []
    for li in range(DEPTH):
        mod = mod_all[li]
        if li % 2 == 0:
            e = li // 2
            lam_init = 0.8 - 0.6 * math.exp(-0.3 * li)
            u, q, k, v, k_cache, v_cache = _even_in(
                x, norm1[li], mod, w_in_even[e].astype(BF16), q_norm[e], k_norm[e])
            new_k.append(k_cache)
            new_v.append(v_cache)
            o = _attention(q, k, v, cache_k, cache_v, e,
                           (lam_q1[e], lam_k1[e], lam_q2[e], lam_k2[e]), subln[e], lam_init)
            x = _even_out(x, u, o, conv_a[e], mod, w_out_even[e].astype(BF16))
        else:
            o_ = li // 2
            u = _norm_in(x, norm1[li], mod, w_in_odd[o_].astype(BF16))
            x = _odd_out(x, u, conv_c[o_], conv_c_b[o_], ln_c_g[o_], ln_c_b[o_],
                         w_pool[o_].astype(BF16), pool_scale[o_], mod, w_out_odd[o_].astype(BF16))
        hrows, gate_b, rank_b, shared, cnt_b = _moe_pre(
            x, norm2[li], mod, w_router[li].T, b_router[li], ws_gate[li].astype(BF16),
            ws_up[li].astype(BF16), ws_down[li].astype(BF16))
        lists, wts = _moe_sort(rank_b, gate_b, cnt_b)
        x = _moe_routed(lists, wts, cnt_b[:, :, 0], hrows, w_gate, w_up, w_down, li, x, shared,
                        mod, split_output=(li == DEPTH - 1))

    y_prompt = x[0].reshape(BATCH, SEQ, D_MODEL)
    y_sample = x[1].reshape(DEC_BATCH, DEC_SEQ, D_MODEL)
    return (y_prompt, y_sample, jnp.stack(new_k, axis=1), jnp.stack(new_v, axis=1))
```

```python
import functools
import math

import numpy as np
import jax
import jax.numpy as jnp
from jax import lax
from jax.experimental import pallas as pl
from jax.experimental.pallas import tpu as pltpu

D_MODEL = 1024
BATCH = 16
SEQ = 256
DEPTH = 2
DEC_BATCH = 2
DEC_SEQ = 4096
PAST_LEN = 512
GRID_W = 64
H_B = 4
DK_B = 64
DV_B = 2 * DK_B
D_A = D_MODEL // 2
D_B = H_B * DV_B
D_C = D_MODEL // 2
D_D = D_MODEL // 2
CONV_C = 31
POOL_WINDOWS = (2, 4, 8, 16)
D_DG = D_D // len(POOL_WINDOWS)
N_EXPERTS = 64
TOP_K = 8
N_GROUPS = 8
TOPK_GROUPS = 4
D_EXPERT = 256
D_SHARED = 256
ROUTED_SCALE = 2.5
ROPE_BASE = 10000.0
EPS = 1e-6

F32 = jnp.float32
BF16 = jnp.bfloat16

T_P = BATCH * SEQ
T_S = DEC_BATCH * DEC_SEQ
T_ALL = T_P + T_S
TILE = 256
N_TILES = T_ALL // TILE
P_TILES = T_P // TILE
S_TILES = DEC_SEQ // TILE
LANES = 128
VMEM_LIMIT = 56 * 1024 * 1024
VMEM_LIMIT_ROUTED = 60 * 1024 * 1024


def _cparams(*sem, vmem_limit=VMEM_LIMIT):
    return pltpu.CompilerParams(dimension_semantics=sem, vmem_limit_bytes=vmem_limit)


def _mod_row(i, tm):
    npt = T_P // tm
    per = DEC_SEQ // tm
    return jnp.where(i < npt, 0, 1 + (i - npt) // per)


def _seq_flags(i):
    j = (i - P_TILES) % S_TILES
    first = jnp.logical_or(i < P_TILES, j == 0)
    last = jnp.logical_or(i < P_TILES, j == S_TILES - 1)
    return first, last


def _seq_tile(i):
    return jnp.where(i < P_TILES, 0, (i - P_TILES) % S_TILES)


def _split_bf16(a):
    hi = a.astype(BF16)
    lo = (a - hi.astype(F32)).astype(BF16)
    return hi, lo


def _dot(a, b):
    return jnp.dot(a, b, preferred_element_type=F32)


def _dot_nt(a, b):
    return lax.dot_general(a, b, (((1,), (1,)), ((), ())), preferred_element_type=F32)


def _dot3(a, b):
    a_hi, a_lo = _split_bf16(a)
    b_hi, b_lo = _split_bf16(b)
    return _dot(a_hi, b_hi) + _dot(a_lo, b_hi) + _dot(a_hi, b_lo)


def _silu(x):
    return x * jax.nn.sigmoid(x)


MOD_TN = 1536


def _mod_kernel(c_ref, w_ref, b_ref, o_ref):
    o_ref[0] = _dot3(_silu(c_ref[...]), w_ref[0]) + b_ref[0]


def _modulation(cond, w_mod, b_mod):
    n = 6 * D_MODEL
    return pl.pallas_call(
        _mod_kernel,
        grid=(DEPTH, n // MOD_TN),
        in_specs=[
            pl.BlockSpec((8, D_MODEL), lambda l, j: (0, 0)),
            pl.BlockSpec((1, D_MODEL, MOD_TN), lambda l, j: (l, 0, j)),
            pl.BlockSpec((1, 1, MOD_TN), lambda l, j: (l, 0, j)),
        ],
        out_specs=pl.BlockSpec((1, 8, MOD_TN), lambda l, j: (l, 0, j)),
        out_shape=jax.ShapeDtypeStruct((DEPTH, 8, n), F32),
        compiler_params=_cparams("arbitrary", "arbitrary"),
        name="modulation",
    )(cond, w_mod, b_mod.reshape(DEPTH, 1, n))


IN_TM = 1024


def _modulated_norm(x, g, shift, scale):
    ms = jnp.mean(x * x, axis=-1, keepdims=True)
    return (x * lax.rsqrt(ms + EPS) * g) * (1.0 + scale) + shift


def _tok_specs(x, tm, width):
    if isinstance(x, tuple):
        n_p = T_P // tm
        return ([pl.BlockSpec((tm, width), lambda i: (jnp.minimum(i, n_p - 1), 0)),
                 pl.BlockSpec((tm, width), lambda i: (jnp.maximum(i - n_p, 0), 0))], list(x))
    return [pl.BlockSpec((tm, width), lambda i: (i, 0))], [x]


def _tok_load(refs, i, tm):
    if len(refs) == 2:
        return jnp.where(i < T_P // tm, refs[0][...], refs[1][...])
    return refs[0][...]


def _norm_in_kernel(*refs, n_x):
    x_refs, (g_ref, sh_ref, sc_ref, w_ref, o_ref) = refs[:n_x], refs[n_x:]
    i = pl.program_id(0)
    r = _mod_row(i, IN_TM)
    h = _modulated_norm(_tok_load(x_refs, i, IN_TM), g_ref[...], sh_ref[pl.ds(r, 1), :],
                        sc_ref[pl.ds(r, 1), :])
    o_ref[...] = _dot(h.astype(BF16), w_ref[...])


def _norm_in(x, g, mod, w_bf16):
    n = w_bf16.shape[1]
    x_specs, x_ops = _tok_specs(x, IN_TM, D_MODEL)
    return pl.pallas_call(
        functools.partial(_norm_in_kernel, n_x=len(x_ops)),
        grid=(T_ALL // IN_TM,),
        in_specs=x_specs + [
            pl.BlockSpec((1, D_MODEL), lambda i: (0, 0)),
            pl.BlockSpec((8, D_MODEL), lambda i: (0, 0)),
            pl.BlockSpec((8, D_MODEL), lambda i: (0, 1)),
            pl.BlockSpec((D_MODEL, n), lambda i: (0, 0)),
        ],
        out_specs=pl.BlockSpec((IN_TM, n), lambda i: (i, 0)),
        out_shape=jax.ShapeDtypeStruct((T_ALL, n), F32),
        compiler_params=_cparams("arbitrary"),
        name="norm_in_proj",
    )(*x_ops, g.reshape(1, D_MODEL), mod, mod, w_bf16)


QKV_TM = 1024


def _rope_tables():
    half = DK_B // 2
    freqs = ROPE_BASE ** (-np.arange(0, half, 2, dtype=np.float64) / half)
    l = np.arange(DEC_SEQ)
    pos_r = (l // GRID_W).astype(np.float64)
    pos_c = (l % GRID_W).astype(np.float64)
    lane = np.arange(LANES)
    jj = lane % DK_B
    m = jj % half
    f = m % (half // 2)
    pos = np.where((jj < half)[None, :], pos_r[:, None], pos_c[:, None])
    ang = pos * freqs[f][None, :]
    sign = np.where(m < half // 2, -1.0, 1.0)[None, :]
    cos = np.concatenate([np.ones((QKV_TM, LANES)), np.cos(ang)], axis=0)
    sin = np.concatenate([np.zeros((QKV_TM, LANES)), sign * np.sin(ang)], axis=0)
    return cos.astype(np.float32), sin.astype(np.float32)


def _segment_mean_matrix():
    lane = np.arange(LANES)
    same = (lane[:, None] // DK_B) == (lane[None, :] // DK_B)
    return (same.astype(np.float32) / DK_B)


def _qk_prep(x, g, cos, sin, seg):
    x2 = x * x
    hi, lo = _split_bf16(x2)
    ms = _dot(hi, seg) + _dot(lo, seg)
    y = x * lax.rsqrt(ms + EPS) * g
    lane = lax.broadcasted_iota(jnp.int32, y.shape, 1)
    lower = (lane % (DK_B // 2)) < (DK_B // 4)
    partner = jnp.where(lower, pltpu.roll(y, LANES - DK_B // 4, 1), pltpu.roll(y, DK_B // 4, 1))
    return y * cos + partner * sin


def _qkv_prep_kernel(q_ref, k_ref, v_ref, qn_ref, kn_ref, cos_ref, sin_ref, seg_ref,
                     qo_ref, ko_ref, vo_ref, kc_ref, vc_ref):
    cos = cos_ref[...]
    sin = sin_ref[...]
    seg = seg_ref[...]
    scale = math.log2(math.e) / math.sqrt(DK_B)
    is_prompt = pl.program_id(0) < T_P // QKV_TM
    for h in range(H_B):
        cols = slice(h * LANES, (h + 1) * LANES)
        qo_ref[h] = (_qk_prep(q_ref[:, cols], qn_ref[...], cos, sin, seg) * scale).astype(BF16)
        k = _qk_prep(k_ref[:, cols], kn_ref[...], cos, sin, seg)
        v = v_ref[:, cols]
        ko_ref[h] = k.astype(BF16)
        vo_ref[h] = v.astype(BF16)

        @pl.when(is_prompt)
        def _():
            for b in range(QKV_TM // SEQ):
                kc_ref[b, h] = k[b * SEQ:(b + 1) * SEQ]
                vc_ref[b, h] = v[b * SEQ:(b + 1) * SEQ]


def _qkv_prep(u, qn, kn):
    cos, sin = _rope_tables()
    seg = jnp.asarray(_segment_mean_matrix(), BF16)
    qn2 = jnp.concatenate([qn, qn]).reshape(1, LANES)
    kn2 = jnp.concatenate([kn, kn]).reshape(1, LANES)
    col0 = 3 * D_A // D_B
    p_steps = T_P // QKV_TM
    s_steps = DEC_SEQ // QKV_TM

    def tab_map(i):
        return (jnp.where(i < p_steps, 0, 1 + (i - p_steps) % s_steps), 0)

    out_b = jax.ShapeDtypeStruct((H_B, T_ALL, LANES), BF16)
    out_spec = pl.BlockSpec((H_B, QKV_TM, LANES), lambda i: (0, i, 0))
    cache = jax.ShapeDtypeStruct((BATCH, H_B, SEQ, LANES), F32)
    cache_spec = pl.BlockSpec((QKV_TM // SEQ, H_B, SEQ, LANES),
                              lambda i: (jnp.minimum(i, p_steps - 1), 0, 0, 0))
    return pl.pallas_call(
        _qkv_prep_kernel,
        grid=(T_ALL // QKV_TM,),
        in_specs=[
            pl.BlockSpec((QKV_TM, D_B), lambda i: (i, col0)),
            pl.BlockSpec((QKV_TM, D_B), lambda i: (i, col0 + 1)),
            pl.BlockSpec((QKV_TM, D_B), lambda i: (i, col0 + 2)),
            pl.BlockSpec((1, LANES), lambda i: (0, 0)),
            pl.BlockSpec((1, LANES), lambda i: (0, 0)),
            pl.BlockSpec((QKV_TM, LANES), tab_map),
            pl.BlockSpec((QKV_TM, LANES), tab_map),
            pl.BlockSpec((LANES, LANES), lambda i: (0, 0)),
        ],
        out_specs=[out_spec] * 3 + [cache_spec] * 2,
        out_shape=[out_b, out_b, out_b, cache, cache],
        compiler_params=_cparams("arbitrary"),
        name="qkv_prep",
    )(u, u, u, qn2, kn2, jnp.asarray(cos), jnp.asarray(sin), seg)


def _lambda(lq1_ref, lk1_ref, lq2_ref, lk2_ref, lam_init):
    a = jnp.sum(lq1_ref[...] * lk1_ref[...], axis=-1, keepdims=True)
    b = jnp.sum(lq2_ref[...] * lk2_ref[...], axis=-1, keepdims=True)
    return jnp.exp(a) - jnp.exp(b) + lam_init


ATTN_TQ = 512


def _attn_body(q, keys, vals, lam, subg, lam_init):
    lane = lax.broadcasted_iota(jnp.int32, q.shape, 1)
    zero = jnp.zeros_like(q)
    qa = jnp.where(lane < DK_B, q, zero)
    qb = jnp.where(lane < DK_B, zero, q)
    scores = [[_dot_nt(qq, k) for k in keys] for qq in (qa, qb)]
    outs = []
    for ss in scores:
        m = functools.reduce(jnp.maximum, [jnp.max(s, axis=-1, keepdims=True) for s in ss])
        ps = [jnp.exp2(s - m) for s in ss]
        l = functools.reduce(jnp.add, [jnp.sum(p, axis=-1, keepdims=True) for p in ps])
        pv = functools.reduce(jnp.add, [_dot(p.astype(BF16), v) for p, v in zip(ps, vals)])
        outs.append(pv / l)
    o = outs[0] - lam * outs[1]
    ms = jnp.mean(o * o, axis=-1, keepdims=True)
    return (o * lax.rsqrt(ms + EPS) * subg) * (1.0 - lam_init)


def _attn_prompt_kernel(q_ref, k_ref, v_ref, lq1, lk1, lq2, lk2, sg_ref, o_ref, *, lam_init):
    lam = _lambda(lq1, lk1, lq2, lk2, lam_init)
    for h in range(H_B):
        o_ref[:, h * DV_B:(h + 1) * DV_B] = _attn_body(
            q_ref[h], [k_ref[h]], [v_ref[h]], lam, sg_ref[...], lam_init)


def _attn_latent_kernel(q_ref, k_ref, v_ref, ck_ref, cv_ref, lq1, lk1, lq2, lk2, sg_ref, o_ref,
                        *, lam_init):
    lam = _lambda(lq1, lk1, lq2, lk2, lam_init)
    keys = [ck_ref[0, 0, 0].astype(BF16), k_ref[0]]
    vals = [cv_ref[0, 0, 0].astype(BF16), v_ref[0]]
    o_ref[...] = _attn_body(q_ref[0], keys, vals, lam, sg_ref[...], lam_init)


def _attention(q, k, v, cache_k, cache_v, e, lam_params, subg, lam_init):
    small = [p.reshape(1, DK_B) for p in lam_params] + [subg.reshape(1, DV_B)]
    small_specs2 = [pl.BlockSpec((1, DK_B), lambda b: (0, 0))] * 4 + \
                   [pl.BlockSpec((1, DV_B), lambda b: (0, 0))]
    small_specs3 = [pl.BlockSpec((1, DK_B), lambda s, h, j: (0, 0))] * 4 + \
                   [pl.BlockSpec((1, DV_B), lambda s, h, j: (0, 0))]
    o_prompt = pl.pallas_call(
        functools.partial(_attn_prompt_kernel, lam_init=lam_init),
        grid=(BATCH,),
        in_specs=[pl.BlockSpec((H_B, SEQ, LANES), lambda b: (0, b, 0))] * 3 + small_specs2,
        out_specs=pl.BlockSpec((SEQ, D_B), lambda b: (b, 0)),
        out_shape=jax.ShapeDtypeStruct((T_P, D_B), F32),
        compiler_params=_cparams("arbitrary"),
        name="attn_prompt",
    )(q, k, v, *small)

    q_tiles = DEC_SEQ // ATTN_TQ
    kv_spec = pl.BlockSpec((1, DEC_SEQ, LANES), lambda s, h, j: (h, T_P // DEC_SEQ + s, 0))
    c_spec = pl.BlockSpec((1, 1, 1, PAST_LEN, LANES), lambda s, h, j: (s, e, h, 0, 0))
    o_latent = pl.pallas_call(
        functools.partial(_attn_latent_kernel, lam_init=lam_init),
        grid=(DEC_BATCH, H_B, q_tiles),
        in_specs=[pl.BlockSpec((1, ATTN_TQ, LANES),
                               lambda s, h, j: (h, T_P // ATTN_TQ + s * q_tiles + j, 0)),
                  kv_spec, kv_spec, c_spec, c_spec] + small_specs3,
        out_specs=pl.BlockSpec((ATTN_TQ, LANES), lambda s, h, j: (s * q_tiles + j, h)),
        out_shape=jax.ShapeDtypeStruct((T_S, D_B), F32),
        compiler_params=_cparams("arbitrary", "arbitrary", "arbitrary"),
        name="attn_latent",
    )(q, k, v, cache_k, cache_v, *small)
    return o_prompt, o_latent


HALO8 = 8
HALO16 = 16


def _prev_block(i, rows):
    return jnp.maximum(i * (TILE // rows) - 1, 0)


def _next_block(i, rows):
    return jnp.minimum((i + 1) * (TILE // rows), T_ALL // rows - 1)


def _even_out_kernel(*refs, n_x, n_o):
    x_refs, o_refs = refs[:n_x], refs[n_x:n_x + n_o]
    (bg_ref, cg_ref, hin_ref, cgp_ref, hinp_ref, cgn_ref, hinn_ref, cw_ref, g1_ref, w_ref,
     out_ref) = refs[n_x + n_o:]
    i = pl.program_id(0)
    first, last = _seq_flags(i)
    r = _mod_row(i, TILE)
    z = cg_ref[...] * hin_ref[...]
    zp = jnp.where(first, 0.0, cgp_ref[HALO8 - 1:HALO8, :] * hinp_ref[HALO8 - 1:HALO8, :])
    zn = jnp.where(last, 0.0, cgn_ref[0:1, :] * hinn_ref[0:1, :])
    row = lax.broadcasted_iota(jnp.int32, z.shape, 0)
    z_prev = jnp.where(row == 0, zp, pltpu.roll(z, 1, 0))
    z_next = jnp.where(row == TILE - 1, zn, pltpu.roll(z, TILE - 1, 0))
    cw = cw_ref[...]
    ya = bg_ref[...] * (cw[0:1, :] * z_prev + cw[1:2, :] * z + cw[2:3, :] * z_next)
    o = _tok_load(o_refs, i, TILE)
    y = _dot(ya.astype(BF16), w_ref[0:D_A, :]) + _dot(o.astype(BF16), w_ref[D_A:, :])
    out_ref[...] = _tok_load(x_refs, i, TILE) + g1_ref[pl.ds(r, 1), :] * y


def _even_out(x, u, o, conv_w, mod, w_out_bf16):
    tile_spec = lambda c: pl.BlockSpec((TILE, D_A), lambda i: (i, c))
    prev_spec = lambda c: pl.BlockSpec((HALO8, D_A), lambda i: (_prev_block(i, HALO8), c))
    next_spec = lambda c: pl.BlockSpec((HALO8, D_A), lambda i: (_next_block(i, HALO8), c))
    x_specs, x_ops = _tok_specs(x, TILE, D_MODEL)
    o_specs, o_ops = _tok_specs(o, TILE, D_B)
    return pl.pallas_call(
        functools.partial(_even_out_kernel, n_x=len(x_ops), n_o=len(o_ops)),
        grid=(N_TILES,),
        in_specs=x_specs + o_specs + [
            tile_spec(0), tile_spec(1), tile_spec(2),
            prev_spec(1), prev_spec(2), next_spec(1), next_spec(2),
            pl.BlockSpec((3, D_A), lambda i: (0, 0)),
            pl.BlockSpec((8, D_MODEL), lambda i: (0, 2)),
            pl.BlockSpec((D_MODEL, D_MODEL), lambda i: (0, 0)),
        ],
        out_specs=pl.BlockSpec((TILE, D_MODEL), lambda i: (i, 0)),
        out_shape=jax.ShapeDtypeStruct((T_ALL, D_MODEL), F32),
        compiler_params=_cparams("arbitrary"),
        name="even_mixer_out",
    )(*x_ops, *o_ops, u, u, u, u, u, u, u, conv_w, mod, w_out_bf16)


def _odd_out_kernel(x_ref, a_ref, b_ref, pd_ref, ap_ref, bp_ref, an_ref, bn_ref, pp_ref, pn_ref,
                    cw_ref, cb_ref, lg_ref, lb_ref, wp_ref, ps_ref, g1_ref, w_ref, out_ref,
                    ext_ref, extp_ref, shift_ref):
    i = pl.program_id(0)
    first, last = _seq_flags(i)
    r = _mod_row(i, TILE)
    ext_ref[0:HALO16, :] = jnp.where(first, 0.0, ap_ref[...] * jax.nn.sigmoid(bp_ref[...]))
    ext_ref[HALO16:HALO16 + TILE, :] = a_ref[...] * jax.nn.sigmoid(b_ref[...])
    ext_ref[HALO16 + TILE:, :] = jnp.where(last, 0.0, an_ref[...] * jax.nn.sigmoid(bn_ref[...]))
    base = HALO16 - CONV_C // 2
    parts = []
    for cb in range(D_C // LANES):
        cols = slice(cb * LANES, (cb + 1) * LANES)
        acc = jnp.zeros((TILE, LANES), F32)
        for phase in range(8):
            taps = [j for j in range(CONV_C) if (base + j) % 8 == phase]
            reach = max((base + j) // 8 for j in taps)
            rows = TILE + 8 * reach
            shift_ref[0:rows, :] = ext_ref[pl.ds(phase, rows), cols]
            for j in taps:
                a = (base + j) // 8
                acc = acc + cw_ref[j:j + 1, cols] * shift_ref[8 * a:8 * a + TILE, :]
        parts.append(acc)
    g = jnp.concatenate(parts, axis=-1) + cb_ref[...]
    mu = jnp.mean(g, axis=-1, keepdims=True)
    var = jnp.mean(jnp.square(g - mu), axis=-1, keepdims=True)
    g = _silu(((g - mu) * lax.rsqrt(var + EPS)) * lg_ref[...] + lb_ref[...])
    extp_ref[0:HALO8, :] = jnp.where(first, 0.0, pp_ref[...])
    extp_ref[HALO8:HALO8 + TILE, :] = pd_ref[...]
    extp_ref[HALO8 + TILE:, :] = jnp.where(last, 0.0, pn_ref[...])
    seq_len = jnp.where(i < P_TILES, SEQ, DEC_SEQ)
    pos = _seq_tile(i) * TILE + lax.broadcasted_iota(jnp.int32, (TILE, 1), 0)
    yd = []
    for gi, w in enumerate(POOL_WINDOWS):
        cols = slice(gi * D_DG, (gi + 1) * D_DG)
        s = jnp.zeros((TILE, D_DG), F32)
        for d in range(-(w // 2), w - w // 2):
            s = s + extp_ref[pl.ds(HALO8 + d, TILE), cols]
        lo = jnp.maximum(pos - w // 2, 0)
        hi = jnp.minimum(pos - w // 2 + w, seq_len)
        pooled = s / (hi - lo).astype(F32) - pd_ref[:, cols]
        yd.append(_dot(pooled.astype(BF16), wp_ref[gi]))
    yd = jnp.concatenate(yd, axis=-1) * ps_ref[...]
    y = _dot(g.astype(BF16), w_ref[0:D_C, :]) + _dot(yd.astype(BF16), w_ref[D_C:, :])
    out_ref[...] = x_ref[...] + g1_ref[pl.ds(r, 1), :] * y


def _odd_out(x, u, conv_w, conv_b, ln_g, ln_b, w_pool_bf16, p_scale, mod, w_out_bf16):
    tile_spec = lambda c: pl.BlockSpec((TILE, D_C), lambda i: (i, c))
    prev_spec = lambda rows, c: pl.BlockSpec((rows, D_C), lambda i: (_prev_block(i, rows), c))
    next_spec = lambda rows, c: pl.BlockSpec((rows, D_C), lambda i: (_next_block(i, rows), c))
    vec = lambda: pl.BlockSpec((1, D_C), lambda i: (0, 0))
    return pl.pallas_call(
        _odd_out_kernel,
        grid=(N_TILES,),
        in_specs=[
            pl.BlockSpec((TILE, D_MODEL), lambda i: (i, 0)),
            tile_spec(0), tile_spec(1), tile_spec(2),
            prev_spec(HALO16, 0), prev_spec(HALO16, 1), next_spec(HALO16, 0), next_spec(HALO16, 1),
            prev_spec(HALO8, 2), next_spec(HALO8, 2),
            pl.BlockSpec((CONV_C, D_C), lambda i: (0, 0)),
            vec(), vec(), vec(),
            pl.BlockSpec((len(POOL_WINDOWS), D_DG, D_DG), lambda i: (0, 0, 0)),
            vec(),
            pl.BlockSpec((8, D_MODEL), lambda i: (0, 2)),
            pl.BlockSpec((D_MODEL, D_MODEL), lambda i: (0, 0)),
        ],
        out_specs=pl.BlockSpec((TILE, D_MODEL), lambda i: (i, 0)),
        out_shape=jax.ShapeDtypeStruct((T_ALL, D_MODEL), F32),
        scratch_shapes=[pltpu.VMEM((TILE + 2 * HALO16, D_C), F32),
                        pltpu.VMEM((TILE + 2 * HALO8, D_D), F32),
                        pltpu.VMEM((TILE + 2 * HALO16, LANES), F32)],
        compiler_params=_cparams("arbitrary"),
        name="odd_mixer_out",
    )(x, u, u, u, u, u, u, u, u, u, conv_w, conv_b.reshape(1, D_C), ln_g.reshape(1, D_C),
      ln_b.reshape(1, D_C), w_pool_bf16, p_scale.reshape(1, D_D), mod, w_out_bf16)


GROUP = N_EXPERTS // N_GROUPS
NEG_INF = float("-inf")


def _first_argmax(v, idx, axis):
    m = jnp.max(v, axis=axis, keepdims=True)
    big = jnp.int32(2 ** 30)
    am = jnp.min(jnp.where(v == m, idx, big), axis=axis, keepdims=True)
    return m, am


def _expert_of_slot(slot):
    return (slot % N_GROUPS) * GROUP + slot // N_GROUPS


def _slot_order(a):
    return a.reshape(N_GROUPS, GROUP, *a.shape[1:]).swapaxes(0, 1).reshape(a.shape)


def _route(scores, biased):
    shape = biased.shape
    member = lax.broadcasted_iota(jnp.int32, shape, 0)
    m1, a1 = _first_argmax(biased, member, 0)
    m2 = jnp.max(jnp.where(member == a1, NEG_INF, biased), axis=0, keepdims=True)
    gscore = m1 + m2
    gidx = lax.broadcasted_iota(jnp.int32, gscore.shape, 1)
    gsel = jnp.zeros(gscore.shape, jnp.bool_)
    for _ in range(TOPK_GROUPS):
        _, am = _first_argmax(gscore, gidx, 1)
        hit = gidx == am
        gsel = jnp.logical_or(gsel, hit)
        gscore = jnp.where(hit, NEG_INF, gscore)
    cand = jnp.where(gsel, biased, NEG_INF)
    eidx = lax.broadcasted_iota(jnp.int32, shape, 1) * GROUP + member
    sel = jnp.zeros(shape, jnp.bool_)
    for _ in range(TOP_K):
        m = jnp.max(jnp.max(cand, axis=0, keepdims=True), axis=1, keepdims=True)
        big = jnp.int32(2 ** 30)
        am = jnp.where(cand == m, eidx, big)
        am = jnp.min(jnp.min(am, axis=0, keepdims=True), axis=1, keepdims=True)
        hit = eidx == am
        sel = jnp.logical_or(sel, hit)
        cand = jnp.where(hit, NEG_INF, cand)
    wsel = jnp.where(sel, scores, 0.0)
    tot = jnp.sum(jnp.sum(wsel, axis=0, keepdims=True), axis=1, keepdims=True)
    return wsel / tot * ROUTED_SCALE, sel


MOE_TC = 4096
N_CHUNKS = T_ALL // MOE_TC
ROW_TILE = 256
N_FLUSH = MOE_TC // TILE
PRE_TM = 1024
ROW_SUB = D_MODEL // LANES


def _moe_pre_kernel(x_ref, g_ref, sh_ref, sc_ref, wr_ref, br_ref, tri_ref, wsg_ref, wsu_ref,
                    wsd_ref, hrow_ref, gate_ref, rank_ref, shared_ref, cnt_ref, carry_ref):
    i = pl.program_id(0)
    r = _mod_row(i, PRE_TM)
    h = _modulated_norm(x_ref[...], g_ref[...], sh_ref[pl.ds(r, 1), :], sc_ref[pl.ds(r, 1), :])
    hb = h.astype(BF16)
    for s in range(ROW_SUB):
        hrow_ref[pl.ds(s, PRE_TM, stride=ROW_SUB), :] = h[:, s * LANES:(s + 1) * LANES]
    h_hi, h_lo = hb, (h - hb.astype(F32)).astype(BF16)
    w_hi, w_lo = _split_bf16(wr_ref[...])
    logits = _dot_nt(w_hi, h_hi) + _dot_nt(w_lo, h_hi) + _dot_nt(w_hi, h_lo)
    scores = jax.nn.sigmoid(logits)
    biased = scores + br_ref[:, 0:1]
    shape3 = (GROUP, N_GROUPS, PRE_TM)
    gate_t, sel = _route(scores.reshape(shape3), biased.reshape(shape3))
    gate_t = gate_t.reshape(N_EXPERTS, PRE_TM)
    sel = jnp.where(sel.reshape(N_EXPERTS, PRE_TM), 1.0, 0.0)

    @pl.when(i % (MOE_TC // PRE_TM) == 0)
    def _():
        carry_ref[...] = jnp.zeros_like(carry_ref)

    carry = carry_ref[...]
    local = _dot(sel.astype(BF16), tri_ref[...])
    rank = jnp.where(sel > 0.0, local + jnp.concatenate([carry] * (PRE_TM // LANES), axis=1), -1.0)
    carry = carry + jnp.sum(sel, axis=1, keepdims=True)
    carry_ref[...] = carry

    @pl.when(i % (MOE_TC // PRE_TM) == MOE_TC // PRE_TM - 1)
    def _():
        cnt_ref[0] = carry.astype(jnp.int32)
    gate_ref[...] = gate_t
    rank_ref[...] = rank
    a = _silu(_dot(hb, wsg_ref[...])) * _dot(hb, wsu_ref[...])
    shared_ref[...] = _dot(a.astype(BF16), wsd_ref[...])


def _moe_pre(x, g, mod, w_router_t, b_router, wsg, wsu, wsd):
    return pl.pallas_call(
        _moe_pre_kernel,
        grid=(T_ALL // PRE_TM,),
        in_specs=[
            pl.BlockSpec((PRE_TM, D_MODEL), lambda i: (i, 0)),
            pl.BlockSpec((1, D_MODEL), lambda i: (0, 0)),
            pl.BlockSpec((8, D_MODEL), lambda i: (0, 3)),
            pl.BlockSpec((8, D_MODEL), lambda i: (0, 4)),
            pl.BlockSpec((N_EXPERTS, D_MODEL), lambda i: (0, 0)),
            pl.BlockSpec((N_EXPERTS, LANES), lambda i: (0, 0)),
            pl.BlockSpec((PRE_TM, PRE_TM), lambda i: (0, 0)),
            pl.BlockSpec((D_MODEL, D_SHARED), lambda i: (0, 0)),
            pl.BlockSpec((D_MODEL, D_SHARED), lambda i: (0, 0)),
            pl.BlockSpec((D_SHARED, D_MODEL), lambda i: (0, 0)),
        ],
        out_specs=[
            pl.BlockSpec((PRE_TM * ROW_SUB, LANES), lambda i: (i, 0)),
            pl.BlockSpec((N_EXPERTS, PRE_TM), lambda i: (0, i)),
            pl.BlockSpec((N_EXPERTS, PRE_TM), lambda i: (0, i)),
            pl.BlockSpec((PRE_TM, D_MODEL), lambda i: (i, 0)),
            pl.BlockSpec((1, N_EXPERTS, LANES), lambda i: (i // (MOE_TC // PRE_TM), 0, 0)),
        ],
        out_shape=[
            jax.ShapeDtypeStruct((T_ALL * ROW_SUB, LANES), F32),
            jax.ShapeDtypeStruct((N_EXPERTS, T_ALL), F32),
            jax.ShapeDtypeStruct((N_EXPERTS, T_ALL), F32),
            jax.ShapeDtypeStruct((T_ALL, D_MODEL), F32),
            jax.ShapeDtypeStruct((N_CHUNKS, N_EXPERTS, LANES), jnp.int32),
        ],
        scratch_shapes=[pltpu.VMEM((N_EXPERTS, LANES), F32)],
        compiler_params=_cparams("arbitrary"),
        name="moe_pre",
    )(x, g.reshape(1, D_MODEL), mod, mod, _slot_order(w_router_t),
      jnp.broadcast_to(_slot_order(b_router).reshape(N_EXPERTS, 1), (N_EXPERTS, LANES)),
      jnp.asarray(np.triu(np.ones((PRE_TM, PRE_TM), np.float32), 1), BF16), wsg, wsu, wsd)


LIST_ROWS = MOE_TC // LANES
TILE_ROWS = ROW_TILE // LANES
DUMMY_ROW = MOE_TC * ROW_SUB


def _moe_sort_kernel(rank_ref, gate_ref, cnt_ref, list_ref, w_ref):
    rank = rank_ref[...]
    lane = lax.broadcasted_iota(jnp.int32, rank.shape, 1)
    d = jnp.where(rank >= 0.0, lane - rank.astype(jnp.int32), 0)
    w = gate_ref[...]
    for s in range(MOE_TC.bit_length() - 1):
        k = 1 << s
        d_in = pltpu.roll(d, MOE_TC - k, 1)
        w_in = pltpu.roll(w, MOE_TC - k, 1)
        take = (d_in & k) != 0
        leave = (d & k) != 0
        d = jnp.where(take, d_in, jnp.where(leave, 0, d))
        w = jnp.where(take, w_in, w)
    valid = lane < cnt_ref[0][:, 0:1]
    row = (lane + d) * ROW_SUB
    gsrc = jnp.where(valid, row, 0)
    ssrc = jnp.where(valid, row, DUMMY_ROW)
    w = jnp.where(valid, w, 0.0)
    for b in range(LIST_ROWS):
        cols = slice(b * LANES, (b + 1) * LANES)
        list_ref[pl.ds(b, N_EXPERTS, stride=2 * LIST_ROWS), :] = gsrc[:, cols]
        list_ref[pl.ds(LIST_ROWS + b, N_EXPERTS, stride=2 * LIST_ROWS), :] = ssrc[:, cols]
        w_ref[pl.ds(b, N_EXPERTS, stride=LIST_ROWS), :] = w[:, cols]


def _moe_sort(rank_t, gate_t, cnt_b):
    chunk_spec = pl.BlockSpec((N_EXPERTS, MOE_TC), lambda c: (0, c))
    n_rows = N_CHUNKS * N_EXPERTS * LIST_ROWS
    return pl.pallas_call(
        _moe_sort_kernel,
        grid=(N_CHUNKS,),
        in_specs=[chunk_spec, chunk_spec,
                  pl.BlockSpec((1, N_EXPERTS, LANES), lambda c: (c, 0, 0))],
        out_specs=[pl.BlockSpec((N_EXPERTS * 2 * LIST_ROWS, LANES), lambda c: (c, 0)),
                   pl.BlockSpec((N_EXPERTS * LIST_ROWS, LANES), lambda c: (c, 0))],
        out_shape=[jax.ShapeDtypeStruct((2 * n_rows, LANES), jnp.int32),
                   jax.ShapeDtypeStruct((n_rows, LANES), F32)],
        compiler_params=_cparams("arbitrary"),
        name="moe_sort",
    )(rank_t, gate_t, cnt_b)


SCATTER_BATCH = 16


EXPERTS_PER_STEP = 2
EXPERT_STEPS = N_EXPERTS // EXPERTS_PER_STEP


N_PIECES = 8
PIECE_ROWS = ROW_TILE // N_PIECES


class _ExpertRefs:
    def __init__(self, list0, wt0, list_ref, wt_ref, h_ref, acc_ref, wgb_ref, wub_ref, wdb_ref):
        self.list0, self.wt0 = list0, wt0
        self.list_ref, self.wt_ref, self.h_ref, self.acc_ref = list_ref, wt_ref, h_ref, acc_ref
        self.wgb_ref, self.wub_ref, self.wdb_ref = wgb_ref, wub_ref, wdb_ref


def _gather_rows(ex, t, r0, n, xs_ref):
    base = ex.list0 + t * TILE_ROWS
    for r in range(r0, r0 + n):
        tok = pl.multiple_of(ex.list_ref[base + r // LANES, r % LANES], ROW_SUB)
        xs_ref[r // 8, pl.ds(r % 8, ROW_SUB, stride=8), :] = ex.h_ref[pl.ds(tok, ROW_SUB), :]


def _scatter_rows(ex, t, r0, n, ys_ref):
    base = ex.list0 + LIST_ROWS + t * TILE_ROWS
    for b0 in range(r0, r0 + n, SCATTER_BATCH):
        rows = range(b0, b0 + SCATTER_BATCH)
        dsts = [ex.acc_ref.at[pl.ds(pl.multiple_of(ex.list_ref[base + r // LANES, r % LANES],
                                                   ROW_SUB), ROW_SUB), :] for r in rows]
        news = [dst[...] + ys_ref[r // 8, pl.ds(r % 8, ROW_SUB, stride=8), :]
                for dst, r in zip(dsts, rows)]
        for dst, new in zip(dsts, news):
            dst[...] = new


def _tile_ffn(ex, t, xs_ref, ys_ref, side_work=None):
    kc = D_MODEL // (N_PIECES // 2)
    hg = hu = None
    for p in range(N_PIECES // 2):
        if side_work is not None:
            side_work(p)
        xk = jnp.concatenate(
            [xs_ref[:, s * 8:(s + 1) * 8, :].reshape(ROW_TILE, LANES)
             for s in range(p * kc // LANES, (p + 1) * kc // LANES)], axis=1).astype(BF16)
        dg = _dot(xk, ex.wgb_ref[p * kc:(p + 1) * kc, :])
        du = _dot(xk, ex.wub_ref[p * kc:(p + 1) * kc, :])
        hg = dg if hg is None else hg + dg
        hu = du if hu is None else hu + du
    eye = (lax.broadcasted_iota(jnp.int32, (ROW_TILE, LANES), 0) % LANES
           == lax.broadcasted_iota(jnp.int32, (ROW_TILE, LANES), 1))
    row_blk = lax.broadcasted_iota(jnp.int32, (ROW_TILE, LANES), 0) // LANES
    wrows = functools.reduce(
        lambda a, b: a + b,
        [jnp.where(row_blk == k, ex.wt_ref[pl.ds(ex.wt0 + t * TILE_ROWS + k, 1), :], 0.0)
         for k in range(TILE_ROWS)])
    wcol = jnp.sum(jnp.where(eye, wrows, 0.0), axis=1, keepdims=True)
    a = (_silu(hg) * hu * wcol).astype(BF16)
    for p in range(N_PIECES // 2):
        if side_work is not None:
            side_work(N_PIECES // 2 + p)
        y = _dot(a, ex.wdb_ref[:, p * kc:(p + 1) * kc])
        for q in range(kc // LANES):
            s = p * kc // LANES + q
            ys_ref[:, s * 8:(s + 1) * 8, :] = y[:, q * LANES:(q + 1) * LANES].reshape(
                ROW_TILE // 8, 8, LANES)


def _expert_tiles(ex, t, n, xs_refs, ys_refs):
    _gather_rows(ex, t, 0, ROW_TILE, xs_refs[0])
    for i in range(n):
        def side_work(p, i=i):
            if i + 1 < n:
                _gather_rows(ex, t + i + 1, p * PIECE_ROWS, PIECE_ROWS, xs_refs[(i + 1) % 2])
            if i >= 1:
                _scatter_rows(ex, t + i - 1, p * PIECE_ROWS, PIECE_ROWS, ys_refs[(i - 1) % 2])

        _tile_ffn(ex, t + i, xs_refs[i % 2], ys_refs[i % 2], side_work if n > 1 else None)
    _scatter_rows(ex, t + n - 1, 0, ROW_TILE, ys_refs[(n - 1) % 2])


def _moe_routed_kernel(cnt_ref, list_ref, wt_ref, h_ref, *refs):
    n_w = 3 * EXPERTS_PER_STEP
    w_refs, (x_ref, sh_ref, g2_ref), refs = refs[:n_w], refs[n_w:n_w + 3], refs[n_w + 3:]
    out_refs, scratch = refs[:-8], refs[-8:]
    acc_ref, wgb_ref, wub_ref, wdb_ref = scratch[:4]
    xs_refs, ys_refs = scratch[4:6], scratch[6:8]
    c = pl.program_id(0)
    j = pl.program_id(1)

    @pl.when(j == 0)
    def _():
        acc_ref[...] = jnp.zeros_like(acc_ref)

    for k in range(EXPERTS_PER_STEP):
        expert = jnp.minimum(j, EXPERT_STEPS - 1) * EXPERTS_PER_STEP + k
        count = cnt_ref[c * N_EXPERTS + expert]

        @pl.when(jnp.logical_and(j < EXPERT_STEPS, count > 0))
        def _():
            wg_ref, wu_ref, wd_ref = w_refs[3 * k:3 * k + 3]
            wgb_ref[...] = wg_ref[0, 0].astype(BF16)
            wub_ref[...] = wu_ref[0, 0].astype(BF16)
            wdb_ref[...] = wd_ref[0, 0].astype(BF16)

            ex = _ExpertRefs(k * 2 * LIST_ROWS, k * LIST_ROWS, list_ref, wt_ref, h_ref, acc_ref,
                             wgb_ref, wub_ref, wdb_ref)
            n_tiles = (count + ROW_TILE - 1) // ROW_TILE

            last3 = jnp.logical_and(n_tiles % 2 == 1, n_tiles >= 3)
            n_pairs = (n_tiles - jnp.where(last3, 3, n_tiles % 2)) // 2

            def tile_pair(m, carry):
                _expert_tiles(ex, 2 * m, 2, xs_refs, ys_refs)
                return carry

            lax.fori_loop(0, n_pairs, tile_pair, 0)

            @pl.when(last3)
            def _():
                _expert_tiles(ex, n_tiles - 3, 3, xs_refs, ys_refs)

            @pl.when(n_tiles == 1)
            def _():
                _expert_tiles(ex, 0, 1, xs_refs, ys_refs)

    @pl.when(j >= EXPERT_STEPS)
    def _():
        base = (j - EXPERT_STEPS) * (TILE * ROW_SUB)
        moe = jnp.concatenate(
            [acc_ref[pl.ds(base + s, TILE, stride=ROW_SUB), :] for s in range(ROW_SUB)], axis=1)
        new_x = x_ref[...] + g2_ref[pl.ds(c, 1), :] * (moe + sh_ref[...])
        if len(out_refs) == 1:
            out_refs[0][...] = new_x
        else:
            prompt_ref, latent_ref = out_refs

            @pl.when(c < T_P // MOE_TC)
            def _():
                prompt_ref[...] = new_x

            @pl.when(c >= T_P // MOE_TC)
            def _():
                latent_ref[...] = new_x


def _moe_routed(lists, wts, counts, hrows, w_gate, w_up, w_down, li, x, shared, mod,
                split_output):
    group = lambda j: jnp.minimum(j, EXPERT_STEPS - 1)
    list_map = lambda c, j, nt: (c * EXPERT_STEPS + group(j), 0)
    w_specs, w_ops = [], []
    for k in range(EXPERTS_PER_STEP):
        w_map = lambda c, j, nt, k=k: (
            li, _expert_of_slot(group(j) * EXPERTS_PER_STEP + k), 0, 0)
        w_specs += [pl.BlockSpec((1, 1, D_MODEL, D_EXPERT), w_map),
                    pl.BlockSpec((1, 1, D_MODEL, D_EXPERT), w_map),
                    pl.BlockSpec((1, 1, D_EXPERT, D_MODEL), w_map)]
        w_ops += [w_gate, w_up, w_down]
    out_blk = lambda c, j: c * N_FLUSH + jnp.maximum(j - EXPERT_STEPS, 0)
    tok_spec = pl.BlockSpec((TILE, D_MODEL), lambda c, j, nt: (out_blk(c, j), 0))
    if split_output:
        out_specs = [
            pl.BlockSpec((TILE, D_MODEL), lambda c, j, nt: (jnp.minimum(out_blk(c, j), P_TILES - 1), 0)),
            pl.BlockSpec((TILE, D_MODEL), lambda c, j, nt: (jnp.maximum(out_blk(c, j) - P_TILES, 0), 0)),
        ]
        out_shape = [jax.ShapeDtypeStruct((T_P, D_MODEL), F32),
                     jax.ShapeDtypeStruct((T_S, D_MODEL), F32)]
    else:
        out_specs = tok_spec
        out_shape = jax.ShapeDtypeStruct((T_ALL, D_MODEL), F32)
    stage = pltpu.VMEM((ROW_TILE // 8, 8 * ROW_SUB, LANES), F32)
    grid_spec = pltpu.PrefetchScalarGridSpec(
        num_scalar_prefetch=1,
        grid=(N_CHUNKS, EXPERT_STEPS + N_FLUSH),
        in_specs=[
            pl.BlockSpec((EXPERTS_PER_STEP * 2 * LIST_ROWS, LANES), list_map,
                         memory_space=pltpu.SMEM),
            pl.BlockSpec((EXPERTS_PER_STEP * LIST_ROWS, LANES), list_map),
            pl.BlockSpec((MOE_TC * ROW_SUB, LANES), lambda c, j, nt: (c, 0),
                         pipeline_mode=pl.Buffered(1)),
            *w_specs,
            tok_spec, tok_spec,
            pl.BlockSpec((8, D_MODEL), lambda c, j, nt: (0, 5)),
        ],
        out_specs=out_specs,
        scratch_shapes=[
            pltpu.VMEM(((MOE_TC + 1) * ROW_SUB, LANES), F32),
            pltpu.VMEM((D_MODEL, D_EXPERT), BF16),
            pltpu.VMEM((D_MODEL, D_EXPERT), BF16),
            pltpu.VMEM((D_EXPERT, D_MODEL), BF16),
            stage, stage, stage, stage,
        ],
    )
    return pl.pallas_call(
        _moe_routed_kernel,
        grid_spec=grid_spec,
        out_shape=out_shape,
        compiler_params=_cparams("arbitrary", "arbitrary", vmem_limit=VMEM_LIMIT_ROUTED),
        name="moe_routed",
    )(counts.reshape(-1), lists, wts, hrows, *w_ops, x, shared, mod)


def kernel(x_prompt, x_sample, cache_k, cache_v, c, c_ctx, w_mod, b_mod, norm1, norm2, w_in_even, conv_a, q_norm, k_norm, lam_q1, lam_k1, lam_q2, lam_k2, subln, w_out_even, w_in_odd, conv_c, conv_c_b, ln_c_g, ln_c_b, w_pool, pool_scale, w_out_odd, w_router, b_router, w_gate, w_up, w_down, ws_gate, ws_up, ws_down):
    x = (x_prompt.reshape(T_P, D_MODEL), x_sample.reshape(T_S, D_MODEL))
    cond = jnp.concatenate([c_ctx[None, :], c, jnp.zeros((8 - 1 - DEC_BATCH, D_MODEL), F32)], axis=0)
    mod_all = _modulation(cond, w_mod, b_mod)

    new_k, new_v = [], []
    for li in range(DEPTH):
        mod = mod_all[li]
        if li % 2 == 0:
            e = li // 2
            lam_init = 0.8 - 0.6 * math.exp(-0.3 * li)
            u = _norm_in(x, norm1[li], mod, w_in_even[e].astype(BF16))
            q, k, v, k_cache, v_cache = _qkv_prep(u, q_norm[e], k_norm[e])
            new_k.append(k_cache)
            new_v.append(v_cache)
            o = _attention(q, k, v, cache_k, cache_v, e,
                           (lam_q1[e], lam_k1[e], lam_q2[e], lam_k2[e]), subln[e], lam_init)
            x = _even_out(x, u, o, conv_a[e], mod, w_out_even[e].astype(BF16))
        else:
            o_ = li // 2
            u = _norm_in(x, norm1[li], mod, w_in_odd[o_].astype(BF16))
            x = _odd_out(x, u, conv_c[o_], conv_c_b[o_], ln_c_g[o_], ln_c_b[o_],
                         w_pool[o_].astype(BF16), pool_scale[o_], mod, w_out_odd[o_].astype(BF16))
        hrows, gate_b, rank_b, shared, cnt_b = _moe_pre(
            x, norm2[li], mod, w_router[li].T, b_router[li], ws_gate[li].astype(BF16),
            ws_up[li].astype(BF16), ws_down[li].astype(BF16))
        lists, wts = _moe_sort(rank_b, gate_b, cnt_b)
        x = _moe_routed(lists, wts, cnt_b[:, :, 0], hrows, w_gate, w_up, w_down, li, x, shared,
                        mod, split_output=(li == DEPTH - 1))

    y_prompt = x[0].reshape(BATCH, SEQ, D_MODEL)
    y_sample = x[1].reshape(DEC_BATCH, DEC_SEQ, D_MODEL)
    return (y_prompt, y_sample, jnp.stack(new_k, axis=1), jnp.stack(new_v, axis=1))
```

```python
import functools
import math

import numpy as np
import jax
import jax.numpy as jnp
from jax import lax
from jax.experimental import pallas as pl
from jax.experimental.pallas import tpu as pltpu

D_MODEL = 1024
BATCH = 16
SEQ = 256
DEPTH = 2
DEC_BATCH = 2
DEC_SEQ = 4096
PAST_LEN = 512
GRID_W = 64
H_B = 4
DK_B = 64
DV_B = 2 * DK_B
D_A = D_MODEL // 2
D_B = H_B * DV_B
D_C = D_MODEL // 2
D_D = D_MODEL // 2
CONV_C = 31
POOL_WINDOWS = (2, 4, 8, 16)
D_DG = D_D // len(POOL_WINDOWS)
N_EXPERTS = 64
TOP_K = 8
N_GROUPS = 8
TOPK_GROUPS = 4
D_EXPERT = 256
D_SHARED = 256
ROUTED_SCALE = 2.5
ROPE_BASE = 10000.0
EPS = 1e-6

F32 = jnp.float32
BF16 = jnp.bfloat16

T_P = BATCH * SEQ
T_S = DEC_BATCH * DEC_SEQ
T_ALL = T_P + T_S
TILE = 256
N_TILES = T_ALL // TILE
P_TILES = T_P // TILE
S_TILES = DEC_SEQ // TILE
LANES = 128
VMEM_LIMIT = 56 * 1024 * 1024
VMEM_LIMIT_ROUTED = 60 * 1024 * 1024


def _cparams(*sem, vmem_limit=VMEM_LIMIT):
    return pltpu.CompilerParams(dimension_semantics=sem, vmem_limit_bytes=vmem_limit)


def _mod_row(i, tm):
    npt = T_P // tm
    per = DEC_SEQ // tm
    return jnp.where(i < npt, 0, 1 + (i - npt) // per)


def _seq_flags(i):
    j = (i - P_TILES) % S_TILES
    first = jnp.logical_or(i < P_TILES, j == 0)
    last = jnp.logical_or(i < P_TILES, j == S_TILES - 1)
    return first, last


def _seq_tile(i):
    return jnp.where(i < P_TILES, 0, (i - P_TILES) % S_TILES)


def _split_bf16(a):
    hi = a.astype(BF16)
    lo = (a - hi.astype(F32)).astype(BF16)
    return hi, lo


def _dot(a, b):
    return jnp.dot(a, b, preferred_element_type=F32)


def _dot_nt(a, b):
    return lax.dot_general(a, b, (((1,), (1,)), ((), ())), preferred_element_type=F32)


def _dot3(a, b):
    a_hi, a_lo = _split_bf16(a)
    b_hi, b_lo = _split_bf16(b)
    return _dot(a_hi, b_hi) + _dot(a_lo, b_hi) + _dot(a_hi, b_lo)


def _silu(x):
    return x * jax.nn.sigmoid(x)


MOD_TN = 1536


def _mod_kernel(c_ref, w_ref, b_ref, o_ref):
    o_ref[0] = _dot3(_silu(c_ref[...]), w_ref[0]) + b_ref[0]


def _modulation(cond, w_mod, b_mod):
    n = 6 * D_MODEL
    return pl.pallas_call(
        _mod_kernel,
        grid=(DEPTH, n // MOD_TN),
        in_specs=[
            pl.BlockSpec((8, D_MODEL), lambda l, j: (0, 0)),
            pl.BlockSpec((1, D_MODEL, MOD_TN), lambda l, j: (l, 0, j)),
            pl.BlockSpec((1, 1, MOD_TN), lambda l, j: (l, 0, j)),
        ],
        out_specs=pl.BlockSpec((1, 8, MOD_TN), lambda l, j: (l, 0, j)),
        out_shape=jax.ShapeDtypeStruct((DEPTH, 8, n), F32),
        compiler_params=_cparams("arbitrary", "arbitrary"),
        name="modulation",
    )(cond, w_mod, b_mod.reshape(DEPTH, 1, n))


IN_TM = 1024


def _modulated_norm(x, g, shift, scale):
    ms = jnp.mean(x * x, axis=-1, keepdims=True)
    return (x * lax.rsqrt(ms + EPS) * g) * (1.0 + scale) + shift


def _tok_specs(x, tm, width):
    if isinstance(x, tuple):
        n_p = T_P // tm
        return ([pl.BlockSpec((tm, width), lambda i: (jnp.minimum(i, n_p - 1), 0)),
                 pl.BlockSpec((tm, width), lambda i: (jnp.maximum(i - n_p, 0), 0))], list(x))
    return [pl.BlockSpec((tm, width), lambda i: (i, 0))], [x]


def _tok_load(refs, i, tm):
    if len(refs) == 2:
        return jnp.where(i < T_P // tm, refs[0][...], refs[1][...])
    return refs[0][...]


def _norm_in_kernel(*refs, n_x):
    x_refs, (g_ref, sh_ref, sc_ref, w_ref, o_ref) = refs[:n_x], refs[n_x:]
    i = pl.program_id(0)
    r = _mod_row(i, IN_TM)
    h = _modulated_norm(_tok_load(x_refs, i, IN_TM), g_ref[...], sh_ref[pl.ds(r, 1), :],
                        sc_ref[pl.ds(r, 1), :])
    o_ref[...] = _dot(h.astype(BF16), w_ref[...])


def _norm_in(x, g, mod, w_bf16):
    n = w_bf16.shape[1]
    x_specs, x_ops = _tok_specs(x, IN_TM, D_MODEL)
    return pl.pallas_call(
        functools.partial(_norm_in_kernel, n_x=len(x_ops)),
        grid=(T_ALL // IN_TM,),
        in_specs=x_specs + [
            pl.BlockSpec((1, D_MODEL), lambda i: (0, 0)),
            pl.BlockSpec((8, D_MODEL), lambda i: (0, 0)),
            pl.BlockSpec((8, D_MODEL), lambda i: (0, 1)),
            pl.BlockSpec((D_MODEL, n), lambda i: (0, 0)),
        ],
        out_specs=pl.BlockSpec((IN_TM, n), lambda i: (i, 0)),
        out_shape=jax.ShapeDtypeStruct((T_ALL, n), F32),
        compiler_params=_cparams("arbitrary"),
        name="norm_in_proj",
    )(*x_ops, g.reshape(1, D_MODEL), mod, mod, w_bf16)


QKV_TM = 1024


def _rope_tables():
    half = DK_B // 2
    freqs = ROPE_BASE ** (-np.arange(0, half, 2, dtype=np.float64) / half)
    l = np.arange(DEC_SEQ)
    pos_r = (l // GRID_W).astype(np.float64)
    pos_c = (l % GRID_W).astype(np.float64)
    lane = np.arange(LANES)
    jj = lane % DK_B
    m = jj % half
    f = m % (half // 2)
    pos = np.where((jj < half)[None, :], pos_r[:, None], pos_c[:, None])
    ang = pos * freqs[f][None, :]
    sign = np.where(m < half // 2, -1.0, 1.0)[None, :]
    cos = np.concatenate([np.ones((QKV_TM, LANES)), np.cos(ang)], axis=0)
    sin = np.concatenate([np.zeros((QKV_TM, LANES)), sign * np.sin(ang)], axis=0)
    return cos.astype(np.float32), sin.astype(np.float32)


def _segment_mean_matrix():
    lane = np.arange(LANES)
    same = (lane[:, None] // DK_B) == (lane[None, :] // DK_B)
    return (same.astype(np.float32) / DK_B)


def _qk_prep(x, g, cos, sin, seg):
    x2 = x * x
    hi, lo = _split_bf16(x2)
    ms = _dot(hi, seg) + _dot(lo, seg)
    y = x * lax.rsqrt(ms + EPS) * g
    lane = lax.broadcasted_iota(jnp.int32, y.shape, 1)
    lower = (lane % (DK_B // 2)) < (DK_B // 4)
    partner = jnp.where(lower, pltpu.roll(y, LANES - DK_B // 4, 1), pltpu.roll(y, DK_B // 4, 1))
    return y * cos + partner * sin


def _qkv_prep_kernel(q_ref, k_ref, v_ref, qn_ref, kn_ref, cos_ref, sin_ref, seg_ref,
                     qo_ref, ko_ref, vo_ref, kc_ref, vc_ref):
    cos = cos_ref[...]
    sin = sin_ref[...]
    seg = seg_ref[...]
    scale = math.log2(math.e) / math.sqrt(DK_B)
    is_prompt = pl.program_id(0) < T_P // QKV_TM
    for h in range(H_B):
        cols = slice(h * LANES, (h + 1) * LANES)
        qo_ref[h] = (_qk_prep(q_ref[:, cols], qn_ref[...], cos, sin, seg) * scale).astype(BF16)
        k = _qk_prep(k_ref[:, cols], kn_ref[...], cos, sin, seg)
        v = v_ref[:, cols]
        ko_ref[h] = k.astype(BF16)
        vo_ref[h] = v.astype(BF16)

        @pl.when(is_prompt)
        def _():
            for b in range(QKV_TM // SEQ):
                kc_ref[b, h] = k[b * SEQ:(b + 1) * SEQ]
                vc_ref[b, h] = v[b * SEQ:(b + 1) * SEQ]


def _qkv_prep(u, qn, kn):
    cos, sin = _rope_tables()
    seg = jnp.asarray(_segment_mean_matrix(), BF16)
    qn2 = jnp.concatenate([qn, qn]).reshape(1, LANES)
    kn2 = jnp.concatenate([kn, kn]).reshape(1, LANES)
    col0 = 3 * D_A // D_B
    p_steps = T_P // QKV_TM
    s_steps = DEC_SEQ // QKV_TM

    def tab_map(i):
        return (jnp.where(i < p_steps, 0, 1 + (i - p_steps) % s_steps), 0)

    out_b = jax.ShapeDtypeStruct((H_B, T_ALL, LANES), BF16)
    out_spec = pl.BlockSpec((H_B, QKV_TM, LANES), lambda i: (0, i, 0))
    cache = jax.ShapeDtypeStruct((BATCH, H_B, SEQ, LANES), F32)
    cache_spec = pl.BlockSpec((QKV_TM // SEQ, H_B, SEQ, LANES),
                              lambda i: (jnp.minimum(i, p_steps - 1), 0, 0, 0))
    return pl.pallas_call(
        _qkv_prep_kernel,
        grid=(T_ALL // QKV_TM,),
        in_specs=[
            pl.BlockSpec((QKV_TM, D_B), lambda i: (i, col0)),
            pl.BlockSpec((QKV_TM, D_B), lambda i: (i, col0 + 1)),
            pl.BlockSpec((QKV_TM, D_B), lambda i: (i, col0 + 2)),
            pl.BlockSpec((1, LANES), lambda i: (0, 0)),
            pl.BlockSpec((1, LANES), lambda i: (0, 0)),
            pl.BlockSpec((QKV_TM, LANES), tab_map),
            pl.BlockSpec((QKV_TM, LANES), tab_map),
            pl.BlockSpec((LANES, LANES), lambda i: (0, 0)),
        ],
        out_specs=[out_spec] * 3 + [cache_spec] * 2,
        out_shape=[out_b, out_b, out_b, cache, cache],
        compiler_params=_cparams("arbitrary"),
        name="qkv_prep",
    )(u, u, u, qn2, kn2, jnp.asarray(cos), jnp.asarray(sin), seg)


def _lambda(lq1_ref, lk1_ref, lq2_ref, lk2_ref, lam_init):
    a = jnp.sum(lq1_ref[...] * lk1_ref[...], axis=-1, keepdims=True)
    b = jnp.sum(lq2_ref[...] * lk2_ref[...], axis=-1, keepdims=True)
    return jnp.exp(a) - jnp.exp(b) + lam_init


ATTN_TQ = 512


def _attn_body(q, keys, vals, lam, subg, lam_init):
    lane = lax.broadcasted_iota(jnp.int32, q.shape, 1)
    zero = jnp.zeros_like(q)
    qa = jnp.where(lane < DK_B, q, zero)
    qb = jnp.where(lane < DK_B, zero, q)
    scores = [[_dot_nt(qq, k) for k in keys] for qq in (qa, qb)]
    outs = []
    for ss in scores:
        m = functools.reduce(jnp.maximum, [jnp.max(s, axis=-1, keepdims=True) for s in ss])
        ps = [jnp.exp2(s - m) for s in ss]
        l = functools.reduce(jnp.add, [jnp.sum(p, axis=-1, keepdims=True) for p in ps])
        pv = functools.reduce(jnp.add, [_dot(p.astype(BF16), v) for p, v in zip(ps, vals)])
        outs.append(pv / l)
    o = outs[0] - lam * outs[1]
    ms = jnp.mean(o * o, axis=-1, keepdims=True)
    return ((o * lax.rsqrt(ms + EPS) * subg) * (1.0 - lam_init)).astype(BF16)


def _attn_prompt_kernel(q_ref, k_ref, v_ref, lq1, lk1, lq2, lk2, sg_ref, o_ref, *, lam_init):
    lam = _lambda(lq1, lk1, lq2, lk2, lam_init)
    for h in range(H_B):
        o_ref[:, h * DV_B:(h + 1) * DV_B] = _attn_body(
            q_ref[h], [k_ref[h]], [v_ref[h]], lam, sg_ref[...], lam_init)


def _attn_latent_kernel(q_ref, k_ref, v_ref, ck_ref, cv_ref, lq1, lk1, lq2, lk2, sg_ref, o_ref,
                        *, lam_init):
    lam = _lambda(lq1, lk1, lq2, lk2, lam_init)
    keys = [ck_ref[0, 0, 0].astype(BF16), k_ref[0]]
    vals = [cv_ref[0, 0, 0].astype(BF16), v_ref[0]]
    o_ref[...] = _attn_body(q_ref[0], keys, vals, lam, sg_ref[...], lam_init)


def _attention(q, k, v, cache_k, cache_v, e, lam_params, subg, lam_init):
    small = [p.reshape(1, DK_B) for p in lam_params] + [subg.reshape(1, DV_B)]
    small_specs2 = [pl.BlockSpec((1, DK_B), lambda b: (0, 0))] * 4 + \
                   [pl.BlockSpec((1, DV_B), lambda b: (0, 0))]
    small_specs3 = [pl.BlockSpec((1, DK_B), lambda s, h, j: (0, 0))] * 4 + \
                   [pl.BlockSpec((1, DV_B), lambda s, h, j: (0, 0))]
    o_prompt = pl.pallas_call(
        functools.partial(_attn_prompt_kernel, lam_init=lam_init),
        grid=(BATCH,),
        in_specs=[pl.BlockSpec((H_B, SEQ, LANES), lambda b: (0, b, 0))] * 3 + small_specs2,
        out_specs=pl.BlockSpec((SEQ, D_B), lambda b: (b, 0)),
        out_shape=jax.ShapeDtypeStruct((T_P, D_B), BF16),
        compiler_params=_cparams("arbitrary"),
        name="attn_prompt",
    )(q, k, v, *small)

    q_tiles = DEC_SEQ // ATTN_TQ
    kv_spec = pl.BlockSpec((1, DEC_SEQ, LANES), lambda s, h, j: (h, T_P // DEC_SEQ + s, 0))
    c_spec = pl.BlockSpec((1, 1, 1, PAST_LEN, LANES), lambda s, h, j: (s, e, h, 0, 0))
    o_latent = pl.pallas_call(
        functools.partial(_attn_latent_kernel, lam_init=lam_init),
        grid=(DEC_BATCH, H_B, q_tiles),
        in_specs=[pl.BlockSpec((1, ATTN_TQ, LANES),
                               lambda s, h, j: (h, T_P // ATTN_TQ + s * q_tiles + j, 0)),
                  kv_spec, kv_spec, c_spec, c_spec] + small_specs3,
        out_specs=pl.BlockSpec((ATTN_TQ, LANES), lambda s, h, j: (s * q_tiles + j, h)),
        out_shape=jax.ShapeDtypeStruct((T_S, D_B), BF16),
        compiler_params=_cparams("arbitrary", "arbitrary", "arbitrary"),
        name="attn_latent",
    )(q, k, v, cache_k, cache_v, *small)
    return o_prompt, o_latent


HALO8 = 8
HALO16 = 16


def _prev_block(i, rows):
    return jnp.maximum(i * (TILE // rows) - 1, 0)


def _next_block(i, rows):
    return jnp.minimum((i + 1) * (TILE // rows), T_ALL // rows - 1)


def _even_out_kernel(*refs, n_x, n_o):
    x_refs, o_refs = refs[:n_x], refs[n_x:n_x + n_o]
    (bg_ref, cg_ref, hin_ref, cgp_ref, hinp_ref, cgn_ref, hinn_ref, cw_ref, g1_ref, w_ref,
     out_ref) = refs[n_x + n_o:]
    i = pl.program_id(0)
    first, last = _seq_flags(i)
    r = _mod_row(i, TILE)
    z = cg_ref[...] * hin_ref[...]
    zp = jnp.where(first, 0.0, cgp_ref[HALO8 - 1:HALO8, :] * hinp_ref[HALO8 - 1:HALO8, :])
    zn = jnp.where(last, 0.0, cgn_ref[0:1, :] * hinn_ref[0:1, :])
    row = lax.broadcasted_iota(jnp.int32, z.shape, 0)
    z_prev = jnp.where(row == 0, zp, pltpu.roll(z, 1, 0))
    z_next = jnp.where(row == TILE - 1, zn, pltpu.roll(z, TILE - 1, 0))
    cw = cw_ref[...]
    ya = bg_ref[...] * (cw[0:1, :] * z_prev + cw[1:2, :] * z + cw[2:3, :] * z_next)
    o = _tok_load(o_refs, i, TILE)
    y = _dot(ya.astype(BF16), w_ref[0:D_A, :]) + _dot(o, w_ref[D_A:, :])
    out_ref[...] = _tok_load(x_refs, i, TILE) + g1_ref[pl.ds(r, 1), :] * y


def _even_out(x, u, o, conv_w, mod, w_out_bf16):
    tile_spec = lambda c: pl.BlockSpec((TILE, D_A), lambda i: (i, c))
    prev_spec = lambda c: pl.BlockSpec((HALO8, D_A), lambda i: (_prev_block(i, HALO8), c))
    next_spec = lambda c: pl.BlockSpec((HALO8, D_A), lambda i: (_next_block(i, HALO8), c))
    x_specs, x_ops = _tok_specs(x, TILE, D_MODEL)
    o_specs, o_ops = _tok_specs(o, TILE, D_B)
    return pl.pallas_call(
        functools.partial(_even_out_kernel, n_x=len(x_ops), n_o=len(o_ops)),
        grid=(N_TILES,),
        in_specs=x_specs + o_specs + [
            tile_spec(0), tile_spec(1), tile_spec(2),
            prev_spec(1), prev_spec(2), next_spec(1), next_spec(2),
            pl.BlockSpec((3, D_A), lambda i: (0, 0)),
            pl.BlockSpec((8, D_MODEL), lambda i: (0, 2)),
            pl.BlockSpec((D_MODEL, D_MODEL), lambda i: (0, 0)),
        ],
        out_specs=pl.BlockSpec((TILE, D_MODEL), lambda i: (i, 0)),
        out_shape=jax.ShapeDtypeStruct((T_ALL, D_MODEL), F32),
        compiler_params=_cparams("arbitrary"),
        name="even_mixer_out",
    )(*x_ops, *o_ops, u, u, u, u, u, u, u, conv_w, mod, w_out_bf16)


def _odd_out_kernel(x_ref, a_ref, b_ref, pd_ref, ap_ref, bp_ref, an_ref, bn_ref, pp_ref, pn_ref,
                    cw_ref, cb_ref, lg_ref, lb_ref, wp_ref, ps_ref, g1_ref, w_ref, out_ref,
                    ext_ref, extp_ref, shift_ref):
    i = pl.program_id(0)
    first, last = _seq_flags(i)
    r = _mod_row(i, TILE)
    ext_ref[0:HALO16, :] = jnp.where(first, 0.0, ap_ref[...] * jax.nn.sigmoid(bp_ref[...]))
    ext_ref[HALO16:HALO16 + TILE, :] = a_ref[...] * jax.nn.sigmoid(b_ref[...])
    ext_ref[HALO16 + TILE:, :] = jnp.where(last, 0.0, an_ref[...] * jax.nn.sigmoid(bn_ref[...]))
    base = HALO16 - CONV_C // 2
    parts = []
    for cb in range(D_C // LANES):
        cols = slice(cb * LANES, (cb + 1) * LANES)
        acc = jnp.zeros((TILE, LANES), F32)
        for phase in range(8):
            taps = [j for j in range(CONV_C) if (base + j) % 8 == phase]
            reach = max((base + j) // 8 for j in taps)
            rows = TILE + 8 * reach
            shift_ref[0:rows, :] = ext_ref[pl.ds(phase, rows), cols]
            for j in taps:
                a = (base + j) // 8
                acc = acc + cw_ref[j:j + 1, cols] * shift_ref[8 * a:8 * a + TILE, :]
        parts.append(acc)
    g = jnp.concatenate(parts, axis=-1) + cb_ref[...]
    mu = jnp.mean(g, axis=-1, keepdims=True)
    var = jnp.mean(jnp.square(g - mu), axis=-1, keepdims=True)
    g = _silu(((g - mu) * lax.rsqrt(var + EPS)) * lg_ref[...] + lb_ref[...])
    extp_ref[0:HALO8, :] = jnp.where(first, 0.0, pp_ref[...])
    extp_ref[HALO8:HALO8 + TILE, :] = pd_ref[...]
    extp_ref[HALO8 + TILE:, :] = jnp.where(last, 0.0, pn_ref[...])
    seq_len = jnp.where(i < P_TILES, SEQ, DEC_SEQ)
    pos = _seq_tile(i) * TILE + lax.broadcasted_iota(jnp.int32, (TILE, 1), 0)
    yd = []
    for gi, w in enumerate(POOL_WINDOWS):
        cols = slice(gi * D_DG, (gi + 1) * D_DG)
        s = jnp.zeros((TILE, D_DG), F32)
        for d in range(-(w // 2), w - w // 2):
            s = s + extp_ref[pl.ds(HALO8 + d, TILE), cols]
        lo = jnp.maximum(pos - w // 2, 0)
        hi = jnp.minimum(pos - w // 2 + w, seq_len)
        pooled = s / (hi - lo).astype(F32) - pd_ref[:, cols]
        yd.append(_dot(pooled.astype(BF16), wp_ref[gi]))
    yd = jnp.concatenate(yd, axis=-1) * ps_ref[...]
    y = _dot(g.astype(BF16), w_ref[0:D_C, :]) + _dot(yd.astype(BF16), w_ref[D_C:, :])
    out_ref[...] = x_ref[...] + g1_ref[pl.ds(r, 1), :] * y


def _odd_out(x, u, conv_w, conv_b, ln_g, ln_b, w_pool_bf16, p_scale, mod, w_out_bf16):
    tile_spec = lambda c: pl.BlockSpec((TILE, D_C), lambda i: (i, c))
    prev_spec = lambda rows, c: pl.BlockSpec((rows, D_C), lambda i: (_prev_block(i, rows), c))
    next_spec = lambda rows, c: pl.BlockSpec((rows, D_C), lambda i: (_next_block(i, rows), c))
    vec = lambda: pl.BlockSpec((1, D_C), lambda i: (0, 0))
    return pl.pallas_call(
        _odd_out_kernel,
        grid=(N_TILES,),
        in_specs=[
            pl.BlockSpec((TILE, D_MODEL), lambda i: (i, 0)),
            tile_spec(0), tile_spec(1), tile_spec(2),
            prev_spec(HALO16, 0), prev_spec(HALO16, 1), next_spec(HALO16, 0), next_spec(HALO16, 1),
            prev_spec(HALO8, 2), next_spec(HALO8, 2),
            pl.BlockSpec((CONV_C, D_C), lambda i: (0, 0)),
            vec(), vec(), vec(),
            pl.BlockSpec((len(POOL_WINDOWS), D_DG, D_DG), lambda i: (0, 0, 0)),
            vec(),
            pl.BlockSpec((8, D_MODEL), lambda i: (0, 2)),
            pl.BlockSpec((D_MODEL, D_MODEL), lambda i: (0, 0)),
        ],
        out_specs=pl.BlockSpec((TILE, D_MODEL), lambda i: (i, 0)),
        out_shape=jax.ShapeDtypeStruct((T_ALL, D_MODEL), F32),
        scratch_shapes=[pltpu.VMEM((TILE + 2 * HALO16, D_C), F32),
                        pltpu.VMEM((TILE + 2 * HALO8, D_D), F32),
                        pltpu.VMEM((TILE + 2 * HALO16, LANES), F32)],
        compiler_params=_cparams("arbitrary"),
        name="odd_mixer_out",
    )(x, u, u, u, u, u, u, u, u, u, conv_w, conv_b.reshape(1, D_C), ln_g.reshape(1, D_C),
      ln_b.reshape(1, D_C), w_pool_bf16, p_scale.reshape(1, D_D), mod, w_out_bf16)


GROUP = N_EXPERTS // N_GROUPS
NEG_INF = float("-inf")


def _first_argmax(v, idx, axis):
    m = jnp.max(v, axis=axis, keepdims=True)
    big = jnp.int32(2 ** 30)
    am = jnp.min(jnp.where(v == m, idx, big), axis=axis, keepdims=True)
    return m, am


def _member_major(a):
    return a.reshape(N_GROUPS, GROUP, *a.shape[1:]).swapaxes(0, 1).reshape(a.shape)


def _route(scores, biased):
    shape = biased.shape
    member = lax.broadcasted_iota(jnp.int32, shape, 0)
    m1, a1 = _first_argmax(biased, member, 0)
    m2 = jnp.max(jnp.where(member == a1, NEG_INF, biased), axis=0, keepdims=True)
    gscore = m1 + m2
    gidx = lax.broadcasted_iota(jnp.int32, gscore.shape, 1)
    gsel = jnp.zeros(gscore.shape, jnp.bool_)
    for _ in range(TOPK_GROUPS):
        _, am = _first_argmax(gscore, gidx, 1)
        hit = gidx == am
        gsel = jnp.logical_or(gsel, hit)
        gscore = jnp.where(hit, NEG_INF, gscore)
    cand = jnp.where(gsel, biased, NEG_INF)
    eidx = lax.broadcasted_iota(jnp.int32, shape, 1) * GROUP + member
    sel = jnp.zeros(shape, jnp.bool_)
    for _ in range(TOP_K):
        m = jnp.max(jnp.max(cand, axis=0, keepdims=True), axis=1, keepdims=True)
        big = jnp.int32(2 ** 30)
        am = jnp.where(cand == m, eidx, big)
        am = jnp.min(jnp.min(am, axis=0, keepdims=True), axis=1, keepdims=True)
        hit = eidx == am
        sel = jnp.logical_or(sel, hit)
        cand = jnp.where(hit, NEG_INF, cand)
    wsel = jnp.where(sel, scores, 0.0)
    tot = jnp.sum(jnp.sum(wsel, axis=0, keepdims=True), axis=1, keepdims=True)
    return wsel / tot * ROUTED_SCALE, sel


MOE_TC = 4096
N_CHUNKS = T_ALL // MOE_TC
ROW_TILE = 256
N_FLUSH = MOE_TC // TILE
PRE_TM = 1024
ROW_SUB = D_MODEL // LANES


def _moe_pre_kernel(x_ref, g_ref, sh_ref, sc_ref, wr_ref, br_ref, tri_ref, wsg_ref, wsu_ref,
                    wsd_ref, hrow_ref, gate_ref, rank_ref, shared_ref, cnt_ref, carry_ref):
    i = pl.program_id(0)
    r = _mod_row(i, PRE_TM)
    h = _modulated_norm(x_ref[...], g_ref[...], sh_ref[pl.ds(r, 1), :], sc_ref[pl.ds(r, 1), :])
    hb = h.astype(BF16)
    for s in range(ROW_SUB):
        hrow_ref[pl.ds(s, PRE_TM, stride=ROW_SUB), :] = h[:, s * LANES:(s + 1) * LANES]
    h_hi, h_lo = hb, (h - hb.astype(F32)).astype(BF16)
    w_hi, w_lo = _split_bf16(wr_ref[...])
    logits = _dot_nt(w_hi, h_hi) + _dot_nt(w_lo, h_hi) + _dot_nt(w_hi, h_lo)
    scores = jax.nn.sigmoid(logits)
    biased = scores + br_ref[:, 0:1]
    shape3 = (GROUP, N_GROUPS, PRE_TM)
    gate_t, sel = _route(scores.reshape(shape3), biased.reshape(shape3))
    gate_t = jnp.swapaxes(gate_t, 0, 1).reshape(N_EXPERTS, PRE_TM)
    sel = jnp.swapaxes(jnp.where(sel, 1.0, 0.0), 0, 1).reshape(N_EXPERTS, PRE_TM)

    @pl.when(i % (MOE_TC // PRE_TM) == 0)
    def _():
        carry_ref[...] = jnp.zeros_like(carry_ref)

    carry = carry_ref[...]
    local = _dot(sel.astype(BF16), tri_ref[...])
    rank = jnp.where(sel > 0.0, local + jnp.concatenate([carry] * (PRE_TM // LANES), axis=1), -1.0)
    carry = carry + jnp.sum(sel, axis=1, keepdims=True)
    carry_ref[...] = carry

    @pl.when(i % (MOE_TC // PRE_TM) == MOE_TC // PRE_TM - 1)
    def _():
        cnt_ref[0] = carry.astype(jnp.int32)
    gate_ref[...] = gate_t
    rank_ref[...] = rank
    a = _silu(_dot(hb, wsg_ref[...])) * _dot(hb, wsu_ref[...])
    shared_ref[...] = _dot(a.astype(BF16), wsd_ref[...])


def _moe_pre(x, g, mod, w_router_t, b_router, wsg, wsu, wsd):
    return pl.pallas_call(
        _moe_pre_kernel,
        grid=(T_ALL // PRE_TM,),
        in_specs=[
            pl.BlockSpec((PRE_TM, D_MODEL), lambda i: (i, 0)),
            pl.BlockSpec((1, D_MODEL), lambda i: (0, 0)),
            pl.BlockSpec((8, D_MODEL), lambda i: (0, 3)),
            pl.BlockSpec((8, D_MODEL), lambda i: (0, 4)),
            pl.BlockSpec((N_EXPERTS, D_MODEL), lambda i: (0, 0)),
            pl.BlockSpec((N_EXPERTS, LANES), lambda i: (0, 0)),
            pl.BlockSpec((PRE_TM, PRE_TM), lambda i: (0, 0)),
            pl.BlockSpec((D_MODEL, D_SHARED), lambda i: (0, 0)),
            pl.BlockSpec((D_MODEL, D_SHARED), lambda i: (0, 0)),
            pl.BlockSpec((D_SHARED, D_MODEL), lambda i: (0, 0)),
        ],
        out_specs=[
            pl.BlockSpec((PRE_TM * ROW_SUB, LANES), lambda i: (i, 0)),
            pl.BlockSpec((N_EXPERTS, PRE_TM), lambda i: (0, i)),
            pl.BlockSpec((N_EXPERTS, PRE_TM), lambda i: (0, i)),
            pl.BlockSpec((PRE_TM, D_MODEL), lambda i: (i, 0)),
            pl.BlockSpec((1, N_EXPERTS, LANES), lambda i: (i // (MOE_TC // PRE_TM), 0, 0)),
        ],
        out_shape=[
            jax.ShapeDtypeStruct((T_ALL * ROW_SUB, LANES), F32),
            jax.ShapeDtypeStruct((N_EXPERTS, T_ALL), F32),
            jax.ShapeDtypeStruct((N_EXPERTS, T_ALL), F32),
            jax.ShapeDtypeStruct((T_ALL, D_MODEL), F32),
            jax.ShapeDtypeStruct((N_CHUNKS, N_EXPERTS, LANES), jnp.int32),
        ],
        scratch_shapes=[pltpu.VMEM((N_EXPERTS, LANES), F32)],
        compiler_params=_cparams("arbitrary"),
        name="moe_pre",
    )(x, g.reshape(1, D_MODEL), mod, mod, _member_major(w_router_t),
      jnp.broadcast_to(_member_major(b_router).reshape(N_EXPERTS, 1), (N_EXPERTS, LANES)),
      jnp.asarray(np.triu(np.ones((PRE_TM, PRE_TM), np.float32), 1), BF16), wsg, wsu, wsd)


LIST_ROWS = MOE_TC // LANES
TILE_ROWS = ROW_TILE // LANES
DUMMY_ROW = MOE_TC * ROW_SUB


def _moe_sort_kernel(rank_ref, gate_ref, cnt_ref, list_ref, w_ref):
    rank = rank_ref[...]
    lane = lax.broadcasted_iota(jnp.int32, rank.shape, 1)
    d = jnp.where(rank >= 0.0, lane - rank.astype(jnp.int32), 0)
    w = gate_ref[...]
    for s in range(MOE_TC.bit_length() - 1):
        k = 1 << s
        d_in = pltpu.roll(d, MOE_TC - k, 1)
        w_in = pltpu.roll(w, MOE_TC - k, 1)
        take = (d_in & k) != 0
        leave = (d & k) != 0
        d = jnp.where(take, d_in, jnp.where(leave, 0, d))
        w = jnp.where(take, w_in, w)
    valid = lane < cnt_ref[0][:, 0:1]
    row = (lane + d) * ROW_SUB
    gsrc = jnp.where(valid, row, 0)
    ssrc = jnp.where(valid, row, DUMMY_ROW)
    w = jnp.where(valid, w, 0.0)
    for b in range(LIST_ROWS):
        cols = slice(b * LANES, (b + 1) * LANES)
        list_ref[pl.ds(b, N_EXPERTS, stride=2 * LIST_ROWS), :] = gsrc[:, cols]
        list_ref[pl.ds(LIST_ROWS + b, N_EXPERTS, stride=2 * LIST_ROWS), :] = ssrc[:, cols]
        w_ref[pl.ds(b, N_EXPERTS, stride=LIST_ROWS), :] = w[:, cols]


def _moe_sort(rank_t, gate_t, cnt_b):
    chunk_spec = pl.BlockSpec((N_EXPERTS, MOE_TC), lambda c: (0, c))
    n_rows = N_CHUNKS * N_EXPERTS * LIST_ROWS
    return pl.pallas_call(
        _moe_sort_kernel,
        grid=(N_CHUNKS,),
        in_specs=[chunk_spec, chunk_spec,
                  pl.BlockSpec((1, N_EXPERTS, LANES), lambda c: (c, 0, 0))],
        out_specs=[pl.BlockSpec((N_EXPERTS * 2 * LIST_ROWS, LANES), lambda c: (c, 0)),
                   pl.BlockSpec((N_EXPERTS * LIST_ROWS, LANES), lambda c: (c, 0))],
        out_shape=[jax.ShapeDtypeStruct((2 * n_rows, LANES), jnp.int32),
                   jax.ShapeDtypeStruct((n_rows, LANES), F32)],
        compiler_params=_cparams("arbitrary"),
        name="moe_sort",
    )(rank_t, gate_t, cnt_b)


SCATTER_BATCH = 16


EXPERTS_PER_STEP = 2
EXPERT_STEPS = N_EXPERTS // EXPERTS_PER_STEP


N_PIECES = 8
PIECE_ROWS = ROW_TILE // N_PIECES


class _ExpertRefs:
    def __init__(self, list0, wt0, list_ref, wt_ref, h_ref, acc_ref, wgb_ref, wub_ref, wdb_ref):
        self.list0, self.wt0 = list0, wt0
        self.list_ref, self.wt_ref, self.h_ref, self.acc_ref = list_ref, wt_ref, h_ref, acc_ref
        self.wgb_ref, self.wub_ref, self.wdb_ref = wgb_ref, wub_ref, wdb_ref


def _gather_rows(ex, t, r0, n, xs_ref):
    base = ex.list0 + t * TILE_ROWS
    for r in range(r0, r0 + n):
        tok = pl.multiple_of(ex.list_ref[base + r // LANES, r % LANES], ROW_SUB)
        xs_ref[r // 8, pl.ds(r % 8, ROW_SUB, stride=8), :] = ex.h_ref[pl.ds(tok, ROW_SUB), :]


def _scatter_rows(ex, t, r0, n, ys_ref):
    base = ex.list0 + LIST_ROWS + t * TILE_ROWS
    for b0 in range(r0, r0 + n, SCATTER_BATCH):
        rows = range(b0, b0 + SCATTER_BATCH)
        dsts = [ex.acc_ref.at[pl.ds(pl.multiple_of(ex.list_ref[base + r // LANES, r % LANES],
                                                   ROW_SUB), ROW_SUB), :] for r in rows]
        news = [dst[...] + ys_ref[r // 8, pl.ds(r % 8, ROW_SUB, stride=8), :]
                for dst, r in zip(dsts, rows)]
        for dst, new in zip(dsts, news):
            dst[...] = new


def _tile_ffn(ex, t, xs_ref, ys_ref, side_work=None):
    kc = D_MODEL // (N_PIECES // 2)
    hg = hu = None
    for p in range(N_PIECES // 2):
        if side_work is not None:
            side_work(p)
        xk = jnp.concatenate(
            [xs_ref[:, s * 8:(s + 1) * 8, :].reshape(ROW_TILE, LANES)
             for s in range(p * kc // LANES, (p + 1) * kc // LANES)], axis=1).astype(BF16)
        dg = _dot(xk, ex.wgb_ref[p * kc:(p + 1) * kc, :])
        du = _dot(xk, ex.wub_ref[p * kc:(p + 1) * kc, :])
        hg = dg if hg is None else hg + dg
        hu = du if hu is None else hu + du
    eye = (lax.broadcasted_iota(jnp.int32, (ROW_TILE, LANES), 0) % LANES
           == lax.broadcasted_iota(jnp.int32, (ROW_TILE, LANES), 1))
    row_blk = lax.broadcasted_iota(jnp.int32, (ROW_TILE, LANES), 0) // LANES
    wrows = functools.reduce(
        lambda a, b: a + b,
        [jnp.where(row_blk == k, ex.wt_ref[pl.ds(ex.wt0 + t * TILE_ROWS + k, 1), :], 0.0)
         for k in range(TILE_ROWS)])
    wcol = jnp.sum(jnp.where(eye, wrows, 0.0), axis=1, keepdims=True)
    a = (_silu(hg) * hu * wcol).astype(BF16)
    for p in range(N_PIECES // 2):
        if side_work is not None:
            side_work(N_PIECES // 2 + p)
        y = _dot(a, ex.wdb_ref[:, p * kc:(p + 1) * kc])
        for q in range(kc // LANES):
            s = p * kc // LANES + q
            ys_ref[:, s * 8:(s + 1) * 8, :] = y[:, q * LANES:(q + 1) * LANES].reshape(
                ROW_TILE // 8, 8, LANES)


def _expert_tiles(ex, t, n, xs_refs, ys_refs):
    _gather_rows(ex, t, 0, ROW_TILE, xs_refs[0])
    for i in range(n):
        def side_work(p, i=i):
            if i + 1 < n:
                _gather_rows(ex, t + i + 1, p * PIECE_ROWS, PIECE_ROWS, xs_refs[(i + 1) % 2])
            if i >= 1:
                _scatter_rows(ex, t + i - 1, p * PIECE_ROWS, PIECE_ROWS, ys_refs[(i - 1) % 2])

        _tile_ffn(ex, t + i, xs_refs[i % 2], ys_refs[i % 2], side_work if n > 1 else None)
    _scatter_rows(ex, t + n - 1, 0, ROW_TILE, ys_refs[(n - 1) % 2])


def _moe_routed_kernel(cnt_ref, list_ref, wt_ref, h_ref, wg_ref, wu_ref, wd_ref,
                       x_ref, sh_ref, g2_ref, *refs):
    out_refs, scratch = refs[:-8], refs[-8:]
    acc_ref, wgb_ref, wub_ref, wdb_ref = scratch[:4]
    xs_refs, ys_refs = scratch[4:6], scratch[6:8]
    c = pl.program_id(0)
    j = pl.program_id(1)

    @pl.when(j == 0)
    def _():
        acc_ref[...] = jnp.zeros_like(acc_ref)

    for k in range(EXPERTS_PER_STEP):
        expert = jnp.minimum(j, EXPERT_STEPS - 1) * EXPERTS_PER_STEP + k
        count = cnt_ref[c * N_EXPERTS + expert]

        @pl.when(jnp.logical_and(j < EXPERT_STEPS, count > 0))
        def _():
            wgb_ref[...] = wg_ref[0, k].astype(BF16)
            wub_ref[...] = wu_ref[0, k].astype(BF16)
            wdb_ref[...] = wd_ref[0, k].astype(BF16)

            ex = _ExpertRefs(k * 2 * LIST_ROWS, k * LIST_ROWS, list_ref, wt_ref, h_ref, acc_ref,
                             wgb_ref, wub_ref, wdb_ref)
            n_tiles = (count + ROW_TILE - 1) // ROW_TILE

            last3 = jnp.logical_and(n_tiles % 2 == 1, n_tiles >= 3)
            n_pairs = (n_tiles - jnp.where(last3, 3, n_tiles % 2)) // 2

            def tile_pair(m, carry):
                _expert_tiles(ex, 2 * m, 2, xs_refs, ys_refs)
                return carry

            lax.fori_loop(0, n_pairs, tile_pair, 0)

            @pl.when(last3)
            def _():
                _expert_tiles(ex, n_tiles - 3, 3, xs_refs, ys_refs)

            @pl.when(n_tiles == 1)
            def _():
                _expert_tiles(ex, 0, 1, xs_refs, ys_refs)

    @pl.when(j >= EXPERT_STEPS)
    def _():
        base = (j - EXPERT_STEPS) * (TILE * ROW_SUB)
        moe = jnp.concatenate(
            [acc_ref[pl.ds(base + s, TILE, stride=ROW_SUB), :] for s in range(ROW_SUB)], axis=1)
        new_x = x_ref[...] + g2_ref[pl.ds(c, 1), :] * (moe + sh_ref[...])
        if len(out_refs) == 1:
            out_refs[0][...] = new_x
        else:
            prompt_ref, latent_ref = out_refs

            @pl.when(c < T_P // MOE_TC)
            def _():
                prompt_ref[...] = new_x

            @pl.when(c >= T_P // MOE_TC)
            def _():
                latent_ref[...] = new_x


def _moe_routed(lists, wts, counts, hrows, w_gate, w_up, w_down, li, x, shared, mod,
                split_output):
    group = lambda j: jnp.minimum(j, EXPERT_STEPS - 1)
    list_map = lambda c, j, nt: (c * EXPERT_STEPS + group(j), 0)
    w_map = lambda c, j, nt: (li, group(j), 0, 0)
    w_in_spec = pl.BlockSpec((1, EXPERTS_PER_STEP, D_MODEL, D_EXPERT), w_map)
    w_out_spec = pl.BlockSpec((1, EXPERTS_PER_STEP, D_EXPERT, D_MODEL), w_map)
    out_blk = lambda c, j: c * N_FLUSH + jnp.maximum(j - EXPERT_STEPS, 0)
    tok_spec = pl.BlockSpec((TILE, D_MODEL), lambda c, j, nt: (out_blk(c, j), 0))
    if split_output:
        out_specs = [
            pl.BlockSpec((TILE, D_MODEL), lambda c, j, nt: (jnp.minimum(out_blk(c, j), P_TILES - 1), 0)),
            pl.BlockSpec((TILE, D_MODEL), lambda c, j, nt: (jnp.maximum(out_blk(c, j) - P_TILES, 0), 0)),
        ]
        out_shape = [jax.ShapeDtypeStruct((T_P, D_MODEL), F32),
                     jax.ShapeDtypeStruct((T_S, D_MODEL), F32)]
    else:
        out_specs = tok_spec
        out_shape = jax.ShapeDtypeStruct((T_ALL, D_MODEL), F32)
    stage = pltpu.VMEM((ROW_TILE // 8, 8 * ROW_SUB, LANES), F32)
    grid_spec = pltpu.PrefetchScalarGridSpec(
        num_scalar_prefetch=1,
        grid=(N_CHUNKS, EXPERT_STEPS + N_FLUSH),
        in_specs=[
            pl.BlockSpec((EXPERTS_PER_STEP * 2 * LIST_ROWS, LANES), list_map,
                         memory_space=pltpu.SMEM),
            pl.BlockSpec((EXPERTS_PER_STEP * LIST_ROWS, LANES), list_map),
            pl.BlockSpec((MOE_TC * ROW_SUB, LANES), lambda c, j, nt: (c, 0),
                         pipeline_mode=pl.Buffered(1)),
            w_in_spec, w_in_spec, w_out_spec,
            tok_spec, tok_spec,
            pl.BlockSpec((8, D_MODEL), lambda c, j, nt: (0, 5)),
        ],
        out_specs=out_specs,
        scratch_shapes=[
            pltpu.VMEM(((MOE_TC + 1) * ROW_SUB, LANES), F32),
            pltpu.VMEM((D_MODEL, D_EXPERT), BF16),
            pltpu.VMEM((D_MODEL, D_EXPERT), BF16),
            pltpu.VMEM((D_EXPERT, D_MODEL), BF16),
            stage, stage, stage, stage,
        ],
    )
    return pl.pallas_call(
        _moe_routed_kernel,
        grid_spec=grid_spec,
        out_shape=out_shape,
        compiler_params=_cparams("arbitrary", "arbitrary", vmem_limit=VMEM_LIMIT_ROUTED),
        name="moe_routed",
    )(counts.reshape(-1), lists, wts, hrows, w_gate, w_up, w_down, x, shared, mod)


def kernel(x_prompt, x_sample, cache_k, cache_v, c, c_ctx, w_mod, b_mod, norm1, norm2, w_in_even, conv_a, q_norm, k_norm, lam_q1, lam_k1, lam_q2, lam_k2, subln, w_out_even, w_in_odd, conv_c, conv_c_b, ln_c_g, ln_c_b, w_pool, pool_scale, w_out_odd, w_router, b_router, w_gate, w_up, w_down, ws_gate, ws_up, ws_down):
    x = (x_prompt.reshape(T_P, D_MODEL), x_sample.reshape(T_S, D_MODEL))
    cond = jnp.concatenate([c_ctx[None, :], c, jnp.zeros((8 - 1 - DEC_BATCH, D_MODEL), F32)], axis=0)
    mod_all = _modulation(cond, w_mod, b_mod)

    new_k, new_v = [], []
    for li in range(DEPTH):
        mod = mod_all[li]
        if li % 2 == 0:
            e = li // 2
            lam_init = 0.8 - 0.6 * math.exp(-0.3 * li)
            u = _norm_in(x, norm1[li], mod, w_in_even[e].astype(BF16))
            q, k, v, k_cache, v_cache = _qkv_prep(u, q_norm[e], k_norm[e])
            new_k.append(k_cache)
            new_v.append(v_cache)
            o = _attention(q, k, v, cache_k, cache_v, e,
                           (lam_q1[e], lam_k1[e], lam_q2[e], lam_k2[e]), subln[e], lam_init)
            x = _even_out(x, u, o, conv_a[e], mod, w_out_even[e].astype(BF16))
        else:
            o_ = li // 2
            u = _norm_in(x, norm1[li], mod, w_in_odd[o_].astype(BF16))
            x = _odd_out(x, u, conv_c[o_], conv_c_b[o_], ln_c_g[o_], ln_c_b[o_],
                         w_pool[o_].astype(BF16), pool_scale[o_], mod, w_out_odd[o_].astype(BF16))
        hrows, gate_b, rank_b, shared, cnt_b = _moe_pre(
            x, norm2[li], mod, w_router[li].T, b_router[li], ws_gate[li].astype(BF16),
            ws_up[li].astype(BF16), ws_down[li].astype(BF16))
        lists, wts = _moe_sort(rank_b, gate_b, cnt_b)
        x = _moe_routed(lists, wts, cnt_b[:, :, 0], hrows, w_gate, w_up, w_down, li, x, shared,
                        mod, split_output=(li == DEPTH - 1))

    y_prompt = x[0].reshape(BATCH, SEQ, D_MODEL)
    y_sample = x[1].reshape(DEC_BATCH, DEC_SEQ, D_MODEL)
    return (y_prompt, y_sample, jnp.stack(new_k, axis=1), jnp.stack(new_v, axis=1))
```
